```python
import math
import jax
import jax.numpy as jnp
from jax import lax
import numpy as np

D_MODEL = 1024
BATCH = 32
SEQ = 2048
DEPTH = 2

GRID_W = 64
CTX_LEN = 256
MIX_W = D_MODEL
N_MIXERS = 4
GROUP_W = MIX_W // N_MIXERS
DA_HEADS = 4
DA_HD = GROUP_W // DA_HEADS // 2
DA_VD = 2 * DA_HD
QBLOCK = 128
ROPE_BASE = 10000.0
HG_HEADS = 4
HG_HD = GROUP_W // HG_HEADS
HG_CHUNK = 16
SC_WIDTH = 3
S5_GROUP = 16
S5_NGROUPS = GROUP_W // S5_GROUP
S5_STATE = 64
FFN_HIDDEN = ((8 * D_MODEL + 3 * 256 - 1) // (3 * 256)) * 256
IN_COLS = 12 * GROUP_W
EPS = 1e-6

kernel_name = 'hybrid_parallel_heads_flow_block'


def rmsnorm(x, g):
    xf = x.astype(jnp.float32)
    xf = xf * lax.rsqrt(jnp.mean(xf * xf, axis=-1, keepdims=True) + EPS)
    return xf.astype(x.dtype) * g


def modulate(h, shift, scale):
    return h * (1 + scale) + shift


def swiglu(h, w_in, w_out):
    gate, up = jnp.split(h @ w_in, 2, axis=-1)
    return (jax.nn.silu(gate) * up) @ w_out


def axial_rope_tables(n_rows):
    rows = jnp.broadcast_to(jnp.arange(n_rows, dtype=jnp.float32)[:, None], (n_rows, GRID_W)).reshape(-1)
    cols = jnp.broadcast_to(jnp.arange(GRID_W, dtype=jnp.float32)[None, :], (n_rows, GRID_W)).reshape(-1)
    n_freq = DA_HD // 4
    inv = ROPE_BASE ** (-jnp.arange(n_freq, dtype=jnp.float32) / n_freq)
    ang = jnp.concatenate([rows[:, None] * inv, cols[:, None] * inv], axis=-1)
    return jnp.cos(ang), jnp.sin(ang)


def apply_rope(x, cos, sin):
    xf = x.astype(jnp.float32)
    x1, x2 = xf[..., 0::2], xf[..., 1::2]
    out = jnp.stack([x1 * cos - x2 * sin, x1 * sin + x2 * cos], axis=-1)
    return out.reshape(x.shape).astype(x.dtype)


def _qk_heads(a):
    b, t, _ = a.shape
    return a.reshape(b, t, DA_HEADS, 2, DA_HD).transpose(0, 2, 3, 1, 4)


def _v_heads(a):
    b, t, _ = a.shape
    return a.reshape(b, t, DA_HEADS, DA_VD).transpose(0, 2, 1, 3)


def diff_attend(q, k, v, lam):
    s = jnp.einsum('bhmqd,bhmkd->bhmqk', q, k).astype(jnp.float32) * (DA_HD ** -0.5)
    p = jax.nn.softmax(s, axis=-1)
    a = (p[:, :, 0] - lam * p[:, :, 1]).astype(v.dtype)
    return jnp.einsum('bhqk,bhkv->bqhv', a, v)


def diff_attention_mixer(px, pc, cos, sin, lam_vecs, subln_g, layer_idx, want_ctx_out):
    lam_init = 0.8 - 0.6 * math.exp(-0.3 * layer_idx)
    lv = lam_vecs.astype(jnp.float32)
    lam = jnp.exp(jnp.sum(lv[0] * lv[1])) - jnp.exp(jnp.sum(lv[2] * lv[3])) + lam_init

    def finish(o):
        b, t = o.shape[0], o.shape[1]
        return (rmsnorm(o, subln_g) * (1.0 - lam_init)).reshape(b, t, GROUP_W)

    qx = apply_rope(_qk_heads(px[0]), cos, sin)
    kx = apply_rope(_qk_heads(px[1]), cos, sin)
    kc, vc = _qk_heads(pc[1]), _v_heads(pc[2])
    keys = jnp.concatenate([kx, kc], axis=3)
    vals = jnp.concatenate([_v_heads(px[2]), vc], axis=2)
    b, h, _, t, d = qx.shape
    nb = t // QBLOCK
    q_blocks = qx.reshape(b, h, 2, nb, QBLOCK, d).transpose(3, 0, 1, 2, 4, 5)
    o_x = lax.map(lambda qb: diff_attend(qb, keys, vals, lam), q_blocks)
    out_x = finish(o_x.transpose(1, 0, 2, 3, 4).reshape(b, t, DA_HEADS, DA_VD))
    out_c = finish(diff_attend(_qk_heads(pc[0]), kc, vc, lam)) if want_ctx_out else None
    return out_x, out_c


def hgrn2_chunk_scan(q, k, v, log_f, s0, want_out):
    b, h, t, dk = k.shape
    n = t // HG_CHUNK

    def chunks(a):
        return a.reshape(b, h, n, HG_CHUNK, a.shape[-1]).transpose(2, 0, 1, 3, 4)

    kc, vc = chunks(k), chunks(v)
    G = jnp.cumsum(chunks(log_f), axis=-2)
    g_tot = G[..., -1, :]
    k_end = kc * jnp.exp(g_tot[..., None, :] - G)
    dec = jnp.exp(g_tot)
    if not want_out:
        def step_state(S, inp):
            d_n, k_n, v_n = inp
            return d_n[..., None] * S + jnp.einsum('bhjk,bhjv->bhkv', k_n, v_n), None
        s_fin, _ = lax.scan(step_state, s0, (dec, k_end, vc))
        return None, s_fin
    q_dec = chunks(q) * jnp.exp(G)
    k_inv = kc * jnp.exp(-G)
    mask = jnp.tril(jnp.ones((HG_CHUNK, HG_CHUNK), dtype=bool))
    A = jnp.where(mask, jnp.einsum('nbhik,nbhjk->nbhij', q_dec, k_inv), 0.0)
    o_intra = jnp.einsum('nbhij,nbhjv->nbhiv', A, vc)

    def step(S, inp):
        d_n, k_n, v_n, q_n = inp
        o_n = jnp.einsum('bhik,bhkv->bhiv', q_n, S)
        return d_n[..., None] * S + jnp.einsum('bhjk,bhjv->bhkv', k_n, v_n), o_n

    s_fin, o_inter = lax.scan(step, s0, (dec, k_end, vc, q_dec))
    o = (o_intra + o_inter).transpose(1, 2, 0, 3, 4).reshape(b, h, t, v.shape[-1])
    return o, s_fin


def _hg_heads(a):
    b, t, _ = a.shape
    return a.astype(jnp.float32).reshape(b, t, HG_HEADS, HG_HD).transpose(0, 2, 1, 3)


def _hg_gate(f_raw, lb):
    lb = lb.reshape(HG_HEADS, 1, HG_HD)
    f = lb + (1.0 - lb) * jax.nn.sigmoid(_hg_heads(f_raw))
    return 1.0 - f, jnp.log(f)


def _tflip(a):
    return jnp.flip(a, axis=2)


def hgrn2_mixer(px, pc, lb, norm_g, want_ctx_out):
    b = px[0].shape[0]
    s0 = jnp.zeros((b, HG_HEADS, HG_HD, HG_HD), jnp.float32)
    kcf, lcf = _hg_gate(pc[1], lb[0])
    kcb, lcb = _hg_gate(pc[2], lb[1])
    ic = _hg_heads(pc[3])
    qc = jax.nn.silu(_hg_heads(pc[0])) if want_ctx_out else None
    oc_f, s_f = hgrn2_chunk_scan(qc, kcf, ic, lcf, s0, want_ctx_out)
    oc_b, s_b = hgrn2_chunk_scan(_tflip(qc) if want_ctx_out else None, _tflip(kcb), _tflip(ic), _tflip(lcb), s0, want_ctx_out)
    kxf, lxf = _hg_gate(px[1], lb[0])
    kxb, lxb = _hg_gate(px[2], lb[1])
    ix = _hg_heads(px[3])
    qx = jax.nn.silu(_hg_heads(px[0]))
    ox_f, _ = hgrn2_chunk_scan(qx, kxf, ix, lxf, s_f, True)
    ox_b, _ = hgrn2_chunk_scan(_tflip(qx), _tflip(kxb), _tflip(ix), _tflip(lxb), s_b, True)

    def finish(o, g_raw):
        bb, t = g_raw.shape[0], g_raw.shape[1]
        o = rmsnorm(o.transpose(0, 2, 1, 3), norm_g.astype(jnp.float32))
        o = o * jax.nn.silu(g_raw.astype(jnp.float32).reshape(bb, t, HG_HEADS, HG_HD))
        return o.reshape(bb, t, GROUP_W).astype(g_raw.dtype)

    out_x = finish(ox_f + _tflip(ox_b), px[4])
    out_c = finish(oc_f + _tflip(oc_b), pc[4]) if want_ctx_out else None
    return out_x, out_c


def short_conv_mixer(p, w, bias):
    b_gate, c_gate, u = p
    v = c_gate * u
    y = lax.conv_general_dilated(v, w[:, None, :].astype(v.dtype), window_strides=(1,),
                                 padding=((SC_WIDTH // 2, SC_WIDTH // 2),),
                                 dimension_numbers=('NWC', 'WIO', 'NWC'),
                                 feature_group_count=GROUP_W)
    return b_gate * (y + bias)


def s5_discretise(a_re, a_im, log_dt, b_re, b_im):
    dt = jnp.exp(log_dt)[:, None]
    mag = jnp.exp(dt * a_re)
    ab_re, ab_im = mag * jnp.cos(dt * a_im), mag * jnp.sin(dt * a_im)
    den = a_re * a_re + a_im * a_im
    nr = ab_re - 1.0
    f_re = (nr * a_re + ab_im * a_im) / den
    f_im = (ab_im * a_re - nr * a_im) / den
    bb_re = f_re[..., None] * b_re - f_im[..., None] * b_im
    bb_im = f_re[..., None] * b_im + f_im[..., None] * b_re
    return ab_re, ab_im, bb_re, bb_im


def _complex_affine_combine(e1, e2):
    a1r, a1i, b1r, b1i = e1
    a2r, a2i, b2r, b2i = e2
    return (a2r * a1r - a2i * a1i, a2r * a1i + a2i * a1r,
            a2r * b1r - a2i * b1i + b2r, a2r * b1i + a2i * b1r + b2i)


def s5_scan(u, ab_re, ab_im, bb_re, bb_im, x0_re, x0_im):
    bu_re = jnp.einsum('btgp,gnp->btgn', u, bb_re)
    bu_im = jnp.einsum('btgp,gnp->btgn', u, bb_im)
    bu_re = bu_re.at[:, 0].add(ab_re * x0_re - ab_im * x0_im)
    bu_im = bu_im.at[:, 0].add(ab_re * x0_im + ab_im * x0_re)
    a_re = jnp.broadcast_to(ab_re, bu_re.shape)
    a_im = jnp.broadcast_to(ab_im, bu_im.shape)
    _, _, x_re, x_im = lax.associative_scan(_complex_affine_combine, (a_re, a_im, bu_re, bu_im), axis=1)
    return x_re, x_im


def s5_mixer(u_x, u_c, a_re, a_im, log_dt, b_re, b_im, c_re, c_im, d, w_glu, b_glu, want_ctx_out):
    f32 = jnp.float32
    ux = u_x.astype(f32).reshape(u_x.shape[0], u_x.shape[1], S5_NGROUPS, S5_GROUP)
    uc = u_c.astype(f32).reshape(u_c.shape[0], u_c.shape[1], S5_NGROUPS, S5_GROUP)
    zeros = jnp.zeros((ux.shape[0], S5_NGROUPS, S5_STATE), f32)
    d_g = d.astype(f32).reshape(S5_NGROUPS, S5_GROUP)
    y_x = d_g * ux
    y_c = d_g * uc if want_ctx_out else None
    for direction in range(2):
        rev = direction == 1
        disc = s5_discretise(a_re[direction].astype(f32), a_im[direction].astype(f32),
                             log_dt[direction].astype(f32), b_re[direction].astype(f32),
                             b_im[direction].astype(f32))
        cr, ci = c_re[direction].astype(f32), c_im[direction].astype(f32)
        uc_o = jnp.flip(uc, axis=1) if rev else uc
        ux_o = jnp.flip(ux, axis=1) if rev else ux
        xc_re, xc_im = s5_scan(uc_o, *disc, zeros, zeros)
        xx_re, xx_im = s5_scan(ux_o, *disc, xc_re[:, -1], xc_im[:, -1])
        yx = jnp.einsum('btgn,gpn->btgp', xx_re, cr) - jnp.einsum('btgn,gpn->btgp', xx_im, ci)
        y_x = y_x + (jnp.flip(yx, axis=1) if rev else yx)
        if want_ctx_out:
            yc = jnp.einsum('btgn,gpn->btgp', xc_re, cr) - jnp.einsum('btgn,gpn->btgp', xc_im, ci)
            y_c = y_c + (jnp.flip(yc, axis=1) if rev else yc)

    def glu(y, dtype):
        y = jax.nn.gelu(y.reshape(y.shape[0], y.shape[1], GROUP_W))
        return (y * jax.nn.sigmoid(y @ w_glu.astype(f32) + b_glu.astype(f32))).astype(dtype)

    out_c = glu(y_c, u_c.dtype) if want_ctx_out else None
    return glu(y_x, u_x.dtype), out_c


def setup_inputs(seed: int = 0) -> dict:
    key = jax.random.key(seed)
    ks = jax.random.split(key, 27)
    f32 = jnp.float32
    L = DEPTH

    def nrm(k, shape, s):
        return jax.random.normal(k, shape, f32) * s

    return {
        'x': nrm(ks[0], (BATCH, SEQ, D_MODEL), 1.0),
        'c': nrm(ks[1], (BATCH, D_MODEL), 1.0),
        'ctx': nrm(ks[2], (BATCH, CTX_LEN, D_MODEL), 1.0),
        'c_ctx': nrm(ks[3], (D_MODEL,), 1.0),
        'w_ada': nrm(ks[4], (L, D_MODEL, 6 * D_MODEL), 0.5 * D_MODEL ** -0.5),
        'b_ada': nrm(ks[5], (L, 6 * D_MODEL), 0.02),
        'norm_g': 1.0 + nrm(ks[6], (L, 4, D_MODEL), 0.05),
        'w_in': nrm(ks[7], (L, D_MODEL, IN_COLS), D_MODEL ** -0.5),
        'w_out': nrm(ks[8], (L, MIX_W, D_MODEL), MIX_W ** -0.5),
        'da_lambda': nrm(ks[9], (L, 4, DA_HD), 0.1),
        'da_subln': 1.0 + nrm(ks[10], (L, DA_VD), 0.05),
        'hg_lb': nrm(ks[11], (L, 2, GROUP_W), 0.5),
        'hg_norm': 1.0 + nrm(ks[12], (L, HG_HD), 0.05),
        'sc_w': nrm(ks[13], (L, SC_WIDTH, GROUP_W), SC_WIDTH ** -0.5),
        'sc_b': nrm(ks[14], (L, GROUP_W), 0.02),
        's5_a_re': -0.5 + nrm(ks[15], (L, 2, S5_NGROUPS, S5_STATE), 0.02),
        's5_a_im': jnp.pi * jnp.arange(S5_STATE, dtype=f32) + nrm(ks[16], (L, 2, S5_NGROUPS, S5_STATE), 0.02),
        's5_log_dt': jax.random.uniform(ks[17], (L, 2, S5_NGROUPS), f32, math.log(1e-3), math.log(1e-1)),
        's5_b_re': nrm(ks[18], (L, 2, S5_NGROUPS, S5_STATE, S5_GROUP), (2 * S5_GROUP) ** -0.5),
        's5_b_im': nrm(ks[19], (L, 2, S5_NGROUPS, S5_STATE, S5_GROUP), (2 * S5_GROUP) ** -0.5),
        's5_c_re': nrm(ks[20], (L, 2, S5_NGROUPS, S5_GROUP, S5_STATE), (2 * S5_STATE) ** -0.5),
        's5_c_im': nrm(ks[21], (L, 2, S5_NGROUPS, S5_GROUP, S5_STATE), (2 * S5_STATE) ** -0.5),
        's5_d': nrm(ks[22], (L, GROUP_W), 1.0),
        's5_w_glu': nrm(ks[23], (L, GROUP_W, GROUP_W), GROUP_W ** -0.5),
        's5_b_glu': nrm(ks[24], (L, GROUP_W), 0.02),
        'w_ffn_in': nrm(ks[25], (L, D_MODEL, 2 * FFN_HIDDEN), D_MODEL ** -0.5),
        'w_ffn_out': nrm(ks[26], (L, FFN_HIDDEN, D_MODEL), FFN_HIDDEN ** -0.5),
    }


def reference(x, c, ctx, c_ctx, w_ada, b_ada, norm_g, w_in, w_out, da_lambda, da_subln,
              hg_lb, hg_norm, sc_w, sc_b, s5_a_re, s5_a_im, s5_log_dt, s5_b_re, s5_b_im,
              s5_c_re, s5_c_im, s5_d, s5_w_glu, s5_b_glu, w_ffn_in, w_ffn_out):
    n_rows = x.shape[1] // GRID_W
    cos, sin = axial_rope_tables(n_rows)
    lb = jnp.cumsum(jax.nn.softmax(hg_lb.astype(jnp.float32), axis=0), axis=0)
    lb = lb - lb[:1]
    silu_c = jax.nn.silu(c)
    silu_cc = jax.nn.silu(c_ctx)
    h = ctx
    for l in range(DEPTH):
        ctx_out = l < DEPTH - 1
        mod_x = jnp.split((silu_c @ w_ada[l] + b_ada[l])[:, None, :], 6, axis=-1)
        mod_c = jnp.split((silu_cc @ w_ada[l] + b_ada[l])[None, None, :], 6, axis=-1)
        px = jnp.split(modulate(rmsnorm(x, norm_g[l, 0]), mod_x[0], mod_x[1]) @ w_in[l], 12, axis=-1)
        pc = jnp.split(modulate(rmsnorm(h, norm_g[l, 0]), mod_c[0], mod_c[1]) @ w_in[l], 12, axis=-1)
        a_x, a_c = diff_attention_mixer(px[0:3], pc[0:3], cos, sin, da_lambda[l], da_subln[l], l, ctx_out)
        b_x, b_c = hgrn2_mixer(px[3:8], pc[3:8], lb[l], hg_norm[l], ctx_out)
        c_x = short_conv_mixer(px[8:11], sc_w[l], sc_b[l])
        d_x, d_c = s5_mixer(px[11], pc[11], s5_a_re[l], s5_a_im[l], s5_log_dt[l], s5_b_re[l], s5_b_im[l],
                            s5_c_re[l], s5_c_im[l], s5_d[l], s5_w_glu[l], s5_b_glu[l], ctx_out)
        mix_x = jnp.concatenate([a_x, b_x, c_x, d_x], axis=-1) @ w_out[l]
        x = x + mod_x[2] * rmsnorm(mix_x, norm_g[l, 1])
        ffn_x = swiglu(modulate(rmsnorm(x, norm_g[l, 2]), mod_x[3], mod_x[4]), w_ffn_in[l], w_ffn_out[l])
        x = x + mod_x[5] * rmsnorm(ffn_x, norm_g[l, 3])
        if ctx_out:
            c_c = short_conv_mixer(pc[8:11], sc_w[l], sc_b[l])
            mix_c = jnp.concatenate([a_c, b_c, c_c, d_c], axis=-1) @ w_out[l]
            h = h + mod_c[2] * rmsnorm(mix_c, norm_g[l, 1])
            ffn_c = swiglu(modulate(rmsnorm(h, norm_g[l, 2]), mod_c[3], mod_c[4]), w_ffn_in[l], w_ffn_out[l])
            h = h + mod_c[5] * rmsnorm(ffn_c, norm_g[l, 3])
    return x
```

```python
import functools
import math

import jax
import jax.numpy as jnp
from jax import lax
from jax.experimental import pallas as pl
from jax.experimental.pallas import tpu as pltpu

F32 = jnp.float32
BF16 = jnp.bfloat16

EPS = 1e-6
GRID_W = 64
ROPE_BASE = 10000.0
GROUP_W = 256
DA_HEADS = 4
DA_HD = 32
HG_HEADS = 4
HG_HD = 64
HG_CHUNK = 16
S5_NGROUPS = 16
S5_GROUP = 16
S5_STATE = 64
S5_COLS = S5_NGROUPS * S5_STATE
LANES = 128
SUBLANES = 8
TILE = 256
ATT_ROWS = 32
VMEM_LIMIT = 56 * 1024 * 1024

_NT = (((1,), (1,)), ((), ()))


def _dot(a, b):
    return jnp.dot(a, b, preferred_element_type=F32)


def _dot_nt(a, b):
    return lax.dot_general(a, b, _NT, preferred_element_type=F32)


def _split3_dot(m, x):
    x1 = x.astype(BF16)
    r1 = x - x1.astype(F32)
    x2 = r1.astype(BF16)
    x3 = (r1 - x2.astype(F32)).astype(BF16)
    return _dot(m, x1) + _dot(m, x2) + _dot(m, x3)


def _split2_dot(x, m):
    x1 = x.astype(BF16)
    x2 = (x - x1.astype(F32)).astype(BF16)
    return _dot(x1, m) + _dot(x2, m)


def _silu(x):
    return x * jax.nn.sigmoid(x)


def _rms_scale(x):
    return lax.rsqrt(jnp.mean(x * x, axis=-1, keepdims=True) + EPS)


def _group_mean_matrix(n, group):
    sh = int(math.log2(group))
    r = lax.broadcasted_iota(jnp.int32, (n, n), 0) >> sh
    c = lax.broadcasted_iota(jnp.int32, (n, n), 1) >> sh
    return jnp.where(r == c, 1.0 / group, 0.0).astype(BF16)


def _params(*sem):
    return pltpu.CompilerParams(dimension_semantics=sem, vmem_limit_bytes=VMEM_LIMIT)


def _mod_kernel(c_ref, w_ref, b_ref, o_ref):
    sc = _silu(c_ref[...])
    o_ref[0] = _dot(sc.astype(BF16), w_ref[0].astype(BF16)) + b_ref[0]


def _modulation(cvec, w_ada, b_ada):
    L, D, N = w_ada.shape
    Bp = cvec.shape[0]
    tn = 1536
    return pl.pallas_call(
        _mod_kernel,
        grid=(L, N // tn),
        in_specs=[
            pl.BlockSpec((Bp, D), lambda l, j: (0, 0)),
            pl.BlockSpec((1, D, tn), lambda l, j: (l, 0, j)),
            pl.BlockSpec((1, 1, tn), lambda l, j: (l, 0, j)),
        ],
        out_specs=pl.BlockSpec((1, Bp, tn), lambda l, j: (l, 0, j)),
        out_shape=jax.ShapeDtypeStruct((L, Bp, N), F32),
        compiler_params=_params("arbitrary", "arbitrary"),
        name="modulation",
    )(cvec, w_ada, b_ada.reshape(L, 1, N))


def _rope(x, cos, sin_a, sin_b):
    return (x * cos + pltpu.roll(x, GROUP_W - DA_HD // 2, 1) * sin_a
            + pltpu.roll(x, DA_HD // 2, 1) * sin_b)


def _inproj_kernel(h_ref, mod_ref, g_ref, w_ref, cos_ref, sa_ref, sb_ref, qkv_ref, rest_ref):
    h = h_ref[0]
    hn = h * _rms_scale(h) * g_ref[...]
    y = hn * (1.0 + mod_ref[0, 1:2, :]) + mod_ref[0, 0:1, :]
    p = _dot(y.astype(BF16), w_ref[...])
    cos, sa, sb = cos_ref[...], sa_ref[...], sb_ref[...]
    q = _rope(p[:, 0:GROUP_W], cos, sa, sb) * (DA_HD ** -0.5)
    k = _rope(p[:, GROUP_W:2 * GROUP_W], cos, sa, sb)
    qkv_ref[0, :, 0:GROUP_W] = q.astype(BF16)
    qkv_ref[0, :, GROUP_W:2 * GROUP_W] = k.astype(BF16)
    qkv_ref[0, :, 2 * GROUP_W:3 * GROUP_W] = p[:, 2 * GROUP_W:3 * GROUP_W].astype(BF16)
    rest_ref[0] = p[:, 3 * GROUP_W:]


def _inproj(h, mod, g, w, cos, sa, sb, n_batch, n_ctx_tiles):
    B, S, D = h.shape
    N = w.shape[1]
    nt = S // TILE

    def mod_idx(b, t):
        return (jnp.where(t < n_ctx_tiles, n_batch, b), 0, 0)

    tab = pl.BlockSpec((TILE, GROUP_W), lambda b, t: (t, 0))
    return pl.pallas_call(
        _inproj_kernel,
        grid=(B, nt),
        in_specs=[
            pl.BlockSpec((1, TILE, D), lambda b, t: (b, t, 0)),
            pl.BlockSpec((1, 6, D), mod_idx),
            pl.BlockSpec((1, D), lambda b, t: (0, 0)),
            pl.BlockSpec((D, N), lambda b, t: (0, 0)),
            tab, tab, tab,
        ],
        out_specs=[
            pl.BlockSpec((1, TILE, 3 * GROUP_W), lambda b, t: (b, t, 0)),
            pl.BlockSpec((1, TILE, N - 3 * GROUP_W), lambda b, t: (b, t, 0)),
        ],
        out_shape=[
            jax.ShapeDtypeStruct((B, S, 3 * GROUP_W), BF16),
            jax.ShapeDtypeStruct((B, S, N - 3 * GROUP_W), F32),
        ],
        compiler_params=_params("parallel", "arbitrary"),
        name="inproj",
    )(h, mod, g.reshape(1, D), w, cos, sa, sb)


def _attn_kernel(lam_init, ctx_len, q_off, q_ref, k_ref, v_ref, lamv_ref, g_ref, o_ref,
                 sc_ref, a_ref):
    tq = q_ref.shape[1]
    n_keys = k_ref.shape[1]
    lv = lamv_ref[...]
    lam = (jnp.exp(jnp.sum(lv[0:1] * lv[1:2], axis=-1, keepdims=True))
           - jnp.exp(jnp.sum(lv[2:3] * lv[3:4], axis=-1, keepdims=True)) + lam_init)
    lane = lax.broadcasted_iota(jnp.int32, (1, GROUP_W), 1)
    lane_map = lane >> 5
    lane_head = lane >> 6
    q = q_ref[0]
    q8 = jnp.concatenate(
        [jnp.where(lane_map == j, q, jnp.zeros_like(q)) for j in range(2 * DA_HEADS)], axis=0)

    def attend(nk):
        sc_ref[:, 0:nk] = _dot_nt(q8, k_ref[0, 0:nk, :])
        acc = jnp.zeros((tq, GROUP_W), F32)
        for h in range(DA_HEADS):
            def body(r, carry):
                r0 = pl.multiple_of(r * ATT_ROWS, ATT_ROWS)
                s0 = sc_ref[pl.ds(pl.multiple_of(2 * h * tq + r0, ATT_ROWS), ATT_ROWS), 0:nk]
                s1 = sc_ref[pl.ds(pl.multiple_of((2 * h + 1) * tq + r0, ATT_ROWS), ATT_ROWS), 0:nk]
                e0 = jnp.exp(s0 - jnp.max(s0, axis=-1, keepdims=True))
                e1 = jnp.exp(s1 - jnp.max(s1, axis=-1, keepdims=True))
                w0 = 1.0 / jnp.sum(e0, axis=-1, keepdims=True)
                w1 = lam / jnp.sum(e1, axis=-1, keepdims=True)
                a_ref[pl.ds(r0, ATT_ROWS), 0:nk] = (e0 * w0 - e1 * w1).astype(BF16)
                return carry

            lax.fori_loop(0, tq // ATT_ROWS, body, 0)
            oh = _dot(a_ref[:, 0:nk], v_ref[0, 0:nk, :])
            acc = acc + jnp.where(lane_head == h, oh, 0.0)
        ms = _split2_dot(acc * acc, _group_mean_matrix(GROUP_W, 2 * DA_HD))
        o_ref[0] = acc * lax.rsqrt(ms + EPS) * g_ref[...] * (1.0 - lam_init)

    if q_off == 0 and ctx_len == tq:
        @pl.when(pl.program_id(1) == 0)
        def _():
            attend(ctx_len)

        @pl.when(pl.program_id(1) != 0)
        def _():
            attend(n_keys)
    else:
        attend(n_keys)


def _attention(qkv, lam_vecs, subln_g, lam_init, ctx_len, want_ctx):
    B, S, _ = qkv.shape
    tq = TILE
    q_off = 0 if want_ctx else ctx_len // tq
    nq = S // tq - q_off
    g = jnp.tile(subln_g.astype(F32), DA_HEADS).reshape(1, GROUP_W)
    return pl.pallas_call(
        functools.partial(_attn_kernel, lam_init, ctx_len, q_off),
        grid=(B, nq),
        in_specs=[
            pl.BlockSpec((1, tq, GROUP_W), lambda b, i: (b, i + q_off, 0)),
            pl.BlockSpec((1, S, GROUP_W), lambda b, i: (b, 0, 1)),
            pl.BlockSpec((1, S, GROUP_W), lambda b, i: (b, 0, 2)),
            pl.BlockSpec((4, DA_HD), lambda b, i: (0, 0)),
            pl.BlockSpec((1, GROUP_W), lambda b, i: (0, 0)),
        ],
        out_specs=pl.BlockSpec((1, tq, GROUP_W), lambda b, i: (b, i, 0)),
        out_shape=jax.ShapeDtypeStruct((B, nq * tq, GROUP_W), F32),
        scratch_shapes=[
            pltpu.VMEM((2 * DA_HEADS * tq, S), F32),
            pltpu.VMEM((tq, S), BF16),
        ],
        compiler_params=_params("parallel", "arbitrary"),
        name="diff_attention",
    )(qkv, qkv, qkv, lam_vecs.astype(F32), g)


def _hgrn_kernel(n_tiles, q_ref, ff_ref, fb_ref, i_ref, g_ref, lb_ref, ng_ref, o_ref,
                 of_ref, st_ref, u_ref, oc_ref):
    n_chunks = TILE // HG_CHUNK
    row = lax.broadcasted_iota(jnp.int32, (TILE, TILE), 0)
    col = lax.broadcasted_iota(jnp.int32, (TILE, TILE), 1)
    same_chunk = (row >> 4) == (col >> 4)
    ones_blk = jnp.where(same_chunk, 1.0, 0.0).astype(BF16)
    lane = lax.broadcasted_iota(jnp.int32, (1, GROUP_W), 1)
    lane_head = lane >> 6
    lane_chunk = lane >> 4
    gmean = _group_mean_matrix(GROUP_W, HG_HD)

    for d in range(2):
        causal = same_chunk & ((col <= row) if d == 0 else (col >= row))
        tri = jnp.where(causal, 1.0, 0.0).astype(BF16)
        f_ref = ff_ref if d == 0 else fb_ref
        lb = lb_ref[d:d + 1, :]
        st_ref[...] = jnp.zeros_like(st_ref)

        def tile_body(ti, carry):
            if d == 0:
                t = ti
            else:
                t = jnp.where(ti == 0, 0, n_tiles - ti)
            rows = pl.ds(pl.multiple_of(t * TILE, TILE), TILE)
            f = lb + (1.0 - lb) * jax.nn.sigmoid(f_ref[0, rows, :])
            kk = 1.0 - f
            lf = jnp.log(f)
            G = _split3_dot(tri, lf)
            gt = _split3_dot(ones_blk, lf)
            qd = _silu(q_ref[0, rows, :]) * jnp.exp(G)
            kinv = (kk * jnp.exp(-G)).astype(BF16)
            kend = (kk * jnp.exp(gt - G)).astype(BF16)
            dec = jnp.exp(gt)
            v = i_ref[0, rows, :]
            vb = v.astype(BF16)
            zero = jnp.zeros_like(qd)

            q4 = jnp.concatenate(
                [jnp.where(lane_head == h, qd, zero).astype(BF16) for h in range(HG_HEADS)], axis=0)
            a4 = _dot_nt(q4, kinv)
            a_cat = jnp.concatenate(
                [jnp.where(causal, a4[h * TILE:(h + 1) * TILE], 0.0).astype(BF16)
                 for h in range(HG_HEADS)], axis=1)
            zb = jnp.zeros_like(vb)
            v4 = jnp.concatenate(
                [jnp.where(lane_head == h, vb, zb) for h in range(HG_HEADS)], axis=0)
            o_tile = _dot(a_cat, v4)

            vt = v.T
            vt_stack = jnp.concatenate(
                [jnp.where(lane_chunk == c, vt, zero).astype(BF16) for c in range(n_chunks)], axis=0)
            u_ref[...] = _dot(vt_stack, kend)

            order = range(n_chunks) if d == 0 else range(n_chunks - 1, -1, -1)
            for c in order:
                st = st_ref[...]
                q4c = jnp.concatenate(
                    [q4[h * TILE + c * HG_CHUNK:h * TILE + (c + 1) * HG_CHUNK] for h in range(HG_HEADS)],
                    axis=0)
                oc4 = _dot_nt(q4c, st.astype(BF16))
                oc = jnp.zeros((HG_CHUNK, GROUP_W), F32)
                for h in range(HG_HEADS):
                    oc = oc + jnp.where(lane_head == h, oc4[h * HG_CHUNK:(h + 1) * HG_CHUNK], 0.0)
                oc_ref[c * HG_CHUNK:(c + 1) * HG_CHUNK, :] = oc
                st_ref[...] = st * dec[c * HG_CHUNK:c * HG_CHUNK + 1, :] + u_ref[c * TILE:(c + 1) * TILE, :]
            o_tile = o_tile + oc_ref[...]

            if d == 0:
                of_ref[rows, :] = o_tile
            else:
                tot = of_ref[rows, :] + o_tile
                ms = _split2_dot(tot * tot, gmean)
                o_ref[0, rows, :] = (tot * lax.rsqrt(ms + EPS) * ng_ref[...]
                                     * _silu(g_ref[0, rows, :]))
            return carry

        lax.fori_loop(0, n_tiles, tile_body, 0)


def _hgrn(rest, lb, norm_g):
    B, S, _ = rest.shape
    n_tiles = S // TILE

    def part(j):
        return pl.BlockSpec((1, S, GROUP_W), lambda b: (b, 0, j))

    ng = jnp.tile(norm_g.astype(F32), HG_HEADS).reshape(1, GROUP_W)
    return pl.pallas_call(
        functools.partial(_hgrn_kernel, n_tiles),
        grid=(B,),
        in_specs=[part(0), part(1), part(2), part(3), part(4),
                  pl.BlockSpec((2, GROUP_W), lambda b: (0, 0)),
                  pl.BlockSpec((1, GROUP_W), lambda b: (0, 0))],
        out_specs=pl.BlockSpec((1, S, GROUP_W), lambda b: (b, 0, 0)),
        out_shape=jax.ShapeDtypeStruct((B, S, GROUP_W), F32),
        scratch_shapes=[
            pltpu.VMEM((S, GROUP_W), F32),
            pltpu.VMEM((GROUP_W, GROUP_W), F32),
            pltpu.VMEM((TILE // HG_CHUNK * TILE, GROUP_W), F32),
            pltpu.VMEM((TILE, GROUP_W), F32),
        ],
        compiler_params=_params("parallel"),
        name="hgrn2",
    )(rest, rest, rest, rest, rest, lb, ng)


def _conv_kernel(ctx_len, b_ref, c_ref, u_ref, w_ref, bias_ref, o_ref):
    S = b_ref.shape[1]
    v = c_ref[0] * u_ref[0]
    row = lax.broadcasted_iota(jnp.int32, (S, 1), 0)
    prev = jnp.where((row == 0) | (row == ctx_len), 0.0, pltpu.roll(v, 1, 0))
    nxt = jnp.where((row == ctx_len - 1) | (row == S - 1), 0.0, pltpu.roll(v, S - 1, 0))
    y = w_ref[0:1, :] * prev + w_ref[1:2, :] * v + w_ref[2:3, :] * nxt
    o_ref[0] = b_ref[0] * (y + bias_ref[...])


def _short_conv(rest, w, bias, ctx_len):
    B, S, _ = rest.shape

    def part(j):
        return pl.BlockSpec((1, S, GROUP_W), lambda b: (b, 0, j))

    return pl.pallas_call(
        functools.partial(_conv_kernel, ctx_len),
        grid=(B,),
        in_specs=[part(5), part(6), part(7),
                  pl.BlockSpec((3, GROUP_W), lambda b: (0, 0)),
                  pl.BlockSpec((1, GROUP_W), lambda b: (0, 0))],
        out_specs=pl.BlockSpec((1, S, GROUP_W), lambda b: (b, 0, 0)),
        out_shape=jax.ShapeDtypeStruct((B, S, GROUP_W), F32),
        compiler_params=_params("parallel"),
        name="short_conv",
    )(rest, rest, rest, w.astype(F32), bias.astype(F32).reshape(1, GROUP_W))


def _s5_kernel(n_tiles, u_ref, bmat_ref, cmat_ref, tab_ref, d_ref, wg_ref, bg_ref, o_ref,
               xs_ref, y_ref, cr_ref, ci_ref):
    n_groups = TILE // SUBLANES
    n_strips = S5_COLS // LANES

    for d in range(2):
        cr_ref[...] = jnp.zeros_like(cr_ref)
        ci_ref[...] = jnp.zeros_like(ci_ref)
        shifts = (1, 2, 4) if d == 0 else (SUBLANES - 1, SUBLANES - 2, SUBLANES - 4)
        last = SUBLANES - 1 if d == 0 else 0

        def tile_body(ti, carry):
            if d == 0:
                t = ti
            else:
                t = jnp.where(ti == 0, 0, n_tiles - ti)
            rows = pl.ds(pl.multiple_of(t * TILE, TILE), TILE)
            u = u_ref[0, rows, :]
            xs_ref[...] = _dot(u.astype(BF16), bmat_ref[d])

            def group_body(r, c2):
                rr = r if d == 0 else n_groups - 1 - r
                g_rows = pl.ds(pl.multiple_of(rr * SUBLANES, SUBLANES), SUBLANES)
                for j in range(n_strips):
                    re_cols = slice(j * LANES, (j + 1) * LANES)
                    im_cols = slice(S5_COLS + j * LANES, S5_COLS + (j + 1) * LANES)
                    xr = xs_ref[g_rows, re_cols]
                    xi = xs_ref[g_rows, im_cols]
                    for s, sh in enumerate(shifts):
                        tr = tab_ref[d, 2 * s, :, re_cols]
                        tim = tab_ref[d, 2 * s + 1, :, re_cols]
                        sr = pltpu.roll(xr, sh, 0)
                        si = pltpu.roll(xi, sh, 0)
                        xr, xi = xr + tr * sr - tim * si, xi + tr * si + tim * sr
                    pr = tab_ref[d, 6, :, re_cols]
                    pim = tab_ref[d, 7, :, re_cols]
                    c_re = cr_ref[:, re_cols]
                    c_im = ci_ref[:, re_cols]
                    xr, xi = xr + pr * c_re - pim * c_im, xi + pr * c_im + pim * c_re
                    xs_ref[g_rows, re_cols] = xr
                    xs_ref[g_rows, im_cols] = xi
                    cr_ref[:, re_cols] = jnp.broadcast_to(xr[last:last + 1, :], (SUBLANES, LANES))
                    ci_ref[:, re_cols] = jnp.broadcast_to(xi[last:last + 1, :], (SUBLANES, LANES))
                return c2

            lax.fori_loop(0, n_groups, group_body, 0)
            yd = _dot(xs_ref[...].astype(BF16), cmat_ref[d])
            if d == 0:
                y_ref[rows, :] = d_ref[...] * u + yd
            else:
                y = jax.nn.gelu(y_ref[rows, :] + yd)
                gate = jax.nn.sigmoid(_dot(y.astype(BF16), wg_ref[...]) + bg_ref[...])
                o_ref[0, rows, :] = y * gate
            return carry

        lax.fori_loop(0, n_tiles, tile_body, 0)


def _s5_tables(ab_re, ab_im):
    a = (ab_re + 1j * ab_im).reshape(2, 1, S5_COLS).astype(jnp.complex64)
    i = jnp.arange(SUBLANES).reshape(1, SUBLANES, 1)
    out = []
    for d in range(2):
        ad = a[d]
        tabs = []
        for sh in (1, 2, 4):
            live = (i[0] >= sh) if d == 0 else (i[0] <= SUBLANES - 1 - sh)
            p = jnp.where(live, ad ** sh, 0.0)
            tabs += [jnp.real(p), jnp.imag(p)]
        expo = (i[0] + 1) if d == 0 else (SUBLANES - i[0])
        p = ad ** expo
        tabs += [jnp.real(p), jnp.imag(p)]
        out.append(jnp.stack(tabs))
    return jnp.stack(out).astype(F32)


def _s5_discretise(a_re, a_im, log_dt, b_re, b_im):
    dt = jnp.exp(log_dt)[..., None]
    mag = jnp.exp(dt * a_re)
    ab_re, ab_im = mag * jnp.cos(dt * a_im), mag * jnp.sin(dt * a_im)
    den = a_re * a_re + a_im * a_im
    nr = ab_re - 1.0
    f_re = (nr * a_re + ab_im * a_im) / den
    f_im = (ab_im * a_re - nr * a_im) / den
    bb_re = f_re[..., None] * b_re - f_im[..., None] * b_im
    bb_im = f_re[..., None] * b_im + f_im[..., None] * b_re
    return ab_re, ab_im, bb_re, bb_im


def _block_diag(blocks):
    G, r, c = blocks.shape
    eye = jnp.eye(G, dtype=blocks.dtype)
    return (eye[:, None, :, None] * blocks[:, :, None, :]).reshape(G * r, G * c)


def _s5_matrices(a_re, a_im, log_dt, b_re, b_im, c_re, c_im):
    ab_re, ab_im, bb_re, bb_im = _s5_discretise(
        a_re.astype(F32), a_im.astype(F32), log_dt.astype(F32), b_re.astype(F32), b_im.astype(F32))
    bmats, cmats = [], []
    for d in range(2):
        br = _block_diag(jnp.swapaxes(bb_re[d], 1, 2))
        bi = _block_diag(jnp.swapaxes(bb_im[d], 1, 2))
        bmats.append(jnp.concatenate([br, bi], axis=1))
        cr = _block_diag(jnp.swapaxes(c_re[d].astype(F32), 1, 2))
        ci = _block_diag(jnp.swapaxes(c_im[d].astype(F32), 1, 2))
        cmats.append(jnp.concatenate([cr, -ci], axis=0))
    return (jnp.stack(bmats).astype(BF16), jnp.stack(cmats).astype(BF16),
            _s5_tables(ab_re, ab_im))


def _s5(rest, bmat, cmat, tabs, d_skip, w_glu, b_glu):
    B, S, _ = rest.shape
    n_tiles = S // TILE
    return pl.pallas_call(
        functools.partial(_s5_kernel, n_tiles),
        grid=(B,),
        in_specs=[
            pl.BlockSpec((1, S, GROUP_W), lambda b: (b, 0, 8)),
            pl.BlockSpec(bmat.shape, lambda b: (0, 0, 0)),
            pl.BlockSpec(cmat.shape, lambda b: (0, 0, 0)),
            pl.BlockSpec(tabs.shape, lambda b: (0, 0, 0, 0)),
            pl.BlockSpec((1, GROUP_W), lambda b: (0, 0)),
            pl.BlockSpec((GROUP_W, GROUP_W), lambda b: (0, 0)),
            pl.BlockSpec((1, GROUP_W), lambda b: (0, 0)),
        ],
        out_specs=pl.BlockSpec((1, S, GROUP_W), lambda b: (b, 0, 0)),
        out_shape=jax.ShapeDtypeStruct((B, S, GROUP_W), F32),
        scratch_shapes=[
            pltpu.VMEM((TILE, 2 * S5_COLS), F32),
            pltpu.VMEM((S, GROUP_W), F32),
            pltpu.VMEM((SUBLANES, S5_COLS), F32),
            pltpu.VMEM((SUBLANES, S5_COLS), F32),
        ],
        compiler_params=_params("parallel"),
        name="s5",
    )(rest, bmat, cmat, tabs, d_skip.astype(F32).reshape(1, GROUP_W), w_glu.astype(BF16),
      b_glu.astype(F32).reshape(1, GROUP_W))


def _post_kernel(n_hidden_chunks, h_ref, a_ref, b_ref, c_ref, d_ref, mod_ref, ng_ref,
                 wo_ref, wi_ref, w2_ref, o_ref):
    h = h_ref[0]
    mix = (_dot(a_ref[0].astype(BF16), wo_ref[0:GROUP_W, :])
           + _dot(b_ref[0].astype(BF16), wo_ref[GROUP_W:2 * GROUP_W, :])
           + _dot(c_ref[0].astype(BF16), wo_ref[2 * GROUP_W:3 * GROUP_W, :])
           + _dot(d_ref[0].astype(BF16), wo_ref[3 * GROUP_W:4 * GROUP_W, :]))
    h = h + mod_ref[0, 2:3, :] * (mix * _rms_scale(mix) * ng_ref[1:2, :])
    y = (h * _rms_scale(h) * ng_ref[2:3, :]) * (1.0 + mod_ref[0, 4:5, :]) + mod_ref[0, 3:4, :]
    yb = y.astype(BF16)
    hidden = w2_ref.shape[0]
    hc = hidden // n_hidden_chunks
    ffn = jnp.zeros_like(h)
    for j in range(n_hidden_chunks):
        gate = _dot(yb, wi_ref[:, j * hc:(j + 1) * hc])
        up = _dot(yb, wi_ref[:, hidden + j * hc:hidden + (j + 1) * hc])
        ffn = ffn + _dot((_silu(gate) * up).astype(BF16), w2_ref[j * hc:(j + 1) * hc, :])
    o_ref[0] = h + mod_ref[0, 5:6, :] * (ffn * _rms_scale(ffn) * ng_ref[3:4, :])


def _post(h, a, b, c, d, mod, norm_g, w_out, w_ffn_in, w_ffn_out, n_batch, n_ctx_tiles, want_ctx):
    B, S, D = h.shape
    off = 0 if want_ctx else n_ctx_tiles
    nt = S // TILE - off
    a_off = off if a.shape[1] == S else 0
    hidden = w_ffn_out.shape[0]

    def mod_idx(bi, t):
        return (jnp.where(t + off < n_ctx_tiles, n_batch, bi), 0, 0)

    def rows(o):
        return lambda bi, t: (bi, t + o, 0)

    def whole(arr):
        return pl.BlockSpec(arr.shape, lambda bi, t: (0,) * arr.ndim, pipeline_mode=pl.Buffered(1))

    mix_spec = lambda o: pl.BlockSpec((1, TILE, GROUP_W), rows(o))
    return pl.pallas_call(
        functools.partial(_post_kernel, 2),
        grid=(B, nt),
        in_specs=[
            pl.BlockSpec((1, TILE, D), rows(off)),
            mix_spec(a_off), mix_spec(off), mix_spec(off), mix_spec(off),
            pl.BlockSpec((1, 6, D), mod_idx),
            whole(norm_g), whole(w_out), whole(w_ffn_in), whole(w_ffn_out),
        ],
        out_specs=pl.BlockSpec((1, TILE, D), lambda bi, t: (bi, t, 0)),
        out_shape=jax.ShapeDtypeStruct((B, nt * TILE, D), F32),
        compiler_params=_params("parallel", "arbitrary"),
        name="post",
    )(h, a, b, c, d, mod, norm_g, w_out, w_ffn_in, w_ffn_out)


def _rope_tables(n_rows, ctx_len):
    rows = jnp.broadcast_to(jnp.arange(n_rows, dtype=F32)[:, None], (n_rows, GRID_W)).reshape(-1)
    cols = jnp.broadcast_to(jnp.arange(GRID_W, dtype=F32)[None, :], (n_rows, GRID_W)).reshape(-1)
    n_freq = DA_HD // 4
    inv = ROPE_BASE ** (-jnp.arange(n_freq, dtype=F32) / n_freq)
    ang = jnp.concatenate([rows[:, None] * inv, cols[:, None] * inv], axis=-1)
    cos, sin = jnp.cos(ang), jnp.sin(ang)
    zero = jnp.zeros_like(sin)
    reps = GROUP_W // DA_HD

    def lanes(first, second, ctx_value):
        t = jnp.tile(jnp.concatenate([first, second], axis=-1), (1, reps))
        return jnp.concatenate([jnp.full((ctx_len, GROUP_W), ctx_value, F32), t], axis=0)

    return lanes(cos, cos, 1.0), lanes(-sin, zero, 0.0), lanes(zero, sin, 0.0)


def _deinterleave_qk(w_in):
    perm32 = jnp.concatenate([jnp.arange(0, DA_HD, 2), jnp.arange(1, DA_HD, 2)])
    perm = (jnp.arange(0, 2 * GROUP_W, DA_HD)[:, None] + perm32[None, :]).reshape(-1)
    cols = jnp.concatenate([perm, jnp.arange(2 * GROUP_W, w_in.shape[-1])])
    return w_in[..., cols]


def kernel(x, c, ctx, c_ctx, w_ada, b_ada, norm_g, w_in, w_out, da_lambda, da_subln, hg_lb, hg_norm, sc_w, sc_b, s5_a_re, s5_a_im, s5_log_dt, s5_b_re, s5_b_im, s5_c_re, s5_c_im, s5_d, s5_w_glu, s5_b_glu, w_ffn_in, w_ffn_out):
    B, T, D = x.shape
    ctx_len = ctx.shape[1]
    L = w_ada.shape[0]
    assert ctx_len % TILE == 0 and T % TILE == 0 and T % GRID_W == 0
    n_ctx_tiles = ctx_len // TILE

    bp = -(-(B + 1) // SUBLANES) * SUBLANES
    cvec = jnp.concatenate([c, c_ctx[None, :], jnp.zeros((bp - B - 1, D), c.dtype)], axis=0)
    mods = _modulation(cvec.astype(F32), w_ada, b_ada).reshape(L, bp, 6, D)

    cos, sin_a, sin_b = _rope_tables(T // GRID_W, ctx_len)
    lb = jnp.cumsum(jax.nn.softmax(hg_lb.astype(F32), axis=0), axis=0)
    lb = lb - lb[:1]
    w_in_b = _deinterleave_qk(w_in).astype(BF16)
    w_out_b = w_out.astype(BF16)
    w_ffn_in_b = w_ffn_in.astype(BF16)
    w_ffn_out_b = w_ffn_out.astype(BF16)

    h = jnp.concatenate([ctx, x], axis=1)
    for l in range(L):
        want_ctx = l < L - 1
        lam_init = 0.8 - 0.6 * math.exp(-0.3 * l)
        qkv, rest = _inproj(h, mods[l], norm_g[l, 0], w_in_b[l], cos, sin_a, sin_b, B, n_ctx_tiles)
        a = _attention(qkv, da_lambda[l], da_subln[l], lam_init, ctx_len, want_ctx)
        b = _hgrn(rest, lb[l], hg_norm[l])
        cc = _short_conv(rest, sc_w[l], sc_b[l], ctx_len)
        bmat, cmat, tabs = _s5_matrices(s5_a_re[l], s5_a_im[l], s5_log_dt[l], s5_b_re[l],
                                        s5_b_im[l], s5_c_re[l], s5_c_im[l])
        dd = _s5(rest, bmat, cmat, tabs, s5_d[l], s5_w_glu[l], s5_b_glu[l])
        h = _post(h, a, b, cc, dd, mods[l], norm_g[l].astype(F32), w_out_b[l], w_ffn_in_b[l],
                  w_ffn_out_b[l], B, n_ctx_tiles, want_ctx)
    return h
```

```python
import functools
import math

import jax
import jax.numpy as jnp
from jax import lax
from jax.experimental import pallas as pl
from jax.experimental.pallas import tpu as pltpu

F32 = jnp.float32
BF16 = jnp.bfloat16

EPS = 1e-6
GRID_W = 64
ROPE_BASE = 10000.0
GROUP_W = 256
DA_HEADS = 4
DA_HD = 32
HG_HEADS = 4
HG_HD = 64
HG_CHUNK = 16
S5_NGROUPS = 16
S5_GROUP = 16
S5_STATE = 64
S5_COLS = S5_NGROUPS * S5_STATE
LANES = 128
SUBLANES = 8
TILE = 256
S5_TT = 128
VMEM_LIMIT = 56 * 1024 * 1024

_NT = (((1,), (1,)), ((), ()))


def _dot(a, b):
    return jnp.dot(a, b, preferred_element_type=F32)


def _dot_nt(a, b):
    return lax.dot_general(a, b, _NT, preferred_element_type=F32)


def _split3_dot(m, x):
    x1 = x.astype(BF16)
    r1 = x - x1.astype(F32)
    x2 = r1.astype(BF16)
    x3 = (r1 - x2.astype(F32)).astype(BF16)
    return _dot(m, x1) + _dot(m, x2) + _dot(m, x3)


def _split3_dot_lhs(x, m):
    x1 = x.astype(BF16)
    r1 = x - x1.astype(F32)
    x2 = r1.astype(BF16)
    x3 = (r1 - x2.astype(F32)).astype(BF16)
    return _dot(x1, m) + _dot(x2, m) + _dot(x3, m)


def _split2_dot(x, m):
    x1 = x.astype(BF16)
    x2 = (x - x1.astype(F32)).astype(BF16)
    return _dot(x1, m) + _dot(x2, m)


def _silu(x):
    return x * jax.nn.sigmoid(x)


def _rms_scale(x):
    return lax.rsqrt(jnp.mean(x * x, axis=-1, keepdims=True) + EPS)


def _group_mean_matrix(n, group):
    sh = int(math.log2(group))
    r = lax.broadcasted_iota(jnp.int32, (n, n), 0) >> sh
    c = lax.broadcasted_iota(jnp.int32, (n, n), 1) >> sh
    return jnp.where(r == c, 1.0 / group, 0.0).astype(BF16)


def _params(*sem):
    return pltpu.CompilerParams(dimension_semantics=sem, vmem_limit_bytes=VMEM_LIMIT)


def _mod_kernel(c_ref, w_ref, b_ref, o_ref):
    sc = _silu(c_ref[...])
    o_ref[0] = _dot(sc.astype(BF16), w_ref[0].astype(BF16)) + b_ref[0]


def _modulation(cvec, w_ada, b_ada):
    L, D, N = w_ada.shape
    Bp = cvec.shape[0]
    tn = 1536
    return pl.pallas_call(
        _mod_kernel,
        grid=(L, N // tn),
        in_specs=[
            pl.BlockSpec((Bp, D), lambda l, j: (0, 0)),
            pl.BlockSpec((1, D, tn), lambda l, j: (l, 0, j)),
            pl.BlockSpec((1, 1, tn), lambda l, j: (l, 0, j)),
        ],
        out_specs=pl.BlockSpec((1, Bp, tn), lambda l, j: (l, 0, j)),
        out_shape=jax.ShapeDtypeStruct((L, Bp, N), F32),
        compiler_params=_params("arbitrary", "arbitrary"),
        name="modulation",
    )(cvec, w_ada, b_ada.reshape(L, 1, N))


def _rope(x, cos, sin_a, sin_b):
    return (x * cos + pltpu.roll(x, GROUP_W - DA_HD // 2, 1) * sin_a
            + pltpu.roll(x, DA_HD // 2, 1) * sin_b)


def _inproj_kernel(h_ref, mod_ref, g_ref, w_ref, cos_ref, sa_ref, sb_ref, qkv_ref, rest_ref):
    h = h_ref[0]
    hn = h * _rms_scale(h) * g_ref[...]
    y = hn * (1.0 + mod_ref[0, 1:2, :]) + mod_ref[0, 0:1, :]
    p = _dot(y.astype(BF16), w_ref[...])
    cos, sa, sb = cos_ref[...], sa_ref[...], sb_ref[...]
    q = _rope(p[:, 0:GROUP_W], cos, sa, sb) * (DA_HD ** -0.5 * math.log2(math.e))
    k = _rope(p[:, GROUP_W:2 * GROUP_W], cos, sa, sb)
    qkv_ref[0, :, 0:GROUP_W] = q.astype(BF16)
    qkv_ref[0, :, GROUP_W:2 * GROUP_W] = k.astype(BF16)
    qkv_ref[0, :, 2 * GROUP_W:3 * GROUP_W] = p[:, 2 * GROUP_W:3 * GROUP_W].astype(BF16)
    rest_ref[0] = p[:, 3 * GROUP_W:]


def _inproj(h, mod, g, w, cos, sa, sb, n_batch, n_ctx_tiles):
    B, S, D = h.shape
    N = w.shape[1]
    nt = S // TILE

    def mod_idx(b, t):
        return (jnp.where(t < n_ctx_tiles, n_batch, b), 0, 0)

    tab = pl.BlockSpec((TILE, GROUP_W), lambda b, t: (t, 0))
    return pl.pallas_call(
        _inproj_kernel,
        grid=(B, nt),
        in_specs=[
            pl.BlockSpec((1, TILE, D), lambda b, t: (b, t, 0)),
            pl.BlockSpec((1, 6, D), mod_idx),
            pl.BlockSpec((1, D), lambda b, t: (0, 0)),
            pl.BlockSpec((D, N), lambda b, t: (0, 0)),
            tab, tab, tab,
        ],
        out_specs=[
            pl.BlockSpec((1, TILE, 3 * GROUP_W), lambda b, t: (b, t, 0)),
            pl.BlockSpec((1, TILE, N - 3 * GROUP_W), lambda b, t: (b, t, 0)),
        ],
        out_shape=[
            jax.ShapeDtypeStruct((B, S, 3 * GROUP_W), BF16),
            jax.ShapeDtypeStruct((B, S, N - 3 * GROUP_W), F32),
        ],
        compiler_params=_params("parallel", "arbitrary"),
        name="inproj",
    )(h, mod, g.reshape(1, D), w, cos, sa, sb)


def _attn_kernel(lam_init, ctx_len, q_off, q_ref, k_ref, v_ref, lamv_ref, g_ref, o_ref,
                 sc_ref, e_ref, va_ref, vb_ref):
    tq = q_ref.shape[1]
    n_keys = k_ref.shape[1]
    lv = lamv_ref[...]
    lam = (jnp.exp(jnp.sum(lv[0:1] * lv[1:2], axis=-1, keepdims=True))
           - jnp.exp(jnp.sum(lv[2:3] * lv[3:4], axis=-1, keepdims=True)) + lam_init)
    lane = lax.broadcasted_iota(jnp.int32, (1, GROUP_W), 1)
    lane_map = lane >> 5
    lane_head = lane >> 6

    @pl.when(pl.program_id(1) == 0)
    def _():
        v = v_ref[0]
        one = jnp.ones_like(v)
        va_ref[...] = jnp.where(lane == GROUP_W - 1, one, v)
        vb_ref[...] = jnp.where(lane == 0, one, v)

    q = q_ref[0]
    q8 = jnp.concatenate(
        [jnp.where(lane_map == j, q, jnp.zeros_like(q)) for j in range(2 * DA_HEADS)], axis=0)

    def attend(nk):
        sc_ref[:, 0:nk] = _dot_nt(q8, k_ref[0, 0:nk, :])
        acc = jnp.zeros((tq, GROUP_W), F32)
        for h in range(DA_HEADS):
            for m in range(2):
                rows = slice((2 * h + m) * tq, (2 * h + m + 1) * tq)
                s = sc_ref[rows, 0:nk]
                e_ref[rows, 0:nk] = jnp.exp2(s - jnp.max(s, axis=-1, keepdims=True)).astype(BF16)
            vv = va_ref if h < DA_HEADS - 1 else vb_ref
            sum_col = GROUP_W - 1 if h < DA_HEADS - 1 else 0
            o2 = _dot(e_ref[2 * h * tq:(2 * h + 2) * tq, 0:nk], vv[0:nk, :])
            o0, o1 = o2[0:tq], o2[tq:2 * tq]
            w0 = 1.0 / o0[:, sum_col:sum_col + 1]
            w1 = lam / o1[:, sum_col:sum_col + 1]
            acc = acc + jnp.where(lane_head == h, o0 * w0 - o1 * w1, 0.0)
        ms = _split2_dot(acc * acc, _group_mean_matrix(GROUP_W, 2 * DA_HD))
        o_ref[0] = acc * lax.rsqrt(ms + EPS) * g_ref[...] * (1.0 - lam_init)

    if q_off == 0 and ctx_len == tq:
        @pl.when(pl.program_id(1) == 0)
        def _():
            attend(ctx_len)

        @pl.when(pl.program_id(1) != 0)
        def _():
            attend(n_keys)
    else:
        attend(n_keys)


def _attention(qkv, lam_vecs, subln_g, lam_init, ctx_len, want_ctx):
    B, S, _ = qkv.shape
    tq = TILE
    q_off = 0 if want_ctx else ctx_len // tq
    nq = S // tq - q_off
    g = jnp.tile(subln_g.astype(F32), DA_HEADS).reshape(1, GROUP_W)
    return pl.pallas_call(
        functools.partial(_attn_kernel, lam_init, ctx_len, q_off),
        grid=(B, nq),
        in_specs=[
            pl.BlockSpec((1, tq, GROUP_W), lambda b, i: (b, i + q_off, 0)),
            pl.BlockSpec((1, S, GROUP_W), lambda b, i: (b, 0, 1)),
            pl.BlockSpec((1, S, GROUP_W), lambda b, i: (b, 0, 2)),
            pl.BlockSpec((4, DA_HD), lambda b, i: (0, 0)),
            pl.BlockSpec((1, GROUP_W), lambda b, i: (0, 0)),
        ],
        out_specs=pl.BlockSpec((1, tq, GROUP_W), lambda b, i: (b, i, 0)),
        out_shape=jax.ShapeDtypeStruct((B, nq * tq, GROUP_W), F32),
        scratch_shapes=[
            pltpu.VMEM((2 * DA_HEADS * tq, S), F32),
            pltpu.VMEM((2 * DA_HEADS * tq, S), BF16),
            pltpu.VMEM((S, GROUP_W), BF16),
            pltpu.VMEM((S, GROUP_W), BF16),
        ],
        compiler_params=_params("parallel", "arbitrary"),
        name="diff_attention",
    )(qkv, qkv, qkv, lam_vecs.astype(F32), g)


def _hgrn_kernel(n_tiles, q_ref, ff_ref, fb_ref, i_ref, g_ref, lb_ref, lbt_ref, ng_ref, o_ref,
                 od_ref, st2_ref, u2_ref, oc2_ref):
    n_chunks = TILE // HG_CHUNK
    row = lax.broadcasted_iota(jnp.int32, (TILE, TILE), 0)
    col = lax.broadcasted_iota(jnp.int32, (TILE, TILE), 1)
    same_chunk = (row >> 4) == (col >> 4)
    ones_blk = jnp.where(same_chunk, 1.0, 0.0).astype(BF16)
    lane = lax.broadcasted_iota(jnp.int32, (1, GROUP_W), 1)
    lane_head = lane >> 6
    lane_chunk = lane >> 4
    gmean = _group_mean_matrix(GROUP_W, HG_HD)

    st2_ref[...] = jnp.zeros_like(st2_ref)

    def tile_body(ti, carry):
        for d in range(2):
            causal = same_chunk & ((col <= row) if d == 0 else (col >= row))
            tri = jnp.where(causal, 1.0, 0.0).astype(BF16)
            tri_t = jnp.where(same_chunk & ((row <= col) if d == 0 else (row >= col)),
                              1.0, 0.0).astype(BF16)
            f_ref = ff_ref if d == 0 else fb_ref
            lb = lb_ref[d:d + 1, :]
            lb_t = lbt_ref[:, d:d + 1]
            st_ref, u_ref, oc_ref = st2_ref.at[d], u2_ref.at[d], oc2_ref.at[d]
            if d == 0:
                t = ti
            else:
                t = jnp.where(ti == 0, 0, n_tiles - ti)
            rows = pl.ds(pl.multiple_of(t * TILE, TILE), TILE)
            fr = f_ref[0, rows, :]
            lf = jnp.log(lb + (1.0 - lb) * jax.nn.sigmoid(fr))
            G = _split3_dot(tri, lf)
            qd = _silu(q_ref[0, rows, :]) * jnp.exp(G)
            f_t = lb_t + (1.0 - lb_t) * jax.nn.sigmoid(fr.T)
            kk_t = 1.0 - f_t
            lf_t = jnp.log(f_t)
            g_t = _split3_dot_lhs(lf_t, tri_t)
            gt_t = _split3_dot_lhs(lf_t, ones_blk)
            kinv_t = (kk_t * jnp.exp(-g_t)).astype(BF16)
            kend_t = kk_t * jnp.exp(gt_t - g_t)
            dec_t = jnp.exp(gt_t)
            vb = i_ref[0, rows, :].astype(BF16)
            zero = jnp.zeros_like(qd)

            q4 = jnp.concatenate(
                [jnp.where(lane_head == h, qd, zero).astype(BF16) for h in range(HG_HEADS)], axis=0)
            a4 = _dot(q4, kinv_t)
            a_cat = jnp.concatenate(
                [jnp.where(causal, a4[h * TILE:(h + 1) * TILE], 0.0).astype(BF16)
                 for h in range(HG_HEADS)], axis=1)
            zb = jnp.zeros_like(vb)
            v4 = jnp.concatenate(
                [jnp.where(lane_head == h, vb, zb) for h in range(HG_HEADS)], axis=0)
            o_tile = _dot(a_cat, v4)

            k_stack = jnp.concatenate(
                [jnp.where(lane_chunk == c, kend_t, zero).astype(BF16) for c in range(n_chunks)], axis=0)
            u_ref[...] = _dot(k_stack, vb)

            order = range(n_chunks) if d == 0 else range(n_chunks - 1, -1, -1)
            for c in order:
                st = st_ref[...]
                q4c = jnp.concatenate(
                    [q4[h * TILE + c * HG_CHUNK:h * TILE + (c + 1) * HG_CHUNK] for h in range(HG_HEADS)],
                    axis=0)
                oc4 = _dot(q4c, st.astype(BF16))
                oc = jnp.zeros((HG_CHUNK, GROUP_W), F32)
                for h in range(HG_HEADS):
                    oc = oc + jnp.where(lane_head == h, oc4[h * HG_CHUNK:(h + 1) * HG_CHUNK], 0.0)
                oc_ref[c * HG_CHUNK:(c + 1) * HG_CHUNK, :] = oc
                st_ref[...] = (st * dec_t[:, c * HG_CHUNK:c * HG_CHUNK + 1]
                               + u_ref[c * TILE:(c + 1) * TILE, :])
            od_ref[d, rows, :] = o_tile + oc_ref[...]
        return carry

    lax.fori_loop(0, n_tiles, tile_body, 0)

    def finish_body(t, carry):
        rows = pl.ds(pl.multiple_of(t * TILE, TILE), TILE)
        tot = od_ref[0, rows, :] + od_ref[1, rows, :]
        ms = _split2_dot(tot * tot, gmean)
        o_ref[0, rows, :] = (tot * lax.rsqrt(ms + EPS) * ng_ref[...] * _silu(g_ref[0, rows, :]))
        return carry

    lax.fori_loop(0, n_tiles, finish_body, 0)


def _hgrn(rest, lb, norm_g):
    B, S, _ = rest.shape
    n_tiles = S // TILE

    def part(j):
        return pl.BlockSpec((1, S, GROUP_W), lambda b: (b, 0, j))

    ng = jnp.tile(norm_g.astype(F32), HG_HEADS).reshape(1, GROUP_W)
    return pl.pallas_call(
        functools.partial(_hgrn_kernel, n_tiles),
        grid=(B,),
        in_specs=[part(0), part(1), part(2), part(3), part(4),
                  pl.BlockSpec((2, GROUP_W), lambda b: (0, 0)),
                  pl.BlockSpec((GROUP_W, 2), lambda b: (0, 0)),
                  pl.BlockSpec((1, GROUP_W), lambda b: (0, 0))],
        out_specs=pl.BlockSpec((1, S, GROUP_W), lambda b: (b, 0, 0)),
        out_shape=jax.ShapeDtypeStruct((B, S, GROUP_W), F32),
        scratch_shapes=[
            pltpu.VMEM((2, S, GROUP_W), F32),
            pltpu.VMEM((2, GROUP_W, GROUP_W), F32),
            pltpu.VMEM((2, TILE // HG_CHUNK * TILE, GROUP_W), F32),
            pltpu.VMEM((2, TILE, GROUP_W), F32),
        ],
        compiler_params=_params("parallel"),
        name="hgrn2",
    )(rest, rest, rest, rest, rest, lb, lb.T, ng)


def _conv_kernel(ctx_len, b_ref, c_ref, u_ref, w_ref, bias_ref, o_ref):
    S = b_ref.shape[1]
    v = c_ref[0] * u_ref[0]
    row = lax.broadcasted_iota(jnp.int32, (S, 1), 0)
    prev = jnp.where((row == 0) | (row == ctx_len), 0.0, pltpu.roll(v, 1, 0))
    nxt = jnp.where((row == ctx_len - 1) | (row == S - 1), 0.0, pltpu.roll(v, S - 1, 0))
    y = w_ref[0:1, :] * prev + w_ref[1:2, :] * v + w_ref[2:3, :] * nxt
    o_ref[0] = b_ref[0] * (y + bias_ref[...])


def _short_conv(rest, w, bias, ctx_len):
    B, S, _ = rest.shape

    def part(j):
        return pl.BlockSpec((1, S, GROUP_W), lambda b: (b, 0, j))

    return pl.pallas_call(
        functools.partial(_conv_kernel, ctx_len),
        grid=(B,),
        in_specs=[part(5), part(6), part(7),
                  pl.BlockSpec((3, GROUP_W), lambda b: (0, 0)),
                  pl.BlockSpec((1, GROUP_W), lambda b: (0, 0))],
        out_specs=pl.BlockSpec((1, S, GROUP_W), lambda b: (b, 0, 0)),
        out_shape=jax.ShapeDtypeStruct((B, S, GROUP_W), F32),
        compiler_params=_params("parallel"),
        name="short_conv",
    )(rest, rest, rest, w.astype(F32), bias.astype(F32).reshape(1, GROUP_W))


def _s5_time_tile(d, i, n_ctx_tiles, n_tiles):
    back = jnp.where(i < n_ctx_tiles, n_ctx_tiles - 1 - i, n_tiles - 1 - (i - n_ctx_tiles))
    return jnp.where(d == 0, i, back)


def _s5_kernel(n_ctx_tiles, n_tiles, u_ref, bmat_ref, cmat_ref, a_ref, d_ref, wg_ref, bg_ref,
               o_ref, xs_ref, yf_ref, st_ref):
    d = pl.program_id(1)
    i = pl.program_id(2)
    tt = u_ref.shape[1]
    n_rows = tt * SUBLANES
    n_strips = S5_COLS // LANES

    @pl.when(i == 0)
    def _():
        st_ref[...] = jnp.zeros_like(st_ref)

    u = u_ref[0].reshape(n_rows, GROUP_W)
    xs_ref[...] = _dot(u.astype(BF16), bmat_ref[0])

    def lanes(j):
        return slice(j * LANES, (j + 1) * LANES)

    a_re = [jnp.broadcast_to(a_ref[0, 0:1, lanes(j)], (SUBLANES, LANES)) for j in range(n_strips)]
    a_im = [jnp.broadcast_to(a_ref[0, 1:2, lanes(j)], (SUBLANES, LANES)) for j in range(n_strips)]

    def step(s, carry):
        t = jnp.where(d == 0, s, tt - 1 - s)
        r = pl.ds(pl.multiple_of(t * SUBLANES, SUBLANES), SUBLANES)
        new = []
        for j in range(n_strips):
            xr, xi = carry[2 * j], carry[2 * j + 1]
            nr = a_re[j] * xr - a_im[j] * xi + xs_ref[r, lanes(j)]
            ni = a_re[j] * xi + a_im[j] * xr + xs_ref[r, lanes(n_strips + j)]
            xs_ref[r, lanes(j)] = nr
            xs_ref[r, lanes(n_strips + j)] = ni
            new += [nr, ni]
        return tuple(new)

    state = lax.fori_loop(0, tt, step, tuple(st_ref[k] for k in range(2 * n_strips)), unroll=2)
    for k in range(2 * n_strips):
        st_ref[k] = state[k]

    yd = _dot(xs_ref[...].astype(BF16), cmat_ref[0])
    t_idx = _s5_time_tile(d, i, n_ctx_tiles, n_tiles)
    rows = pl.ds(pl.multiple_of(t_idx * n_rows, n_rows), n_rows)

    @pl.when(d == 0)
    def _():
        yf_ref[rows, :] = d_ref[...] * u + yd

    @pl.when(d == 1)
    def _():
        y = jax.nn.gelu(yf_ref[rows, :] + yd)
        gate = jax.nn.sigmoid(_dot(y.astype(BF16), wg_ref[...]) + bg_ref[...])
        o_ref[0] = (y * gate).reshape(tt, SUBLANES, GROUP_W)


def _s5_discretise(a_re, a_im, log_dt, b_re, b_im):
    dt = jnp.exp(log_dt)[..., None]
    mag = jnp.exp(dt * a_re)
    ab_re, ab_im = mag * jnp.cos(dt * a_im), mag * jnp.sin(dt * a_im)
    den = a_re * a_re + a_im * a_im
    nr = ab_re - 1.0
    f_re = (nr * a_re + ab_im * a_im) / den
    f_im = (ab_im * a_re - nr * a_im) / den
    bb_re = f_re[..., None] * b_re - f_im[..., None] * b_im
    bb_im = f_re[..., None] * b_im + f_im[..., None] * b_re
    return ab_re, ab_im, bb_re, bb_im


def _block_diag(blocks):
    G, r, c = blocks.shape
    eye = jnp.eye(G, dtype=blocks.dtype)
    return (eye[:, None, :, None] * blocks[:, :, None, :]).reshape(G * r, G * c)


def _s5_matrices(a_re, a_im, log_dt, b_re, b_im, c_re, c_im):
    ab_re, ab_im, bb_re, bb_im = _s5_discretise(
        a_re.astype(F32), a_im.astype(F32), log_dt.astype(F32), b_re.astype(F32), b_im.astype(F32))
    bmats, cmats = [], []
    for d in range(2):
        br = _block_diag(jnp.swapaxes(bb_re[d], 1, 2))
        bi = _block_diag(jnp.swapaxes(bb_im[d], 1, 2))
        bmats.append(jnp.concatenate([br, bi], axis=1))
        cr = _block_diag(jnp.swapaxes(c_re[d].astype(F32), 1, 2))
        ci = _block_diag(jnp.swapaxes(c_im[d].astype(F32), 1, 2))
        cmats.append(jnp.concatenate([cr, -ci], axis=0))
    abar = jnp.stack([ab_re.reshape(2, S5_COLS), ab_im.reshape(2, S5_COLS)], axis=1)
    return jnp.stack(bmats).astype(BF16), jnp.stack(cmats).astype(BF16), abar


def _s5(rest, bmat, cmat, abar, d_skip, w_glu, b_glu, ctx_len):
    B, S, _ = rest.shape
    assert B % SUBLANES == 0 and ctx_len % S5_TT == 0 and S % S5_TT == 0
    ng = B // SUBLANES
    n_tiles = S // S5_TT
    n_ctx_tiles = ctx_len // S5_TT
    u = rest[:, :, 8 * GROUP_W:9 * GROUP_W].reshape(ng, SUBLANES, S, GROUP_W).transpose(0, 2, 1, 3)

    def in_idx(g, d, i):
        return (g, _s5_time_tile(d, i, n_ctx_tiles, n_tiles), 0, 0)

    def out_idx(g, d, i):
        return (g, _s5_time_tile(1, jnp.where(d == 0, 0, i), n_ctx_tiles, n_tiles), 0, 0)

    out = pl.pallas_call(
        functools.partial(_s5_kernel, n_ctx_tiles, n_tiles),
        grid=(ng, 2, n_tiles),
        in_specs=[
            pl.BlockSpec((1, S5_TT, SUBLANES, GROUP_W), in_idx),
            pl.BlockSpec((1,) + bmat.shape[1:], lambda g, d, i: (d, 0, 0)),
            pl.BlockSpec((1,) + cmat.shape[1:], lambda g, d, i: (d, 0, 0)),
            pl.BlockSpec((1,) + abar.shape[1:], lambda g, d, i: (d, 0, 0)),
            pl.BlockSpec((1, GROUP_W), lambda g, d, i: (0, 0)),
            pl.BlockSpec((GROUP_W, GROUP_W), lambda g, d, i: (0, 0)),
            pl.BlockSpec((1, GROUP_W), lambda g, d, i: (0, 0)),
        ],
        out_specs=pl.BlockSpec((1, S5_TT, SUBLANES, GROUP_W), out_idx),
        out_shape=jax.ShapeDtypeStruct((ng, S, SUBLANES, GROUP_W), F32),
        scratch_shapes=[
            pltpu.VMEM((S5_TT * SUBLANES, 2 * S5_COLS), F32),
            pltpu.VMEM((S * SUBLANES, GROUP_W), F32),
            pltpu.VMEM((2 * S5_COLS // LANES, SUBLANES, LANES), F32),
        ],
        compiler_params=_params("parallel", "arbitrary", "arbitrary"),
        name="s5",
    )(u, bmat, cmat, abar, d_skip.astype(F32).reshape(1, GROUP_W), w_glu.astype(BF16),
      b_glu.astype(F32).reshape(1, GROUP_W))
    return out.transpose(0, 2, 1, 3).reshape(B, S, GROUP_W)


def _post_kernel(n_hidden_chunks, h_ref, a_ref, b_ref, c_ref, d_ref, mod_ref, ng_ref,
                 wo_ref, wi_ref, w2_ref, o_ref):
    h = h_ref[0]
    mix = (_dot(a_ref[0].astype(BF16), wo_ref[0:GROUP_W, :])
           + _dot(b_ref[0].astype(BF16), wo_ref[GROUP_W:2 * GROUP_W, :])
           + _dot(c_ref[0].astype(BF16), wo_ref[2 * GROUP_W:3 * GROUP_W, :])
           + _dot(d_ref[0].astype(BF16), wo_ref[3 * GROUP_W:4 * GROUP_W, :]))
    h = h + mod_ref[0, 2:3, :] * (mix * _rms_scale(mix) * ng_ref[1:2, :])
    y = (h * _rms_scale(h) * ng_ref[2:3, :]) * (1.0 + mod_ref[0, 4:5, :]) + mod_ref[0, 3:4, :]
    yb = y.astype(BF16)
    hidden = w2_ref.shape[0]
    hc = hidden // n_hidden_chunks
    ffn = jnp.zeros_like(h)
    for j in range(n_hidden_chunks):
        gate = _dot(yb, wi_ref[:, j * hc:(j + 1) * hc])
        up = _dot(yb, wi_ref[:, hidden + j * hc:hidden + (j + 1) * hc])
        ffn = ffn + _dot((_silu(gate) * up).astype(BF16), w2_ref[j * hc:(j + 1) * hc, :])
    o_ref[0] = h + mod_ref[0, 5:6, :] * (ffn * _rms_scale(ffn) * ng_ref[3:4, :])


def _post(h, a, b, c, d, mod, norm_g, w_out, w_ffn_in, w_ffn_out, n_batch, n_ctx_tiles, want_ctx):
    B, S, D = h.shape
    off = 0 if want_ctx else n_ctx_tiles
    nt = S // TILE - off
    a_off = off if a.shape[1] == S else 0
    hidden = w_ffn_out.shape[0]

    def mod_idx(bi, t):
        return (jnp.where(t + off < n_ctx_tiles, n_batch, bi), 0, 0)

    def rows(o):
        return lambda bi, t: (bi, t + o, 0)

    def whole(arr):
        return pl.BlockSpec(arr.shape, lambda bi, t: (0,) * arr.ndim, pipeline_mode=pl.Buffered(1))

    mix_spec = lambda o: pl.BlockSpec((1, TILE, GROUP_W), rows(o))
    return pl.pallas_call(
        functools.partial(_post_kernel, 2),
        grid=(B, nt),
        in_specs=[
            pl.BlockSpec((1, TILE, D), rows(off)),
            mix_spec(a_off), mix_spec(off), mix_spec(off), mix_spec(off),
            pl.BlockSpec((1, 6, D), mod_idx),
            whole(norm_g), whole(w_out), whole(w_ffn_in), whole(w_ffn_out),
        ],
        out_specs=pl.BlockSpec((1, TILE, D), lambda bi, t: (bi, t, 0)),
        out_shape=jax.ShapeDtypeStruct((B, nt * TILE, D), F32),
        compiler_params=_params("parallel", "arbitrary"),
        name="post",
    )(h, a, b, c, d, mod, norm_g, w_out, w_ffn_in, w_ffn_out)


def _rope_tables(n_rows, ctx_len):
    rows = jnp.broadcast_to(jnp.arange(n_rows, dtype=F32)[:, None], (n_rows, GRID_W)).reshape(-1)
    cols = jnp.broadcast_to(jnp.arange(GRID_W, dtype=F32)[None, :], (n_rows, GRID_W)).reshape(-1)
    n_freq = DA_HD // 4
    inv = ROPE_BASE ** (-jnp.arange(n_freq, dtype=F32) / n_freq)
    ang = jnp.concatenate([rows[:, None] * inv, cols[:, None] * inv], axis=-1)
    cos, sin = jnp.cos(ang), jnp.sin(ang)
    zero = jnp.zeros_like(sin)
    reps = GROUP_W // DA_HD

    def lanes(first, second, ctx_value):
        t = jnp.tile(jnp.concatenate([first, second], axis=-1), (1, reps))
        return jnp.concatenate([jnp.full((ctx_len, GROUP_W), ctx_value, F32), t], axis=0)

    return lanes(cos, cos, 1.0), lanes(-sin, zero, 0.0), lanes(zero, sin, 0.0)


def _deinterleave_qk(w_in):
    perm32 = jnp.concatenate([jnp.arange(0, DA_HD, 2), jnp.arange(1, DA_HD, 2)])
    perm = (jnp.arange(0, 2 * GROUP_W, DA_HD)[:, None] + perm32[None, :]).reshape(-1)
    cols = jnp.concatenate([perm, jnp.arange(2 * GROUP_W, w_in.shape[-1])])
    return w_in[..., cols]


def kernel(x, c, ctx, c_ctx, w_ada, b_ada, norm_g, w_in, w_out, da_lambda, da_subln, hg_lb, hg_norm, sc_w, sc_b, s5_a_re, s5_a_im, s5_log_dt, s5_b_re, s5_b_im, s5_c_re, s5_c_im, s5_d, s5_w_glu, s5_b_glu, w_ffn_in, w_ffn_out):
    B, T, D = x.shape
    ctx_len = ctx.shape[1]
    L = w_ada.shape[0]
    assert ctx_len % TILE == 0 and T % TILE == 0 and T % GRID_W == 0
    n_ctx_tiles = ctx_len // TILE

    bp = -(-(B + 1) // SUBLANES) * SUBLANES
    cvec = jnp.concatenate([c, c_ctx[None, :], jnp.zeros((bp - B - 1, D), c.dtype)], axis=0)
    mods = _modulation(cvec.astype(F32), w_ada, b_ada).reshape(L, bp, 6, D)

    cos, sin_a, sin_b = _rope_tables(T // GRID_W, ctx_len)
    lb = jnp.cumsum(jax.nn.softmax(hg_lb.astype(F32), axis=0), axis=0)
    lb = lb - lb[:1]
    w_in_b = _deinterleave_qk(w_in).astype(BF16)
    w_out_b = w_out.astype(BF16)
    w_ffn_in_b = w_ffn_in.astype(BF16)
    w_ffn_out_b = w_ffn_out.astype(BF16)

    h = jnp.concatenate([ctx, x], axis=1)
    for l in range(L):
        want_ctx = l < L - 1
        lam_init = 0.8 - 0.6 * math.exp(-0.3 * l)
        qkv, rest = _inproj(h, mods[l], norm_g[l, 0], w_in_b[l], cos, sin_a, sin_b, B, n_ctx_tiles)
        a = _attention(qkv, da_lambda[l], da_subln[l], lam_init, ctx_len, want_ctx)
        b = _hgrn(rest, lb[l], hg_norm[l])
        cc = _short_conv(rest, sc_w[l], sc_b[l], ctx_len)
        bmat, cmat, abar = _s5_matrices(s5_a_re[l], s5_a_im[l], s5_log_dt[l], s5_b_re[l],
                                        s5_b_im[l], s5_c_re[l], s5_c_im[l])
        dd = _s5(rest, bmat, cmat, abar, s5_d[l], s5_w_glu[l], s5_b_glu[l], ctx_len)
        h = _post(h, a, b, cc, dd, mods[l], norm_g[l].astype(F32), w_out_b[l], w_ffn_in_b[l],
                  w_ffn_out_b[l], B, n_ctx_tiles, want_ctx)
    return h
```

```python
import functools
import math

import jax
import jax.numpy as jnp
from jax import lax
from jax.experimental import pallas as pl
from jax.experimental.pallas import tpu as pltpu

F32 = jnp.float32
BF16 = jnp.bfloat16

EPS = 1e-6
GRID_W = 64
ROPE_BASE = 10000.0
GROUP_W = 256
DA_HEADS = 4
DA_HD = 32
HG_HEADS = 4
HG_HD = 64
HG_CHUNK = 16
S5_NGROUPS = 16
S5_GROUP = 16
S5_STATE = 64
S5_COLS = S5_NGROUPS * S5_STATE
LANES = 128
SUBLANES = 8
TILE = 256
ATT_ROWS = 16
INPROJ_NB = 4
POST_NB = 2
S5_TT = 128
VMEM_LIMIT = 56 * 1024 * 1024

_NT = (((1,), (1,)), ((), ()))


def _dot(a, b):
    return jnp.dot(a, b, preferred_element_type=F32)


def _dot_nt(a, b):
    return lax.dot_general(a, b, _NT, preferred_element_type=F32)


def _split3_dot(m, x):
    x1 = x.astype(BF16)
    r1 = x - x1.astype(F32)
    x2 = r1.astype(BF16)
    x3 = (r1 - x2.astype(F32)).astype(BF16)
    return _dot(m, x1) + _dot(m, x2) + _dot(m, x3)


def _split3_dot_lhs(x, m):
    x1 = x.astype(BF16)
    r1 = x - x1.astype(F32)
    x2 = r1.astype(BF16)
    x3 = (r1 - x2.astype(F32)).astype(BF16)
    return _dot(x1, m) + _dot(x2, m) + _dot(x3, m)


def _split2_dot(x, m):
    x1 = x.astype(BF16)
    x2 = (x - x1.astype(F32)).astype(BF16)
    return _dot(x1, m) + _dot(x2, m)


def _silu(x):
    return x * jax.nn.sigmoid(x)


def _rms_scale(x):
    return lax.rsqrt(jnp.mean(x * x, axis=-1, keepdims=True) + EPS)


def _group_mean_matrix(n, group):
    sh = int(math.log2(group))
    r = lax.broadcasted_iota(jnp.int32, (n, n), 0) >> sh
    c = lax.broadcasted_iota(jnp.int32, (n, n), 1) >> sh
    return jnp.where(r == c, 1.0 / group, 0.0).astype(BF16)


def _params(*sem):
    return pltpu.CompilerParams(dimension_semantics=sem, vmem_limit_bytes=VMEM_LIMIT)


def _mod_kernel(c_ref, w_ref, b_ref, o_ref):
    sc = _silu(c_ref[...])
    o_ref[0] = _dot(sc.astype(BF16), w_ref[0].astype(BF16)) + b_ref[0]


def _modulation(cvec, w_ada, b_ada):
    L, D, N = w_ada.shape
    Bp = cvec.shape[0]
    tn = 1536
    return pl.pallas_call(
        _mod_kernel,
        grid=(L, N // tn),
        in_specs=[
            pl.BlockSpec((Bp, D), lambda l, j: (0, 0)),
            pl.BlockSpec((1, D, tn), lambda l, j: (l, 0, j)),
            pl.BlockSpec((1, 1, tn), lambda l, j: (l, 0, j)),
        ],
        out_specs=pl.BlockSpec((1, Bp, tn), lambda l, j: (l, 0, j)),
        out_shape=jax.ShapeDtypeStruct((L, Bp, N), F32),
        compiler_params=_params("arbitrary", "arbitrary"),
        name="modulation",
    )(cvec, w_ada, b_ada.reshape(L, 1, N))


def _rope(x, cos, sin_a, sin_b):
    return (x * cos + pltpu.roll(x, GROUP_W - DA_HD // 2, 1) * sin_a
            + pltpu.roll(x, DA_HD // 2, 1) * sin_b)


def _tile_mod(is_ctx, modx_ref, modc_ref):
    return jnp.where(is_ctx, modc_ref[...], modx_ref[...])


def _inproj_kernel(n_ctx_tiles, h_ref, modx_ref, modc_ref, g_ref, w_ref, cos_ref, sa_ref, sb_ref,
                   qkv_ref, rest_ref):
    nb, tile, d_model = h_ref.shape
    mod = _tile_mod(pl.program_id(1) < n_ctx_tiles, modx_ref, modc_ref)
    h = h_ref[...]
    hn = h * _rms_scale(h) * g_ref[...]
    y = hn * (1.0 + mod[:, 1:2, :]) + mod[:, 0:1, :]
    p = _dot(y.reshape(nb * tile, d_model).astype(BF16), w_ref[...])
    cos, sa, sb = cos_ref[...], sa_ref[...], sb_ref[...]
    for j in range(nb):
        pj = p[j * tile:(j + 1) * tile]
        q = _rope(pj[:, 0:GROUP_W], cos, sa, sb) * (DA_HD ** -0.5 * math.log2(math.e))
        k = _rope(pj[:, GROUP_W:2 * GROUP_W], cos, sa, sb)
        qkv_ref[j, :, 0:GROUP_W] = q.astype(BF16)
        qkv_ref[j, :, GROUP_W:2 * GROUP_W] = k.astype(BF16)
        qkv_ref[j, :, 2 * GROUP_W:3 * GROUP_W] = pj[:, 2 * GROUP_W:3 * GROUP_W].astype(BF16)
        rest_ref[j] = pj[:, 3 * GROUP_W:]


def _inproj(h, mod, g, w, cos, sa, sb, n_batch, n_ctx_tiles):
    B, S, D = h.shape
    N = w.shape[1]
    nt = S // TILE
    nb = INPROJ_NB
    assert B % nb == 0
    tab = pl.BlockSpec((TILE, GROUP_W), lambda b, t: (t, 0))
    return pl.pallas_call(
        functools.partial(_inproj_kernel, n_ctx_tiles),
        grid=(B // nb, nt),
        in_specs=[
            pl.BlockSpec((nb, TILE, D), lambda b, t: (b, t, 0)),
            pl.BlockSpec((nb, 6, D), lambda b, t: (b, 0, 0)),
            pl.BlockSpec((1, 6, D), lambda b, t: (n_batch, 0, 0)),
            pl.BlockSpec((1, D), lambda b, t: (0, 0)),
            pl.BlockSpec((D, N), lambda b, t: (0, 0)),
            tab, tab, tab,
        ],
        out_specs=[
            pl.BlockSpec((nb, TILE, 3 * GROUP_W), lambda b, t: (b, t, 0)),
            pl.BlockSpec((nb, TILE, N - 3 * GROUP_W), lambda b, t: (b, t, 0)),
        ],
        out_shape=[
            jax.ShapeDtypeStruct((B, S, 3 * GROUP_W), BF16),
            jax.ShapeDtypeStruct((B, S, N - 3 * GROUP_W), F32),
        ],
        compiler_params=_params("parallel", "arbitrary"),
        name="inproj",
    )(h, mod, mod, g.reshape(1, D), w, cos, sa, sb)


def _attn_kernel(lam_init, ctx_len, q_off, q_ref, k_ref, v_ref, lamv_ref, g_ref, o_ref,
                 *scratch):
    sc_refs, a_refs = scratch[:DA_HEADS], scratch[DA_HEADS:]
    tq = q_ref.shape[1]
    n_keys = k_ref.shape[1]
    lv = lamv_ref[...]
    lam = (jnp.exp(jnp.sum(lv[0:1] * lv[1:2], axis=-1, keepdims=True))
           - jnp.exp(jnp.sum(lv[2:3] * lv[3:4], axis=-1, keepdims=True)) + lam_init)
    lane = lax.broadcasted_iota(jnp.int32, (1, GROUP_W), 1)
    lane_map = lane >> 5
    lane_head = lane >> 6
    q = q_ref[0]
    q8 = jnp.concatenate(
        [jnp.where(lane_map == j, q, jnp.zeros_like(q)) for j in range(2 * DA_HEADS)], axis=0)

    def attend(nk):
        n_blocks = tq // ATT_ROWS
        key_tiles = [(c, min(c + TILE, nk)) for c in range(0, nk, TILE)]
        outs = [None] * DA_HEADS

        def scores(h):
            return [functools.partial(score_tile, h, c0, c1) for c0, c1 in key_tiles]

        def score_tile(h, c0, c1):
            sc_refs[h][:, c0:c1] = _dot_nt(q8[2 * h * tq:(2 * h + 2) * tq], k_ref[0, c0:c1, :])

        def weighted_values(h):
            outs[h] = _dot(a_refs[h][:, 0:nk], v_ref[0, 0:nk, :])

        for job in scores(0):
            job()
        for h in range(DA_HEADS):
            jobs = scores(h + 1) if h + 1 < DA_HEADS else []
            if h > 0:
                jobs.append(functools.partial(weighted_values, h - 1))
            issued = 0
            for rb in range(n_blocks):
                terms = []
                for m in range(2):
                    r0 = m * tq + rb * ATT_ROWS
                    s = sc_refs[h][r0:r0 + ATT_ROWS, 0:nk]
                    e = jnp.exp2(s - jnp.max(s, axis=-1, keepdims=True))
                    w = (1.0 if m == 0 else lam) / jnp.sum(e, axis=-1, keepdims=True)
                    terms.append(e * w)
                a_refs[h][rb * ATT_ROWS:(rb + 1) * ATT_ROWS, 0:nk] = (terms[0] - terms[1]).astype(BF16)
                while issued < (rb + 1) * len(jobs) // n_blocks:
                    jobs[issued]()
                    issued += 1
        weighted_values(DA_HEADS - 1)
        acc = jnp.zeros((tq, GROUP_W), F32)
        for h in range(DA_HEADS):
            acc = acc + jnp.where(lane_head == h, outs[h], 0.0)
        ms = _split2_dot(acc * acc, _group_mean_matrix(GROUP_W, 2 * DA_HD))
        o_ref[0] = acc * lax.rsqrt(ms + EPS) * g_ref[...] * (1.0 - lam_init)

    if q_off == 0 and ctx_len == tq:
        @pl.when(pl.program_id(1) == 0)
        def _():
            attend(ctx_len)

        @pl.when(pl.program_id(1) != 0)
        def _():
            attend(n_keys)
    else:
        attend(n_keys)


def _attention(qkv, lam_vecs, subln_g, lam_init, ctx_len, want_ctx):
    B, S, _ = qkv.shape
    tq = TILE
    q_off = 0 if want_ctx else ctx_len // tq
    nq = S // tq - q_off
    g = jnp.tile(subln_g.astype(F32), DA_HEADS).reshape(1, GROUP_W)
    return pl.pallas_call(
        functools.partial(_attn_kernel, lam_init, ctx_len, q_off),
        grid=(B, nq),
        in_specs=[
            pl.BlockSpec((1, tq, GROUP_W), lambda b, i: (b, i + q_off, 0)),
            pl.BlockSpec((1, S, GROUP_W), lambda b, i: (b, 0, 1)),
            pl.BlockSpec((1, S, GROUP_W), lambda b, i: (b, 0, 2)),
            pl.BlockSpec((4, DA_HD), lambda b, i: (0, 0)),
            pl.BlockSpec((1, GROUP_W), lambda b, i: (0, 0)),
        ],
        out_specs=pl.BlockSpec((1, tq, GROUP_W), lambda b, i: (b, i, 0)),
        out_shape=jax.ShapeDtypeStruct((B, nq * tq, GROUP_W), F32),
        scratch_shapes=(
            [pltpu.VMEM((2 * tq, S), F32) for _ in range(DA_HEADS)]
            + [pltpu.VMEM((tq, S), BF16) for _ in range(DA_HEADS)]),
        compiler_params=_params("parallel", "arbitrary"),
        name="diff_attention",
    )(qkv, qkv, qkv, lam_vecs.astype(F32), g)


def _hgrn_kernel(n_tiles, q_ref, ff_ref, fb_ref, i_ref, g_ref, lb_ref, lbt_ref, ng_ref, o_ref,
                 od_ref, st2_ref, u2_ref, oc2_ref):
    n_chunks = TILE // HG_CHUNK
    row = lax.broadcasted_iota(jnp.int32, (TILE, TILE), 0)
    col = lax.broadcasted_iota(jnp.int32, (TILE, TILE), 1)
    same_chunk = (row >> 4) == (col >> 4)
    ones_blk = jnp.where(same_chunk, 1.0, 0.0).astype(BF16)
    lane = lax.broadcasted_iota(jnp.int32, (1, GROUP_W), 1)
    lane_head = lane >> 6
    lane_chunk = lane >> 4
    gmean = _group_mean_matrix(GROUP_W, HG_HD)

    st2_ref[...] = jnp.zeros_like(st2_ref)

    def tile_body(ti, carry):
        for d in range(2):
            causal = same_chunk & ((col <= row) if d == 0 else (col >= row))
            tri = jnp.where(causal, 1.0, 0.0).astype(BF16)
            tri_t = jnp.where(same_chunk & ((row <= col) if d == 0 else (row >= col)),
                              1.0, 0.0).astype(BF16)
            f_ref = ff_ref if d == 0 else fb_ref
            lb = lb_ref[d:d + 1, :]
            lb_t = lbt_ref[:, d:d + 1]
            st_ref, u_ref, oc_ref = st2_ref.at[d], u2_ref.at[d], oc2_ref.at[d]
            if d == 0:
                t = ti
            else:
                t = jnp.where(ti == 0, 0, n_tiles - ti)
            rows = pl.ds(pl.multiple_of(t * TILE, TILE), TILE)
            fr = f_ref[0, rows, :]
            lf = jnp.log(lb + (1.0 - lb) * jax.nn.sigmoid(fr))
            G = _split3_dot(tri, lf)
            qd = _silu(q_ref[0, rows, :]) * jnp.exp(G)
            f_t = lb_t + (1.0 - lb_t) * jax.nn.sigmoid(fr.T)
            kk_t = 1.0 - f_t
            lf_t = jnp.log(f_t)
            g_t = _split3_dot_lhs(lf_t, tri_t)
            gt_t = _split3_dot_lhs(lf_t, ones_blk)
            kinv_t = (kk_t * jnp.exp(-g_t)).astype(BF16)
            kend_t = kk_t * jnp.exp(gt_t - g_t)
            dec_t = jnp.exp(gt_t)
            vb = i_ref[0, rows, :].astype(BF16)
            zero = jnp.zeros_like(qd)

            q4 = jnp.concatenate(
                [jnp.where(lane_head == h, qd, zero).astype(BF16) for h in range(HG_HEADS)], axis=0)
            a4 = _dot(q4, kinv_t)
            a_cat = jnp.concatenate(
                [jnp.where(causal, a4[h * TILE:(h + 1) * TILE], 0.0).astype(BF16)
                 for h in range(HG_HEADS)], axis=1)
            zb = jnp.zeros_like(vb)
            v4 = jnp.concatenate(
                [jnp.where(lane_head == h, vb, zb) for h in range(HG_HEADS)], axis=0)
            o_tile = _dot(a_cat, v4)

            k_stack = jnp.concatenate(
                [jnp.where(lane_chunk == c, kend_t, zero).astype(BF16) for c in range(n_chunks)], axis=0)
            u_ref[...] = _dot(k_stack, vb)

            order = range(n_chunks) if d == 0 else range(n_chunks - 1, -1, -1)
            for c in order:
                st = st_ref[...]
                q4c = jnp.concatenate(
                    [q4[h * TILE + c * HG_CHUNK:h * TILE + (c + 1) * HG_CHUNK] for h in range(HG_HEADS)],
                    axis=0)
                oc4 = _dot(q4c, st.astype(BF16))
                oc = jnp.zeros((HG_CHUNK, GROUP_W), F32)
                for h in range(HG_HEADS):
                    oc = oc + jnp.where(lane_head == h, oc4[h * HG_CHUNK:(h + 1) * HG_CHUNK], 0.0)
                oc_ref[c * HG_CHUNK:(c + 1) * HG_CHUNK, :] = oc
                st_ref[...] = (st * dec_t[:, c * HG_CHUNK:c * HG_CHUNK + 1]
                               + u_ref[c * TILE:(c + 1) * TILE, :])
            od_ref[d, rows, :] = o_tile + oc_ref[...]
        return carry

    lax.fori_loop(0, n_tiles, tile_body, 0)

    def finish_body(t, carry):
        rows = pl.ds(pl.multiple_of(t * TILE, TILE), TILE)
        tot = od_ref[0, rows, :] + od_ref[1, rows, :]
        ms = _split2_dot(tot * tot, gmean)
        o_ref[0, rows, :] = (tot * lax.rsqrt(ms + EPS) * ng_ref[...] * _silu(g_ref[0, rows, :]))
        return carry

    lax.fori_loop(0, n_tiles, finish_body, 0)


def _hgrn(rest, lb, norm_g):
    B, S, _ = rest.shape
    n_tiles = S // TILE

    def part(j):
        return pl.BlockSpec((1, S, GROUP_W), lambda b: (b, 0, j))

    ng = jnp.tile(norm_g.astype(F32), HG_HEADS).reshape(1, GROUP_W)
    return pl.pallas_call(
        functools.partial(_hgrn_kernel, n_tiles),
        grid=(B,),
        in_specs=[part(0), part(1), part(2), part(3), part(4),
                  pl.BlockSpec((2, GROUP_W), lambda b: (0, 0)),
                  pl.BlockSpec((GROUP_W, 2), lambda b: (0, 0)),
                  pl.BlockSpec((1, GROUP_W), lambda b: (0, 0))],
        out_specs=pl.BlockSpec((1, S, GROUP_W), lambda b: (b, 0, 0)),
        out_shape=jax.ShapeDtypeStruct((B, S, GROUP_W), F32),
        scratch_shapes=[
            pltpu.VMEM((2, S, GROUP_W), F32),
            pltpu.VMEM((2, GROUP_W, GROUP_W), F32),
            pltpu.VMEM((2, TILE // HG_CHUNK * TILE, GROUP_W), F32),
            pltpu.VMEM((2, TILE, GROUP_W), F32),
        ],
        compiler_params=_params("parallel"),
        name="hgrn2",
    )(rest, rest, rest, rest, rest, lb, lb.T, ng)


def _conv_kernel(ctx_len, b_ref, c_ref, u_ref, w_ref, bias_ref, o_ref):
    S = b_ref.shape[1]
    v = c_ref[0] * u_ref[0]
    row = lax.broadcasted_iota(jnp.int32, (S, 1), 0)
    prev = jnp.where((row == 0) | (row == ctx_len), 0.0, pltpu.roll(v, 1, 0))
    nxt = jnp.where((row == ctx_len - 1) | (row == S - 1), 0.0, pltpu.roll(v, S - 1, 0))
    y = w_ref[0:1, :] * prev + w_ref[1:2, :] * v + w_ref[2:3, :] * nxt
    o_ref[0] = b_ref[0] * (y + bias_ref[...])


def _short_conv(rest, w, bias, ctx_len):
    B, S, _ = rest.shape

    def part(j):
        return pl.BlockSpec((1, S, GROUP_W), lambda b: (b, 0, j))

    return pl.pallas_call(
        functools.partial(_conv_kernel, ctx_len),
        grid=(B,),
        in_specs=[part(5), part(6), part(7),
                  pl.BlockSpec((3, GROUP_W), lambda b: (0, 0)),
                  pl.BlockSpec((1, GROUP_W), lambda b: (0, 0))],
        out_specs=pl.BlockSpec((1, S, GROUP_W), lambda b: (b, 0, 0)),
        out_shape=jax.ShapeDtypeStruct((B, S, GROUP_W), F32),
        compiler_params=_params("parallel"),
        name="short_conv",
    )(rest, rest, rest, w.astype(F32), bias.astype(F32).reshape(1, GROUP_W))


def _s5_time_tile(d, i, n_ctx_tiles, n_tiles):
    back = jnp.where(i < n_ctx_tiles, n_ctx_tiles - 1 - i, n_tiles - 1 - (i - n_ctx_tiles))
    return jnp.where(d == 0, i, back)


def _s5_kernel(n_ctx_tiles, n_tiles, u_ref, bmat_ref, cmat_ref, a_ref, d_ref, wg_ref, bg_ref,
               o_ref, xs_ref, yf_ref, st_ref):
    d = pl.program_id(1)
    i = pl.program_id(2)
    tt = u_ref.shape[1]
    n_rows = tt * SUBLANES
    n_strips = S5_COLS // LANES

    @pl.when(i == 0)
    def _():
        st_ref[...] = jnp.zeros_like(st_ref)

    u = u_ref[0].reshape(n_rows, GROUP_W)
    xs_ref[...] = _dot(u.astype(BF16), bmat_ref[0])

    def lanes(j):
        return slice(j * LANES, (j + 1) * LANES)

    a_re = [jnp.broadcast_to(a_ref[0, 0:1, lanes(j)], (SUBLANES, LANES)) for j in range(n_strips)]
    a_im = [jnp.broadcast_to(a_ref[0, 1:2, lanes(j)], (SUBLANES, LANES)) for j in range(n_strips)]

    def step(s, carry):
        t = jnp.where(d == 0, s, tt - 1 - s)
        r = pl.ds(pl.multiple_of(t * SUBLANES, SUBLANES), SUBLANES)
        new = []
        for j in range(n_strips):
            xr, xi = carry[2 * j], carry[2 * j + 1]
            nr = a_re[j] * xr - a_im[j] * xi + xs_ref[r, lanes(j)]
            ni = a_re[j] * xi + a_im[j] * xr + xs_ref[r, lanes(n_strips + j)]
            xs_ref[r, lanes(j)] = nr
            xs_ref[r, lanes(n_strips + j)] = ni
            new += [nr, ni]
        return tuple(new)

    state = lax.fori_loop(0, tt, step, tuple(st_ref[k] for k in range(2 * n_strips)), unroll=2)
    for k in range(2 * n_strips):
        st_ref[k] = state[k]

    yd = _dot(xs_ref[...].astype(BF16), cmat_ref[0])
    t_idx = _s5_time_tile(d, i, n_ctx_tiles, n_tiles)
    rows = pl.ds(pl.multiple_of(t_idx * n_rows, n_rows), n_rows)

    @pl.when(d == 0)
    def _():
        yf_ref[rows, :] = d_ref[...] * u + yd

    @pl.when(d == 1)
    def _():
        y = jax.nn.gelu(yf_ref[rows, :] + yd)
        gate = jax.nn.sigmoid(_dot(y.astype(BF16), wg_ref[...]) + bg_ref[...])
        o_ref[0] = (y * gate).reshape(tt, SUBLANES, GROUP_W)


def _s5_discretise(a_re, a_im, log_dt, b_re, b_im):
    dt = jnp.exp(log_dt)[..., None]
    mag = jnp.exp(dt * a_re)
    ab_re, ab_im = mag * jnp.cos(dt * a_im), mag * jnp.sin(dt * a_im)
    den = a_re * a_re + a_im * a_im
    nr = ab_re - 1.0
    f_re = (nr * a_re + ab_im * a_im) / den
    f_im = (ab_im * a_re - nr * a_im) / den
    bb_re = f_re[..., None] * b_re - f_im[..., None] * b_im
    bb_im = f_re[..., None] * b_im + f_im[..., None] * b_re
    return ab_re, ab_im, bb_re, bb_im


def _block_diag(blocks):
    G, r, c = blocks.shape
    eye = jnp.eye(G, dtype=blocks.dtype)
    return (eye[:, None, :, None] * blocks[:, :, None, :]).reshape(G * r, G * c)


def _s5_matrices(a_re, a_im, log_dt, b_re, b_im, c_re, c_im):
    ab_re, ab_im, bb_re, bb_im = _s5_discretise(
        a_re.astype(F32), a_im.astype(F32), log_dt.astype(F32), b_re.astype(F32), b_im.astype(F32))
    bmats, cmats = [], []
    for d in range(2):
        br = _block_diag(jnp.swapaxes(bb_re[d], 1, 2))
        bi = _block_diag(jnp.swapaxes(bb_im[d], 1, 2))
        bmats.append(jnp.concatenate([br, bi], axis=1))
        cr = _block_diag(jnp.swapaxes(c_re[d].astype(F32), 1, 2))
        ci = _block_diag(jnp.swapaxes(c_im[d].astype(F32), 1, 2))
        cmats.append(jnp.concatenate([cr, -ci], axis=0))
    abar = jnp.stack([ab_re.reshape(2, S5_COLS), ab_im.reshape(2, S5_COLS)], axis=1)
    return jnp.stack(bmats).astype(BF16), jnp.stack(cmats).astype(BF16), abar


def _s5(rest, bmat, cmat, abar, d_skip, w_glu, b_glu, ctx_len):
    B, S, _ = rest.shape
    assert B % SUBLANES == 0 and ctx_len % S5_TT == 0 and S % S5_TT == 0
    ng = B // SUBLANES
    n_tiles = S // S5_TT
    n_ctx_tiles = ctx_len // S5_TT
    u = rest[:, :, 8 * GROUP_W:9 * GROUP_W].reshape(ng, SUBLANES, S, GROUP_W).transpose(0, 2, 1, 3)

    def in_idx(g, d, i):
        return (g, _s5_time_tile(d, i, n_ctx_tiles, n_tiles), 0, 0)

    def out_idx(g, d, i):
        return (g, _s5_time_tile(1, jnp.where(d == 0, 0, i), n_ctx_tiles, n_tiles), 0, 0)

    out = pl.pallas_call(
        functools.partial(_s5_kernel, n_ctx_tiles, n_tiles),
        grid=(ng, 2, n_tiles),
        in_specs=[
            pl.BlockSpec((1, S5_TT, SUBLANES, GROUP_W), in_idx),
            pl.BlockSpec((1,) + bmat.shape[1:], lambda g, d, i: (d, 0, 0)),
            pl.BlockSpec((1,) + cmat.shape[1:], lambda g, d, i: (d, 0, 0)),
            pl.BlockSpec((1,) + abar.shape[1:], lambda g, d, i: (d, 0, 0)),
            pl.BlockSpec((1, GROUP_W), lambda g, d, i: (0, 0)),
            pl.BlockSpec((GROUP_W, GROUP_W), lambda g, d, i: (0, 0)),
            pl.BlockSpec((1, GROUP_W), lambda g, d, i: (0, 0)),
        ],
        out_specs=pl.BlockSpec((1, S5_TT, SUBLANES, GROUP_W), out_idx),
        out_shape=jax.ShapeDtypeStruct((ng, S, SUBLANES, GROUP_W), F32),
        scratch_shapes=[
            pltpu.VMEM((S5_TT * SUBLANES, 2 * S5_COLS), F32),
            pltpu.VMEM((S * SUBLANES, GROUP_W), F32),
            pltpu.VMEM((2 * S5_COLS // LANES, SUBLANES, LANES), F32),
        ],
        compiler_params=_params("parallel", "arbitrary", "arbitrary"),
        name="s5",
    )(u, bmat, cmat, abar, d_skip.astype(F32).reshape(1, GROUP_W), w_glu.astype(BF16),
      b_glu.astype(F32).reshape(1, GROUP_W))
    return out.transpose(0, 2, 1, 3).reshape(B, S, GROUP_W)


def _post_kernel(n_hidden_chunks, ctx_tiles_here, h_ref, a_ref, b_ref, c_ref, d_ref, modx_ref,
                 modc_ref, ng_ref, wo_ref, wi_ref, w2_ref, o_ref):
    nb, tile, d_model = h_ref.shape
    n_rows = nb * tile
    mod = _tile_mod(pl.program_id(1) < ctx_tiles_here, modx_ref, modc_ref)

    def flat(ref):
        return ref[...].reshape(n_rows, GROUP_W).astype(BF16)

    mix = (_dot(flat(a_ref), wo_ref[0:GROUP_W, :])
           + _dot(flat(b_ref), wo_ref[GROUP_W:2 * GROUP_W, :])
           + _dot(flat(c_ref), wo_ref[2 * GROUP_W:3 * GROUP_W, :])
           + _dot(flat(d_ref), wo_ref[3 * GROUP_W:4 * GROUP_W, :])).reshape(nb, tile, d_model)
    h = h_ref[...] + mod[:, 2:3, :] * (mix * _rms_scale(mix) * ng_ref[1:2, :])
    y = (h * _rms_scale(h) * ng_ref[2:3, :]) * (1.0 + mod[:, 4:5, :]) + mod[:, 3:4, :]
    yb = y.reshape(n_rows, d_model).astype(BF16)
    hidden = w2_ref.shape[0]
    hc = hidden // n_hidden_chunks
    ffn = jnp.zeros((n_rows, d_model), F32)
    for j in range(n_hidden_chunks):
        gate = _dot(yb, wi_ref[:, j * hc:(j + 1) * hc])
        up = _dot(yb, wi_ref[:, hidden + j * hc:hidden + (j + 1) * hc])
        ffn = ffn + _dot((_silu(gate) * up).astype(BF16), w2_ref[j * hc:(j + 1) * hc, :])
    ffn = ffn.reshape(nb, tile, d_model)
    o_ref[...] = h + mod[:, 5:6, :] * (ffn * _rms_scale(ffn) * ng_ref[3:4, :])


def _post(h, a, b, c, d, mod, norm_g, w_out, w_ffn_in, w_ffn_out, n_batch, n_ctx_tiles, want_ctx):
    B, S, D = h.shape
    off = 0 if want_ctx else n_ctx_tiles
    nt = S // TILE - off
    a_off = off if a.shape[1] == S else 0
    nb = POST_NB
    assert B % nb == 0

    def rows(o):
        return lambda bi, t: (bi, t + o, 0)

    def whole(arr):
        return pl.BlockSpec(arr.shape, lambda bi, t: (0,) * arr.ndim, pipeline_mode=pl.Buffered(1))

    mix_spec = lambda o: pl.BlockSpec((nb, TILE, GROUP_W), rows(o))
    return pl.pallas_call(
        functools.partial(_post_kernel, 2, n_ctx_tiles - off),
        grid=(B // nb, nt),
        in_specs=[
            pl.BlockSpec((nb, TILE, D), rows(off)),
            mix_spec(a_off), mix_spec(off), mix_spec(off), mix_spec(off),
            pl.BlockSpec((nb, 6, D), lambda bi, t: (bi, 0, 0)),
            pl.BlockSpec((1, 6, D), lambda bi, t: (n_batch, 0, 0)),
            whole(norm_g), whole(w_out), whole(w_ffn_in), whole(w_ffn_out),
        ],
        out_specs=pl.BlockSpec((nb, TILE, D), lambda bi, t: (bi, t, 0)),
        out_shape=jax.ShapeDtypeStruct((B, nt * TILE, D), F32),
        compiler_params=_params("parallel", "arbitrary"),
        name="post",
    )(h, a, b, c, d, mod, mod, norm_g, w_out, w_ffn_in, w_ffn_out)


def _rope_tables(n_rows, ctx_len):
    rows = jnp.broadcast_to(jnp.arange(n_rows, dtype=F32)[:, None], (n_rows, GRID_W)).reshape(-1)
    cols = jnp.broadcast_to(jnp.arange(GRID_W, dtype=F32)[None, :], (n_rows, GRID_W)).reshape(-1)
    n_freq = DA_HD // 4
    inv = ROPE_BASE ** (-jnp.arange(n_freq, dtype=F32) / n_freq)
    ang = jnp.concatenate([rows[:, None] * inv, cols[:, None] * inv], axis=-1)
    cos, sin = jnp.cos(ang), jnp.sin(ang)
    zero = jnp.zeros_like(sin)
    reps = GROUP_W // DA_HD

    def lanes(first, second, ctx_value):
        t = jnp.tile(jnp.concatenate([first, second], axis=-1), (1, reps))
        return jnp.concatenate([jnp.full((ctx_len, GROUP_W), ctx_value, F32), t], axis=0)

    return lanes(cos, cos, 1.0), lanes(-sin, zero, 0.0), lanes(zero, sin, 0.0)


def _deinterleave_qk(w_in):
    perm32 = jnp.concatenate([jnp.arange(0, DA_HD, 2), jnp.arange(1, DA_HD, 2)])
    perm = (jnp.arange(0, 2 * GROUP_W, DA_HD)[:, None] + perm32[None, :]).reshape(-1)
    cols = jnp.concatenate([perm, jnp.arange(2 * GROUP_W, w_in.shape[-1])])
    return w_in[..., cols]


def kernel(x, c, ctx, c_ctx, w_ada, b_ada, norm_g, w_in, w_out, da_lambda, da_subln, hg_lb, hg_norm, sc_w, sc_b, s5_a_re, s5_a_im, s5_log_dt, s5_b_re, s5_b_im, s5_c_re, s5_c_im, s5_d, s5_w_glu, s5_b_glu, w_ffn_in, w_ffn_out):
    B, T, D = x.shape
    ctx_len = ctx.shape[1]
    L = w_ada.shape[0]
    assert ctx_len % TILE == 0 and T % TILE == 0 and T % GRID_W == 0
    n_ctx_tiles = ctx_len // TILE

    bp = -(-(B + 1) // SUBLANES) * SUBLANES
    cvec = jnp.concatenate([c, c_ctx[None, :], jnp.zeros((bp - B - 1, D), c.dtype)], axis=0)
    mods = _modulation(cvec.astype(F32), w_ada, b_ada).reshape(L, bp, 6, D)

    cos, sin_a, sin_b = _rope_tables(T // GRID_W, ctx_len)
    lb = jnp.cumsum(jax.nn.softmax(hg_lb.astype(F32), axis=0), axis=0)
    lb = lb - lb[:1]
    w_in_b = _deinterleave_qk(w_in).astype(BF16)
    w_out_b = w_out.astype(BF16)
    w_ffn_in_b = w_ffn_in.astype(BF16)
    w_ffn_out_b = w_ffn_out.astype(BF16)

    h = jnp.concatenate([ctx, x], axis=1)
    for l in range(L):
        want_ctx = l < L - 1
        lam_init = 0.8 - 0.6 * math.exp(-0.3 * l)
        qkv, rest = _inproj(h, mods[l], norm_g[l, 0], w_in_b[l], cos, sin_a, sin_b, B, n_ctx_tiles)
        a = _attention(qkv, da_lambda[l], da_subln[l], lam_init, ctx_len, want_ctx)
        b = _hgrn(rest, lb[l], hg_norm[l])
        cc = _short_conv(rest, sc_w[l], sc_b[l], ctx_len)
        bmat, cmat, abar = _s5_matrices(s5_a_re[l], s5_a_im[l], s5_log_dt[l], s5_b_re[l],
                                        s5_b_im[l], s5_c_re[l], s5_c_im[l])
        dd = _s5(rest, bmat, cmat, abar, s5_d[l], s5_w_glu[l], s5_b_glu[l], ctx_len)
        h = _post(h, a, b, cc, dd, mods[l], norm_g[l].astype(F32), w_out_b[l], w_ffn_in_b[l],
                  w_ffn_out_b[l], B, n_ctx_tiles, want_ctx)
    return h
```

```python
import functools
import math

import jax
import jax.numpy as jnp
from jax import lax
from jax.experimental import pallas as pl
from jax.experimental.pallas import tpu as pltpu

F32 = jnp.float32
BF16 = jnp.bfloat16

EPS = 1e-6
GRID_W = 64
ROPE_BASE = 10000.0
GROUP_W = 256
DA_HEADS = 4
DA_HD = 32
HG_HEADS = 4
HG_HD = 64
HG_CHUNK = 16
HG_BLOCK = 64
S5_NGROUPS = 16
S5_GROUP = 16
S5_STATE = 64
S5_COLS = S5_NGROUPS * S5_STATE
LANES = 128
SUBLANES = 8
TILE = 256
INPROJ_NB = 4
POST_NB = 2
S5_TT = 128
VMEM_LIMIT = 56 * 1024 * 1024

_NT = (((1,), (1,)), ((), ()))


def _dot(a, b):
    return jnp.dot(a, b, preferred_element_type=F32)


def _dot_nt(a, b):
    return lax.dot_general(a, b, _NT, preferred_element_type=F32)


def _split2_dot_rhs(m, x):
    x1 = x.astype(BF16)
    x2 = (x - x1.astype(F32)).astype(BF16)
    return _dot(m, x1) + _dot(m, x2)


def _split2_dot(x, m):
    x1 = x.astype(BF16)
    x2 = (x - x1.astype(F32)).astype(BF16)
    return _dot(x1, m) + _dot(x2, m)


def _silu(x):
    return x * jax.nn.sigmoid(x)


def _rms_scale(x):
    return lax.rsqrt(jnp.mean(x * x, axis=-1, keepdims=True) + EPS)


def _group_mean_matrix(n, group):
    sh = int(math.log2(group))
    r = lax.broadcasted_iota(jnp.int32, (n, n), 0) >> sh
    c = lax.broadcasted_iota(jnp.int32, (n, n), 1) >> sh
    return jnp.where(r == c, 1.0 / group, 0.0).astype(BF16)


def _params(*sem):
    return pltpu.CompilerParams(dimension_semantics=sem, vmem_limit_bytes=VMEM_LIMIT)


def _mod_kernel(c_ref, w_ref, b_ref, o_ref):
    sc = _silu(c_ref[...])
    o_ref[0] = _dot(sc.astype(BF16), w_ref[0].astype(BF16)) + b_ref[0]


def _modulation(cvec, w_ada, b_ada):
    L, D, N = w_ada.shape
    Bp = cvec.shape[0]
    tn = 1536
    return pl.pallas_call(
        _mod_kernel,
        grid=(L, N // tn),
        in_specs=[
            pl.BlockSpec((Bp, D), lambda l, j: (0, 0)),
            pl.BlockSpec((1, D, tn), lambda l, j: (l, 0, j)),
            pl.BlockSpec((1, 1, tn), lambda l, j: (l, 0, j)),
        ],
        out_specs=pl.BlockSpec((1, Bp, tn), lambda l, j: (l, 0, j)),
        out_shape=jax.ShapeDtypeStruct((L, Bp, N), F32),
        compiler_params=_params("arbitrary", "arbitrary"),
        name="modulation",
    )(cvec, w_ada, b_ada.reshape(L, 1, N))


def _rope(x, cos, sin_a, sin_b):
    return (x * cos + pltpu.roll(x, GROUP_W - DA_HD // 2, 1) * sin_a
            + pltpu.roll(x, DA_HD // 2, 1) * sin_b)


def _tile_mod(is_ctx, modx_ref, modc_ref):
    return jnp.where(is_ctx, modc_ref[...], modx_ref[...])


def _inproj_kernel(n_ctx_tiles, h_ref, modx_ref, modc_ref, g_ref, w_ref, cos_ref, sa_ref, sb_ref,
                   qkv_ref, rest_ref):
    nb, tile, d_model = h_ref.shape
    mod = _tile_mod(pl.program_id(1) < n_ctx_tiles, modx_ref, modc_ref)
    h = h_ref[...]
    hn = h * _rms_scale(h) * g_ref[...]
    y = hn * (1.0 + mod[:, 1:2, :]) + mod[:, 0:1, :]
    p = _dot(y.reshape(nb * tile, d_model).astype(BF16), w_ref[...])
    cos, sa, sb = cos_ref[...], sa_ref[...], sb_ref[...]
    for j in range(nb):
        pj = p[j * tile:(j + 1) * tile]
        q = _rope(pj[:, 0:GROUP_W], cos, sa, sb) * (DA_HD ** -0.5 * math.log2(math.e))
        k = _rope(pj[:, GROUP_W:2 * GROUP_W], cos, sa, sb)
        qkv_ref[j, :, 0:GROUP_W] = q.astype(BF16)
        qkv_ref[j, :, GROUP_W:2 * GROUP_W] = k.astype(BF16)
        qkv_ref[j, :, 2 * GROUP_W:3 * GROUP_W] = pj[:, 2 * GROUP_W:3 * GROUP_W].astype(BF16)
        rest_ref[j] = pj[:, 3 * GROUP_W:]


def _inproj(h, mod, g, w, cos, sa, sb, n_batch, n_ctx_tiles):
    B, S, D = h.shape
    N = w.shape[1]
    nt = S // TILE
    nb = INPROJ_NB
    assert B % nb == 0
    tab = pl.BlockSpec((TILE, GROUP_W), lambda b, t: (t, 0))
    return pl.pallas_call(
        functools.partial(_inproj_kernel, n_ctx_tiles),
        grid=(B // nb, nt),
        in_specs=[
            pl.BlockSpec((nb, TILE, D), lambda b, t: (b, t, 0)),
            pl.BlockSpec((nb, 6, D), lambda b, t: (b, 0, 0)),
            pl.BlockSpec((1, 6, D), lambda b, t: (n_batch, 0, 0)),
            pl.BlockSpec((1, D), lambda b, t: (0, 0)),
            pl.BlockSpec((D, N), lambda b, t: (0, 0)),
            tab, tab, tab,
        ],
        out_specs=[
            pl.BlockSpec((nb, TILE, 3 * GROUP_W), lambda b, t: (b, t, 0)),
            pl.BlockSpec((nb, TILE, N - 3 * GROUP_W), lambda b, t: (b, t, 0)),
        ],
        out_shape=[
            jax.ShapeDtypeStruct((B, S, 3 * GROUP_W), BF16),
            jax.ShapeDtypeStruct((B, S, N - 3 * GROUP_W), F32),
        ],
        compiler_params=_params("parallel", "arbitrary"),
        name="inproj",
    )(h, mod, mod, g.reshape(1, D), w, cos, sa, sb)


def _attn_kernel(lam_init, ctx_len, q_off, q_ref, k_ref, v_ref, lamv_ref, g_ref, o_ref,
                 sc_ref, e_ref, va_ref, vb_ref):
    tq = q_ref.shape[1]
    n_keys = k_ref.shape[1]
    lv = lamv_ref[...]
    lam = (jnp.exp(jnp.sum(lv[0:1] * lv[1:2], axis=-1, keepdims=True))
           - jnp.exp(jnp.sum(lv[2:3] * lv[3:4], axis=-1, keepdims=True)) + lam_init)
    lane = lax.broadcasted_iota(jnp.int32, (1, GROUP_W), 1)
    lane_map = lane >> 5
    lane_head = lane >> 6

    @pl.when(pl.program_id(1) == 0)
    def _():
        v = v_ref[0]
        one = jnp.ones_like(v)
        va_ref[...] = jnp.where(lane == GROUP_W - 1, one, v)
        vb_ref[...] = jnp.where(lane == 0, one, v)

    q = q_ref[0]
    q8 = jnp.concatenate(
        [jnp.where(lane_map == j, q, jnp.zeros_like(q)) for j in range(2 * DA_HEADS)], axis=0)

    def attend(nk):
        sc_ref[:, 0:nk] = _dot_nt(q8, k_ref[0, 0:nk, :])
        acc = jnp.zeros((tq, GROUP_W), F32)
        for h in range(DA_HEADS):
            for m in range(2):
                rows = slice((2 * h + m) * tq, (2 * h + m + 1) * tq)
                s = sc_ref[rows, 0:nk]
                e_ref[rows, 0:nk] = jnp.exp2(s - jnp.max(s, axis=-1, keepdims=True)).astype(BF16)
            vv = va_ref if h < DA_HEADS - 1 else vb_ref
            sum_col = GROUP_W - 1 if h < DA_HEADS - 1 else 0
            o2 = _dot(e_ref[2 * h * tq:(2 * h + 2) * tq, 0:nk], vv[0:nk, :])
            o0, o1 = o2[0:tq], o2[tq:2 * tq]
            w0 = 1.0 / o0[:, sum_col:sum_col + 1]
            w1 = lam / o1[:, sum_col:sum_col + 1]
            acc = acc + jnp.where(lane_head == h, o0 * w0 - o1 * w1, 0.0)
        ms = _split2_dot(acc * acc, _group_mean_matrix(GROUP_W, 2 * DA_HD))
        o_ref[0] = acc * lax.rsqrt(ms + EPS) * g_ref[...] * (1.0 - lam_init)

    if q_off == 0 and ctx_len == tq:
        @pl.when(pl.program_id(1) == 0)
        def _():
            attend(ctx_len)

        @pl.when(pl.program_id(1) != 0)
        def _():
            attend(n_keys)
    else:
        attend(n_keys)


def _attention(qkv, lam_vecs, subln_g, lam_init, ctx_len, want_ctx):
    B, S, _ = qkv.shape
    tq = TILE
    q_off = 0 if want_ctx else ctx_len // tq
    nq = S // tq - q_off
    g = jnp.tile(subln_g.astype(F32), DA_HEADS).reshape(1, GROUP_W)
    return pl.pallas_call(
        functools.partial(_attn_kernel, lam_init, ctx_len, q_off),
        grid=(B, nq),
        in_specs=[
            pl.BlockSpec((1, tq, GROUP_W), lambda b, i: (b, i + q_off, 0)),
            pl.BlockSpec((1, S, GROUP_W), lambda b, i: (b, 0, 1)),
            pl.BlockSpec((1, S, GROUP_W), lambda b, i: (b, 0, 2)),
            pl.BlockSpec((4, DA_HD), lambda b, i: (0, 0)),
            pl.BlockSpec((1, GROUP_W), lambda b, i: (0, 0)),
        ],
        out_specs=pl.BlockSpec((1, tq, GROUP_W), lambda b, i: (b, i, 0)),
        out_shape=jax.ShapeDtypeStruct((B, nq * tq, GROUP_W), F32),
        scratch_shapes=[
            pltpu.VMEM((2 * DA_HEADS * tq, S), F32),
            pltpu.VMEM((2 * DA_HEADS * tq, S), BF16),
            pltpu.VMEM((S, GROUP_W), BF16),
            pltpu.VMEM((S, GROUP_W), BF16),
        ],
        compiler_params=_params("parallel", "arbitrary"),
        name="diff_attention",
    )(qkv, qkv, qkv, lam_vecs.astype(F32), g)


def _hgrn_kernel(n_tiles, q_ref, ff_ref, fb_ref, i_ref, g_ref, lb_ref, lbt_ref, ng_ref, o_ref,
                 od_ref, st2_ref, u2_ref, oc2_ref):
    n_blocks = TILE // HG_BLOCK
    chunks_per_block = HG_BLOCK // HG_CHUNK
    row = lax.broadcasted_iota(jnp.int32, (TILE, TILE), 0)
    col = lax.broadcasted_iota(jnp.int32, (TILE, TILE), 1)
    same_chunk = (row >> 4) == (col >> 4)
    same_block = (row >> 6) == (col >> 6)
    lane = lax.broadcasted_iota(jnp.int32, (1, GROUP_W), 1)
    lane_head = lane >> 6
    lane_block = lane >> 6
    gmean = _group_mean_matrix(GROUP_W, HG_HD)

    def ones(mask):
        return jnp.where(mask, 1.0, 0.0).astype(BF16)

    consts = []
    for d in range(2):
        dist = ((row >> 4) - (col >> 4)) if d == 0 else ((col >> 4) - (row >> 4))
        causal = same_chunk & ((col <= row) if d == 0 else (col >= row))
        earlier = [same_block & (dist == n) for n in range(1, chunks_per_block)]
        row_mats = [ones(causal), ones(same_block & (dist >= 1))]
        for n in range(1, chunks_per_block - 1):
            row_mats.append(ones(same_block & (dist >= 1) & (dist <= n)))
        col_mats = [ones(same_chunk & ((row <= col) if d == 0 else (row >= col))), ones(same_chunk),
                    ones(same_block & (dist >= 1)), ones(same_block)]
        consts.append((causal, earlier, jnp.concatenate(row_mats, axis=0),
                       jnp.concatenate(col_mats, axis=1)))

    st2_ref[...] = jnp.zeros_like(st2_ref)

    def stack_heads(x):
        zero = jnp.zeros_like(x)
        return jnp.concatenate(
            [jnp.where(lane_head == h, x, zero).astype(BF16) for h in range(HG_HEADS)], axis=0)

    def tile_body(ti, carry):
        for d in range(2):
            causal, earlier, row_mat, col_mat = consts[d]
            f_ref = ff_ref if d == 0 else fb_ref
            lb = lb_ref[d:d + 1, :]
            lb_t = lbt_ref[:, d:d + 1]
            st_ref, u_ref, oc_ref = st2_ref.at[d], u2_ref.at[d], oc2_ref.at[d]
            if d == 0:
                t = ti
            else:
                t = jnp.where(ti == 0, 0, n_tiles - ti)
            rows = pl.ds(pl.multiple_of(t * TILE, TILE), TILE)
            fr = f_ref[0, rows, :]
            lf = jnp.log(lb + (1.0 - lb) * jax.nn.sigmoid(fr))
            sums = _split2_dot_rhs(row_mat, lf)
            qd = _silu(q_ref[0, rows, :]) * jnp.exp(sums[0:TILE])
            q_state = qd * jnp.exp(sums[TILE:2 * TILE])
            q_far = [qd * jnp.exp(sums[n * TILE:(n + 1) * TILE]) for n in range(2, chunks_per_block)]
            f_t = lb_t + (1.0 - lb_t) * jax.nn.sigmoid(fr.T)
            kk_t = 1.0 - f_t
            sums_t = _split2_dot(jnp.log(f_t), col_mat)
            g_t, gt_t = sums_t[:, 0:TILE], sums_t[:, TILE:2 * TILE]
            kinv_t = (kk_t * jnp.exp(-g_t)).astype(BF16)
            kend_t = kk_t * jnp.exp(gt_t - g_t)
            kblk_t = kend_t * jnp.exp(sums_t[:, 2 * TILE:3 * TILE])
            dec_t = jnp.exp(sums_t[:, 3 * TILE:4 * TILE])
            vb = i_ref[0, rows, :].astype(BF16)

            qd4 = stack_heads(qd)
            x_same = _dot(qd4, kinv_t)
            x_prev = _dot(jnp.concatenate([qd4] + [stack_heads(x) for x in q_far], axis=0),
                          kend_t.astype(BF16))
            a_heads = []
            for h in range(HG_HEADS):
                a = jnp.zeros((TILE, TILE), F32)
                for n in range(1, chunks_per_block):
                    r0 = ((n - 1) * HG_HEADS + h) * TILE
                    a = jnp.where(earlier[n - 1], x_prev[r0:r0 + TILE], a)
                a = jnp.where(causal, x_same[h * TILE:(h + 1) * TILE], a)
                a_heads.append(a.astype(BF16))
            zb = jnp.zeros_like(vb)
            v4 = jnp.concatenate(
                [jnp.where(lane_head == h, vb, zb) for h in range(HG_HEADS)], axis=0)
            o_tile = _dot(jnp.concatenate(a_heads, axis=1), v4)

            zero = jnp.zeros_like(kblk_t)
            k_stack = jnp.concatenate(
                [jnp.where(lane_block == b, kblk_t, zero).astype(BF16) for b in range(n_blocks)], axis=0)
            u_ref[...] = _dot(k_stack, vb)

            qs4 = stack_heads(q_state)
            order = range(n_blocks) if d == 0 else range(n_blocks - 1, -1, -1)
            for b in order:
                st = st_ref[...]
                q4b = jnp.concatenate(
                    [qs4[h * TILE + b * HG_BLOCK:h * TILE + (b + 1) * HG_BLOCK] for h in range(HG_HEADS)],
                    axis=0)
                ob4 = _dot(q4b, st.astype(BF16))
                ob = jnp.zeros((HG_BLOCK, GROUP_W), F32)
                for h in range(HG_HEADS):
                    ob = ob + jnp.where(lane_head == h, ob4[h * HG_BLOCK:(h + 1) * HG_BLOCK], 0.0)
                oc_ref[b * HG_BLOCK:(b + 1) * HG_BLOCK, :] = ob
                st_ref[...] = (st * dec_t[:, b * HG_BLOCK:b * HG_BLOCK + 1]
                               + u_ref[b * TILE:(b + 1) * TILE, :])
            od_ref[d, rows, :] = o_tile + oc_ref[...]
        return carry

    lax.fori_loop(0, n_tiles, tile_body, 0)

    def finish_body(t, carry):
        rows = pl.ds(pl.multiple_of(t * TILE, TILE), TILE)
        tot = od_ref[0, rows, :] + od_ref[1, rows, :]
        ms = _split2_dot(tot * tot, gmean)
        o_ref[0, rows, :] = (tot * lax.rsqrt(ms + EPS) * ng_ref[...] * _silu(g_ref[0, rows, :]))
        return carry

    lax.fori_loop(0, n_tiles, finish_body, 0)


def _hgrn(rest, lb, norm_g):
    B, S, _ = rest.shape
    n_tiles = S // TILE

    def part(j):
        return pl.BlockSpec((1, S, GROUP_W), lambda b: (b, 0, j))

    ng = jnp.tile(norm_g.astype(F32), HG_HEADS).reshape(1, GROUP_W)
    return pl.pallas_call(
        functools.partial(_hgrn_kernel, n_tiles),
        grid=(B,),
        in_specs=[part(0), part(1), part(2), part(3), part(4),
                  pl.BlockSpec((2, GROUP_W), lambda b: (0, 0)),
                  pl.BlockSpec((GROUP_W, 2), lambda b: (0, 0)),
                  pl.BlockSpec((1, GROUP_W), lambda b: (0, 0))],
        out_specs=pl.BlockSpec((1, S, GROUP_W), lambda b: (b, 0, 0)),
        out_shape=jax.ShapeDtypeStruct((B, S, GROUP_W), F32),
        scratch_shapes=[
            pltpu.VMEM((2, S, GROUP_W), F32),
            pltpu.VMEM((2, GROUP_W, GROUP_W), F32),
            pltpu.VMEM((2, TILE // HG_BLOCK * TILE, GROUP_W), F32),
            pltpu.VMEM((2, TILE, GROUP_W), F32),
        ],
        compiler_params=_params("parallel"),
        name="hgrn2",
    )(rest, rest, rest, rest, rest, lb, lb.T, ng)


def _conv_kernel(ctx_len, b_ref, c_ref, u_ref, w_ref, bias_ref, o_ref):
    S = b_ref.shape[1]
    v = c_ref[0] * u_ref[0]
    row = lax.broadcasted_iota(jnp.int32, (S, 1), 0)
    prev = jnp.where((row == 0) | (row == ctx_len), 0.0, pltpu.roll(v, 1, 0))
    nxt = jnp.where((row == ctx_len - 1) | (row == S - 1), 0.0, pltpu.roll(v, S - 1, 0))
    y = w_ref[0:1, :] * prev + w_ref[1:2, :] * v + w_ref[2:3, :] * nxt
    o_ref[0] = b_ref[0] * (y + bias_ref[...])


def _short_conv(rest, w, bias, ctx_len):
    B, S, _ = rest.shape

    def part(j):
        return pl.BlockSpec((1, S, GROUP_W), lambda b: (b, 0, j))

    return pl.pallas_call(
        functools.partial(_conv_kernel, ctx_len),
        grid=(B,),
        in_specs=[part(5), part(6), part(7),
                  pl.BlockSpec((3, GROUP_W), lambda b: (0, 0)),
                  pl.BlockSpec((1, GROUP_W), lambda b: (0, 0))],
        out_specs=pl.BlockSpec((1, S, GROUP_W), lambda b: (b, 0, 0)),
        out_shape=jax.ShapeDtypeStruct((B, S, GROUP_W), F32),
        compiler_params=_params("parallel"),
        name="short_conv",
    )(rest, rest, rest, w.astype(F32), bias.astype(F32).reshape(1, GROUP_W))


def _s5_time_tile(d, i, n_ctx_tiles, n_tiles):
    back = jnp.where(i < n_ctx_tiles, n_ctx_tiles - 1 - i, n_tiles - 1 - (i - n_ctx_tiles))
    return jnp.where(d == 0, i, back)


def _s5_kernel(n_ctx_tiles, n_tiles, u_ref, bmat_ref, cmat_ref, a_ref, d_ref, wg_ref, bg_ref,
               o_ref, xs_ref, yf_ref, st_ref):
    d = pl.program_id(1)
    i = pl.program_id(2)
    tt = u_ref.shape[1]
    n_rows = tt * SUBLANES
    n_strips = S5_COLS // LANES

    @pl.when(i == 0)
    def _():
        st_ref[...] = jnp.zeros_like(st_ref)

    u = u_ref[0].reshape(n_rows, GROUP_W)
    xs_ref[...] = _dot(u.astype(BF16), bmat_ref[0])

    def lanes(j):
        return slice(j * LANES, (j + 1) * LANES)

    a_re = [jnp.broadcast_to(a_ref[0, 0:1, lanes(j)], (SUBLANES, LANES)) for j in range(n_strips)]
    a_im = [jnp.broadcast_to(a_ref[0, 1:2, lanes(j)], (SUBLANES, LANES)) for j in range(n_strips)]

    def step(s, carry):
        t = jnp.where(d == 0, s, tt - 1 - s)
        r = pl.ds(pl.multiple_of(t * SUBLANES, SUBLANES), SUBLANES)
        new = []
        for j in range(n_strips):
            xr, xi = carry[2 * j], carry[2 * j + 1]
            nr = a_re[j] * xr - a_im[j] * xi + xs_ref[r, lanes(j)]
            ni = a_re[j] * xi + a_im[j] * xr + xs_ref[r, lanes(n_strips + j)]
            xs_ref[r, lanes(j)] = nr
            xs_ref[r, lanes(n_strips + j)] = ni
            new += [nr, ni]
        return tuple(new)

    state = lax.fori_loop(0, tt, step, tuple(st_ref[k] for k in range(2 * n_strips)), unroll=2)
    for k in range(2 * n_strips):
        st_ref[k] = state[k]

    yd = _dot(xs_ref[...].astype(BF16), cmat_ref[0])
    t_idx = _s5_time_tile(d, i, n_ctx_tiles, n_tiles)
    rows = pl.ds(pl.multiple_of(t_idx * n_rows, n_rows), n_rows)

    @pl.when(d == 0)
    def _():
        yf_ref[rows, :] = d_ref[...] * u + yd

    @pl.when(d == 1)
    def _():
        y = jax.nn.gelu(yf_ref[rows, :] + yd)
        gate = jax.nn.sigmoid(_dot(y.astype(BF16), wg_ref[...]) + bg_ref[...])
        o_ref[0] = (y * gate).reshape(tt, SUBLANES, GROUP_W)


def _s5_discretise(a_re, a_im, log_dt, b_re, b_im):
    dt = jnp.exp(log_dt)[..., None]
    mag = jnp.exp(dt * a_re)
    ab_re, ab_im = mag * jnp.cos(dt * a_im), mag * jnp.sin(dt * a_im)
    den = a_re * a_re + a_im * a_im
    nr = ab_re - 1.0
    f_re = (nr * a_re + ab_im * a_im) / den
    f_im = (ab_im * a_re - nr * a_im) / den
    bb_re = f_re[..., None] * b_re - f_im[..., None] * b_im
    bb_im = f_re[..., None] * b_im + f_im[..., None] * b_re
    return ab_re, ab_im, bb_re, bb_im


def _block_diag(blocks):
    G, r, c = blocks.shape
    eye = jnp.eye(G, dtype=blocks.dtype)
    return (eye[:, None, :, None] * blocks[:, :, None, :]).reshape(G * r, G * c)


def _s5_matrices(a_re, a_im, log_dt, b_re, b_im, c_re, c_im):
    ab_re, ab_im, bb_re, bb_im = _s5_discretise(
        a_re.astype(F32), a_im.astype(F32), log_dt.astype(F32), b_re.astype(F32), b_im.astype(F32))
    bmats, cmats = [], []
    for d in range(2):
        br = _block_diag(jnp.swapaxes(bb_re[d], 1, 2))
        bi = _block_diag(jnp.swapaxes(bb_im[d], 1, 2))
        bmats.append(jnp.concatenate([br, bi], axis=1))
        cr = _block_diag(jnp.swapaxes(c_re[d].astype(F32), 1, 2))
        ci = _block_diag(jnp.swapaxes(c_im[d].astype(F32), 1, 2))
        cmats.append(jnp.concatenate([cr, -ci], axis=0))
    abar = jnp.stack([ab_re.reshape(2, S5_COLS), ab_im.reshape(2, S5_COLS)], axis=1)
    return jnp.stack(bmats).astype(BF16), jnp.stack(cmats).astype(BF16), abar


def _s5(rest, bmat, cmat, abar, d_skip, w_glu, b_glu, ctx_len):
    B, S, _ = rest.shape
    assert B % SUBLANES == 0 and ctx_len % S5_TT == 0 and S % S5_TT == 0
    ng = B // SUBLANES
    n_tiles = S // S5_TT
    n_ctx_tiles = ctx_len // S5_TT
    u = rest[:, :, 8 * GROUP_W:9 * GROUP_W].reshape(ng, SUBLANES, S, GROUP_W).transpose(0, 2, 1, 3)

    def in_idx(g, d, i):
        return (g, _s5_time_tile(d, i, n_ctx_tiles, n_tiles), 0, 0)

    def out_idx(g, d, i):
        return (g, _s5_time_tile(1, jnp.where(d == 0, 0, i), n_ctx_tiles, n_tiles), 0, 0)

    out = pl.pallas_call(
        functools.partial(_s5_kernel, n_ctx_tiles, n_tiles),
        grid=(ng, 2, n_tiles),
        in_specs=[
            pl.BlockSpec((1, S5_TT, SUBLANES, GROUP_W), in_idx),
            pl.BlockSpec((1,) + bmat.shape[1:], lambda g, d, i: (d, 0, 0)),
            pl.BlockSpec((1,) + cmat.shape[1:], lambda g, d, i: (d, 0, 0)),
            pl.BlockSpec((1,) + abar.shape[1:], lambda g, d, i: (d, 0, 0)),
            pl.BlockSpec((1, GROUP_W), lambda g, d, i: (0, 0)),
            pl.BlockSpec((GROUP_W, GROUP_W), lambda g, d, i: (0, 0)),
            pl.BlockSpec((1, GROUP_W), lambda g, d, i: (0, 0)),
        ],
        out_specs=pl.BlockSpec((1, S5_TT, SUBLANES, GROUP_W), out_idx),
        out_shape=jax.ShapeDtypeStruct((ng, S, SUBLANES, GROUP_W), F32),
        scratch_shapes=[
            pltpu.VMEM((S5_TT * SUBLANES, 2 * S5_COLS), F32),
            pltpu.VMEM((S * SUBLANES, GROUP_W), F32),
            pltpu.VMEM((2 * S5_COLS // LANES, SUBLANES, LANES), F32),
        ],
        compiler_params=_params("parallel", "arbitrary", "arbitrary"),
        name="s5",
    )(u, bmat, cmat, abar, d_skip.astype(F32).reshape(1, GROUP_W), w_glu.astype(BF16),
      b_glu.astype(F32).reshape(1, GROUP_W))
    return out.transpose(0, 2, 1, 3).reshape(B, S, GROUP_W)


def _post_kernel(n_hidden_chunks, ctx_tiles_here, h_ref, a_ref, b_ref, c_ref, d_ref, modx_ref,
                 modc_ref, ng_ref, wo_ref, wi_ref, w2_ref, o_ref):
    nb, tile, d_model = h_ref.shape
    n_rows = nb * tile
    mod = _tile_mod(pl.program_id(1) < ctx_tiles_here, modx_ref, modc_ref)

    def flat(ref):
        return ref[...].reshape(n_rows, GROUP_W).astype(BF16)

    mix = (_dot(flat(a_ref), wo_ref[0:GROUP_W, :])
           + _dot(flat(b_ref), wo_ref[GROUP_W:2 * GROUP_W, :])
           + _dot(flat(c_ref), wo_ref[2 * GROUP_W:3 * GROUP_W, :])
           + _dot(flat(d_ref), wo_ref[3 * GROUP_W:4 * GROUP_W, :])).reshape(nb, tile, d_model)
    h = h_ref[...] + mod[:, 2:3, :] * (mix * _rms_scale(mix) * ng_ref[1:2, :])
    y = (h * _rms_scale(h) * ng_ref[2:3, :]) * (1.0 + mod[:, 4:5, :]) + mod[:, 3:4, :]
    yb = y.reshape(n_rows, d_model).astype(BF16)
    hidden = w2_ref.shape[0]
    hc = hidden // n_hidden_chunks
    ffn = jnp.zeros((n_rows, d_model), F32)
    for j in range(n_hidden_chunks):
        gate = _dot(yb, wi_ref[:, j * hc:(j + 1) * hc])
        up = _dot(yb, wi_ref[:, hidden + j * hc:hidden + (j + 1) * hc])
        ffn = ffn + _dot((_silu(gate) * up).astype(BF16), w2_ref[j * hc:(j + 1) * hc, :])
    ffn = ffn.reshape(nb, tile, d_model)
    o_ref[...] = h + mod[:, 5:6, :] * (ffn * _rms_scale(ffn) * ng_ref[3:4, :])


def _post(h, a, b, c, d, mod, norm_g, w_out, w_ffn_in, w_ffn_out, n_batch, n_ctx_tiles, want_ctx):
    B, S, D = h.shape
    off = 0 if want_ctx else n_ctx_tiles
    nt = S // TILE - off
    a_off = off if a.shape[1] == S else 0
    nb = POST_NB
    assert B % nb == 0

    def rows(o):
        return lambda bi, t: (bi, t + o, 0)

    def whole(arr):
        return pl.BlockSpec(arr.shape, lambda bi, t: (0,) * arr.ndim, pipeline_mode=pl.Buffered(1))

    mix_spec = lambda o: pl.BlockSpec((nb, TILE, GROUP_W), rows(o))
    return pl.pallas_call(
        functools.partial(_post_kernel, 2, n_ctx_tiles - off),
        grid=(B // nb, nt),
        in_specs=[
            pl.BlockSpec((nb, TILE, D), rows(off)),
            mix_spec(a_off), mix_spec(off), mix_spec(off), mix_spec(off),
            pl.BlockSpec((nb, 6, D), lambda bi, t: (bi, 0, 0)),
            pl.BlockSpec((1, 6, D), lambda bi, t: (n_batch, 0, 0)),
            whole(norm_g), whole(w_out), whole(w_ffn_in), whole(w_ffn_out),
        ],
        out_specs=pl.BlockSpec((nb, TILE, D), lambda bi, t: (bi, t, 0)),
        out_shape=jax.ShapeDtypeStruct((B, nt * TILE, D), F32),
        compiler_params=_params("parallel", "arbitrary"),
        name="post",
    )(h, a, b, c, d, mod, mod, norm_g, w_out, w_ffn_in, w_ffn_out)


def _rope_tables(n_rows, ctx_len):
    rows = jnp.broadcast_to(jnp.arange(n_rows, dtype=F32)[:, None], (n_rows, GRID_W)).reshape(-1)
    cols = jnp.broadcast_to(jnp.arange(GRID_W, dtype=F32)[None, :], (n_rows, GRID_W)).reshape(-1)
    n_freq = DA_HD // 4
    inv = ROPE_BASE ** (-jnp.arange(n_freq, dtype=F32) / n_freq)
    ang = jnp.concatenate([rows[:, None] * inv, cols[:, None] * inv], axis=-1)
    cos, sin = jnp.cos(ang), jnp.sin(ang)
    zero = jnp.zeros_like(sin)
    reps = GROUP_W // DA_HD

    def lanes(first, second, ctx_value):
        t = jnp.tile(jnp.concatenate([first, second], axis=-1), (1, reps))
        return jnp.concatenate([jnp.full((ctx_len, GROUP_W), ctx_value, F32), t], axis=0)

    return lanes(cos, cos, 1.0), lanes(-sin, zero, 0.0), lanes(zero, sin, 0.0)


def _deinterleave_qk(w_in):
    perm32 = jnp.concatenate([jnp.arange(0, DA_HD, 2), jnp.arange(1, DA_HD, 2)])
    perm = (jnp.arange(0, 2 * GROUP_W, DA_HD)[:, None] + perm32[None, :]).reshape(-1)
    cols = jnp.concatenate([perm, jnp.arange(2 * GROUP_W, w_in.shape[-1])])
    return w_in[..., cols]


def kernel(x, c, ctx, c_ctx, w_ada, b_ada, norm_g, w_in, w_out, da_lambda, da_subln, hg_lb, hg_norm, sc_w, sc_b, s5_a_re, s5_a_im, s5_log_dt, s5_b_re, s5_b_im, s5_c_re, s5_c_im, s5_d, s5_w_glu, s5_b_glu, w_ffn_in, w_ffn_out):
    B, T, D = x.shape
    ctx_len = ctx.shape[1]
    L = w_ada.shape[0]
    assert ctx_len % TILE == 0 and T % TILE == 0 and T % GRID_W == 0
    n_ctx_tiles = ctx_len // TILE

    bp = -(-(B + 1) // SUBLANES) * SUBLANES
    cvec = jnp.concatenate([c, c_ctx[None, :], jnp.zeros((bp - B - 1, D), c.dtype)], axis=0)
    mods = _modulation(cvec.astype(F32), w_ada, b_ada).reshape(L, bp, 6, D)

    cos, sin_a, sin_b = _rope_tables(T // GRID_W, ctx_len)
    lb = jnp.cumsum(jax.nn.softmax(hg_lb.astype(F32), axis=0), axis=0)
    lb = lb - lb[:1]
    w_in_b = _deinterleave_qk(w_in).astype(BF16)
    w_out_b = w_out.astype(BF16)
    w_ffn_in_b = w_ffn_in.astype(BF16)
    w_ffn_out_b = w_ffn_out.astype(BF16)

    h = jnp.concatenate([ctx, x], axis=1)
    for l in range(L):
        want_ctx = l < L - 1
        lam_init = 0.8 - 0.6 * math.exp(-0.3 * l)
        qkv, rest = _inproj(h, mods[l], norm_g[l, 0], w_in_b[l], cos, sin_a, sin_b, B, n_ctx_tiles)
        a = _attention(qkv, da_lambda[l], da_subln[l], lam_init, ctx_len, want_ctx)
        b = _hgrn(rest, lb[l], hg_norm[l])
        cc = _short_conv(rest, sc_w[l], sc_b[l], ctx_len)
        bmat, cmat, abar = _s5_matrices(s5_a_re[l], s5_a_im[l], s5_log_dt[l], s5_b_re[l],
                                        s5_b_im[l], s5_c_re[l], s5_c_im[l])
        dd = _s5(rest, bmat, cmat, abar, s5_d[l], s5_w_glu[l], s5_b_glu[l], ctx_len)
        h = _post(h, a, b, cc, dd, mods[l], norm_g[l].astype(F32), w_out_b[l], w_ffn_in_b[l],
                  w_ffn_out_b[l], B, n_ctx_tiles, want_ctx)
    return h
```

```python
import functools
import math

import jax
import jax.numpy as jnp
from jax import lax
from jax.experimental import pallas as pl
from jax.experimental.pallas import tpu as pltpu

F32 = jnp.float32
BF16 = jnp.bfloat16

EPS = 1e-6
GRID_W = 64
ROPE_BASE = 10000.0
GROUP_W = 256
DA_HEADS = 4
DA_HD = 32
HG_HEADS = 4
HG_HD = 64
HG_CHUNK = 16
HG_BLOCK = 64
S5_NGROUPS = 16
S5_GROUP = 16
S5_STATE = 64
S5_COLS = S5_NGROUPS * S5_STATE
LANES = 128
SUBLANES = 8
TILE = 256
INPROJ_NB = 4
POST_NB = 2
S5_TT = 128
VMEM_LIMIT = 56 * 1024 * 1024

_NT = (((1,), (1,)), ((), ()))


def _dot(a, b):
    return jnp.dot(a, b, preferred_element_type=F32)


def _dot_nt(a, b):
    return lax.dot_general(a, b, _NT, preferred_element_type=F32)


def _split2_dot_rhs(m, x):
    x1 = x.astype(BF16)
    x2 = (x - x1.astype(F32)).astype(BF16)
    return _dot(m, x1) + _dot(m, x2)


def _split2_dot(x, m):
    x1 = x.astype(BF16)
    x2 = (x - x1.astype(F32)).astype(BF16)
    return _dot(x1, m) + _dot(x2, m)


def _silu(x):
    return x * jax.nn.sigmoid(x)


def _rms_scale(x):
    return lax.rsqrt(jnp.mean(x * x, axis=-1, keepdims=True) + EPS)


def _group_mean_matrix(n, group):
    sh = int(math.log2(group))
    r = lax.broadcasted_iota(jnp.int32, (n, n), 0) >> sh
    c = lax.broadcasted_iota(jnp.int32, (n, n), 1) >> sh
    return jnp.where(r == c, 1.0 / group, 0.0).astype(BF16)


def _params(*sem):
    return pltpu.CompilerParams(dimension_semantics=sem, vmem_limit_bytes=VMEM_LIMIT)


def _mod_kernel(c_ref, w_ref, b_ref, o_ref):
    sc = _silu(c_ref[...])
    o_ref[0] = _dot(sc.astype(BF16), w_ref[0].astype(BF16)) + b_ref[0]


def _modulation(cvec, w_ada, b_ada):
    L, D, N = w_ada.shape
    Bp = cvec.shape[0]
    tn = 1536
    return pl.pallas_call(
        _mod_kernel,
        grid=(L, N // tn),
        in_specs=[
            pl.BlockSpec((Bp, D), lambda l, j: (0, 0)),
            pl.BlockSpec((1, D, tn), lambda l, j: (l, 0, j)),
            pl.BlockSpec((1, 1, tn), lambda l, j: (l, 0, j)),
        ],
        out_specs=pl.BlockSpec((1, Bp, tn), lambda l, j: (l, 0, j)),
        out_shape=jax.ShapeDtypeStruct((L, Bp, N), F32),
        compiler_params=_params("arbitrary", "arbitrary"),
        name="modulation",
    )(cvec, w_ada, b_ada.reshape(L, 1, N))


def _rope(x, cos, sin_a, sin_b):
    return (x * cos + pltpu.roll(x, GROUP_W - DA_HD // 2, 1) * sin_a
            + pltpu.roll(x, DA_HD // 2, 1) * sin_b)


def _tile_mod(is_ctx, modx_ref, modc_ref):
    return jnp.where(is_ctx, modc_ref[...], modx_ref[...])


def _inproj_kernel(n_ctx_tiles, h_ref, modx_ref, modc_ref, g_ref, w_ref, cos_ref, sa_ref, sb_ref,
                   qkv_ref, rest_ref):
    nb, tile, d_model = h_ref.shape
    mod = _tile_mod(pl.program_id(1) < n_ctx_tiles, modx_ref, modc_ref)
    h = h_ref[...]
    hn = h * _rms_scale(h) * g_ref[...]
    y = hn * (1.0 + mod[:, 1:2, :]) + mod[:, 0:1, :]
    p = _dot(y.reshape(nb * tile, d_model).astype(BF16), w_ref[...])
    cos, sa, sb = cos_ref[...], sa_ref[...], sb_ref[...]
    for j in range(nb):
        pj = p[j * tile:(j + 1) * tile]
        q = _rope(pj[:, 0:GROUP_W], cos, sa, sb) * (DA_HD ** -0.5 * math.log2(math.e))
        k = _rope(pj[:, GROUP_W:2 * GROUP_W], cos, sa, sb)
        qkv_ref[j, :, 0:GROUP_W] = q.astype(BF16)
        qkv_ref[j, :, GROUP_W:2 * GROUP_W] = k.astype(BF16)
        qkv_ref[j, :, 2 * GROUP_W:3 * GROUP_W] = pj[:, 2 * GROUP_W:3 * GROUP_W].astype(BF16)
        rest_ref[j] = pj[:, 3 * GROUP_W:]


def _inproj(h, mod, g, w, cos, sa, sb, n_batch, n_ctx_tiles):
    B, S, D = h.shape
    N = w.shape[1]
    nt = S // TILE
    nb = INPROJ_NB
    assert B % nb == 0
    tab = pl.BlockSpec((TILE, GROUP_W), lambda b, t: (t, 0))
    return pl.pallas_call(
        functools.partial(_inproj_kernel, n_ctx_tiles),
        grid=(B // nb, nt),
        in_specs=[
            pl.BlockSpec((nb, TILE, D), lambda b, t: (b, t, 0)),
            pl.BlockSpec((nb, 6, D), lambda b, t: (b, 0, 0)),
            pl.BlockSpec((1, 6, D), lambda b, t: (n_batch, 0, 0)),
            pl.BlockSpec((1, D), lambda b, t: (0, 0)),
            pl.BlockSpec((D, N), lambda b, t: (0, 0)),
            tab, tab, tab,
        ],
        out_specs=[
            pl.BlockSpec((nb, TILE, 3 * GROUP_W), lambda b, t: (b, t, 0)),
            pl.BlockSpec((nb, TILE, N - 3 * GROUP_W), lambda b, t: (b, t, 0)),
        ],
        out_shape=[
            jax.ShapeDtypeStruct((B, S, 3 * GROUP_W), BF16),
            jax.ShapeDtypeStruct((B, S, N - 3 * GROUP_W), F32),
        ],
        compiler_params=_params("parallel", "arbitrary"),
        name="inproj",
    )(h, mod, mod, g.reshape(1, D), w, cos, sa, sb)


def _attn_kernel(lam_init, ctx_len, q_off, q_ref, k_ref, v_ref, lamv_ref, g_ref, o_ref,
                 sc_ref, e_ref, va_ref, vb_ref):
    tq = q_ref.shape[1]
    n_keys = k_ref.shape[1]
    lv = lamv_ref[...]
    lam = (jnp.exp(jnp.sum(lv[0:1] * lv[1:2], axis=-1, keepdims=True))
           - jnp.exp(jnp.sum(lv[2:3] * lv[3:4], axis=-1, keepdims=True)) + lam_init)
    lane = lax.broadcasted_iota(jnp.int32, (1, GROUP_W), 1)
    lane_map = lane >> 5
    lane_head = lane >> 6

    @pl.when(pl.program_id(1) == 0)
    def _():
        v = v_ref[0]
        one = jnp.ones_like(v)
        va_ref[...] = jnp.where(lane == GROUP_W - 1, one, v)
        vb_ref[...] = jnp.where(lane == 0, one, v)

    q = q_ref[0]
    q8 = jnp.concatenate(
        [jnp.where(lane_map == j, q, jnp.zeros_like(q)) for j in range(2 * DA_HEADS)], axis=0)

    def attend(nk):
        sc_ref[:, 0:nk] = _dot_nt(q8, k_ref[0, 0:nk, :])
        acc = jnp.zeros((tq, GROUP_W), F32)
        for h in range(DA_HEADS):
            for m in range(2):
                rows = slice((2 * h + m) * tq, (2 * h + m + 1) * tq)
                s = sc_ref[rows, 0:nk]
                e_ref[rows, 0:nk] = jnp.exp2(s - jnp.max(s, axis=-1, keepdims=True)).astype(BF16)
            vv = va_ref if h < DA_HEADS - 1 else vb_ref
            sum_col = GROUP_W - 1 if h < DA_HEADS - 1 else 0
            o2 = _dot(e_ref[2 * h * tq:(2 * h + 2) * tq, 0:nk], vv[0:nk, :])
            o0, o1 = o2[0:tq], o2[tq:2 * tq]
            w0 = 1.0 / o0[:, sum_col:sum_col + 1]
            w1 = lam / o1[:, sum_col:sum_col + 1]
            acc = acc + jnp.where(lane_head == h, o0 * w0 - o1 * w1, 0.0)
        ms = _split2_dot(acc * acc, _group_mean_matrix(GROUP_W, 2 * DA_HD))
        o_ref[0] = acc * lax.rsqrt(ms + EPS) * g_ref[...] * (1.0 - lam_init)

    if q_off == 0 and ctx_len == tq:
        @pl.when(pl.program_id(1) == 0)
        def _():
            attend(ctx_len)

        @pl.when(pl.program_id(1) != 0)
        def _():
            attend(n_keys)
    else:
        attend(n_keys)


def _attention(qkv, lam_vecs, subln_g, lam_init, ctx_len, want_ctx):
    B, S, _ = qkv.shape
    tq = TILE
    q_off = 0 if want_ctx else ctx_len // tq
    nq = S // tq - q_off
    g = jnp.tile(subln_g.astype(F32), DA_HEADS).reshape(1, GROUP_W)
    return pl.pallas_call(
        functools.partial(_attn_kernel, lam_init, ctx_len, q_off),
        grid=(B, nq),
        in_specs=[
            pl.BlockSpec((1, tq, GROUP_W), lambda b, i: (b, i + q_off, 0)),
            pl.BlockSpec((1, S, GROUP_W), lambda b, i: (b, 0, 1)),
            pl.BlockSpec((1, S, GROUP_W), lambda b, i: (b, 0, 2)),
            pl.BlockSpec((4, DA_HD), lambda b, i: (0, 0)),
            pl.BlockSpec((1, GROUP_W), lambda b, i: (0, 0)),
        ],
        out_specs=pl.BlockSpec((1, tq, GROUP_W), lambda b, i: (b, i, 0)),
        out_shape=jax.ShapeDtypeStruct((B, nq * tq, GROUP_W), F32),
        scratch_shapes=[
            pltpu.VMEM((2 * DA_HEADS * tq, S), F32),
            pltpu.VMEM((2 * DA_HEADS * tq, S), BF16),
            pltpu.VMEM((S, GROUP_W), BF16),
            pltpu.VMEM((S, GROUP_W), BF16),
        ],
        compiler_params=_params("parallel", "arbitrary"),
        name="diff_attention",
    )(qkv, qkv, qkv, lam_vecs.astype(F32), g)


def _hgrn_kernel(n_tiles, q_ref, ff_ref, fb_ref, i_ref, g_ref, lb_ref, lbt_ref, ng_ref, o_ref,
                 od_ref, st2_ref, u2_ref, oc2_ref):
    n_blocks = TILE // HG_BLOCK
    chunks_per_block = HG_BLOCK // HG_CHUNK
    row = lax.broadcasted_iota(jnp.int32, (TILE, TILE), 0)
    col = lax.broadcasted_iota(jnp.int32, (TILE, TILE), 1)
    same_chunk = (row >> 4) == (col >> 4)
    same_block = (row >> 6) == (col >> 6)
    lane = lax.broadcasted_iota(jnp.int32, (1, GROUP_W), 1)
    lane_head = lane >> 6
    lane_block = lane >> 6
    gmean = _group_mean_matrix(GROUP_W, HG_HD)

    def ones(mask):
        return jnp.where(mask, 1.0, 0.0).astype(BF16)

    consts = []
    for d in range(2):
        dist = ((row >> 4) - (col >> 4)) if d == 0 else ((col >> 4) - (row >> 4))
        causal = same_chunk & ((col <= row) if d == 0 else (col >= row))
        earlier = [same_block & (dist == n) for n in range(1, chunks_per_block)]
        row_mats = [ones(causal), ones(same_block & (dist >= 1))]
        for n in range(1, chunks_per_block - 1):
            row_mats.append(ones(same_block & (dist >= 1) & (dist <= n)))
        col_mats = [ones(same_chunk & ((row <= col) if d == 0 else (row >= col))), ones(same_chunk),
                    ones(same_block & (dist >= 1)), ones(same_block)]
        consts.append((causal, earlier, jnp.concatenate(row_mats, axis=0),
                       jnp.concatenate(col_mats, axis=1)))

    st2_ref[...] = jnp.zeros_like(st2_ref)

    def stack_heads(x):
        zero = jnp.zeros_like(x)
        return jnp.concatenate(
            [jnp.where(lane_head == h, x, zero).astype(BF16) for h in range(HG_HEADS)], axis=0)

    def tile_body(ti, carry):
        for d in range(2):
            causal, earlier, row_mat, col_mat = consts[d]
            f_ref = ff_ref if d == 0 else fb_ref
            lb = lb_ref[d:d + 1, :]
            lb_t = lbt_ref[:, d:d + 1]
            st_ref, u_ref, oc_ref = st2_ref.at[d], u2_ref.at[d], oc2_ref.at[d]
            if d == 0:
                t = ti
            else:
                t = jnp.where(ti == 0, 0, n_tiles - ti)
            rows = pl.ds(pl.multiple_of(t * TILE, TILE), TILE)
            fr = f_ref[0, rows, :]
            lf = jnp.log(lb + (1.0 - lb) * jax.nn.sigmoid(fr))
            sums = _split2_dot_rhs(row_mat, lf)
            qd = _silu(q_ref[0, rows, :]) * jnp.exp(sums[0:TILE])
            q_state = qd * jnp.exp(sums[TILE:2 * TILE])
            q_far = [qd * jnp.exp(sums[n * TILE:(n + 1) * TILE]) for n in range(2, chunks_per_block)]
            f_t = lb_t + (1.0 - lb_t) * jax.nn.sigmoid(fr.T)
            kk_t = 1.0 - f_t
            sums_t = _split2_dot(jnp.log(f_t), col_mat)
            g_t, gt_t = sums_t[:, 0:TILE], sums_t[:, TILE:2 * TILE]
            kinv_t = (kk_t * jnp.exp(-g_t)).astype(BF16)
            kend_t = kk_t * jnp.exp(gt_t - g_t)
            kblk_t = kend_t * jnp.exp(sums_t[:, 2 * TILE:3 * TILE])
            dec_t = jnp.exp(sums_t[:, 3 * TILE:4 * TILE])
            vb = i_ref[0, rows, :].astype(BF16)

            qd4 = stack_heads(qd)
            x_same = _dot(qd4, kinv_t)
            x_prev = _dot(jnp.concatenate([qd4] + [stack_heads(x) for x in q_far], axis=0),
                          kend_t.astype(BF16))
            a_heads = []
            for h in range(HG_HEADS):
                a = jnp.zeros((TILE, TILE), F32)
                for n in range(1, chunks_per_block):
                    r0 = ((n - 1) * HG_HEADS + h) * TILE
                    a = jnp.where(earlier[n - 1], x_prev[r0:r0 + TILE], a)
                a = jnp.where(causal, x_same[h * TILE:(h + 1) * TILE], a)
                a_heads.append(a.astype(BF16))
            zb = jnp.zeros_like(vb)
            v4 = jnp.concatenate(
                [jnp.where(lane_head == h, vb, zb) for h in range(HG_HEADS)], axis=0)
            o_tile = _dot(jnp.concatenate(a_heads, axis=1), v4)

            zero = jnp.zeros_like(kblk_t)
            k_stack = jnp.concatenate(
                [jnp.where(lane_block == b, kblk_t, zero).astype(BF16) for b in range(n_blocks)], axis=0)
            u_ref[...] = _dot(k_stack, vb)

            qs4 = stack_heads(q_state)
            order = range(n_blocks) if d == 0 else range(n_blocks - 1, -1, -1)
            for b in order:
                st = st_ref[...]
                q4b = jnp.concatenate(
                    [qs4[h * TILE + b * HG_BLOCK:h * TILE + (b + 1) * HG_BLOCK] for h in range(HG_HEADS)],
                    axis=0)
                ob4 = _dot(q4b, st.astype(BF16))
                ob = jnp.zeros((HG_BLOCK, GROUP_W), F32)
                for h in range(HG_HEADS):
                    ob = ob + jnp.where(lane_head == h, ob4[h * HG_BLOCK:(h + 1) * HG_BLOCK], 0.0)
                oc_ref[b * HG_BLOCK:(b + 1) * HG_BLOCK, :] = ob
                st_ref[...] = (st * dec_t[:, b * HG_BLOCK:b * HG_BLOCK + 1]
                               + u_ref[b * TILE:(b + 1) * TILE, :])
            od_ref[d, rows, :] = o_tile + oc_ref[...]
        return carry

    lax.fori_loop(0, n_tiles, tile_body, 0)

    def finish_body(t, carry):
        rows = pl.ds(pl.multiple_of(t * TILE, TILE), TILE)
        tot = od_ref[0, rows, :] + od_ref[1, rows, :]
        ms = _split2_dot(tot * tot, gmean)
        o_ref[0, rows, :] = (tot * lax.rsqrt(ms + EPS) * ng_ref[...] * _silu(g_ref[0, rows, :]))
        return carry

    lax.fori_loop(0, n_tiles, finish_body, 0)


def _hgrn(rest, lb, norm_g):
    B, S, _ = rest.shape
    n_tiles = S // TILE

    def part(j):
        return pl.BlockSpec((1, S, GROUP_W), lambda b: (b, 0, j))

    ng = jnp.tile(norm_g.astype(F32), HG_HEADS).reshape(1, GROUP_W)
    return pl.pallas_call(
        functools.partial(_hgrn_kernel, n_tiles),
        grid=(B,),
        in_specs=[part(0), part(1), part(2), part(3), part(4),
                  pl.BlockSpec((2, GROUP_W), lambda b: (0, 0)),
                  pl.BlockSpec((GROUP_W, 2), lambda b: (0, 0)),
                  pl.BlockSpec((1, GROUP_W), lambda b: (0, 0))],
        out_specs=pl.BlockSpec((1, S, GROUP_W), lambda b: (b, 0, 0)),
        out_shape=jax.ShapeDtypeStruct((B, S, GROUP_W), F32),
        scratch_shapes=[
            pltpu.VMEM((2, S, GROUP_W), F32),
            pltpu.VMEM((2, GROUP_W, GROUP_W), F32),
            pltpu.VMEM((2, TILE // HG_BLOCK * TILE, GROUP_W), F32),
            pltpu.VMEM((2, TILE, GROUP_W), F32),
        ],
        compiler_params=_params("parallel"),
        name="hgrn2",
    )(rest, rest, rest, rest, rest, lb, lb.T, ng)


def _conv_kernel(ctx_len, b_ref, c_ref, u_ref, w_ref, bias_ref, o_ref):
    S = b_ref.shape[1]
    v = c_ref[0] * u_ref[0]
    row = lax.broadcasted_iota(jnp.int32, (S, 1), 0)
    prev = jnp.where((row == 0) | (row == ctx_len), 0.0, pltpu.roll(v, 1, 0))
    nxt = jnp.where((row == ctx_len - 1) | (row == S - 1), 0.0, pltpu.roll(v, S - 1, 0))
    y = w_ref[0:1, :] * prev + w_ref[1:2, :] * v + w_ref[2:3, :] * nxt
    o_ref[0] = b_ref[0] * (y + bias_ref[...])


def _short_conv(rest, w, bias, ctx_len):
    B, S, _ = rest.shape

    def part(j):
        return pl.BlockSpec((1, S, GROUP_W), lambda b: (b, 0, j))

    return pl.pallas_call(
        functools.partial(_conv_kernel, ctx_len),
        grid=(B,),
        in_specs=[part(5), part(6), part(7),
                  pl.BlockSpec((3, GROUP_W), lambda b: (0, 0)),
                  pl.BlockSpec((1, GROUP_W), lambda b: (0, 0))],
        out_specs=pl.BlockSpec((1, S, GROUP_W), lambda b: (b, 0, 0)),
        out_shape=jax.ShapeDtypeStruct((B, S, GROUP_W), F32),
        compiler_params=_params("parallel"),
        name="short_conv",
    )(rest, rest, rest, w.astype(F32), bias.astype(F32).reshape(1, GROUP_W))


def _s5_back_tile(i, n_ctx_tiles, n_tiles):
    return jnp.where(i < n_ctx_tiles, n_ctx_tiles - 1 - i, n_tiles - 1 - (i - n_ctx_tiles))


def _s5_scan_kernel(uf_ref, ub_ref, bmat_ref, cmat_ref, a_ref, d_ref, yf_ref, yb_ref,
                    xsf_ref, xsb_ref, st_ref):
    tt = uf_ref.shape[1]
    n_rows = tt * SUBLANES
    n_strips = S5_COLS // LANES

    @pl.when(pl.program_id(1) == 0)
    def _():
        st_ref[...] = jnp.zeros_like(st_ref)

    def lanes(j):
        return slice(j * LANES, (j + 1) * LANES)

    us = []
    for d, (u_ref, xs_ref) in enumerate(((uf_ref, xsf_ref), (ub_ref, xsb_ref))):
        u = u_ref[0].reshape(n_rows, GROUP_W)
        xs_ref[...] = _dot(u.astype(BF16), bmat_ref[d])
        us.append(u)

    for d, (xs_ref, y_ref) in enumerate(((xsf_ref, yf_ref), (xsb_ref, yb_ref))):
        a_re = [jnp.broadcast_to(a_ref[d, 0:1, lanes(j)], (SUBLANES, LANES)) for j in range(n_strips)]
        a_im = [jnp.broadcast_to(a_ref[d, 1:2, lanes(j)], (SUBLANES, LANES)) for j in range(n_strips)]
        state = [st_ref[d, k] for k in range(2 * n_strips)]
        for s in range(tt):
            t = s if d == 0 else tt - 1 - s
            r = slice(t * SUBLANES, (t + 1) * SUBLANES)
            for j in range(n_strips):
                xr, xi = state[2 * j], state[2 * j + 1]
                nr = a_re[j] * xr - a_im[j] * xi + xs_ref[r, lanes(j)]
                ni = a_re[j] * xi + a_im[j] * xr + xs_ref[r, lanes(n_strips + j)]
                xs_ref[r, lanes(j)] = nr
                xs_ref[r, lanes(n_strips + j)] = ni
                state[2 * j], state[2 * j + 1] = nr, ni
        for k in range(2 * n_strips):
            st_ref[d, k] = state[k]
        y = _dot(xs_ref[...].astype(BF16), cmat_ref[d])
        if d == 0:
            y = y + d_ref[...] * us[0]
        y_ref[0] = y.reshape(tt, SUBLANES, GROUP_W)


def _s5_finish_kernel(yf_ref, yb_ref, wg_ref, bg_ref, o_ref):
    _, tt, nb, width = yf_ref.shape
    y = jax.nn.gelu(yf_ref[0].reshape(tt * nb, width) + yb_ref[0].reshape(tt * nb, width))
    gate = jax.nn.sigmoid(_dot(y.astype(BF16), wg_ref[...]) + bg_ref[...])
    o_ref[0] = (y * gate).reshape(tt, nb, width)


def _s5_discretise(a_re, a_im, log_dt, b_re, b_im):
    dt = jnp.exp(log_dt)[..., None]
    mag = jnp.exp(dt * a_re)
    ab_re, ab_im = mag * jnp.cos(dt * a_im), mag * jnp.sin(dt * a_im)
    den = a_re * a_re + a_im * a_im
    nr = ab_re - 1.0
    f_re = (nr * a_re + ab_im * a_im) / den
    f_im = (ab_im * a_re - nr * a_im) / den
    bb_re = f_re[..., None] * b_re - f_im[..., None] * b_im
    bb_im = f_re[..., None] * b_im + f_im[..., None] * b_re
    return ab_re, ab_im, bb_re, bb_im


def _block_diag(blocks):
    G, r, c = blocks.shape
    eye = jnp.eye(G, dtype=blocks.dtype)
    return (eye[:, None, :, None] * blocks[:, :, None, :]).reshape(G * r, G * c)


def _s5_matrices(a_re, a_im, log_dt, b_re, b_im, c_re, c_im):
    ab_re, ab_im, bb_re, bb_im = _s5_discretise(
        a_re.astype(F32), a_im.astype(F32), log_dt.astype(F32), b_re.astype(F32), b_im.astype(F32))
    bmats, cmats = [], []
    for d in range(2):
        br = _block_diag(jnp.swapaxes(bb_re[d], 1, 2))
        bi = _block_diag(jnp.swapaxes(bb_im[d], 1, 2))
        bmats.append(jnp.concatenate([br, bi], axis=1))
        cr = _block_diag(jnp.swapaxes(c_re[d].astype(F32), 1, 2))
        ci = _block_diag(jnp.swapaxes(c_im[d].astype(F32), 1, 2))
        cmats.append(jnp.concatenate([cr, -ci], axis=0))
    abar = jnp.stack([ab_re.reshape(2, S5_COLS), ab_im.reshape(2, S5_COLS)], axis=1)
    return jnp.stack(bmats).astype(BF16), jnp.stack(cmats).astype(BF16), abar


def _s5(rest, bmat, cmat, abar, d_skip, w_glu, b_glu, ctx_len):
    B, S, _ = rest.shape
    assert B % SUBLANES == 0 and ctx_len % S5_TT == 0 and S % S5_TT == 0
    ng = B // SUBLANES
    n_tiles = S // S5_TT
    n_ctx_tiles = ctx_len // S5_TT
    u = rest[:, :, 8 * GROUP_W:9 * GROUP_W].reshape(ng, SUBLANES, S, GROUP_W).transpose(0, 2, 1, 3)

    def fwd_idx(g, i):
        return (g, i, 0, 0)

    def bwd_idx(g, i):
        return (g, _s5_back_tile(i, n_ctx_tiles, n_tiles), 0, 0)

    def whole(arr):
        return pl.BlockSpec(arr.shape, lambda g, i: (0,) * arr.ndim)

    tile_shape = (1, S5_TT, SUBLANES, GROUP_W)
    xs_shape = pltpu.VMEM((S5_TT * SUBLANES, 2 * S5_COLS), F32)
    y_sds = jax.ShapeDtypeStruct((ng, S, SUBLANES, GROUP_W), F32)
    d_row = d_skip.astype(F32).reshape(1, GROUP_W)
    yf, yb = pl.pallas_call(
        _s5_scan_kernel,
        grid=(ng, n_tiles),
        in_specs=[pl.BlockSpec(tile_shape, fwd_idx), pl.BlockSpec(tile_shape, bwd_idx),
                  whole(bmat), whole(cmat), whole(abar), whole(d_row)],
        out_specs=[pl.BlockSpec(tile_shape, fwd_idx), pl.BlockSpec(tile_shape, bwd_idx)],
        out_shape=[y_sds, y_sds],
        scratch_shapes=[
            xs_shape, xs_shape,
            pltpu.VMEM((2, 2 * S5_COLS // LANES, SUBLANES, LANES), F32),
        ],
        compiler_params=_params("parallel", "arbitrary"),
        name="s5_scan",
    )(u, u, bmat, cmat, abar, d_row)

    w_b = w_glu.astype(BF16)
    b_row = b_glu.astype(F32).reshape(1, GROUP_W)
    fin_shape = (1, TILE, SUBLANES, GROUP_W)
    out = pl.pallas_call(
        _s5_finish_kernel,
        grid=(ng, S // TILE),
        in_specs=[pl.BlockSpec(fin_shape, fwd_idx), pl.BlockSpec(fin_shape, fwd_idx),
                  whole(w_b), whole(b_row)],
        out_specs=pl.BlockSpec(fin_shape, fwd_idx),
        out_shape=y_sds,
        compiler_params=_params("parallel", "arbitrary"),
        name="s5_finish",
    )(yf, yb, w_b, b_row)
    return out.transpose(0, 2, 1, 3).reshape(B, S, GROUP_W)


def _post_kernel(n_hidden_chunks, ctx_tiles_here, h_ref, a_ref, b_ref, c_ref, d_ref, modx_ref,
                 modc_ref, ng_ref, wo_ref, wi_ref, w2_ref, o_ref):
    nb, tile, d_model = h_ref.shape
    n_rows = nb * tile
    mod = _tile_mod(pl.program_id(1) < ctx_tiles_here, modx_ref, modc_ref)

    def flat(ref):
        return ref[...].reshape(n_rows, GROUP_W).astype(BF16)

    mix = (_dot(flat(a_ref), wo_ref[0:GROUP_W, :])
           + _dot(flat(b_ref), wo_ref[GROUP_W:2 * GROUP_W, :])
           + _dot(flat(c_ref), wo_ref[2 * GROUP_W:3 * GROUP_W, :])
           + _dot(flat(d_ref), wo_ref[3 * GROUP_W:4 * GROUP_W, :])).reshape(nb, tile, d_model)
    h = h_ref[...] + mod[:, 2:3, :] * (mix * _rms_scale(mix) * ng_ref[1:2, :])
    y = (h * _rms_scale(h) * ng_ref[2:3, :]) * (1.0 + mod[:, 4:5, :]) + mod[:, 3:4, :]
    yb = y.reshape(n_rows, d_model).astype(BF16)
    hidden = w2_ref.shape[0]
    hc = hidden // n_hidden_chunks
    ffn = jnp.zeros((n_rows, d_model), F32)
    for j in range(n_hidden_chunks):
        gate = _dot(yb, wi_ref[:, j * hc:(j + 1) * hc])
        up = _dot(yb, wi_ref[:, hidden + j * hc:hidden + (j + 1) * hc])
        ffn = ffn + _dot((_silu(gate) * up).astype(BF16), w2_ref[j * hc:(j + 1) * hc, :])
    ffn = ffn.reshape(nb, tile, d_model)
    o_ref[...] = h + mod[:, 5:6, :] * (ffn * _rms_scale(ffn) * ng_ref[3:4, :])


def _post(h, a, b, c, d, mod, norm_g, w_out, w_ffn_in, w_ffn_out, n_batch, n_ctx_tiles, want_ctx):
    B, S, D = h.shape
    off = 0 if want_ctx else n_ctx_tiles
    nt = S // TILE - off
    a_off = off if a.shape[1] == S else 0
    nb = POST_NB
    assert B % nb == 0

    def rows(o):
        return lambda bi, t: (bi, t + o, 0)

    def whole(arr):
        return pl.BlockSpec(arr.shape, lambda bi, t: (0,) * arr.ndim, pipeline_mode=pl.Buffered(1))

    mix_spec = lambda o: pl.BlockSpec((nb, TILE, GROUP_W), rows(o))
    return pl.pallas_call(
        functools.partial(_post_kernel, 2, n_ctx_tiles - off),
        grid=(B // nb, nt),
        in_specs=[
            pl.BlockSpec((nb, TILE, D), rows(off)),
            mix_spec(a_off), mix_spec(off), mix_spec(off), mix_spec(off),
            pl.BlockSpec((nb, 6, D), lambda bi, t: (bi, 0, 0)),
            pl.BlockSpec((1, 6, D), lambda bi, t: (n_batch, 0, 0)),
            whole(norm_g), whole(w_out), whole(w_ffn_in), whole(w_ffn_out),
        ],
        out_specs=pl.BlockSpec((nb, TILE, D), lambda bi, t: (bi, t, 0)),
        out_shape=jax.ShapeDtypeStruct((B, nt * TILE, D), F32),
        compiler_params=_params("parallel", "arbitrary"),
        name="post",
    )(h, a, b, c, d, mod, mod, norm_g, w_out, w_ffn_in, w_ffn_out)


def _rope_tables(n_rows, ctx_len):
    rows = jnp.broadcast_to(jnp.arange(n_rows, dtype=F32)[:, None], (n_rows, GRID_W)).reshape(-1)
    cols = jnp.broadcast_to(jnp.arange(GRID_W, dtype=F32)[None, :], (n_rows, GRID_W)).reshape(-1)
    n_freq = DA_HD // 4
    inv = ROPE_BASE ** (-jnp.arange(n_freq, dtype=F32) / n_freq)
    ang = jnp.concatenate([rows[:, None] * inv, cols[:, None] * inv], axis=-1)
    cos, sin = jnp.cos(ang), jnp.sin(ang)
    zero = jnp.zeros_like(sin)
    reps = GROUP_W // DA_HD

    def lanes(first, second, ctx_value):
        t = jnp.tile(jnp.concatenate([first, second], axis=-1), (1, reps))
        return jnp.concatenate([jnp.full((ctx_len, GROUP_W), ctx_value, F32), t], axis=0)

    return lanes(cos, cos, 1.0), lanes(-sin, zero, 0.0), lanes(zero, sin, 0.0)


def _deinterleave_qk(w_in):
    perm32 = jnp.concatenate([jnp.arange(0, DA_HD, 2), jnp.arange(1, DA_HD, 2)])
    perm = (jnp.arange(0, 2 * GROUP_W, DA_HD)[:, None] + perm32[None, :]).reshape(-1)
    cols = jnp.concatenate([perm, jnp.arange(2 * GROUP_W, w_in.shape[-1])])
    return w_in[..., cols]


def kernel(x, c, ctx, c_ctx, w_ada, b_ada, norm_g, w_in, w_out, da_lambda, da_subln, hg_lb, hg_norm, sc_w, sc_b, s5_a_re, s5_a_im, s5_log_dt, s5_b_re, s5_b_im, s5_c_re, s5_c_im, s5_d, s5_w_glu, s5_b_glu, w_ffn_in, w_ffn_out):
    B, T, D = x.shape
    ctx_len = ctx.shape[1]
    L = w_ada.shape[0]
    assert ctx_len % TILE == 0 and T % TILE == 0 and T % GRID_W == 0
    n_ctx_tiles = ctx_len // TILE

    bp = -(-(B + 1) // SUBLANES) * SUBLANES
    cvec = jnp.concatenate([c, c_ctx[None, :], jnp.zeros((bp - B - 1, D), c.dtype)], axis=0)
    mods = _modulation(cvec.astype(F32), w_ada, b_ada).reshape(L, bp, 6, D)

    cos, sin_a, sin_b = _rope_tables(T // GRID_W, ctx_len)
    lb = jnp.cumsum(jax.nn.softmax(hg_lb.astype(F32), axis=0), axis=0)
    lb = lb - lb[:1]
    w_in_b = _deinterleave_qk(w_in).astype(BF16)
    w_out_b = w_out.astype(BF16)
    w_ffn_in_b = w_ffn_in.astype(BF16)
    w_ffn_out_b = w_ffn_out.astype(BF16)

    h = jnp.concatenate([ctx, x], axis=1)
    for l in range(L):
        want_ctx = l < L - 1
        lam_init = 0.8 - 0.6 * math.exp(-0.3 * l)
        qkv, rest = _inproj(h, mods[l], norm_g[l, 0], w_in_b[l], cos, sin_a, sin_b, B, n_ctx_tiles)
        a = _attention(qkv, da_lambda[l], da_subln[l], lam_init, ctx_len, want_ctx)
        b = _hgrn(rest, lb[l], hg_norm[l])
        cc = _short_conv(rest, sc_w[l], sc_b[l], ctx_len)
        bmat, cmat, abar = _s5_matrices(s5_a_re[l], s5_a_im[l], s5_log_dt[l], s5_b_re[l],
                                        s5_b_im[l], s5_c_re[l], s5_c_im[l])
        dd = _s5(rest, bmat, cmat, abar, s5_d[l], s5_w_glu[l], s5_b_glu[l], ctx_len)
        h = _post(h, a, b, cc, dd, mods[l], norm_g[l].astype(F32), w_out_b[l], w_ffn_in_b[l],
                  w_ffn_out_b[l], B, n_ctx_tiles, want_ctx)
    return h
```

```python
import functools
import math

import jax
import jax.numpy as jnp
from jax import lax
from jax.experimental import pallas as pl
from jax.experimental.pallas import tpu as pltpu

F32 = jnp.float32
BF16 = jnp.bfloat16

EPS = 1e-6
GRID_W = 64
ROPE_BASE = 10000.0
GROUP_W = 256
DA_HEADS = 4
DA_HD = 32
HG_HEADS = 4
HG_HD = 64
HG_CHUNK = 16
HG_BLOCK = 64
S5_NGROUPS = 16
S5_GROUP = 16
S5_STATE = 64
S5_COLS = S5_NGROUPS * S5_STATE
LANES = 128
SUBLANES = 8
TILE = 256
INPROJ_NB = 4
POST_NB = 2
S5_TT = 128
VMEM_LIMIT = 56 * 1024 * 1024

_NT = (((1,), (1,)), ((), ()))


def _dot(a, b):
    return jnp.dot(a, b, preferred_element_type=F32)


def _dot_nt(a, b):
    return lax.dot_general(a, b, _NT, preferred_element_type=F32)


def _split2_dot_rhs(m, x):
    x1 = x.astype(BF16)
    x2 = (x - x1.astype(F32)).astype(BF16)
    return _dot(m, x1) + _dot(m, x2)


def _split2_dot(x, m):
    x1 = x.astype(BF16)
    x2 = (x - x1.astype(F32)).astype(BF16)
    return _dot(x1, m) + _dot(x2, m)


def _silu(x):
    return x * jax.nn.sigmoid(x)


def _rms_scale(x):
    return lax.rsqrt(jnp.mean(x * x, axis=-1, keepdims=True) + EPS)


def _group_mean_matrix(n, group):
    sh = int(math.log2(group))
    r = lax.broadcasted_iota(jnp.int32, (n, n), 0) >> sh
    c = lax.broadcasted_iota(jnp.int32, (n, n), 1) >> sh
    return jnp.where(r == c, 1.0 / group, 0.0).astype(BF16)


def _params(*sem):
    return pltpu.CompilerParams(dimension_semantics=sem, vmem_limit_bytes=VMEM_LIMIT)


def _mod_kernel(c_ref, w_ref, b_ref, o_ref):
    sc = _silu(c_ref[...])
    o_ref[0] = _dot(sc.astype(BF16), w_ref[0].astype(BF16)) + b_ref[0]


def _modulation(cvec, w_ada, b_ada):
    L, D, N = w_ada.shape
    Bp = cvec.shape[0]
    tn = 1536
    return pl.pallas_call(
        _mod_kernel,
        grid=(L, N // tn),
        in_specs=[
            pl.BlockSpec((Bp, D), lambda l, j: (0, 0)),
            pl.BlockSpec((1, D, tn), lambda l, j: (l, 0, j)),
            pl.BlockSpec((1, 1, tn), lambda l, j: (l, 0, j)),
        ],
        out_specs=pl.BlockSpec((1, Bp, tn), lambda l, j: (l, 0, j)),
        out_shape=jax.ShapeDtypeStruct((L, Bp, N), F32),
        compiler_params=_params("arbitrary", "arbitrary"),
        name="modulation",
    )(cvec, w_ada, b_ada.reshape(L, 1, N))


def _rope(x, cos, sin_a, sin_b):
    return (x * cos + pltpu.roll(x, GROUP_W - DA_HD // 2, 1) * sin_a
            + pltpu.roll(x, DA_HD // 2, 1) * sin_b)


def _tile_mod(is_ctx, modx_ref, modc_ref):
    return jnp.where(is_ctx, modc_ref[...], modx_ref[...])


def _seq_specs(h_parts, block, n_ctx_tiles, first_tile=0):
    _, _, lat_off = h_parts

    def ctx_idx(b, t):
        return (b, jnp.minimum(t + first_tile, n_ctx_tiles - 1), 0)

    def lat_idx(b, t):
        return (b, jnp.maximum(t + first_tile, n_ctx_tiles) - n_ctx_tiles + lat_off, 0)

    return [pl.BlockSpec(block, ctx_idx), pl.BlockSpec(block, lat_idx)]


def _inproj_kernel(n_ctx_tiles, hc_ref, hx_ref, modx_ref, modc_ref, g_ref, w_ref, cos_ref, sa_ref,
                   sb_ref, qkv_ref, rest_ref, u5_ref):
    nb, tile, d_model = hx_ref.shape
    is_ctx = pl.program_id(1) < n_ctx_tiles
    mod = _tile_mod(is_ctx, modx_ref, modc_ref)
    h = jnp.where(is_ctx, hc_ref[...], hx_ref[...])
    hn = h * _rms_scale(h) * g_ref[...]
    y = hn * (1.0 + mod[:, 1:2, :]) + mod[:, 0:1, :]
    p = _dot(y.reshape(nb * tile, d_model).astype(BF16), w_ref[...])
    cos, sa, sb = cos_ref[...], sa_ref[...], sb_ref[...]
    for j in range(nb):
        pj = p[j * tile:(j + 1) * tile]
        q = _rope(pj[:, 0:GROUP_W], cos, sa, sb) * (DA_HD ** -0.5 * math.log2(math.e))
        k = _rope(pj[:, GROUP_W:2 * GROUP_W], cos, sa, sb)
        qkv_ref[j, :, 0:GROUP_W] = q.astype(BF16)
        qkv_ref[j, :, GROUP_W:2 * GROUP_W] = k.astype(BF16)
        qkv_ref[j, :, 2 * GROUP_W:3 * GROUP_W] = pj[:, 2 * GROUP_W:3 * GROUP_W].astype(BF16)
        rest_ref[j] = pj[:, 3 * GROUP_W:11 * GROUP_W]
        u5_ref[j] = pj[:, 11 * GROUP_W:]


def _inproj(h_parts, S, mod, g, w, cos, sa, sb, n_batch, n_ctx_tiles):
    B, _, D = h_parts[1].shape
    N = w.shape[1]
    nt = S // TILE
    nb = INPROJ_NB
    assert B % nb == 0
    tab = pl.BlockSpec((TILE, GROUP_W), lambda b, t: (t, 0))
    return pl.pallas_call(
        functools.partial(_inproj_kernel, n_ctx_tiles),
        grid=(B // nb, nt),
        in_specs=_seq_specs(h_parts, (nb, TILE, D), n_ctx_tiles) + [
            pl.BlockSpec((nb, 6, D), lambda b, t: (b, 0, 0)),
            pl.BlockSpec((1, 6, D), lambda b, t: (n_batch, 0, 0)),
            pl.BlockSpec((1, D), lambda b, t: (0, 0)),
            pl.BlockSpec((D, N), lambda b, t: (0, 0)),
            tab, tab, tab,
        ],
        out_specs=[
            pl.BlockSpec((nb, TILE, 3 * GROUP_W), lambda b, t: (b, t, 0)),
            pl.BlockSpec((nb, TILE, 8 * GROUP_W), lambda b, t: (b, t, 0)),
            pl.BlockSpec((nb, TILE, GROUP_W), lambda b, t: (b, t, 0)),
        ],
        out_shape=[
            jax.ShapeDtypeStruct((B, S, 3 * GROUP_W), BF16),
            jax.ShapeDtypeStruct((B, S, 8 * GROUP_W), F32),
            jax.ShapeDtypeStruct((B, S, GROUP_W), F32),
        ],
        compiler_params=_params("parallel", "arbitrary"),
        name="inproj",
    )(h_parts[0], h_parts[1], mod, mod, g.reshape(1, D), w, cos, sa, sb)


def _attn_kernel(lam_init, ctx_len, q_off, q_ref, k_ref, v_ref, lamv_ref, g_ref, o_ref,
                 sc_ref, e_ref, va_ref, vb_ref):
    tq = q_ref.shape[1]
    n_keys = k_ref.shape[1]
    lv = lamv_ref[...]
    lam = (jnp.exp(jnp.sum(lv[0:1] * lv[1:2], axis=-1, keepdims=True))
           - jnp.exp(jnp.sum(lv[2:3] * lv[3:4], axis=-1, keepdims=True)) + lam_init)
    lane = lax.broadcasted_iota(jnp.int32, (1, GROUP_W), 1)
    lane_map = lane >> 5
    lane_head = lane >> 6

    @pl.when(pl.program_id(1) == 0)
    def _():
        v = v_ref[0]
        one = jnp.ones_like(v)
        va_ref[...] = jnp.where(lane == GROUP_W - 1, one, v)
        vb_ref[...] = jnp.where(lane == 0, one, v)

    q = q_ref[0]
    q8 = jnp.concatenate(
        [jnp.where(lane_map == j, q, jnp.zeros_like(q)) for j in range(2 * DA_HEADS)], axis=0)

    def attend(nk):
        sc_ref[:, 0:nk] = _dot_nt(q8, k_ref[0, 0:nk, :])
        acc = jnp.zeros((tq, GROUP_W), F32)
        for h in range(DA_HEADS):
            for m in range(2):
                rows = slice((2 * h + m) * tq, (2 * h + m + 1) * tq)
                s = sc_ref[rows, 0:nk]
                e_ref[rows, 0:nk] = jnp.exp2(s - jnp.max(s, axis=-1, keepdims=True)).astype(BF16)
            vv = va_ref if h < DA_HEADS - 1 else vb_ref
            sum_col = GROUP_W - 1 if h < DA_HEADS - 1 else 0
            o2 = _dot(e_ref[2 * h * tq:(2 * h + 2) * tq, 0:nk], vv[0:nk, :])
            o0, o1 = o2[0:tq], o2[tq:2 * tq]
            w0 = 1.0 / o0[:, sum_col:sum_col + 1]
            w1 = lam / o1[:, sum_col:sum_col + 1]
            acc = acc + jnp.where(lane_head == h, o0 * w0 - o1 * w1, 0.0)
        ms = _split2_dot(acc * acc, _group_mean_matrix(GROUP_W, 2 * DA_HD))
        o_ref[0] = acc * lax.rsqrt(ms + EPS) * g_ref[...] * (1.0 - lam_init)

    if q_off == 0 and ctx_len == tq:
        @pl.when(pl.program_id(1) == 0)
        def _():
            attend(ctx_len)

        @pl.when(pl.program_id(1) != 0)
        def _():
            attend(n_keys)
    else:
        attend(n_keys)


def _attention(qkv, lam_vecs, subln_g, lam_init, ctx_len, want_ctx):
    B, S, _ = qkv.shape
    tq = TILE
    q_off = 0 if want_ctx else ctx_len // tq
    nq = S // tq - q_off
    g = jnp.tile(subln_g.astype(F32), DA_HEADS).reshape(1, GROUP_W)
    return pl.pallas_call(
        functools.partial(_attn_kernel, lam_init, ctx_len, q_off),
        grid=(B, nq),
        in_specs=[
            pl.BlockSpec((1, tq, GROUP_W), lambda b, i: (b, i + q_off, 0)),
            pl.BlockSpec((1, S, GROUP_W), lambda b, i: (b, 0, 1)),
            pl.BlockSpec((1, S, GROUP_W), lambda b, i: (b, 0, 2)),
            pl.BlockSpec((4, DA_HD), lambda b, i: (0, 0)),
            pl.BlockSpec((1, GROUP_W), lambda b, i: (0, 0)),
        ],
        out_specs=pl.BlockSpec((1, tq, GROUP_W), lambda b, i: (b, i, 0)),
        out_shape=jax.ShapeDtypeStruct((B, nq * tq, GROUP_W), F32),
        scratch_shapes=[
            pltpu.VMEM((2 * DA_HEADS * tq, S), F32),
            pltpu.VMEM((2 * DA_HEADS * tq, S), BF16),
            pltpu.VMEM((S, GROUP_W), BF16),
            pltpu.VMEM((S, GROUP_W), BF16),
        ],
        compiler_params=_params("parallel", "arbitrary"),
        name="diff_attention",
    )(qkv, qkv, qkv, lam_vecs.astype(F32), g)


def _hgrn_kernel(n_tiles, q_ref, ff_ref, fb_ref, i_ref, g_ref, lb_ref, lbt_ref, ng_ref, o_ref,
                 od_ref, st2_ref, u2_ref, oc2_ref):
    n_blocks = TILE // HG_BLOCK
    chunks_per_block = HG_BLOCK // HG_CHUNK
    row = lax.broadcasted_iota(jnp.int32, (TILE, TILE), 0)
    col = lax.broadcasted_iota(jnp.int32, (TILE, TILE), 1)
    same_chunk = (row >> 4) == (col >> 4)
    same_block = (row >> 6) == (col >> 6)
    lane = lax.broadcasted_iota(jnp.int32, (1, GROUP_W), 1)
    lane_head = lane >> 6
    lane_block = lane >> 6
    lane_pos = (lane >> 4) & (chunks_per_block - 1)
    gmean = _group_mean_matrix(GROUP_W, HG_HD)

    def ones(mask):
        return jnp.where(mask, 1.0, 0.0).astype(BF16)

    consts = []
    for d in range(2):
        dist = ((row >> 4) - (col >> 4)) if d == 0 else ((col >> 4) - (row >> 4))
        causal = same_chunk & ((col <= row) if d == 0 else (col >= row))
        allowed = same_block & ((dist >= 1) | causal)
        row_mats = [ones(causal), ones(same_block & (dist >= 1))]
        col_mats = [ones(same_chunk & ((row <= col) if d == 0 else (row >= col))), ones(same_chunk),
                    ones(same_block & (dist >= 1)), ones(same_block)]
        for n in range(1, chunks_per_block - 1):
            col_mats.append(ones(same_block & (dist >= 1) & (dist <= n)))
        consts.append((allowed, jnp.concatenate(row_mats, axis=0), jnp.concatenate(col_mats, axis=1)))

    st2_ref[...] = jnp.zeros_like(st2_ref)

    def stack_heads(x):
        zero = jnp.zeros_like(x)
        return jnp.concatenate(
            [jnp.where(lane_head == h, x, zero).astype(BF16) for h in range(HG_HEADS)], axis=0)

    def tile_body(ti, carry):
        for d in range(2):
            allowed, row_mat, col_mat = consts[d]
            f_ref = ff_ref if d == 0 else fb_ref
            lb = lb_ref[d:d + 1, :]
            lb_t = lbt_ref[:, d:d + 1]
            st_ref, u_ref, oc_ref = st2_ref.at[d], u2_ref.at[d], oc2_ref.at[d]
            if d == 0:
                t = ti
            else:
                t = jnp.where(ti == 0, 0, n_tiles - ti)
            rows = pl.ds(pl.multiple_of(t * TILE, TILE), TILE)
            fr = f_ref[0, rows, :]
            lf = jnp.log(lb + (1.0 - lb) * jax.nn.sigmoid(fr))
            sums = _split2_dot_rhs(row_mat, lf)
            qd = _silu(q_ref[0, rows, :]) * jnp.exp(sums[0:TILE])
            q_state = qd * jnp.exp(sums[TILE:2 * TILE])
            f_t = lb_t + (1.0 - lb_t) * jax.nn.sigmoid(fr.T)
            kk_t = 1.0 - f_t
            sums_t = _split2_dot(jnp.log(f_t), col_mat)
            g_t, gt_t = sums_t[:, 0:TILE], sums_t[:, TILE:2 * TILE]
            kend_t = kk_t * jnp.exp(gt_t - g_t)
            kblk_t = kend_t * jnp.exp(sums_t[:, 2 * TILE:3 * TILE])
            dec_t = jnp.exp(sums_t[:, 3 * TILE:4 * TILE])
            k_var = [kk_t * jnp.exp(-g_t), kend_t]
            for n in range(2, chunks_per_block):
                k_var.append(kend_t * jnp.exp(sums_t[:, (n + 2) * TILE:(n + 3) * TILE]))
            vb = i_ref[0, rows, :].astype(BF16)

            qd4 = stack_heads(qd)
            x_pos = []
            for p in range(chunks_per_block):
                keys = jnp.zeros_like(kend_t)
                for n in range(chunks_per_block):
                    pc = p - n if d == 0 else p + n
                    if 0 <= pc < chunks_per_block:
                        keys = jnp.where(lane_pos == pc, k_var[n], keys)
                q_p = jnp.concatenate(
                    [qd4[h * TILE + b * HG_BLOCK + p * HG_CHUNK:h * TILE + b * HG_BLOCK + (p + 1) * HG_CHUNK]
                     for h in range(HG_HEADS) for b in range(n_blocks)], axis=0)
                x_pos.append(_dot(q_p, keys.astype(BF16)))
            a_heads = []
            for h in range(HG_HEADS):
                a = jnp.concatenate(
                    [x_pos[p][(h * n_blocks + b) * HG_CHUNK:(h * n_blocks + b + 1) * HG_CHUNK]
                     for b in range(n_blocks) for p in range(chunks_per_block)], axis=0)
                a_heads.append(jnp.where(allowed, a, 0.0).astype(BF16))
            zb = jnp.zeros_like(vb)
            v4 = jnp.concatenate(
                [jnp.where(lane_head == h, vb, zb) for h in range(HG_HEADS)], axis=0)
            o_tile = _dot(jnp.concatenate(a_heads, axis=1), v4)

            zero = jnp.zeros_like(kblk_t)
            k_stack = jnp.concatenate(
                [jnp.where(lane_block == b, kblk_t, zero).astype(BF16) for b in range(n_blocks)], axis=0)
            u_ref[...] = _dot(k_stack, vb)

            qs4 = stack_heads(q_state)
            order = range(n_blocks) if d == 0 else range(n_blocks - 1, -1, -1)
            for b in order:
                st = st_ref[...]
                q4b = jnp.concatenate(
                    [qs4[h * TILE + b * HG_BLOCK:h * TILE + (b + 1) * HG_BLOCK] for h in range(HG_HEADS)],
                    axis=0)
                ob4 = _dot(q4b, st.astype(BF16))
                ob = jnp.zeros((HG_BLOCK, GROUP_W), F32)
                for h in range(HG_HEADS):
                    ob = ob + jnp.where(lane_head == h, ob4[h * HG_BLOCK:(h + 1) * HG_BLOCK], 0.0)
                oc_ref[b * HG_BLOCK:(b + 1) * HG_BLOCK, :] = ob
                st_ref[...] = (st * dec_t[:, b * HG_BLOCK:b * HG_BLOCK + 1]
                               + u_ref[b * TILE:(b + 1) * TILE, :])
            od_ref[d, rows, :] = o_tile + oc_ref[...]
        return carry

    lax.fori_loop(0, n_tiles, tile_body, 0)

    def finish_body(t, carry):
        rows = pl.ds(pl.multiple_of(t * TILE, TILE), TILE)
        tot = od_ref[0, rows, :] + od_ref[1, rows, :]
        ms = _split2_dot(tot * tot, gmean)
        o_ref[0, rows, :] = (tot * lax.rsqrt(ms + EPS) * ng_ref[...] * _silu(g_ref[0, rows, :]))
        return carry

    lax.fori_loop(0, n_tiles, finish_body, 0)


def _hgrn(rest, lb, norm_g):
    B, S, _ = rest.shape
    n_tiles = S // TILE

    def part(j):
        return pl.BlockSpec((1, S, GROUP_W), lambda b: (b, 0, j))

    ng = jnp.tile(norm_g.astype(F32), HG_HEADS).reshape(1, GROUP_W)
    return pl.pallas_call(
        functools.partial(_hgrn_kernel, n_tiles),
        grid=(B,),
        in_specs=[part(0), part(1), part(2), part(3), part(4),
                  pl.BlockSpec((2, GROUP_W), lambda b: (0, 0)),
                  pl.BlockSpec((GROUP_W, 2), lambda b: (0, 0)),
                  pl.BlockSpec((1, GROUP_W), lambda b: (0, 0))],
        out_specs=pl.BlockSpec((1, S, GROUP_W), lambda b: (b, 0, 0)),
        out_shape=jax.ShapeDtypeStruct((B, S, GROUP_W), F32),
        scratch_shapes=[
            pltpu.VMEM((2, S, GROUP_W), F32),
            pltpu.VMEM((2, GROUP_W, GROUP_W), F32),
            pltpu.VMEM((2, TILE // HG_BLOCK * TILE, GROUP_W), F32),
            pltpu.VMEM((2, TILE, GROUP_W), F32),
        ],
        compiler_params=_params("parallel"),
        name="hgrn2",
    )(rest, rest, rest, rest, rest, lb, lb.T, ng)


def _conv_kernel(ctx_len, b_ref, c_ref, u_ref, w_ref, bias_ref, o_ref):
    S = b_ref.shape[1]
    v = c_ref[0] * u_ref[0]
    row = lax.broadcasted_iota(jnp.int32, (S, 1), 0)
    prev = jnp.where((row == 0) | (row == ctx_len), 0.0, pltpu.roll(v, 1, 0))
    nxt = jnp.where((row == ctx_len - 1) | (row == S - 1), 0.0, pltpu.roll(v, S - 1, 0))
    y = w_ref[0:1, :] * prev + w_ref[1:2, :] * v + w_ref[2:3, :] * nxt
    o_ref[0] = b_ref[0] * (y + bias_ref[...])


def _short_conv(rest, w, bias, ctx_len):
    B, S, _ = rest.shape

    def part(j):
        return pl.BlockSpec((1, S, GROUP_W), lambda b: (b, 0, j))

    return pl.pallas_call(
        functools.partial(_conv_kernel, ctx_len),
        grid=(B,),
        in_specs=[part(5), part(6), part(7),
                  pl.BlockSpec((3, GROUP_W), lambda b: (0, 0)),
                  pl.BlockSpec((1, GROUP_W), lambda b: (0, 0))],
        out_specs=pl.BlockSpec((1, S, GROUP_W), lambda b: (b, 0, 0)),
        out_shape=jax.ShapeDtypeStruct((B, S, GROUP_W), F32),
        compiler_params=_params("parallel"),
        name="short_conv",
    )(rest, rest, rest, w.astype(F32), bias.astype(F32).reshape(1, GROUP_W))


def _s5_back_tile(i, n_ctx_tiles, n_tiles):
    return jnp.where(i < n_ctx_tiles, n_ctx_tiles - 1 - i, n_tiles - 1 - (i - n_ctx_tiles))


def _s5_scan_kernel(uf_ref, ub_ref, bmat_ref, cmat_ref, a_ref, d_ref, yf_ref, yb_ref,
                    xsf_ref, xsb_ref, st_ref):
    tt = uf_ref.shape[1]
    n_rows = tt * SUBLANES
    n_strips = S5_COLS // LANES

    @pl.when(pl.program_id(1) == 0)
    def _():
        st_ref[...] = jnp.zeros_like(st_ref)

    def lanes(j):
        return slice(j * LANES, (j + 1) * LANES)

    us = []
    for d, (u_ref, xs_ref) in enumerate(((uf_ref, xsf_ref), (ub_ref, xsb_ref))):
        u = u_ref[0].reshape(n_rows, GROUP_W)
        xs_ref[...] = _dot(u.astype(BF16), bmat_ref[d])
        us.append(u)

    for d, (xs_ref, y_ref) in enumerate(((xsf_ref, yf_ref), (xsb_ref, yb_ref))):
        a_re = [jnp.broadcast_to(a_ref[d, 0:1, lanes(j)], (SUBLANES, LANES)) for j in range(n_strips)]
        a_im = [jnp.broadcast_to(a_ref[d, 1:2, lanes(j)], (SUBLANES, LANES)) for j in range(n_strips)]
        state = [st_ref[d, k] for k in range(2 * n_strips)]
        for s in range(tt):
            t = s if d == 0 else tt - 1 - s
            r = slice(t * SUBLANES, (t + 1) * SUBLANES)
            for j in range(n_strips):
                xr, xi = state[2 * j], state[2 * j + 1]
                nr = a_re[j] * xr - a_im[j] * xi + xs_ref[r, lanes(j)]
                ni = a_re[j] * xi + a_im[j] * xr + xs_ref[r, lanes(n_strips + j)]
                xs_ref[r, lanes(j)] = nr
                xs_ref[r, lanes(n_strips + j)] = ni
                state[2 * j], state[2 * j + 1] = nr, ni
        for k in range(2 * n_strips):
            st_ref[d, k] = state[k]
        y = _dot(xs_ref[...].astype(BF16), cmat_ref[d])
        if d == 0:
            y = y + d_ref[...] * us[0]
        y_ref[0] = y.reshape(tt, SUBLANES, GROUP_W)


def _s5_finish_kernel(yf_ref, yb_ref, wg_ref, bg_ref, o_ref):
    _, tt, nb, width = yf_ref.shape
    y = jax.nn.gelu(yf_ref[0].reshape(tt * nb, width) + yb_ref[0].reshape(tt * nb, width))
    gate = jax.nn.sigmoid(_dot(y.astype(BF16), wg_ref[...]) + bg_ref[...])
    o_ref[0] = (y * gate).reshape(tt, nb, width)


def _s5_discretise(a_re, a_im, log_dt, b_re, b_im):
    dt = jnp.exp(log_dt)[..., None]
    mag = jnp.exp(dt * a_re)
    ab_re, ab_im = mag * jnp.cos(dt * a_im), mag * jnp.sin(dt * a_im)
    den = a_re * a_re + a_im * a_im
    nr = ab_re - 1.0
    f_re = (nr * a_re + ab_im * a_im) / den
    f_im = (ab_im * a_re - nr * a_im) / den
    bb_re = f_re[..., None] * b_re - f_im[..., None] * b_im
    bb_im = f_re[..., None] * b_im + f_im[..., None] * b_re
    return ab_re, ab_im, bb_re, bb_im


def _block_diag(blocks):
    G, r, c = blocks.shape
    eye = jnp.eye(G, dtype=blocks.dtype)
    return (eye[:, None, :, None] * blocks[:, :, None, :]).reshape(G * r, G * c)


def _s5_matrices(a_re, a_im, log_dt, b_re, b_im, c_re, c_im):
    ab_re, ab_im, bb_re, bb_im = _s5_discretise(
        a_re.astype(F32), a_im.astype(F32), log_dt.astype(F32), b_re.astype(F32), b_im.astype(F32))
    bmats, cmats = [], []
    for d in range(2):
        br = _block_diag(jnp.swapaxes(bb_re[d], 1, 2))
        bi = _block_diag(jnp.swapaxes(bb_im[d], 1, 2))
        bmats.append(jnp.concatenate([br, bi], axis=1))
        cr = _block_diag(jnp.swapaxes(c_re[d].astype(F32), 1, 2))
        ci = _block_diag(jnp.swapaxes(c_im[d].astype(F32), 1, 2))
        cmats.append(jnp.concatenate([cr, -ci], axis=0))
    abar = jnp.stack([ab_re.reshape(2, S5_COLS), ab_im.reshape(2, S5_COLS)], axis=1)
    return jnp.stack(bmats).astype(BF16), jnp.stack(cmats).astype(BF16), abar


def _s5(u5, bmat, cmat, abar, d_skip, w_glu, b_glu, ctx_len):
    B, S, _ = u5.shape
    assert B % SUBLANES == 0 and ctx_len % S5_TT == 0 and S % S5_TT == 0
    ng = B // SUBLANES
    n_tiles = S // S5_TT
    n_ctx_tiles = ctx_len // S5_TT
    u = u5.reshape(ng, SUBLANES, S, GROUP_W).transpose(0, 2, 1, 3)

    def fwd_idx(g, i):
        return (g, i, 0, 0)

    def bwd_idx(g, i):
        return (g, _s5_back_tile(i, n_ctx_tiles, n_tiles), 0, 0)

    def whole(arr):
        return pl.BlockSpec(arr.shape, lambda g, i: (0,) * arr.ndim)

    tile_shape = (1, S5_TT, SUBLANES, GROUP_W)
    xs_shape = pltpu.VMEM((S5_TT * SUBLANES, 2 * S5_COLS), F32)
    y_sds = jax.ShapeDtypeStruct((ng, S, SUBLANES, GROUP_W), F32)
    d_row = d_skip.astype(F32).reshape(1, GROUP_W)
    yf, yb = pl.pallas_call(
        _s5_scan_kernel,
        grid=(ng, n_tiles),
        in_specs=[pl.BlockSpec(tile_shape, fwd_idx), pl.BlockSpec(tile_shape, bwd_idx),
                  whole(bmat), whole(cmat), whole(abar), whole(d_row)],
        out_specs=[pl.BlockSpec(tile_shape, fwd_idx), pl.BlockSpec(tile_shape, bwd_idx)],
        out_shape=[y_sds, y_sds],
        scratch_shapes=[
            xs_shape, xs_shape,
            pltpu.VMEM((2, 2 * S5_COLS // LANES, SUBLANES, LANES), F32),
        ],
        compiler_params=_params("parallel", "arbitrary"),
        name="s5_scan",
    )(u, u, bmat, cmat, abar, d_row)

    w_b = w_glu.astype(BF16)
    b_row = b_glu.astype(F32).reshape(1, GROUP_W)
    fin_shape = (1, TILE, SUBLANES, GROUP_W)
    out = pl.pallas_call(
        _s5_finish_kernel,
        grid=(ng, S // TILE),
        in_specs=[pl.BlockSpec(fin_shape, fwd_idx), pl.BlockSpec(fin_shape, fwd_idx),
                  whole(w_b), whole(b_row)],
        out_specs=pl.BlockSpec(fin_shape, fwd_idx),
        out_shape=y_sds,
        compiler_params=_params("parallel", "arbitrary"),
        name="s5_finish",
    )(yf, yb, w_b, b_row)
    return out.transpose(0, 2, 1, 3).reshape(B, S, GROUP_W)


def _post_kernel(n_hidden_chunks, ctx_tiles_here, hc_ref, hx_ref, a_ref, b_ref, c_ref, d_ref,
                 modx_ref, modc_ref, ng_ref, wo_ref, wi_ref, w2_ref, o_ref):
    nb, tile, d_model = hx_ref.shape
    n_rows = nb * tile
    is_ctx = pl.program_id(1) < ctx_tiles_here
    mod = _tile_mod(is_ctx, modx_ref, modc_ref)

    def flat(ref):
        return ref[...].reshape(n_rows, GROUP_W).astype(BF16)

    mix = (_dot(flat(a_ref), wo_ref[0:GROUP_W, :])
           + _dot(flat(b_ref), wo_ref[GROUP_W:2 * GROUP_W, :])
           + _dot(flat(c_ref), wo_ref[2 * GROUP_W:3 * GROUP_W, :])
           + _dot(flat(d_ref), wo_ref[3 * GROUP_W:4 * GROUP_W, :])).reshape(nb, tile, d_model)
    h = (jnp.where(is_ctx, hc_ref[...], hx_ref[...])
         + mod[:, 2:3, :] * (mix * _rms_scale(mix) * ng_ref[1:2, :]))
    y = (h * _rms_scale(h) * ng_ref[2:3, :]) * (1.0 + mod[:, 4:5, :]) + mod[:, 3:4, :]
    yb = y.reshape(n_rows, d_model).astype(BF16)
    hidden = w2_ref.shape[0]
    hc = hidden // n_hidden_chunks
    ffn = jnp.zeros((n_rows, d_model), F32)
    for j in range(n_hidden_chunks):
        gate = _dot(yb, wi_ref[:, j * hc:(j + 1) * hc])
        up = _dot(yb, wi_ref[:, hidden + j * hc:hidden + (j + 1) * hc])
        ffn = ffn + _dot((_silu(gate) * up).astype(BF16), w2_ref[j * hc:(j + 1) * hc, :])
    ffn = ffn.reshape(nb, tile, d_model)
    o_ref[...] = h + mod[:, 5:6, :] * (ffn * _rms_scale(ffn) * ng_ref[3:4, :])


def _post(h_parts, S, a, b, c, d, mod, norm_g, w_out, w_ffn_in, w_ffn_out, n_batch, n_ctx_tiles,
          want_ctx):
    B, _, D = h_parts[1].shape
    off = 0 if want_ctx else n_ctx_tiles
    nt = S // TILE - off
    a_off = off if a.shape[1] == S else 0
    nb = POST_NB
    assert B % nb == 0

    def rows(o):
        return lambda bi, t: (bi, t + o, 0)

    def whole(arr):
        return pl.BlockSpec(arr.shape, lambda bi, t: (0,) * arr.ndim, pipeline_mode=pl.Buffered(1))

    mix_spec = lambda o: pl.BlockSpec((nb, TILE, GROUP_W), rows(o))
    return pl.pallas_call(
        functools.partial(_post_kernel, 2, n_ctx_tiles - off),
        grid=(B // nb, nt),
        in_specs=_seq_specs(h_parts, (nb, TILE, D), n_ctx_tiles, first_tile=off) + [
            mix_spec(a_off), mix_spec(off), mix_spec(off), mix_spec(off),
            pl.BlockSpec((nb, 6, D), lambda bi, t: (bi, 0, 0)),
            pl.BlockSpec((1, 6, D), lambda bi, t: (n_batch, 0, 0)),
            whole(norm_g), whole(w_out), whole(w_ffn_in), whole(w_ffn_out),
        ],
        out_specs=pl.BlockSpec((nb, TILE, D), lambda bi, t: (bi, t, 0)),
        out_shape=jax.ShapeDtypeStruct((B, nt * TILE, D), F32),
        compiler_params=_params("parallel", "arbitrary"),
        name="post",
    )(h_parts[0], h_parts[1], a, b, c, d, mod, mod, norm_g, w_out, w_ffn_in, w_ffn_out)


def _rope_tables(n_rows, ctx_len):
    rows = jnp.broadcast_to(jnp.arange(n_rows, dtype=F32)[:, None], (n_rows, GRID_W)).reshape(-1)
    cols = jnp.broadcast_to(jnp.arange(GRID_W, dtype=F32)[None, :], (n_rows, GRID_W)).reshape(-1)
    n_freq = DA_HD // 4
    inv = ROPE_BASE ** (-jnp.arange(n_freq, dtype=F32) / n_freq)
    ang = jnp.concatenate([rows[:, None] * inv, cols[:, None] * inv], axis=-1)
    cos, sin = jnp.cos(ang), jnp.sin(ang)
    zero = jnp.zeros_like(sin)
    reps = GROUP_W // DA_HD

    def lanes(first, second, ctx_value):
        t = jnp.tile(jnp.concatenate([first, second], axis=-1), (1, reps))
        return jnp.concatenate([jnp.full((ctx_len, GROUP_W), ctx_value, F32), t], axis=0)

    return lanes(cos, cos, 1.0), lanes(-sin, zero, 0.0), lanes(zero, sin, 0.0)


def _deinterleave_qk(w_in):
    perm32 = jnp.concatenate([jnp.arange(0, DA_HD, 2), jnp.arange(1, DA_HD, 2)])
    perm = (jnp.arange(0, 2 * GROUP_W, DA_HD)[:, None] + perm32[None, :]).reshape(-1)
    cols = jnp.concatenate([perm, jnp.arange(2 * GROUP_W, w_in.shape[-1])])
    return w_in[..., cols]


def kernel(x, c, ctx, c_ctx, w_ada, b_ada, norm_g, w_in, w_out, da_lambda, da_subln, hg_lb, hg_norm, sc_w, sc_b, s5_a_re, s5_a_im, s5_log_dt, s5_b_re, s5_b_im, s5_c_re, s5_c_im, s5_d, s5_w_glu, s5_b_glu, w_ffn_in, w_ffn_out):
    B, T, D = x.shape
    ctx_len = ctx.shape[1]
    L = w_ada.shape[0]
    assert ctx_len % TILE == 0 and T % TILE == 0 and T % GRID_W == 0
    n_ctx_tiles = ctx_len // TILE

    bp = -(-(B + 1) // SUBLANES) * SUBLANES
    cvec = jnp.concatenate([c, c_ctx[None, :], jnp.zeros((bp - B - 1, D), c.dtype)], axis=0)
    mods = _modulation(cvec.astype(F32), w_ada, b_ada).reshape(L, bp, 6, D)

    cos, sin_a, sin_b = _rope_tables(T // GRID_W, ctx_len)
    lb = jnp.cumsum(jax.nn.softmax(hg_lb.astype(F32), axis=0), axis=0)
    lb = lb - lb[:1]
    w_in_b = _deinterleave_qk(w_in).astype(BF16)
    w_out_b = w_out.astype(BF16)
    w_ffn_in_b = w_ffn_in.astype(BF16)
    w_ffn_out_b = w_ffn_out.astype(BF16)

    S = ctx_len + T
    h_parts = (ctx, x, 0)
    for l in range(L):
        want_ctx = l < L - 1
        lam_init = 0.8 - 0.6 * math.exp(-0.3 * l)
        qkv, rest, u5 = _inproj(h_parts, S, mods[l], norm_g[l, 0], w_in_b[l], cos, sin_a, sin_b, B,
                                n_ctx_tiles)
        a = _attention(qkv, da_lambda[l], da_subln[l], lam_init, ctx_len, want_ctx)
        b = _hgrn(rest, lb[l], hg_norm[l])
        cc = _short_conv(rest, sc_w[l], sc_b[l], ctx_len)
        bmat, cmat, abar = _s5_matrices(s5_a_re[l], s5_a_im[l], s5_log_dt[l], s5_b_re[l],
                                        s5_b_im[l], s5_c_re[l], s5_c_im[l])
        dd = _s5(u5, bmat, cmat, abar, s5_d[l], s5_w_glu[l], s5_b_glu[l], ctx_len)
        h = _post(h_parts, S, a, b, cc, dd, mods[l], norm_g[l].astype(F32), w_out_b[l], w_ffn_in_b[l],
                  w_ffn_out_b[l], B, n_ctx_tiles, want_ctx)
        h_parts = (h, h, n_ctx_tiles)
    return h
```

```python
import functools
import math

import jax
import jax.numpy as jnp
from jax import lax
from jax.experimental import pallas as pl
from jax.experimental.pallas import tpu as pltpu

F32 = jnp.float32
BF16 = jnp.bfloat16

EPS = 1e-6
GRID_W = 64
ROPE_BASE = 10000.0
GROUP_W = 256
DA_HEADS = 4
DA_HD = 32
HG_HEADS = 4
HG_HD = 64
HG_CHUNK = 16
HG_BLOCK = 64
S5_NGROUPS = 16
S5_GROUP = 16
S5_STATE = 64
S5_COLS = S5_NGROUPS * S5_STATE
LANES = 128
SUBLANES = 8
TILE = 256
INPROJ_NB = 4
POST_NB = 2
S5_TT = 128
VMEM_LIMIT = 56 * 1024 * 1024

_NT = (((1,), (1,)), ((), ()))


def _dot(a, b):
    return jnp.dot(a, b, preferred_element_type=F32)


def _dot_nt(a, b):
    return lax.dot_general(a, b, _NT, preferred_element_type=F32)


def _split2_dot_rhs(m, x):
    x1 = x.astype(BF16)
    x2 = (x - x1.astype(F32)).astype(BF16)
    return _dot(m, x1) + _dot(m, x2)


def _split2_dot(x, m):
    x1 = x.astype(BF16)
    x2 = (x - x1.astype(F32)).astype(BF16)
    return _dot(x1, m) + _dot(x2, m)


def _silu(x):
    return x * jax.nn.sigmoid(x)


def _rms_scale(x):
    return lax.rsqrt(jnp.mean(x * x, axis=-1, keepdims=True) + EPS)


def _group_mean_matrix(n, group):
    sh = int(math.log2(group))
    r = lax.broadcasted_iota(jnp.int32, (n, n), 0) >> sh
    c = lax.broadcasted_iota(jnp.int32, (n, n), 1) >> sh
    return jnp.where(r == c, 1.0 / group, 0.0).astype(BF16)


def _params(*sem):
    return pltpu.CompilerParams(dimension_semantics=sem, vmem_limit_bytes=VMEM_LIMIT)


def _mod_kernel(c_ref, w_ref, b_ref, o_ref):
    sc = _silu(c_ref[...])
    o_ref[0] = _dot(sc.astype(BF16), w_ref[0].astype(BF16)) + b_ref[0]


def _modulation(cvec, w_ada, b_ada):
    L, D, N = w_ada.shape
    Bp = cvec.shape[0]
    tn = 1536
    return pl.pallas_call(
        _mod_kernel,
        grid=(L, N // tn),
        in_specs=[
            pl.BlockSpec((Bp, D), lambda l, j: (0, 0)),
            pl.BlockSpec((1, D, tn), lambda l, j: (l, 0, j)),
            pl.BlockSpec((1, 1, tn), lambda l, j: (l, 0, j)),
        ],
        out_specs=pl.BlockSpec((1, Bp, tn), lambda l, j: (l, 0, j)),
        out_shape=jax.ShapeDtypeStruct((L, Bp, N), F32),
        compiler_params=_params("arbitrary", "arbitrary"),
        name="modulation",
    )(cvec, w_ada, b_ada.reshape(L, 1, N))


def _rope(x, cos, sin_a, sin_b):
    return (x * cos + pltpu.roll(x, GROUP_W - DA_HD // 2, 1) * sin_a
            + pltpu.roll(x, DA_HD // 2, 1) * sin_b)


def _tile_mod(is_ctx, modx_ref, modc_ref):
    return jnp.where(is_ctx, modc_ref[...], modx_ref[...])


def _seq_specs(h_parts, block, n_ctx_tiles, first_tile=0):
    _, _, lat_off = h_parts

    def ctx_idx(b, t):
        return (b, jnp.minimum(t + first_tile, n_ctx_tiles - 1), 0)

    def lat_idx(b, t):
        return (b, jnp.maximum(t + first_tile, n_ctx_tiles) - n_ctx_tiles + lat_off, 0)

    return [pl.BlockSpec(block, ctx_idx), pl.BlockSpec(block, lat_idx)]


def _inproj_kernel(n_ctx_tiles, hc_ref, hx_ref, modx_ref, modc_ref, g_ref, w_ref, cos_ref, sa_ref,
                   sb_ref, qkv_ref, rest_ref, u5_ref):
    nb, tile, d_model = hx_ref.shape
    is_ctx = pl.program_id(1) < n_ctx_tiles
    mod = _tile_mod(is_ctx, modx_ref, modc_ref)
    h = jnp.where(is_ctx, hc_ref[...], hx_ref[...])
    hn = h * _rms_scale(h) * g_ref[...]
    y = hn * (1.0 + mod[:, 1:2, :]) + mod[:, 0:1, :]
    p = _dot(y.reshape(nb * tile, d_model).astype(BF16), w_ref[...])
    cos, sa, sb = cos_ref[...], sa_ref[...], sb_ref[...]
    for j in range(nb):
        pj = p[j * tile:(j + 1) * tile]
        q = _rope(pj[:, 0:GROUP_W], cos, sa, sb) * (DA_HD ** -0.5 * math.log2(math.e))
        k = _rope(pj[:, GROUP_W:2 * GROUP_W], cos, sa, sb)
        qkv_ref[j, :, 0:GROUP_W] = q.astype(BF16)
        qkv_ref[j, :, GROUP_W:2 * GROUP_W] = k.astype(BF16)
        qkv_ref[j, :, 2 * GROUP_W:3 * GROUP_W] = pj[:, 2 * GROUP_W:3 * GROUP_W].astype(BF16)
        rest_ref[j] = pj[:, 3 * GROUP_W:11 * GROUP_W]
        u5_ref[j] = pj[:, 11 * GROUP_W:]


def _inproj(h_parts, S, mod, g, w, cos, sa, sb, n_batch, n_ctx_tiles):
    B, _, D = h_parts[1].shape
    N = w.shape[1]
    nt = S // TILE
    nb = INPROJ_NB
    assert B % nb == 0
    tab = pl.BlockSpec((TILE, GROUP_W), lambda b, t: (t, 0))
    return pl.pallas_call(
        functools.partial(_inproj_kernel, n_ctx_tiles),
        grid=(B // nb, nt),
        in_specs=_seq_specs(h_parts, (nb, TILE, D), n_ctx_tiles) + [
            pl.BlockSpec((nb, 6, D), lambda b, t: (b, 0, 0)),
            pl.BlockSpec((1, 6, D), lambda b, t: (n_batch, 0, 0)),
            pl.BlockSpec((1, D), lambda b, t: (0, 0)),
            pl.BlockSpec((D, N), lambda b, t: (0, 0)),
            tab, tab, tab,
        ],
        out_specs=[
            pl.BlockSpec((nb, TILE, 3 * GROUP_W), lambda b, t: (b, t, 0)),
            pl.BlockSpec((nb, TILE, 8 * GROUP_W), lambda b, t: (b, t, 0)),
            pl.BlockSpec((nb, TILE, GROUP_W), lambda b, t: (b, t, 0)),
        ],
        out_shape=[
            jax.ShapeDtypeStruct((B, S, 3 * GROUP_W), BF16),
            jax.ShapeDtypeStruct((B, S, 8 * GROUP_W), F32),
            jax.ShapeDtypeStruct((B, S, GROUP_W), F32),
        ],
        compiler_params=_params("parallel", "arbitrary"),
        name="inproj",
    )(h_parts[0], h_parts[1], mod, mod, g.reshape(1, D), w, cos, sa, sb)


def _attn_kernel(lam_init, ctx_len, q_off, q_ref, k_ref, v_ref, lamv_ref, g_ref, o_ref,
                 sc_ref, e_ref, va_ref, vb_ref):
    tq = q_ref.shape[1]
    n_keys = k_ref.shape[1]
    lv = lamv_ref[...]
    lam = (jnp.exp(jnp.sum(lv[0:1] * lv[1:2], axis=-1, keepdims=True))
           - jnp.exp(jnp.sum(lv[2:3] * lv[3:4], axis=-1, keepdims=True)) + lam_init)
    lane = lax.broadcasted_iota(jnp.int32, (1, GROUP_W), 1)
    lane_map = lane >> 5
    lane_head = lane >> 6

    @pl.when(pl.program_id(1) == 0)
    def _():
        v = v_ref[0]
        one = jnp.ones_like(v)
        va_ref[...] = jnp.where(lane == GROUP_W - 1, one, v)
        vb_ref[...] = jnp.where(lane == 0, one, v)

    q = q_ref[0]
    q8 = jnp.concatenate(
        [jnp.where(lane_map == j, q, jnp.zeros_like(q)) for j in range(2 * DA_HEADS)], axis=0)

    def attend(nk):
        sc_ref[:, 0:nk] = _dot_nt(q8, k_ref[0, 0:nk, :])
        acc = jnp.zeros((tq, GROUP_W), F32)
        for h in range(DA_HEADS):
            for m in range(2):
                rows = slice((2 * h + m) * tq, (2 * h + m + 1) * tq)
                s = sc_ref[rows, 0:nk]
                e_ref[rows, 0:nk] = jnp.exp2(s - jnp.max(s, axis=-1, keepdims=True)).astype(BF16)
            vv = va_ref if h < DA_HEADS - 1 else vb_ref
            sum_col = GROUP_W - 1 if h < DA_HEADS - 1 else 0
            o2 = _dot(e_ref[2 * h * tq:(2 * h + 2) * tq, 0:nk], vv[0:nk, :])
            o0, o1 = o2[0:tq], o2[tq:2 * tq]
            w0 = 1.0 / o0[:, sum_col:sum_col + 1]
            w1 = lam / o1[:, sum_col:sum_col + 1]
            acc = acc + jnp.where(lane_head == h, o0 * w0 - o1 * w1, 0.0)
        ms = _split2_dot(acc * acc, _group_mean_matrix(GROUP_W, 2 * DA_HD))
        o_ref[0] = (acc * lax.rsqrt(ms + EPS) * g_ref[...] * (1.0 - lam_init)).astype(o_ref.dtype)

    if q_off == 0 and ctx_len == tq:
        @pl.when(pl.program_id(1) == 0)
        def _():
            attend(ctx_len)

        @pl.when(pl.program_id(1) != 0)
        def _():
            attend(n_keys)
    else:
        attend(n_keys)


def _attention(qkv, lam_vecs, subln_g, lam_init, ctx_len, want_ctx):
    B, S, _ = qkv.shape
    tq = TILE
    q_off = 0 if want_ctx else ctx_len // tq
    nq = S // tq - q_off
    g = jnp.tile(subln_g.astype(F32), DA_HEADS).reshape(1, GROUP_W)
    return pl.pallas_call(
        functools.partial(_attn_kernel, lam_init, ctx_len, q_off),
        grid=(B, nq),
        in_specs=[
            pl.BlockSpec((1, tq, GROUP_W), lambda b, i: (b, i + q_off, 0)),
            pl.BlockSpec((1, S, GROUP_W), lambda b, i: (b, 0, 1)),
            pl.BlockSpec((1, S, GROUP_W), lambda b, i: (b, 0, 2)),
            pl.BlockSpec((4, DA_HD), lambda b, i: (0, 0)),
            pl.BlockSpec((1, GROUP_W), lambda b, i: (0, 0)),
        ],
        out_specs=pl.BlockSpec((1, tq, GROUP_W), lambda b, i: (b, i, 0)),
        out_shape=jax.ShapeDtypeStruct((B, nq * tq, GROUP_W), BF16),
        scratch_shapes=[
            pltpu.VMEM((2 * DA_HEADS * tq, S), F32),
            pltpu.VMEM((2 * DA_HEADS * tq, S), BF16),
            pltpu.VMEM((S, GROUP_W), BF16),
            pltpu.VMEM((S, GROUP_W), BF16),
        ],
        compiler_params=_params("parallel", "arbitrary"),
        name="diff_attention",
    )(qkv, qkv, qkv, lam_vecs.astype(F32), g)


def _hgrn_kernel(n_tiles, q_ref, ff_ref, fb_ref, i_ref, g_ref, lb_ref, lbt_ref, ng_ref, o_ref,
                 od_ref, st2_ref, u2_ref, oc2_ref):
    n_blocks = TILE // HG_BLOCK
    chunks_per_block = HG_BLOCK // HG_CHUNK
    row = lax.broadcasted_iota(jnp.int32, (TILE, TILE), 0)
    col = lax.broadcasted_iota(jnp.int32, (TILE, TILE), 1)
    same_chunk = (row >> 4) == (col >> 4)
    same_block = (row >> 6) == (col >> 6)
    lane = lax.broadcasted_iota(jnp.int32, (1, GROUP_W), 1)
    lane_head = lane >> 6
    lane_block = lane >> 6
    lane_pos = (lane >> 4) & (chunks_per_block - 1)
    gmean = _group_mean_matrix(GROUP_W, HG_HD)

    def ones(mask):
        return jnp.where(mask, 1.0, 0.0).astype(BF16)

    consts = []
    for d in range(2):
        dist = ((row >> 4) - (col >> 4)) if d == 0 else ((col >> 4) - (row >> 4))
        causal = same_chunk & ((col <= row) if d == 0 else (col >= row))
        allowed = same_block & ((dist >= 1) | causal)
        row_mats = [ones(causal), ones(same_block & (dist >= 1))]
        col_mats = [ones(same_chunk & ((row <= col) if d == 0 else (row >= col))), ones(same_chunk)]
        consts.append((allowed, jnp.concatenate(row_mats, axis=0), jnp.concatenate(col_mats, axis=1)))

    st2_ref[...] = jnp.zeros_like(st2_ref)

    def stack_heads(x):
        zero = jnp.zeros_like(x)
        return jnp.concatenate(
            [jnp.where(lane_head == h, x, zero).astype(BF16) for h in range(HG_HEADS)], axis=0)

    def tile_body(ti, carry):
        for d in range(2):
            allowed, row_mat, col_mat = consts[d]
            f_ref = ff_ref if d == 0 else fb_ref
            lb = lb_ref[d:d + 1, :]
            lb_t = lbt_ref[:, d:d + 1]
            st_ref, u_ref, oc_ref = st2_ref.at[d], u2_ref.at[d], oc2_ref.at[d]
            if d == 0:
                t = ti
            else:
                t = jnp.where(ti == 0, 0, n_tiles - ti)
            rows = pl.ds(pl.multiple_of(t * TILE, TILE), TILE)
            fr = f_ref[0, rows, :]
            lf = jnp.log(lb + (1.0 - lb) * jax.nn.sigmoid(fr))
            sums = _split2_dot_rhs(row_mat, lf)
            qd = _silu(q_ref[0, rows, :]) * jnp.exp(sums[0:TILE])
            q_state = qd * jnp.exp(sums[TILE:2 * TILE])
            f_t = lb_t + (1.0 - lb_t) * jax.nn.sigmoid(fr.T)
            kk_t = 1.0 - f_t
            sums_t = _split2_dot(jnp.log(f_t), col_mat)
            g_t, gt_t = sums_t[:, 0:TILE], sums_t[:, TILE:2 * TILE]
            kend_t = kk_t * jnp.exp(gt_t - g_t)
            k_var = [kk_t * jnp.exp(-g_t), kend_t]
            later = jnp.zeros_like(gt_t)
            for m in range(1, chunks_per_block):
                if d == 0:
                    shifted = pltpu.roll(gt_t, TILE - m * HG_CHUNK, 1)
                    inside = lane_pos <= chunks_per_block - 1 - m
                else:
                    shifted = pltpu.roll(gt_t, m * HG_CHUNK, 1)
                    inside = lane_pos >= m
                later = later + jnp.where(inside, shifted, 0.0)
                if m + 1 < chunks_per_block:
                    k_var.append(kend_t * jnp.exp(later))
            kblk_t = kend_t * jnp.exp(later)
            dec_t = jnp.exp(gt_t + later)
            vb = i_ref[0, rows, :].astype(BF16)

            qd4 = stack_heads(qd)
            x_pos = []
            for p in range(chunks_per_block):
                keys = jnp.zeros_like(kend_t)
                for n in range(chunks_per_block):
                    pc = p - n if d == 0 else p + n
                    if 0 <= pc < chunks_per_block:
                        keys = jnp.where(lane_pos == pc, k_var[n], keys)
                q_p = jnp.concatenate(
                    [qd4[h * TILE + b * HG_BLOCK + p * HG_CHUNK:h * TILE + b * HG_BLOCK + (p + 1) * HG_CHUNK]
                     for h in range(HG_HEADS) for b in range(n_blocks)], axis=0)
                x_pos.append(_dot(q_p, keys.astype(BF16)))
            a_heads = []
            for h in range(HG_HEADS):
                a = jnp.concatenate(
                    [x_pos[p][(h * n_blocks + b) * HG_CHUNK:(h * n_blocks + b + 1) * HG_CHUNK]
                     for b in range(n_blocks) for p in range(chunks_per_block)], axis=0)
                a_heads.append(jnp.where(allowed, a, 0.0).astype(BF16))
            zb = jnp.zeros_like(vb)
            v4 = jnp.concatenate(
                [jnp.where(lane_head == h, vb, zb) for h in range(HG_HEADS)], axis=0)
            o_tile = _dot(jnp.concatenate(a_heads, axis=1), v4)

            zero = jnp.zeros_like(kblk_t)
            k_stack = jnp.concatenate(
                [jnp.where(lane_block == b, kblk_t, zero).astype(BF16) for b in range(n_blocks)], axis=0)
            u_ref[...] = _dot(k_stack, vb)

            qs4 = stack_heads(q_state)
            order = range(n_blocks) if d == 0 else range(n_blocks - 1, -1, -1)
            for b in order:
                st = st_ref[...]
                q4b = jnp.concatenate(
                    [qs4[h * TILE + b * HG_BLOCK:h * TILE + (b + 1) * HG_BLOCK] for h in range(HG_HEADS)],
                    axis=0)
                ob4 = _dot(q4b, st.astype(BF16))
                ob = jnp.zeros((HG_BLOCK, GROUP_W), F32)
                for h in range(HG_HEADS):
                    ob = ob + jnp.where(lane_head == h, ob4[h * HG_BLOCK:(h + 1) * HG_BLOCK], 0.0)
                oc_ref[b * HG_BLOCK:(b + 1) * HG_BLOCK, :] = ob
                first = b * HG_BLOCK + (0 if d == 0 else HG_BLOCK - HG_CHUNK)
                st_ref[...] = st * dec_t[:, first:first + 1] + u_ref[b * TILE:(b + 1) * TILE, :]
            od_ref[d, rows, :] = o_tile + oc_ref[...]
        return carry

    lax.fori_loop(0, n_tiles, tile_body, 0)

    def finish_body(t, carry):
        rows = pl.ds(pl.multiple_of(t * TILE, TILE), TILE)
        tot = od_ref[0, rows, :] + od_ref[1, rows, :]
        ms = _split2_dot(tot * tot, gmean)
        o_ref[0, rows, :] = (tot * lax.rsqrt(ms + EPS) * ng_ref[...]
                             * _silu(g_ref[0, rows, :])).astype(o_ref.dtype)
        return carry

    lax.fori_loop(0, n_tiles, finish_body, 0)


def _hgrn(rest, lb, norm_g):
    B, S, _ = rest.shape
    n_tiles = S // TILE

    def part(j):
        return pl.BlockSpec((1, S, GROUP_W), lambda b: (b, 0, j))

    ng = jnp.tile(norm_g.astype(F32), HG_HEADS).reshape(1, GROUP_W)
    return pl.pallas_call(
        functools.partial(_hgrn_kernel, n_tiles),
        grid=(B,),
        in_specs=[part(0), part(1), part(2), part(3), part(4),
                  pl.BlockSpec((2, GROUP_W), lambda b: (0, 0)),
                  pl.BlockSpec((GROUP_W, 2), lambda b: (0, 0)),
                  pl.BlockSpec((1, GROUP_W), lambda b: (0, 0))],
        out_specs=pl.BlockSpec((1, S, GROUP_W), lambda b: (b, 0, 0)),
        out_shape=jax.ShapeDtypeStruct((B, S, GROUP_W), BF16),
        scratch_shapes=[
            pltpu.VMEM((2, S, GROUP_W), F32),
            pltpu.VMEM((2, GROUP_W, GROUP_W), F32),
            pltpu.VMEM((2, TILE // HG_BLOCK * TILE, GROUP_W), F32),
            pltpu.VMEM((2, TILE, GROUP_W), F32),
        ],
        compiler_params=_params("parallel"),
        name="hgrn2",
    )(rest, rest, rest, rest, rest, lb, lb.T, ng)


def _conv_kernel(ctx_len, b_ref, c_ref, u_ref, w_ref, bias_ref, o_ref):
    S = b_ref.shape[1]
    v = c_ref[0] * u_ref[0]
    row = lax.broadcasted_iota(jnp.int32, (S, 1), 0)
    prev = jnp.where((row == 0) | (row == ctx_len), 0.0, pltpu.roll(v, 1, 0))
    nxt = jnp.where((row == ctx_len - 1) | (row == S - 1), 0.0, pltpu.roll(v, S - 1, 0))
    y = w_ref[0:1, :] * prev + w_ref[1:2, :] * v + w_ref[2:3, :] * nxt
    o_ref[0] = (b_ref[0] * (y + bias_ref[...])).astype(o_ref.dtype)


def _short_conv(rest, w, bias, ctx_len):
    B, S, _ = rest.shape

    def part(j):
        return pl.BlockSpec((1, S, GROUP_W), lambda b: (b, 0, j))

    return pl.pallas_call(
        functools.partial(_conv_kernel, ctx_len),
        grid=(B,),
        in_specs=[part(5), part(6), part(7),
                  pl.BlockSpec((3, GROUP_W), lambda b: (0, 0)),
                  pl.BlockSpec((1, GROUP_W), lambda b: (0, 0))],
        out_specs=pl.BlockSpec((1, S, GROUP_W), lambda b: (b, 0, 0)),
        out_shape=jax.ShapeDtypeStruct((B, S, GROUP_W), BF16),
        compiler_params=_params("parallel"),
        name="short_conv",
    )(rest, rest, rest, w.astype(F32), bias.astype(F32).reshape(1, GROUP_W))


def _s5_back_tile(i, n_ctx_tiles, n_tiles):
    return jnp.where(i < n_ctx_tiles, n_ctx_tiles - 1 - i, n_tiles - 1 - (i - n_ctx_tiles))


def _s5_scan_kernel(uf_ref, ub_ref, bmat_ref, cmat_ref, a_ref, d_ref, yf_ref, yb_ref,
                    xsf_ref, xsb_ref, st_ref):
    tt = uf_ref.shape[1]
    n_rows = tt * SUBLANES
    n_strips = S5_COLS // LANES

    @pl.when(pl.program_id(1) == 0)
    def _():
        st_ref[...] = jnp.zeros_like(st_ref)

    def lanes(j):
        return slice(j * LANES, (j + 1) * LANES)

    us = []
    for d, (u_ref, xs_ref) in enumerate(((uf_ref, xsf_ref), (ub_ref, xsb_ref))):
        u = u_ref[0].reshape(n_rows, GROUP_W)
        xs_ref[...] = _dot(u.astype(BF16), bmat_ref[d])
        us.append(u)

    for d, (xs_ref, y_ref) in enumerate(((xsf_ref, yf_ref), (xsb_ref, yb_ref))):
        a_re = [jnp.broadcast_to(a_ref[d, 0:1, lanes(j)], (SUBLANES, LANES)) for j in range(n_strips)]
        a_im = [jnp.broadcast_to(a_ref[d, 1:2, lanes(j)], (SUBLANES, LANES)) for j in range(n_strips)]
        state = [st_ref[d, k] for k in range(2 * n_strips)]
        for s in range(tt):
            t = s if d == 0 else tt - 1 - s
            r = slice(t * SUBLANES, (t + 1) * SUBLANES)
            for j in range(n_strips):
                xr, xi = state[2 * j], state[2 * j + 1]
                nr = a_re[j] * xr - a_im[j] * xi + xs_ref[r, lanes(j)]
                ni = a_re[j] * xi + a_im[j] * xr + xs_ref[r, lanes(n_strips + j)]
                xs_ref[r, lanes(j)] = nr
                xs_ref[r, lanes(n_strips + j)] = ni
                state[2 * j], state[2 * j + 1] = nr, ni
        for k in range(2 * n_strips):
            st_ref[d, k] = state[k]
        y = _dot(xs_ref[...].astype(BF16), cmat_ref[d])
        if d == 0:
            y = y + d_ref[...] * us[0]
        y_ref[0] = y.reshape(tt, SUBLANES, GROUP_W)


def _s5_finish_kernel(yf_ref, yb_ref, wg_ref, bg_ref, o_ref):
    _, tt, nb, width = yf_ref.shape
    y = jax.nn.gelu(yf_ref[0].reshape(tt * nb, width) + yb_ref[0].reshape(tt * nb, width))
    gate = jax.nn.sigmoid(_dot(y.astype(BF16), wg_ref[...]) + bg_ref[...])
    o_ref[0] = (y * gate).reshape(tt, nb, width)


def _s5_discretise(a_re, a_im, log_dt, b_re, b_im):
    dt = jnp.exp(log_dt)[..., None]
    mag = jnp.exp(dt * a_re)
    ab_re, ab_im = mag * jnp.cos(dt * a_im), mag * jnp.sin(dt * a_im)
    den = a_re * a_re + a_im * a_im
    nr = ab_re - 1.0
    f_re = (nr * a_re + ab_im * a_im) / den
    f_im = (ab_im * a_re - nr * a_im) / den
    bb_re = f_re[..., None] * b_re - f_im[..., None] * b_im
    bb_im = f_re[..., None] * b_im + f_im[..., None] * b_re
    return ab_re, ab_im, bb_re, bb_im


def _block_diag(blocks):
    G, r, c = blocks.shape
    eye = jnp.eye(G, dtype=blocks.dtype)
    return (eye[:, None, :, None] * blocks[:, :, None, :]).reshape(G * r, G * c)


def _s5_matrices(a_re, a_im, log_dt, b_re, b_im, c_re, c_im):
    ab_re, ab_im, bb_re, bb_im = _s5_discretise(
        a_re.astype(F32), a_im.astype(F32), log_dt.astype(F32), b_re.astype(F32), b_im.astype(F32))
    bmats, cmats = [], []
    for d in range(2):
        br = _block_diag(jnp.swapaxes(bb_re[d], 1, 2))
        bi = _block_diag(jnp.swapaxes(bb_im[d], 1, 2))
        bmats.append(jnp.concatenate([br, bi], axis=1))
        cr = _block_diag(jnp.swapaxes(c_re[d].astype(F32), 1, 2))
        ci = _block_diag(jnp.swapaxes(c_im[d].astype(F32), 1, 2))
        cmats.append(jnp.concatenate([cr, -ci], axis=0))
    abar = jnp.stack([ab_re.reshape(2, S5_COLS), ab_im.reshape(2, S5_COLS)], axis=1)
    return jnp.stack(bmats).astype(BF16), jnp.stack(cmats).astype(BF16), abar


def _s5(u5, bmat, cmat, abar, d_skip, w_glu, b_glu, ctx_len):
    B, S, _ = u5.shape
    assert B % SUBLANES == 0 and ctx_len % S5_TT == 0 and S % S5_TT == 0
    ng = B // SUBLANES
    n_tiles = S // S5_TT
    n_ctx_tiles = ctx_len // S5_TT
    u = u5.reshape(ng, SUBLANES, S, GROUP_W).transpose(0, 2, 1, 3)

    def fwd_idx(g, i):
        return (g, i, 0, 0)

    def bwd_idx(g, i):
        return (g, _s5_back_tile(i, n_ctx_tiles, n_tiles), 0, 0)

    def whole(arr):
        return pl.BlockSpec(arr.shape, lambda g, i: (0,) * arr.ndim)

    tile_shape = (1, S5_TT, SUBLANES, GROUP_W)
    xs_shape = pltpu.VMEM((S5_TT * SUBLANES, 2 * S5_COLS), F32)
    y_sds = jax.ShapeDtypeStruct((ng, S, SUBLANES, GROUP_W), F32)
    d_row = d_skip.astype(F32).reshape(1, GROUP_W)
    yf, yb = pl.pallas_call(
        _s5_scan_kernel,
        grid=(ng, n_tiles),
        in_specs=[pl.BlockSpec(tile_shape, fwd_idx), pl.BlockSpec(tile_shape, bwd_idx),
                  whole(bmat), whole(cmat), whole(abar), whole(d_row)],
        out_specs=[pl.BlockSpec(tile_shape, fwd_idx), pl.BlockSpec(tile_shape, bwd_idx)],
        out_shape=[y_sds, y_sds],
        scratch_shapes=[
            xs_shape, xs_shape,
            pltpu.VMEM((2, 2 * S5_COLS // LANES, SUBLANES, LANES), F32),
        ],
        compiler_params=_params("parallel", "arbitrary"),
        name="s5_scan",
    )(u, u, bmat, cmat, abar, d_row)

    w_b = w_glu.astype(BF16)
    b_row = b_glu.astype(F32).reshape(1, GROUP_W)
    fin_shape = (1, TILE, SUBLANES, GROUP_W)
    out = pl.pallas_call(
        _s5_finish_kernel,
        grid=(ng, S // TILE),
        in_specs=[pl.BlockSpec(fin_shape, fwd_idx), pl.BlockSpec(fin_shape, fwd_idx),
                  whole(w_b), whole(b_row)],
        out_specs=pl.BlockSpec(fin_shape, fwd_idx),
        out_shape=y_sds,
        compiler_params=_params("parallel", "arbitrary"),
        name="s5_finish",
    )(yf, yb, w_b, b_row)
    return out.transpose(0, 2, 1, 3).reshape(B, S, GROUP_W).astype(BF16)


def _post_kernel(n_hidden_chunks, ctx_tiles_here, hc_ref, hx_ref, a_ref, b_ref, c_ref, d_ref,
                 modx_ref, modc_ref, ng_ref, wo_ref, wi_ref, w2_ref, o_ref):
    nb, tile, d_model = hx_ref.shape
    n_rows = nb * tile
    is_ctx = pl.program_id(1) < ctx_tiles_here
    mod = _tile_mod(is_ctx, modx_ref, modc_ref)

    def flat(ref):
        return ref[...].reshape(n_rows, GROUP_W).astype(BF16)

    mix = (_dot(flat(a_ref), wo_ref[0:GROUP_W, :])
           + _dot(flat(b_ref), wo_ref[GROUP_W:2 * GROUP_W, :])
           + _dot(flat(c_ref), wo_ref[2 * GROUP_W:3 * GROUP_W, :])
           + _dot(flat(d_ref), wo_ref[3 * GROUP_W:4 * GROUP_W, :])).reshape(nb, tile, d_model)
    h = (jnp.where(is_ctx, hc_ref[...], hx_ref[...])
         + mod[:, 2:3, :] * (mix * _rms_scale(mix) * ng_ref[1:2, :]))
    y = (h * _rms_scale(h) * ng_ref[2:3, :]) * (1.0 + mod[:, 4:5, :]) + mod[:, 3:4, :]
    yb = y.reshape(n_rows, d_model).astype(BF16)
    hidden = w2_ref.shape[0]
    hc = hidden // n_hidden_chunks
    ffn = jnp.zeros((n_rows, d_model), F32)
    for j in range(n_hidden_chunks):
        gate = _dot(yb, wi_ref[:, j * hc:(j + 1) * hc])
        up = _dot(yb, wi_ref[:, hidden + j * hc:hidden + (j + 1) * hc])
        ffn = ffn + _dot((_silu(gate) * up).astype(BF16), w2_ref[j * hc:(j + 1) * hc, :])
    ffn = ffn.reshape(nb, tile, d_model)
    o_ref[...] = h + mod[:, 5:6, :] * (ffn * _rms_scale(ffn) * ng_ref[3:4, :])


def _post(h_parts, S, a, b, c, d, mod, norm_g, w_out, w_ffn_in, w_ffn_out, n_batch, n_ctx_tiles,
          want_ctx):
    B, _, D = h_parts[1].shape
    off = 0 if want_ctx else n_ctx_tiles
    nt = S // TILE - off
    a_off = off if a.shape[1] == S else 0
    nb = POST_NB
    assert B % nb == 0

    def rows(o):
        return lambda bi, t: (bi, t + o, 0)

    def whole(arr):
        return pl.BlockSpec(arr.shape, lambda bi, t: (0,) * arr.ndim, pipeline_mode=pl.Buffered(1))

    mix_spec = lambda o: pl.BlockSpec((nb, TILE, GROUP_W), rows(o))
    return pl.pallas_call(
        functools.partial(_post_kernel, 2, n_ctx_tiles - off),
        grid=(B // nb, nt),
        in_specs=_seq_specs(h_parts, (nb, TILE, D), n_ctx_tiles, first_tile=off) + [
            mix_spec(a_off), mix_spec(off), mix_spec(off), mix_spec(off),
            pl.BlockSpec((nb, 6, D), lambda bi, t: (bi, 0, 0)),
            pl.BlockSpec((1, 6, D), lambda bi, t: (n_batch, 0, 0)),
            whole(norm_g), whole(w_out), whole(w_ffn_in), whole(w_ffn_out),
        ],
        out_specs=pl.BlockSpec((nb, TILE, D), lambda bi, t: (bi, t, 0)),
        out_shape=jax.ShapeDtypeStruct((B, nt * TILE, D), F32),
        compiler_params=_params("parallel", "arbitrary"),
        name="post",
    )(h_parts[0], h_parts[1], a, b, c, d, mod, mod, norm_g, w_out, w_ffn_in, w_ffn_out)


def _rope_tables(n_rows, ctx_len):
    rows = jnp.broadcast_to(jnp.arange(n_rows, dtype=F32)[:, None], (n_rows, GRID_W)).reshape(-1)
    cols = jnp.broadcast_to(jnp.arange(GRID_W, dtype=F32)[None, :], (n_rows, GRID_W)).reshape(-1)
    n_freq = DA_HD // 4
    inv = ROPE_BASE ** (-jnp.arange(n_freq, dtype=F32) / n_freq)
    ang = jnp.concatenate([rows[:, None] * inv, cols[:, None] * inv], axis=-1)
    cos, sin = jnp.cos(ang), jnp.sin(ang)
    zero = jnp.zeros_like(sin)
    reps = GROUP_W // DA_HD

    def lanes(first, second, ctx_value):
        t = jnp.tile(jnp.concatenate([first, second], axis=-1), (1, reps))
        return jnp.concatenate([jnp.full((ctx_len, GROUP_W), ctx_value, F32), t], axis=0)

    return lanes(cos, cos, 1.0), lanes(-sin, zero, 0.0), lanes(zero, sin, 0.0)


def _deinterleave_qk(w_in):
    perm32 = jnp.concatenate([jnp.arange(0, DA_HD, 2), jnp.arange(1, DA_HD, 2)])
    perm = (jnp.arange(0, 2 * GROUP_W, DA_HD)[:, None] + perm32[None, :]).reshape(-1)
    cols = jnp.concatenate([perm, jnp.arange(2 * GROUP_W, w_in.shape[-1])])
    return w_in[..., cols]


def kernel(x, c, ctx, c_ctx, w_ada, b_ada, norm_g, w_in, w_out, da_lambda, da_subln, hg_lb, hg_norm, sc_w, sc_b, s5_a_re, s5_a_im, s5_log_dt, s5_b_re, s5_b_im, s5_c_re, s5_c_im, s5_d, s5_w_glu, s5_b_glu, w_ffn_in, w_ffn_out):
    B, T, D = x.shape
    ctx_len = ctx.shape[1]
    L = w_ada.shape[0]
    assert ctx_len % TILE == 0 and T % TILE == 0 and T % GRID_W == 0
    n_ctx_tiles = ctx_len // TILE

    bp = -(-(B + 1) // SUBLANES) * SUBLANES
    cvec = jnp.concatenate([c, c_ctx[None, :], jnp.zeros((bp - B - 1, D), c.dtype)], axis=0)
    mods = _modulation(cvec.astype(F32), w_ada, b_ada).reshape(L, bp, 6, D)

    cos, sin_a, sin_b = _rope_tables(T // GRID_W, ctx_len)
    lb = jnp.cumsum(jax.nn.softmax(hg_lb.astype(F32), axis=0), axis=0)
    lb = lb - lb[:1]
    w_in_b = _deinterleave_qk(w_in).astype(BF16)
    w_out_b = w_out.astype(BF16)
    w_ffn_in_b = w_ffn_in.astype(BF16)
    w_ffn_out_b = w_ffn_out.astype(BF16)

    S = ctx_len + T
    h_parts = (ctx, x, 0)
    for l in range(L):
        want_ctx = l < L - 1
        lam_init = 0.8 - 0.6 * math.exp(-0.3 * l)
        qkv, rest, u5 = _inproj(h_parts, S, mods[l], norm_g[l, 0], w_in_b[l], cos, sin_a, sin_b, B,
                                n_ctx_tiles)
        a = _attention(qkv, da_lambda[l], da_subln[l], lam_init, ctx_len, want_ctx)
        b = _hgrn(rest, lb[l], hg_norm[l])
        cc = _short_conv(rest, sc_w[l], sc_b[l], ctx_len)
        bmat, cmat, abar = _s5_matrices(s5_a_re[l], s5_a_im[l], s5_log_dt[l], s5_b_re[l],
                                        s5_b_im[l], s5_c_re[l], s5_c_im[l])
        dd = _s5(u5, bmat, cmat, abar, s5_d[l], s5_w_glu[l], s5_b_glu[l], ctx_len)
        h = _post(h_parts, S, a, b, cc, dd, mods[l], norm_g[l].astype(F32), w_out_b[l], w_ffn_in_b[l],
                  w_ffn_out_b[l], B, n_ctx_tiles, want_ctx)
        h_parts = (h, h, n_ctx_tiles)
    return h
```

```python
import functools
import math

import jax
import jax.numpy as jnp
from jax import lax
from jax.experimental import pallas as pl
from jax.experimental.pallas import tpu as pltpu

F32 = jnp.float32
BF16 = jnp.bfloat16

EPS = 1e-6
GRID_W = 64
ROPE_BASE = 10000.0
GROUP_W = 256
DA_HEADS = 4
DA_HD = 32
HG_HEADS = 4
HG_HD = 64
HG_CHUNK = 16
HG_BLOCK = 64
S5_NGROUPS = 16
S5_GROUP = 16
S5_STATE = 64
S5_COLS = S5_NGROUPS * S5_STATE
LANES = 128
SUBLANES = 8
TILE = 256
INPROJ_NB = 4
POST_NB = 2
S5_TT = 128
S5_L = 16
VMEM_LIMIT = 56 * 1024 * 1024

_NT = (((1,), (1,)), ((), ()))


def _dot(a, b):
    return jnp.dot(a, b, preferred_element_type=F32)


def _dot_nt(a, b):
    return lax.dot_general(a, b, _NT, preferred_element_type=F32)


def _split2_dot_rhs(m, x):
    x1 = x.astype(BF16)
    x2 = (x - x1.astype(F32)).astype(BF16)
    return _dot(m, x1) + _dot(m, x2)


def _split2_dot(x, m):
    x1 = x.astype(BF16)
    x2 = (x - x1.astype(F32)).astype(BF16)
    return _dot(x1, m) + _dot(x2, m)


def _silu(x):
    return x * jax.nn.sigmoid(x)


def _rms_scale(x):
    return lax.rsqrt(jnp.mean(x * x, axis=-1, keepdims=True) + EPS)


def _group_mean_matrix(n, group):
    sh = int(math.log2(group))
    r = lax.broadcasted_iota(jnp.int32, (n, n), 0) >> sh
    c = lax.broadcasted_iota(jnp.int32, (n, n), 1) >> sh
    return jnp.where(r == c, 1.0 / group, 0.0).astype(BF16)


def _params(*sem):
    return pltpu.CompilerParams(dimension_semantics=sem, vmem_limit_bytes=VMEM_LIMIT)


def _mod_kernel(c_ref, w_ref, b_ref, o_ref):
    sc = _silu(c_ref[...])
    o_ref[0] = _dot(sc.astype(BF16), w_ref[0].astype(BF16)) + b_ref[0]


def _modulation(cvec, w_ada, b_ada):
    L, D, N = w_ada.shape
    Bp = cvec.shape[0]
    tn = 1536
    return pl.pallas_call(
        _mod_kernel,
        grid=(L, N // tn),
        in_specs=[
            pl.BlockSpec((Bp, D), lambda l, j: (0, 0)),
            pl.BlockSpec((1, D, tn), lambda l, j: (l, 0, j)),
            pl.BlockSpec((1, 1, tn), lambda l, j: (l, 0, j)),
        ],
        out_specs=pl.BlockSpec((1, Bp, tn), lambda l, j: (l, 0, j)),
        out_shape=jax.ShapeDtypeStruct((L, Bp, N), F32),
        compiler_params=_params("arbitrary", "arbitrary"),
        name="modulation",
    )(cvec, w_ada, b_ada.reshape(L, 1, N))


def _rope(x, cos, sin_a, sin_b):
    return (x * cos + pltpu.roll(x, GROUP_W - DA_HD // 2, 1) * sin_a
            + pltpu.roll(x, DA_HD // 2, 1) * sin_b)


def _tile_mod(is_ctx, modx_ref, modc_ref):
    return jnp.where(is_ctx, modc_ref[...], modx_ref[...])


def _seq_specs(h_parts, block, n_ctx_tiles, first_tile=0):
    _, _, lat_off = h_parts

    def ctx_idx(b, t):
        return (b, jnp.minimum(t + first_tile, n_ctx_tiles - 1), 0)

    def lat_idx(b, t):
        return (b, jnp.maximum(t + first_tile, n_ctx_tiles) - n_ctx_tiles + lat_off, 0)

    return [pl.BlockSpec(block, ctx_idx), pl.BlockSpec(block, lat_idx)]


def _inproj_kernel(n_ctx_tiles, hc_ref, hx_ref, modx_ref, modc_ref, g_ref, w_ref, cos_ref, sa_ref,
                   sb_ref, qkv_ref, rest_ref, u5_ref):
    nb, tile, d_model = hx_ref.shape
    is_ctx = pl.program_id(1) < n_ctx_tiles
    mod = _tile_mod(is_ctx, modx_ref, modc_ref)
    h = jnp.where(is_ctx, hc_ref[...], hx_ref[...])
    hn = h * _rms_scale(h) * g_ref[...]
    y = hn * (1.0 + mod[:, 1:2, :]) + mod[:, 0:1, :]
    p = _dot(y.reshape(nb * tile, d_model).astype(BF16), w_ref[...])
    cos, sa, sb = cos_ref[...], sa_ref[...], sb_ref[...]
    for j in range(nb):
        pj = p[j * tile:(j + 1) * tile]
        q = _rope(pj[:, 0:GROUP_W], cos, sa, sb) * (DA_HD ** -0.5 * math.log2(math.e))
        k = _rope(pj[:, GROUP_W:2 * GROUP_W], cos, sa, sb)
        qkv_ref[j, :, 0:GROUP_W] = q.astype(BF16)
        qkv_ref[j, :, GROUP_W:2 * GROUP_W] = k.astype(BF16)
        qkv_ref[j, :, 2 * GROUP_W:3 * GROUP_W] = pj[:, 2 * GROUP_W:3 * GROUP_W].astype(BF16)
        rest_ref[j] = pj[:, 3 * GROUP_W:11 * GROUP_W]
        u5_ref[j] = pj[:, 11 * GROUP_W:]


def _inproj(h_parts, S, mod, g, w, cos, sa, sb, n_batch, n_ctx_tiles):
    B, _, D = h_parts[1].shape
    N = w.shape[1]
    nt = S // TILE
    nb = INPROJ_NB
    assert B % nb == 0
    tab = pl.BlockSpec((TILE, GROUP_W), lambda b, t: (t, 0))
    return pl.pallas_call(
        functools.partial(_inproj_kernel, n_ctx_tiles),
        grid=(B // nb, nt),
        in_specs=_seq_specs(h_parts, (nb, TILE, D), n_ctx_tiles) + [
            pl.BlockSpec((nb, 6, D), lambda b, t: (b, 0, 0)),
            pl.BlockSpec((1, 6, D), lambda b, t: (n_batch, 0, 0)),
            pl.BlockSpec((1, D), lambda b, t: (0, 0)),
            pl.BlockSpec((D, N), lambda b, t: (0, 0)),
            tab, tab, tab,
        ],
        out_specs=[
            pl.BlockSpec((nb, TILE, 3 * GROUP_W), lambda b, t: (b, t, 0)),
            pl.BlockSpec((nb, TILE, 8 * GROUP_W), lambda b, t: (b, t, 0)),
            pl.BlockSpec((nb, TILE, GROUP_W), lambda b, t: (b, t, 0)),
        ],
        out_shape=[
            jax.ShapeDtypeStruct((B, S, 3 * GROUP_W), BF16),
            jax.ShapeDtypeStruct((B, S, 8 * GROUP_W), F32),
            jax.ShapeDtypeStruct((B, S, GROUP_W), F32),
        ],
        compiler_params=_params("parallel", "arbitrary"),
        name="inproj",
    )(h_parts[0], h_parts[1], mod, mod, g.reshape(1, D), w, cos, sa, sb)


def _attn_kernel(lam_init, ctx_len, q_off, q_ref, k_ref, v_ref, lamv_ref, g_ref, o_ref,
                 sc_ref, e_ref, va_ref, vb_ref):
    tq = q_ref.shape[1]
    n_keys = k_ref.shape[1]
    lv = lamv_ref[...]
    lam = (jnp.exp(jnp.sum(lv[0:1] * lv[1:2], axis=-1, keepdims=True))
           - jnp.exp(jnp.sum(lv[2:3] * lv[3:4], axis=-1, keepdims=True)) + lam_init)
    lane = lax.broadcasted_iota(jnp.int32, (1, GROUP_W), 1)
    lane_map = lane >> 5
    lane_head = lane >> 6

    @pl.when(pl.program_id(1) == 0)
    def _():
        v = v_ref[0]
        one = jnp.ones_like(v)
        va_ref[...] = jnp.where(lane == GROUP_W - 1, one, v)
        vb_ref[...] = jnp.where(lane == 0, one, v)

    q = q_ref[0]
    q8 = jnp.concatenate(
        [jnp.where(lane_map == j, q, jnp.zeros_like(q)) for j in range(2 * DA_HEADS)], axis=0)

    def attend(nk):
        sc_ref[:, 0:nk] = _dot_nt(q8, k_ref[0, 0:nk, :])
        acc = jnp.zeros((tq, GROUP_W), F32)
        for h in range(DA_HEADS):
            for m in range(2):
                rows = slice((2 * h + m) * tq, (2 * h + m + 1) * tq)
                s = sc_ref[rows, 0:nk]
                e_ref[rows, 0:nk] = jnp.exp2(s - jnp.max(s, axis=-1, keepdims=True)).astype(BF16)
            vv = va_ref if h < DA_HEADS - 1 else vb_ref
            sum_col = GROUP_W - 1 if h < DA_HEADS - 1 else 0
            o2 = _dot(e_ref[2 * h * tq:(2 * h + 2) * tq, 0:nk], vv[0:nk, :])
            o0, o1 = o2[0:tq], o2[tq:2 * tq]
            w0 = 1.0 / o0[:, sum_col:sum_col + 1]
            w1 = lam / o1[:, sum_col:sum_col + 1]
            acc = acc + jnp.where(lane_head == h, o0 * w0 - o1 * w1, 0.0)
        ms = _split2_dot(acc * acc, _group_mean_matrix(GROUP_W, 2 * DA_HD))
        o_ref[0] = (acc * lax.rsqrt(ms + EPS) * g_ref[...] * (1.0 - lam_init)).astype(o_ref.dtype)

    if q_off == 0 and ctx_len == tq:
        @pl.when(pl.program_id(1) == 0)
        def _():
            attend(ctx_len)

        @pl.when(pl.program_id(1) != 0)
        def _():
            attend(n_keys)
    else:
        attend(n_keys)


def _attention(qkv, lam_vecs, subln_g, lam_init, ctx_len, want_ctx):
    B, S, _ = qkv.shape
    tq = TILE
    q_off = 0 if want_ctx else ctx_len // tq
    nq = S // tq - q_off
    g = jnp.tile(subln_g.astype(F32), DA_HEADS).reshape(1, GROUP_W)
    return pl.pallas_call(
        functools.partial(_attn_kernel, lam_init, ctx_len, q_off),
        grid=(B, nq),
        in_specs=[
            pl.BlockSpec((1, tq, GROUP_W), lambda b, i: (b, i + q_off, 0)),
            pl.BlockSpec((1, S, GROUP_W), lambda b, i: (b, 0, 1)),
            pl.BlockSpec((1, S, GROUP_W), lambda b, i: (b, 0, 2)),
            pl.BlockSpec((4, DA_HD), lambda b, i: (0, 0)),
            pl.BlockSpec((1, GROUP_W), lambda b, i: (0, 0)),
        ],
        out_specs=pl.BlockSpec((1, tq, GROUP_W), lambda b, i: (b, i, 0)),
        out_shape=jax.ShapeDtypeStruct((B, nq * tq, GROUP_W), BF16),
        scratch_shapes=[
            pltpu.VMEM((2 * DA_HEADS * tq, S), F32),
            pltpu.VMEM((2 * DA_HEADS * tq, S), BF16),
            pltpu.VMEM((S, GROUP_W), BF16),
            pltpu.VMEM((S, GROUP_W), BF16),
        ],
        compiler_params=_params("parallel", "arbitrary"),
        name="diff_attention",
    )(qkv, qkv, qkv, lam_vecs.astype(F32), g)


def _hgrn_kernel(n_tiles, q_ref, ff_ref, fb_ref, i_ref, g_ref, lb_ref, lbt_ref, ng_ref, o_ref,
                 od_ref, st2_ref, u2_ref, oc2_ref):
    n_blocks = TILE // HG_BLOCK
    chunks_per_block = HG_BLOCK // HG_CHUNK
    row = lax.broadcasted_iota(jnp.int32, (TILE, TILE), 0)
    col = lax.broadcasted_iota(jnp.int32, (TILE, TILE), 1)
    same_chunk = (row >> 4) == (col >> 4)
    same_block = (row >> 6) == (col >> 6)
    lane = lax.broadcasted_iota(jnp.int32, (1, GROUP_W), 1)
    lane_head = lane >> 6
    lane_block = lane >> 6
    lane_pos = (lane >> 4) & (chunks_per_block - 1)
    gmean = _group_mean_matrix(GROUP_W, HG_HD)

    def ones(mask):
        return jnp.where(mask, 1.0, 0.0).astype(BF16)

    consts = []
    for d in range(2):
        dist = ((row >> 4) - (col >> 4)) if d == 0 else ((col >> 4) - (row >> 4))
        causal = same_chunk & ((col <= row) if d == 0 else (col >= row))
        allowed = same_block & ((dist >= 1) | causal)
        row_mats = [ones(causal), ones(same_block & (dist >= 1))]
        col_mats = [ones(same_chunk & ((row <= col) if d == 0 else (row >= col))), ones(same_chunk)]
        consts.append((allowed, jnp.concatenate(row_mats, axis=0), jnp.concatenate(col_mats, axis=1)))

    st2_ref[...] = jnp.zeros_like(st2_ref)

    def stack_heads(x):
        zero = jnp.zeros_like(x)
        return jnp.concatenate(
            [jnp.where(lane_head == h, x, zero).astype(BF16) for h in range(HG_HEADS)], axis=0)

    def tile_body(ti, carry):
        for d in range(2):
            allowed, row_mat, col_mat = consts[d]
            f_ref = ff_ref if d == 0 else fb_ref
            lb = lb_ref[d:d + 1, :]
            lb_t = lbt_ref[:, d:d + 1]
            st_ref, u_ref, oc_ref = st2_ref.at[d], u2_ref.at[d], oc2_ref.at[d]
            if d == 0:
                t = ti
            else:
                t = jnp.where(ti == 0, 0, n_tiles - ti)
            rows = pl.ds(pl.multiple_of(t * TILE, TILE), TILE)
            fr = f_ref[0, rows, :]
            lf = jnp.log(lb + (1.0 - lb) * jax.nn.sigmoid(fr))
            sums = _split2_dot_rhs(row_mat, lf)
            qd = _silu(q_ref[0, rows, :]) * jnp.exp(sums[0:TILE])
            q_state = qd * jnp.exp(sums[TILE:2 * TILE])
            f_t = lb_t + (1.0 - lb_t) * jax.nn.sigmoid(fr.T)
            kk_t = 1.0 - f_t
            sums_t = _split2_dot(jnp.log(f_t), col_mat)
            g_t, gt_t = sums_t[:, 0:TILE], sums_t[:, TILE:2 * TILE]
            kend_t = kk_t * jnp.exp(gt_t - g_t)
            k_var = [kk_t * jnp.exp(-g_t), kend_t]
            later = jnp.zeros_like(gt_t)
            for m in range(1, chunks_per_block):
                if d == 0:
                    shifted = pltpu.roll(gt_t, TILE - m * HG_CHUNK, 1)
                    inside = lane_pos <= chunks_per_block - 1 - m
                else:
                    shifted = pltpu.roll(gt_t, m * HG_CHUNK, 1)
                    inside = lane_pos >= m
                later = later + jnp.where(inside, shifted, 0.0)
                if m + 1 < chunks_per_block:
                    k_var.append(kend_t * jnp.exp(later))
            kblk_t = kend_t * jnp.exp(later)
            dec_t = jnp.exp(gt_t + later)
            vb = i_ref[0, rows, :].astype(BF16)

            qd4 = stack_heads(qd)
            x_pos = []
            for p in range(chunks_per_block):
                keys = jnp.zeros_like(kend_t)
                for n in range(chunks_per_block):
                    pc = p - n if d == 0 else p + n
                    if 0 <= pc < chunks_per_block:
                        keys = jnp.where(lane_pos == pc, k_var[n], keys)
                q_p = jnp.concatenate(
                    [qd4[h * TILE + b * HG_BLOCK + p * HG_CHUNK:h * TILE + b * HG_BLOCK + (p + 1) * HG_CHUNK]
                     for h in range(HG_HEADS) for b in range(n_blocks)], axis=0)
                x_pos.append(_dot(q_p, keys.astype(BF16)))
            a_heads = []
            for h in range(HG_HEADS):
                a = jnp.concatenate(
                    [x_pos[p][(h * n_blocks + b) * HG_CHUNK:(h * n_blocks + b + 1) * HG_CHUNK]
                     for b in range(n_blocks) for p in range(chunks_per_block)], axis=0)
                a_heads.append(jnp.where(allowed, a, 0.0).astype(BF16))
            zb = jnp.zeros_like(vb)
            v4 = jnp.concatenate(
                [jnp.where(lane_head == h, vb, zb) for h in range(HG_HEADS)], axis=0)
            o_tile = _dot(jnp.concatenate(a_heads, axis=1), v4)

            zero = jnp.zeros_like(kblk_t)
            k_stack = jnp.concatenate(
                [jnp.where(lane_block == b, kblk_t, zero).astype(BF16) for b in range(n_blocks)], axis=0)
            u_ref[...] = _dot(k_stack, vb)

            qs4 = stack_heads(q_state)
            order = range(n_blocks) if d == 0 else range(n_blocks - 1, -1, -1)
            for b in order:
                st = st_ref[...]
                q4b = jnp.concatenate(
                    [qs4[h * TILE + b * HG_BLOCK:h * TILE + (b + 1) * HG_BLOCK] for h in range(HG_HEADS)],
                    axis=0)
                ob4 = _dot(q4b, st.astype(BF16))
                ob = jnp.zeros((HG_BLOCK, GROUP_W), F32)
                for h in range(HG_HEADS):
                    ob = ob + jnp.where(lane_head == h, ob4[h * HG_BLOCK:(h + 1) * HG_BLOCK], 0.0)
                oc_ref[b * HG_BLOCK:(b + 1) * HG_BLOCK, :] = ob
                first = b * HG_BLOCK + (0 if d == 0 else HG_BLOCK - HG_CHUNK)
                st_ref[...] = st * dec_t[:, first:first + 1] + u_ref[b * TILE:(b + 1) * TILE, :]
            od_ref[d, rows, :] = o_tile + oc_ref[...]
        return carry

    lax.fori_loop(0, n_tiles, tile_body, 0)

    def finish_body(t, carry):
        rows = pl.ds(pl.multiple_of(t * TILE, TILE), TILE)
        tot = od_ref[0, rows, :] + od_ref[1, rows, :]
        ms = _split2_dot(tot * tot, gmean)
        o_ref[0, rows, :] = (tot * lax.rsqrt(ms + EPS) * ng_ref[...]
                             * _silu(g_ref[0, rows, :])).astype(o_ref.dtype)
        return carry

    lax.fori_loop(0, n_tiles, finish_body, 0)


def _hgrn(rest, lb, norm_g):
    B, S, _ = rest.shape
    n_tiles = S // TILE

    def part(j):
        return pl.BlockSpec((1, S, GROUP_W), lambda b: (b, 0, j))

    ng = jnp.tile(norm_g.astype(F32), HG_HEADS).reshape(1, GROUP_W)
    return pl.pallas_call(
        functools.partial(_hgrn_kernel, n_tiles),
        grid=(B,),
        in_specs=[part(0), part(1), part(2), part(3), part(4),
                  pl.BlockSpec((2, GROUP_W), lambda b: (0, 0)),
                  pl.BlockSpec((GROUP_W, 2), lambda b: (0, 0)),
                  pl.BlockSpec((1, GROUP_W), lambda b: (0, 0))],
        out_specs=pl.BlockSpec((1, S, GROUP_W), lambda b: (b, 0, 0)),
        out_shape=jax.ShapeDtypeStruct((B, S, GROUP_W), BF16),
        scratch_shapes=[
            pltpu.VMEM((2, S, GROUP_W), F32),
            pltpu.VMEM((2, GROUP_W, GROUP_W), F32),
            pltpu.VMEM((2, TILE // HG_BLOCK * TILE, GROUP_W), F32),
            pltpu.VMEM((2, TILE, GROUP_W), F32),
        ],
        compiler_params=_params("parallel"),
        name="hgrn2",
    )(rest, rest, rest, rest, rest, lb, lb.T, ng)


def _conv_kernel(ctx_len, b_ref, c_ref, u_ref, w_ref, bias_ref, o_ref):
    S = b_ref.shape[1]
    v = c_ref[0] * u_ref[0]
    row = lax.broadcasted_iota(jnp.int32, (S, 1), 0)
    prev = jnp.where((row == 0) | (row == ctx_len), 0.0, pltpu.roll(v, 1, 0))
    nxt = jnp.where((row == ctx_len - 1) | (row == S - 1), 0.0, pltpu.roll(v, S - 1, 0))
    y = w_ref[0:1, :] * prev + w_ref[1:2, :] * v + w_ref[2:3, :] * nxt
    o_ref[0] = (b_ref[0] * (y + bias_ref[...])).astype(o_ref.dtype)


def _short_conv(rest, w, bias, ctx_len):
    B, S, _ = rest.shape

    def part(j):
        return pl.BlockSpec((1, S, GROUP_W), lambda b: (b, 0, j))

    return pl.pallas_call(
        functools.partial(_conv_kernel, ctx_len),
        grid=(B,),
        in_specs=[part(5), part(6), part(7),
                  pl.BlockSpec((3, GROUP_W), lambda b: (0, 0)),
                  pl.BlockSpec((1, GROUP_W), lambda b: (0, 0))],
        out_specs=pl.BlockSpec((1, S, GROUP_W), lambda b: (b, 0, 0)),
        out_shape=jax.ShapeDtypeStruct((B, S, GROUP_W), BF16),
        compiler_params=_params("parallel"),
        name="short_conv",
    )(rest, rest, rest, w.astype(F32), bias.astype(F32).reshape(1, GROUP_W))


def _s5_back_tile(i, n_ctx_tiles, n_tiles):
    return jnp.where(i < n_ctx_tiles, n_ctx_tiles - 1 - i, n_tiles - 1 - (i - n_ctx_tiles))


def _s5_scan_kernel(uf_ref, ub_ref, bmat_ref, cmat_ref, a_ref, d_ref, yf_ref, yb_ref,
                    xsf_ref, xsb_ref, st_ref):
    tt = uf_ref.shape[1]
    n_rows = tt * SUBLANES
    n_strips = S5_COLS // LANES

    @pl.when(pl.program_id(1) == 0)
    def _():
        st_ref[...] = jnp.zeros_like(st_ref)

    def lanes(j):
        return slice(j * LANES, (j + 1) * LANES)

    us = []
    for d, (u_ref, xs_ref) in enumerate(((uf_ref, xsf_ref), (ub_ref, xsb_ref))):
        u = u_ref[0].reshape(n_rows, GROUP_W)
        xs_ref[...] = _dot(u.astype(BF16), bmat_ref[d])
        us.append(u)

    for d, (xs_ref, y_ref) in enumerate(((xsf_ref, yf_ref), (xsb_ref, yb_ref))):
        a_re = [jnp.broadcast_to(a_ref[d, 0:1, lanes(j)], (SUBLANES, LANES)) for j in range(n_strips)]
        a_im = [jnp.broadcast_to(a_ref[d, 1:2, lanes(j)], (SUBLANES, LANES)) for j in range(n_strips)]
        state = [st_ref[d, k] for k in range(2 * n_strips)]
        for s in range(tt):
            t = s if d == 0 else tt - 1 - s
            r = slice(t * SUBLANES, (t + 1) * SUBLANES)
            for j in range(n_strips):
                xr, xi = state[2 * j], state[2 * j + 1]
                nr = a_re[j] * xr - a_im[j] * xi + xs_ref[r, lanes(j)]
                ni = a_re[j] * xi + a_im[j] * xr + xs_ref[r, lanes(n_strips + j)]
                xs_ref[r, lanes(j)] = nr
                xs_ref[r, lanes(n_strips + j)] = ni
                state[2 * j], state[2 * j + 1] = nr, ni
        for k in range(2 * n_strips):
            st_ref[d, k] = state[k]
        y = _dot(xs_ref[...].astype(BF16), cmat_ref[d])
        if d == 0:
            y = y + d_ref[...] * us[0]
        y_ref[0] = y.reshape(tt, SUBLANES, GROUP_W)


def _s5_finish_kernel(yf_ref, yb_ref, wg_ref, bg_ref, o_ref):
    _, tt, nb, width = yf_ref.shape
    y = jax.nn.gelu(yf_ref[0].reshape(tt * nb, width) + yb_ref[0].reshape(tt * nb, width))
    gate = jax.nn.sigmoid(_dot(y.astype(BF16), wg_ref[...]) + bg_ref[...])
    o_ref[0] = (y * gate).reshape(tt, nb, width)


def _s5_discretise(a_re, a_im, log_dt, b_re, b_im):
    dt = jnp.exp(log_dt)[..., None]
    mag = jnp.exp(dt * a_re)
    ab_re, ab_im = mag * jnp.cos(dt * a_im), mag * jnp.sin(dt * a_im)
    den = a_re * a_re + a_im * a_im
    nr = ab_re - 1.0
    f_re = (nr * a_re + ab_im * a_im) / den
    f_im = (ab_im * a_re - nr * a_im) / den
    bb_re = f_re[..., None] * b_re - f_im[..., None] * b_im
    bb_im = f_re[..., None] * b_im + f_im[..., None] * b_re
    return ab_re, ab_im, bb_re, bb_im


def _block_diag(blocks):
    G, r, c = blocks.shape
    eye = jnp.eye(G, dtype=blocks.dtype)
    return (eye[:, None, :, None] * blocks[:, :, None, :]).reshape(G * r, G * c)


def _s5_matrices(a_re, a_im, log_dt, b_re, b_im, c_re, c_im):
    ab_re, ab_im, bb_re, bb_im = _s5_discretise(
        a_re.astype(F32), a_im.astype(F32), log_dt.astype(F32), b_re.astype(F32), b_im.astype(F32))
    bmats, cmats = [], []
    for d in range(2):
        br = _block_diag(jnp.swapaxes(bb_re[d], 1, 2))
        bi = _block_diag(jnp.swapaxes(bb_im[d], 1, 2))
        bmats.append(jnp.concatenate([br, bi], axis=1))
        cr = _block_diag(jnp.swapaxes(c_re[d].astype(F32), 1, 2))
        ci = _block_diag(jnp.swapaxes(c_im[d].astype(F32), 1, 2))
        cmats.append(jnp.concatenate([cr, -ci], axis=0))
    abar = jnp.stack([ab_re.reshape(2, S5_COLS), ab_im.reshape(2, S5_COLS)], axis=1)
    return jnp.stack(bmats).astype(BF16), jnp.stack(cmats).astype(BF16), abar


def _s5(u5, bmat, cmat, abar, d_skip, w_glu, b_glu, ctx_len):
    B, S, _ = u5.shape
    assert B % SUBLANES == 0 and ctx_len % S5_TT == 0 and S % S5_TT == 0
    ng = B // SUBLANES
    n_tiles = S // S5_TT
    n_ctx_tiles = ctx_len // S5_TT
    u = u5.reshape(ng, SUBLANES, S, GROUP_W).transpose(0, 2, 1, 3)

    def fwd_idx(g, i):
        return (g, i, 0, 0)

    def bwd_idx(g, i):
        return (g, _s5_back_tile(i, n_ctx_tiles, n_tiles), 0, 0)

    def whole(arr):
        return pl.BlockSpec(arr.shape, lambda g, i: (0,) * arr.ndim)

    tile_shape = (1, S5_TT, SUBLANES, GROUP_W)
    xs_shape = pltpu.VMEM((S5_TT * SUBLANES, 2 * S5_COLS), F32)
    y_sds = jax.ShapeDtypeStruct((ng, S, SUBLANES, GROUP_W), F32)
    d_row = d_skip.astype(F32).reshape(1, GROUP_W)
    yf, yb = pl.pallas_call(
        _s5_scan_kernel,
        grid=(ng, n_tiles),
        in_specs=[pl.BlockSpec(tile_shape, fwd_idx), pl.BlockSpec(tile_shape, bwd_idx),
                  whole(bmat), whole(cmat), whole(abar), whole(d_row)],
        out_specs=[pl.BlockSpec(tile_shape, fwd_idx), pl.BlockSpec(tile_shape, bwd_idx)],
        out_shape=[y_sds, y_sds],
        scratch_shapes=[
            xs_shape, xs_shape,
            pltpu.VMEM((2, 2 * S5_COLS // LANES, SUBLANES, LANES), F32),
        ],
        compiler_params=_params("parallel", "arbitrary"),
        name="s5_scan",
    )(u, u, bmat, cmat, abar, d_row)

    w_b = w_glu.astype(BF16)
    b_row = b_glu.astype(F32).reshape(1, GROUP_W)
    fin_shape = (1, TILE, SUBLANES, GROUP_W)
    out = pl.pallas_call(
        _s5_finish_kernel,
        grid=(ng, S // TILE),
        in_specs=[pl.BlockSpec(fin_shape, fwd_idx), pl.BlockSpec(fin_shape, fwd_idx),
                  whole(w_b), whole(b_row)],
        out_specs=pl.BlockSpec(fin_shape, fwd_idx),
        out_shape=y_sds,
        compiler_params=_params("parallel", "arbitrary"),
        name="s5_finish",
    )(yf, yb, w_b, b_row)
    return out.transpose(0, 2, 1, 3).reshape(B, S, GROUP_W).astype(BF16)


def _cpow_table(re1, im1, n):
    re, im = [jnp.ones_like(re1)], [jnp.zeros_like(im1)]
    for _ in range(n):
        r, i = re[-1], im[-1]
        re.append(r * re1 - i * im1)
        im.append(r * im1 + i * re1)
    return jnp.stack(re), jnp.stack(im)


def _s5_chunk_matrices(a_re, a_im, log_dt, b_re, b_im, c_re, c_im, d_skip):
    L, G, P, N = S5_L, S5_NGROUPS, S5_GROUP, S5_STATE
    hp = lax.Precision.HIGHEST
    ab_re, ab_im, bb_re, bb_im = _s5_discretise(
        a_re.astype(F32), a_im.astype(F32), log_dt.astype(F32), b_re.astype(F32), b_im.astype(F32))
    c_re, c_im = c_re.astype(F32), c_im.astype(F32)
    pr, pi = _cpow_table(ab_re, ab_im, L)
    ca_re = c_re[None] * pr[:, :, :, None, :] - c_im[None] * pi[:, :, :, None, :]
    ca_im = c_re[None] * pi[:, :, :, None, :] + c_im[None] * pr[:, :, :, None, :]
    k = (jnp.einsum("jdgqn,dgnp->jdgqp", ca_re[:L], bb_re, precision=hp)
         - jnp.einsum("jdgqn,dgnp->jdgqp", ca_im[:L], bb_im, precision=hp))
    steps = jnp.arange(L)
    lag = steps[None, :] - steps[:, None]

    def toeplitz(kd, lag):
        kk = jnp.where((lag >= 0)[:, :, None, None, None], kd[jnp.clip(lag, 0, L - 1)], 0.0)
        return kk.transpose(2, 0, 4, 1, 3)

    tmat = (toeplitz(k[:, 0], lag) + toeplitz(k[:, 1], -lag)).reshape(G, L * P, L * P)
    skip = jnp.tile(d_skip.astype(F32).reshape(G, 1, P), (1, L, 1)).reshape(G, 1, L * P)
    tmat = tmat + jnp.eye(L * P, dtype=F32)[None] * skip

    def to_state(pw_re, pw_im, d):
        br = jnp.swapaxes(bb_re[d], 1, 2)[None]
        bi = jnp.swapaxes(bb_im[d], 1, 2)[None]
        re = pw_re[:, :, None, :] * br - pw_im[:, :, None, :] * bi
        im = pw_re[:, :, None, :] * bi + pw_im[:, :, None, :] * br
        return (re.transpose(1, 0, 2, 3).reshape(G, L * P, N),
                im.transpose(1, 0, 2, 3).reshape(G, L * P, N))

    f_re, f_im = to_state(pr[:L, 0][::-1], pi[:L, 0][::-1], 0)
    g_re, g_im = to_state(pr[:L, 1], pi[:L, 1], 1)
    bs = jnp.concatenate([f_re, g_re, f_im, g_im], axis=2)

    def from_state(x, d, flip):
        xd = x[::-1, d] if flip else x[:, d]
        return xd.transpose(1, 3, 0, 2).reshape(G, N, L * P)

    cs = jnp.concatenate([from_state(ca_re[1:], 0, False), from_state(ca_re[1:], 1, True),
                          -from_state(ca_im[1:], 0, False), -from_state(ca_im[1:], 1, True)], axis=1)
    a_l = jnp.stack([jnp.concatenate([pr[L, 0], pr[L, 1]], axis=-1),
                     jnp.concatenate([pi[L, 0], pi[L, 1]], axis=-1)], axis=1)
    return tmat.astype(BF16), bs.astype(BF16), cs.astype(BF16), a_l


def _s5_chunk_kernel(n_ctx_chunks, n_chunks, u_ref, t_ref, bs_ref, cs_ref, al_ref, y_ref,
                     v_ref, xs_ref):
    nb = u_ref.shape[1] // n_chunks
    n = S5_STATE
    u = u_ref[0].astype(BF16)
    v_ref[...] = _dot(u, bs_ref[0])
    fwd_lanes = lax.broadcasted_iota(jnp.int32, (1, 2 * n), 1) < n
    a_re = jnp.broadcast_to(al_ref[0, 0:1, :], (nb, 2 * n))
    a_im = jnp.broadcast_to(al_ref[0, 1:2, :], (nb, 2 * n))

    def step(i, carry):
        xr, xi = carry
        cb = jnp.where(i < n_ctx_chunks, n_ctx_chunks - 1 - i, n_chunks - 1 - (i - n_ctx_chunks))
        rf = pl.ds(pl.multiple_of(i * nb, nb), nb)
        rb = pl.ds(pl.multiple_of(cb * nb, nb), nb)
        xs_ref[rf, 0:n] = xr[:, 0:n]
        xs_ref[rb, n:2 * n] = xr[:, n:2 * n]
        xs_ref[rf, 2 * n:3 * n] = xi[:, 0:n]
        xs_ref[rb, 3 * n:4 * n] = xi[:, n:2 * n]
        vr = jnp.where(fwd_lanes, v_ref[rf, 0:2 * n], v_ref[rb, 0:2 * n])
        vi = jnp.where(fwd_lanes, v_ref[rf, 2 * n:4 * n], v_ref[rb, 2 * n:4 * n])
        return a_re * xr - a_im * xi + vr, a_re * xi + a_im * xr + vi

    zero = jnp.zeros((nb, 2 * n), F32)
    lax.fori_loop(0, n_chunks, step, (zero, zero))
    y_ref[0] = _dot(u, t_ref[0]) + _dot(xs_ref[...].astype(BF16), cs_ref[0])


def _s5_glu_kernel(y_ref, wg_ref, bg_ref, o_ref):
    nb, tile, width = y_ref.shape
    y = jax.nn.gelu(y_ref[...].reshape(nb * tile, width))
    gate = jax.nn.sigmoid(_dot(y.astype(BF16), wg_ref[...]) + bg_ref[...])
    o_ref[...] = (y * gate).reshape(nb, tile, width).astype(o_ref.dtype)


def _s5_chunked(u5, tmat, bs, cs, a_l, w_glu, b_glu, ctx_len):
    B, S, W = u5.shape
    L, G, P = S5_L, S5_NGROUPS, S5_GROUP
    assert ctx_len % L == 0 and S % L == 0 and B % SUBLANES == 0
    n_chunks = S // L
    rows = n_chunks * B
    u = u5.reshape(B, n_chunks, L, G, P).transpose(3, 1, 0, 2, 4).reshape(G, rows, L * P)

    def per_group(arr):
        return pl.BlockSpec((1,) + arr.shape[1:], lambda g: (g,) + (0,) * (arr.ndim - 1))

    y = pl.pallas_call(
        functools.partial(_s5_chunk_kernel, ctx_len // L, n_chunks),
        grid=(G,),
        in_specs=[per_group(u), per_group(tmat), per_group(bs), per_group(cs), per_group(a_l)],
        out_specs=pl.BlockSpec((1, rows, L * P), lambda g: (g, 0, 0)),
        out_shape=jax.ShapeDtypeStruct((G, rows, L * P), F32),
        scratch_shapes=[pltpu.VMEM((rows, 4 * S5_STATE), F32),
                        pltpu.VMEM((rows, 4 * S5_STATE), F32)],
        compiler_params=_params("parallel"),
        name="s5_chunks",
    )(u, tmat, bs, cs, a_l)
    y = y.reshape(G, n_chunks, B, L, P).transpose(2, 1, 3, 0, 4).reshape(B, S, W)

    nb = POST_NB
    w_b = w_glu.astype(BF16)
    b_row = b_glu.astype(F32).reshape(1, W)
    return pl.pallas_call(
        _s5_glu_kernel,
        grid=(B // nb, S // TILE),
        in_specs=[pl.BlockSpec((nb, TILE, W), lambda b, t: (b, t, 0)),
                  pl.BlockSpec(w_b.shape, lambda b, t: (0, 0)),
                  pl.BlockSpec(b_row.shape, lambda b, t: (0, 0))],
        out_specs=pl.BlockSpec((nb, TILE, W), lambda b, t: (b, t, 0)),
        out_shape=jax.ShapeDtypeStruct((B, S, W), BF16),
        compiler_params=_params("parallel", "arbitrary"),
        name="s5_glu",
    )(y, w_b, b_row)


def _post_kernel(n_hidden_chunks, ctx_tiles_here, hc_ref, hx_ref, a_ref, b_ref, c_ref, d_ref,
                 modx_ref, modc_ref, ng_ref, wo_ref, wi_ref, w2_ref, o_ref):
    nb, tile, d_model = hx_ref.shape
    n_rows = nb * tile
    is_ctx = pl.program_id(1) < ctx_tiles_here
    mod = _tile_mod(is_ctx, modx_ref, modc_ref)

    def flat(ref):
        return ref[...].reshape(n_rows, GROUP_W).astype(BF16)

    mix = (_dot(flat(a_ref), wo_ref[0:GROUP_W, :])
           + _dot(flat(b_ref), wo_ref[GROUP_W:2 * GROUP_W, :])
           + _dot(flat(c_ref), wo_ref[2 * GROUP_W:3 * GROUP_W, :])
           + _dot(flat(d_ref), wo_ref[3 * GROUP_W:4 * GROUP_W, :])).reshape(nb, tile, d_model)
    h = (jnp.where(is_ctx, hc_ref[...], hx_ref[...])
         + mod[:, 2:3, :] * (mix * _rms_scale(mix) * ng_ref[1:2, :]))
    y = (h * _rms_scale(h) * ng_ref[2:3, :]) * (1.0 + mod[:, 4:5, :]) + mod[:, 3:4, :]
    yb = y.reshape(n_rows, d_model).astype(BF16)
    hidden = w2_ref.shape[0]
    hc = hidden // n_hidden_chunks
    ffn = jnp.zeros((n_rows, d_model), F32)
    for j in range(n_hidden_chunks):
        gate = _dot(yb, wi_ref[:, j * hc:(j + 1) * hc])
        up = _dot(yb, wi_ref[:, hidden + j * hc:hidden + (j + 1) * hc])
        ffn = ffn + _dot((_silu(gate) * up).astype(BF16), w2_ref[j * hc:(j + 1) * hc, :])
    ffn = ffn.reshape(nb, tile, d_model)
    o_ref[...] = h + mod[:, 5:6, :] * (ffn * _rms_scale(ffn) * ng_ref[3:4, :])


def _post(h_parts, S, a, b, c, d, mod, norm_g, w_out, w_ffn_in, w_ffn_out, n_batch, n_ctx_tiles,
          want_ctx):
    B, _, D = h_parts[1].shape
    off = 0 if want_ctx else n_ctx_tiles
    nt = S // TILE - off
    a_off = off if a.shape[1] == S else 0
    nb = POST_NB
    assert B % nb == 0

    def rows(o):
        return lambda bi, t: (bi, t + o, 0)

    def whole(arr):
        return pl.BlockSpec(arr.shape, lambda bi, t: (0,) * arr.ndim, pipeline_mode=pl.Buffered(1))

    mix_spec = lambda o: pl.BlockSpec((nb, TILE, GROUP_W), rows(o))
    return pl.pallas_call(
        functools.partial(_post_kernel, 2, n_ctx_tiles - off),
        grid=(B // nb, nt),
        in_specs=_seq_specs(h_parts, (nb, TILE, D), n_ctx_tiles, first_tile=off) + [
            mix_spec(a_off), mix_spec(off), mix_spec(off), mix_spec(off),
            pl.BlockSpec((nb, 6, D), lambda bi, t: (bi, 0, 0)),
            pl.BlockSpec((1, 6, D), lambda bi, t: (n_batch, 0, 0)),
            whole(norm_g), whole(w_out), whole(w_ffn_in), whole(w_ffn_out),
        ],
        out_specs=pl.BlockSpec((nb, TILE, D), lambda bi, t: (bi, t, 0)),
        out_shape=jax.ShapeDtypeStruct((B, nt * TILE, D), F32),
        compiler_params=_params("parallel", "arbitrary"),
        name="post",
    )(h_parts[0], h_parts[1], a, b, c, d, mod, mod, norm_g, w_out, w_ffn_in, w_ffn_out)


def _rope_tables(n_rows, ctx_len):
    rows = jnp.broadcast_to(jnp.arange(n_rows, dtype=F32)[:, None], (n_rows, GRID_W)).reshape(-1)
    cols = jnp.broadcast_to(jnp.arange(GRID_W, dtype=F32)[None, :], (n_rows, GRID_W)).reshape(-1)
    n_freq = DA_HD // 4
    inv = ROPE_BASE ** (-jnp.arange(n_freq, dtype=F32) / n_freq)
    ang = jnp.concatenate([rows[:, None] * inv, cols[:, None] * inv], axis=-1)
    cos, sin = jnp.cos(ang), jnp.sin(ang)
    zero = jnp.zeros_like(sin)
    reps = GROUP_W // DA_HD

    def lanes(first, second, ctx_value):
        t = jnp.tile(jnp.concatenate([first, second], axis=-1), (1, reps))
        return jnp.concatenate([jnp.full((ctx_len, GROUP_W), ctx_value, F32), t], axis=0)

    return lanes(cos, cos, 1.0), lanes(-sin, zero, 0.0), lanes(zero, sin, 0.0)


def _deinterleave_qk(w_in):
    perm32 = jnp.concatenate([jnp.arange(0, DA_HD, 2), jnp.arange(1, DA_HD, 2)])
    perm = (jnp.arange(0, 2 * GROUP_W, DA_HD)[:, None] + perm32[None, :]).reshape(-1)
    cols = jnp.concatenate([perm, jnp.arange(2 * GROUP_W, w_in.shape[-1])])
    return w_in[..., cols]


def kernel(x, c, ctx, c_ctx, w_ada, b_ada, norm_g, w_in, w_out, da_lambda, da_subln, hg_lb, hg_norm, sc_w, sc_b, s5_a_re, s5_a_im, s5_log_dt, s5_b_re, s5_b_im, s5_c_re, s5_c_im, s5_d, s5_w_glu, s5_b_glu, w_ffn_in, w_ffn_out):
    B, T, D = x.shape
    ctx_len = ctx.shape[1]
    L = w_ada.shape[0]
    assert ctx_len % TILE == 0 and T % TILE == 0 and T % GRID_W == 0
    n_ctx_tiles = ctx_len // TILE

    bp = -(-(B + 1) // SUBLANES) * SUBLANES
    cvec = jnp.concatenate([c, c_ctx[None, :], jnp.zeros((bp - B - 1, D), c.dtype)], axis=0)
    mods = _modulation(cvec.astype(F32), w_ada, b_ada).reshape(L, bp, 6, D)

    cos, sin_a, sin_b = _rope_tables(T // GRID_W, ctx_len)
    lb = jnp.cumsum(jax.nn.softmax(hg_lb.astype(F32), axis=0), axis=0)
    lb = lb - lb[:1]
    w_in_b = _deinterleave_qk(w_in).astype(BF16)
    w_out_b = w_out.astype(BF16)
    w_ffn_in_b = w_ffn_in.astype(BF16)
    w_ffn_out_b = w_ffn_out.astype(BF16)

    S = ctx_len + T
    h_parts = (ctx, x, 0)
    for l in range(L):
        want_ctx = l < L - 1
        lam_init = 0.8 - 0.6 * math.exp(-0.3 * l)
        qkv, rest, u5 = _inproj(h_parts, S, mods[l], norm_g[l, 0], w_in_b[l], cos, sin_a, sin_b, B,
                                n_ctx_tiles)
        a = _attention(qkv, da_lambda[l], da_subln[l], lam_init, ctx_len, want_ctx)
        b = _hgrn(rest, lb[l], hg_norm[l])
        cc = _short_conv(rest, sc_w[l], sc_b[l], ctx_len)
        tmat, bs, cs, a_l = _s5_chunk_matrices(s5_a_re[l], s5_a_im[l], s5_log_dt[l], s5_b_re[l],
                                               s5_b_im[l], s5_c_re[l], s5_c_im[l], s5_d[l])
        dd = _s5_chunked(u5, tmat, bs, cs, a_l, s5_w_glu[l], s5_b_glu[l], ctx_len)
        h = _post(h_parts, S, a, b, cc, dd, mods[l], norm_g[l].astype(F32), w_out_b[l], w_ffn_in_b[l],
                  w_ffn_out_b[l], B, n_ctx_tiles, want_ctx)
        h_parts = (h, h, n_ctx_tiles)
    return h
```

```python
import functools
import math

import jax
import jax.numpy as jnp
from jax import lax
from jax.experimental import pallas as pl
from jax.experimental.pallas import tpu as pltpu

F32 = jnp.float32
BF16 = jnp.bfloat16

EPS = 1e-6
GRID_W = 64
ROPE_BASE = 10000.0
GROUP_W = 256
DA_HEADS = 4
DA_HD = 32
HG_HEADS = 4
HG_HD = 64
HG_CHUNK = 16
HG_BLOCK = 64
S5_NGROUPS = 16
S5_GROUP = 16
S5_STATE = 64
S5_COLS = S5_NGROUPS * S5_STATE
LANES = 128
SUBLANES = 8
TILE = 256
INPROJ_NB = 4
POST_NB = 2
S5_TT = 128
S5_L = 16
VMEM_LIMIT = 56 * 1024 * 1024

_NT = (((1,), (1,)), ((), ()))


def _dot(a, b):
    return jnp.dot(a, b, preferred_element_type=F32)


def _dot_nt(a, b):
    return lax.dot_general(a, b, _NT, preferred_element_type=F32)


def _split2_dot_rhs(m, x):
    x1 = x.astype(BF16)
    x2 = (x - x1.astype(F32)).astype(BF16)
    return _dot(m, x1) + _dot(m, x2)


def _split2_dot(x, m):
    x1 = x.astype(BF16)
    x2 = (x - x1.astype(F32)).astype(BF16)
    return _dot(x1, m) + _dot(x2, m)


def _silu(x):
    return x * jax.nn.sigmoid(x)


def _rms_scale(x):
    return lax.rsqrt(jnp.mean(x * x, axis=-1, keepdims=True) + EPS)


def _group_mean_matrix(n, group):
    sh = int(math.log2(group))
    r = lax.broadcasted_iota(jnp.int32, (n, n), 0) >> sh
    c = lax.broadcasted_iota(jnp.int32, (n, n), 1) >> sh
    return jnp.where(r == c, 1.0 / group, 0.0).astype(BF16)


def _params(*sem):
    return pltpu.CompilerParams(dimension_semantics=sem, vmem_limit_bytes=VMEM_LIMIT)


def _mod_kernel(c_ref, w_ref, b_ref, o_ref):
    sc = _silu(c_ref[...])
    o_ref[0] = _dot(sc.astype(BF16), w_ref[0].astype(BF16)) + b_ref[0]


def _modulation(cvec, w_ada, b_ada):
    L, D, N = w_ada.shape
    Bp = cvec.shape[0]
    tn = 1536
    return pl.pallas_call(
        _mod_kernel,
        grid=(L, N // tn),
        in_specs=[
            pl.BlockSpec((Bp, D), lambda l, j: (0, 0)),
            pl.BlockSpec((1, D, tn), lambda l, j: (l, 0, j)),
            pl.BlockSpec((1, 1, tn), lambda l, j: (l, 0, j)),
        ],
        out_specs=pl.BlockSpec((1, Bp, tn), lambda l, j: (l, 0, j)),
        out_shape=jax.ShapeDtypeStruct((L, Bp, N), F32),
        compiler_params=_params("arbitrary", "arbitrary"),
        name="modulation",
    )(cvec, w_ada, b_ada.reshape(L, 1, N))


def _rope(x, cos, sin_a, sin_b):
    return (x * cos + pltpu.roll(x, GROUP_W - DA_HD // 2, 1) * sin_a
            + pltpu.roll(x, DA_HD // 2, 1) * sin_b)


def _tile_mod(is_ctx, modx_ref, modc_ref):
    return jnp.where(is_ctx, modc_ref[...], modx_ref[...])


def _seq_specs(h_parts, block, n_ctx_tiles, first_tile=0):
    _, _, lat_off = h_parts

    def ctx_idx(b, t):
        return (b, jnp.minimum(t + first_tile, n_ctx_tiles - 1), 0)

    def lat_idx(b, t):
        return (b, jnp.maximum(t + first_tile, n_ctx_tiles) - n_ctx_tiles + lat_off, 0)

    return [pl.BlockSpec(block, ctx_idx), pl.BlockSpec(block, lat_idx)]


def _inproj_kernel(n_ctx_tiles, hc_ref, hx_ref, modx_ref, modc_ref, g_ref, w_ref, cos_ref, sa_ref,
                   sb_ref, qkv_ref, rest_ref, u5_ref):
    nb, tile, d_model = hx_ref.shape
    is_ctx = pl.program_id(1) < n_ctx_tiles
    mod = _tile_mod(is_ctx, modx_ref, modc_ref)
    h = jnp.where(is_ctx, hc_ref[...], hx_ref[...])
    hn = h * _rms_scale(h) * g_ref[...]
    y = hn * (1.0 + mod[:, 1:2, :]) + mod[:, 0:1, :]
    p = _dot(y.reshape(nb * tile, d_model).astype(BF16), w_ref[...])
    cos, sa, sb = cos_ref[...], sa_ref[...], sb_ref[...]
    for j in range(nb):
        pj = p[j * tile:(j + 1) * tile]
        q = _rope(pj[:, 0:GROUP_W], cos, sa, sb) * (DA_HD ** -0.5 * math.log2(math.e))
        k = _rope(pj[:, GROUP_W:2 * GROUP_W], cos, sa, sb)
        qkv_ref[j, :, 0:GROUP_W] = q.astype(BF16)
        qkv_ref[j, :, GROUP_W:2 * GROUP_W] = k.astype(BF16)
        qkv_ref[j, :, 2 * GROUP_W:3 * GROUP_W] = pj[:, 2 * GROUP_W:3 * GROUP_W].astype(BF16)
        rest_ref[j] = pj[:, 3 * GROUP_W:11 * GROUP_W]
        u5_ref[:, j, :, :] = pj[:, 11 * GROUP_W:].reshape(tile // S5_L, S5_L, GROUP_W)


def _inproj(h_parts, S, mod, g, w, cos, sa, sb, n_batch, n_ctx_tiles):
    B, _, D = h_parts[1].shape
    N = w.shape[1]
    nt = S // TILE
    nb = INPROJ_NB
    assert B % nb == 0
    tab = pl.BlockSpec((TILE, GROUP_W), lambda b, t: (t, 0))
    return pl.pallas_call(
        functools.partial(_inproj_kernel, n_ctx_tiles),
        grid=(B // nb, nt),
        in_specs=_seq_specs(h_parts, (nb, TILE, D), n_ctx_tiles) + [
            pl.BlockSpec((nb, 6, D), lambda b, t: (b, 0, 0)),
            pl.BlockSpec((1, 6, D), lambda b, t: (n_batch, 0, 0)),
            pl.BlockSpec((1, D), lambda b, t: (0, 0)),
            pl.BlockSpec((D, N), lambda b, t: (0, 0)),
            tab, tab, tab,
        ],
        out_specs=[
            pl.BlockSpec((nb, TILE, 3 * GROUP_W), lambda b, t: (b, t, 0)),
            pl.BlockSpec((nb, TILE, 8 * GROUP_W), lambda b, t: (b, t, 0)),
            pl.BlockSpec((TILE // S5_L, nb, S5_L, GROUP_W), lambda b, t: (t, b, 0, 0)),
        ],
        out_shape=[
            jax.ShapeDtypeStruct((B, S, 3 * GROUP_W), BF16),
            jax.ShapeDtypeStruct((B, S, 8 * GROUP_W), F32),
            jax.ShapeDtypeStruct((S // S5_L, B, S5_L, GROUP_W), F32),
        ],
        compiler_params=_params("parallel", "arbitrary"),
        name="inproj",
    )(h_parts[0], h_parts[1], mod, mod, g.reshape(1, D), w, cos, sa, sb)


def _attn_kernel(lam_init, ctx_len, q_off, q_ref, k_ref, v_ref, lamv_ref, g_ref, o_ref,
                 sc_ref, e_ref, va_ref, vb_ref):
    tq = q_ref.shape[1]
    n_keys = k_ref.shape[1]
    lv = lamv_ref[...]
    lam = (jnp.exp(jnp.sum(lv[0:1] * lv[1:2], axis=-1, keepdims=True))
           - jnp.exp(jnp.sum(lv[2:3] * lv[3:4], axis=-1, keepdims=True)) + lam_init)
    lane = lax.broadcasted_iota(jnp.int32, (1, GROUP_W), 1)
    lane_map = lane >> 5
    lane_head = lane >> 6

    @pl.when(pl.program_id(1) == 0)
    def _():
        v = v_ref[0]
        one = jnp.ones_like(v)
        va_ref[...] = jnp.where(lane == GROUP_W - 1, one, v)
        vb_ref[...] = jnp.where(lane == 0, one, v)

    q = q_ref[0]
    q8 = jnp.concatenate(
        [jnp.where(lane_map == j, q, jnp.zeros_like(q)) for j in range(2 * DA_HEADS)], axis=0)

    def attend(nk):
        sc_ref[:, 0:nk] = _dot_nt(q8, k_ref[0, 0:nk, :])
        acc = jnp.zeros((tq, GROUP_W), F32)
        for h in range(DA_HEADS):
            for m in range(2):
                rows = slice((2 * h + m) * tq, (2 * h + m + 1) * tq)
                s = sc_ref[rows, 0:nk]
                e_ref[rows, 0:nk] = jnp.exp2(s - jnp.max(s, axis=-1, keepdims=True)).astype(BF16)
            vv = va_ref if h < DA_HEADS - 1 else vb_ref
            sum_col = GROUP_W - 1 if h < DA_HEADS - 1 else 0
            o2 = _dot(e_ref[2 * h * tq:(2 * h + 2) * tq, 0:nk], vv[0:nk, :])
            o0, o1 = o2[0:tq], o2[tq:2 * tq]
            w0 = 1.0 / o0[:, sum_col:sum_col + 1]
            w1 = lam / o1[:, sum_col:sum_col + 1]
            acc = acc + jnp.where(lane_head == h, o0 * w0 - o1 * w1, 0.0)
        ms = _split2_dot(acc * acc, _group_mean_matrix(GROUP_W, 2 * DA_HD))
        o_ref[0] = (acc * lax.rsqrt(ms + EPS) * g_ref[...] * (1.0 - lam_init)).astype(o_ref.dtype)

    if q_off == 0 and ctx_len == tq:
        @pl.when(pl.program_id(1) == 0)
        def _():
            attend(ctx_len)

        @pl.when(pl.program_id(1) != 0)
        def _():
            attend(n_keys)
    else:
        attend(n_keys)


def _attention(qkv, lam_vecs, subln_g, lam_init, ctx_len, want_ctx):
    B, S, _ = qkv.shape
    tq = TILE
    q_off = 0 if want_ctx else ctx_len // tq
    nq = S // tq - q_off
    g = jnp.tile(subln_g.astype(F32), DA_HEADS).reshape(1, GROUP_W)
    return pl.pallas_call(
        functools.partial(_attn_kernel, lam_init, ctx_len, q_off),
        grid=(B, nq),
        in_specs=[
            pl.BlockSpec((1, tq, GROUP_W), lambda b, i: (b, i + q_off, 0)),
            pl.BlockSpec((1, S, GROUP_W), lambda b, i: (b, 0, 1)),
            pl.BlockSpec((1, S, GROUP_W), lambda b, i: (b, 0, 2)),
            pl.BlockSpec((4, DA_HD), lambda b, i: (0, 0)),
            pl.BlockSpec((1, GROUP_W), lambda b, i: (0, 0)),
        ],
        out_specs=pl.BlockSpec((1, tq, GROUP_W), lambda b, i: (b, i, 0)),
        out_shape=jax.ShapeDtypeStruct((B, nq * tq, GROUP_W), BF16),
        scratch_shapes=[
            pltpu.VMEM((2 * DA_HEADS * tq, S), F32),
            pltpu.VMEM((2 * DA_HEADS * tq, S), BF16),
            pltpu.VMEM((S, GROUP_W), BF16),
            pltpu.VMEM((S, GROUP_W), BF16),
        ],
        compiler_params=_params("parallel", "arbitrary"),
        name="diff_attention",
    )(qkv, qkv, qkv, lam_vecs.astype(F32), g)


def _hgrn_kernel(n_tiles, q_ref, ff_ref, fb_ref, i_ref, g_ref, lb_ref, lbt_ref, ng_ref, o_ref,
                 od_ref, st2_ref, u2_ref, oc2_ref):
    n_blocks = TILE // HG_BLOCK
    chunks_per_block = HG_BLOCK // HG_CHUNK
    row = lax.broadcasted_iota(jnp.int32, (TILE, TILE), 0)
    col = lax.broadcasted_iota(jnp.int32, (TILE, TILE), 1)
    same_chunk = (row >> 4) == (col >> 4)
    same_block = (row >> 6) == (col >> 6)
    lane = lax.broadcasted_iota(jnp.int32, (1, GROUP_W), 1)
    lane_head = lane >> 6
    lane_block = lane >> 6
    lane_pos = (lane >> 4) & (chunks_per_block - 1)
    gmean = _group_mean_matrix(GROUP_W, HG_HD)

    def ones(mask):
        return jnp.where(mask, 1.0, 0.0).astype(BF16)

    consts = []
    for d in range(2):
        dist = ((row >> 4) - (col >> 4)) if d == 0 else ((col >> 4) - (row >> 4))
        causal = same_chunk & ((col <= row) if d == 0 else (col >= row))
        allowed = same_block & ((dist >= 1) | causal)
        row_mats = [ones(causal), ones(same_block & (dist >= 1))]
        col_mats = [ones(same_chunk & ((row <= col) if d == 0 else (row >= col))), ones(same_chunk)]
        consts.append((allowed, jnp.concatenate(row_mats, axis=0), jnp.concatenate(col_mats, axis=1)))

    st2_ref[...] = jnp.zeros_like(st2_ref)

    def stack_heads(x):
        zero = jnp.zeros_like(x)
        return jnp.concatenate(
            [jnp.where(lane_head == h, x, zero).astype(BF16) for h in range(HG_HEADS)], axis=0)

    def tile_body(ti, carry):
        for d in range(2):
            allowed, row_mat, col_mat = consts[d]
            f_ref = ff_ref if d == 0 else fb_ref
            lb = lb_ref[d:d + 1, :]
            lb_t = lbt_ref[:, d:d + 1]
            st_ref, u_ref, oc_ref = st2_ref.at[d], u2_ref.at[d], oc2_ref.at[d]
            if d == 0:
                t = ti
            else:
                t = jnp.where(ti == 0, 0, n_tiles - ti)
            rows = pl.ds(pl.multiple_of(t * TILE, TILE), TILE)
            fr = f_ref[0, rows, :]
            lf = jnp.log(lb + (1.0 - lb) * jax.nn.sigmoid(fr))
            sums = _split2_dot_rhs(row_mat, lf)
            qd = _silu(q_ref[0, rows, :]) * jnp.exp(sums[0:TILE])
            q_state = qd * jnp.exp(sums[TILE:2 * TILE])
            f_t = lb_t + (1.0 - lb_t) * jax.nn.sigmoid(fr.T)
            kk_t = 1.0 - f_t
            sums_t = _split2_dot(jnp.log(f_t), col_mat)
            g_t, gt_t = sums_t[:, 0:TILE], sums_t[:, TILE:2 * TILE]
            kend_t = kk_t * jnp.exp(gt_t - g_t)
            k_var = [kk_t * jnp.exp(-g_t), kend_t]
            later = jnp.zeros_like(gt_t)
            for m in range(1, chunks_per_block):
                if d == 0:
                    shifted = pltpu.roll(gt_t, TILE - m * HG_CHUNK, 1)
                    inside = lane_pos <= chunks_per_block - 1 - m
                else:
                    shifted = pltpu.roll(gt_t, m * HG_CHUNK, 1)
                    inside = lane_pos >= m
                later = later + jnp.where(inside, shifted, 0.0)
                if m + 1 < chunks_per_block:
                    k_var.append(kend_t * jnp.exp(later))
            kblk_t = kend_t * jnp.exp(later)
            dec_t = jnp.exp(gt_t + later)
            vb = i_ref[0, rows, :].astype(BF16)

            qd4 = stack_heads(qd)
            x_pos = []
            for p in range(chunks_per_block):
                keys = jnp.zeros_like(kend_t)
                for n in range(chunks_per_block):
                    pc = p - n if d == 0 else p + n
                    if 0 <= pc < chunks_per_block:
                        keys = jnp.where(lane_pos == pc, k_var[n], keys)
                q_p = jnp.concatenate(
                    [qd4[h * TILE + b * HG_BLOCK + p * HG_CHUNK:h * TILE + b * HG_BLOCK + (p + 1) * HG_CHUNK]
                     for h in range(HG_HEADS) for b in range(n_blocks)], axis=0)
                x_pos.append(_dot(q_p, keys.astype(BF16)))
            a_heads = []
            for h in range(HG_HEADS):
                a = jnp.concatenate(
                    [x_pos[p][(h * n_blocks + b) * HG_CHUNK:(h * n_blocks + b + 1) * HG_CHUNK]
                     for b in range(n_blocks) for p in range(chunks_per_block)], axis=0)
                a_heads.append(jnp.where(allowed, a, 0.0).astype(BF16))
            zb = jnp.zeros_like(vb)
            v4 = jnp.concatenate(
                [jnp.where(lane_head == h, vb, zb) for h in range(HG_HEADS)], axis=0)
            o_tile = _dot(jnp.concatenate(a_heads, axis=1), v4)

            zero = jnp.zeros_like(kblk_t)
            k_stack = jnp.concatenate(
                [jnp.where(lane_block == b, kblk_t, zero).astype(BF16) for b in range(n_blocks)], axis=0)
            u_ref[...] = _dot(k_stack, vb)

            qs4 = stack_heads(q_state)
            order = range(n_blocks) if d == 0 else range(n_blocks - 1, -1, -1)
            for b in order:
                st = st_ref[...]
                q4b = jnp.concatenate(
                    [qs4[h * TILE + b * HG_BLOCK:h * TILE + (b + 1) * HG_BLOCK] for h in range(HG_HEADS)],
                    axis=0)
                ob4 = _dot(q4b, st.astype(BF16))
                ob = jnp.zeros((HG_BLOCK, GROUP_W), F32)
                for h in range(HG_HEADS):
                    ob = ob + jnp.where(lane_head == h, ob4[h * HG_BLOCK:(h + 1) * HG_BLOCK], 0.0)
                oc_ref[b * HG_BLOCK:(b + 1) * HG_BLOCK, :] = ob
                first = b * HG_BLOCK + (0 if d == 0 else HG_BLOCK - HG_CHUNK)
                st_ref[...] = st * dec_t[:, first:first + 1] + u_ref[b * TILE:(b + 1) * TILE, :]
            od_ref[d, rows, :] = o_tile + oc_ref[...]
        return carry

    lax.fori_loop(0, n_tiles, tile_body, 0)

    def finish_body(t, carry):
        rows = pl.ds(pl.multiple_of(t * TILE, TILE), TILE)
        tot = od_ref[0, rows, :] + od_ref[1, rows, :]
        ms = _split2_dot(tot * tot, gmean)
        o_ref[0, rows, :] = (tot * lax.rsqrt(ms + EPS) * ng_ref[...]
                             * _silu(g_ref[0, rows, :])).astype(o_ref.dtype)
        return carry

    lax.fori_loop(0, n_tiles, finish_body, 0)


def _hgrn(rest, lb, norm_g):
    B, S, _ = rest.shape
    n_tiles = S // TILE

    def part(j):
        return pl.BlockSpec((1, S, GROUP_W), lambda b: (b, 0, j))

    ng = jnp.tile(norm_g.astype(F32), HG_HEADS).reshape(1, GROUP_W)
    return pl.pallas_call(
        functools.partial(_hgrn_kernel, n_tiles),
        grid=(B,),
        in_specs=[part(0), part(1), part(2), part(3), part(4),
                  pl.BlockSpec((2, GROUP_W), lambda b: (0, 0)),
                  pl.BlockSpec((GROUP_W, 2), lambda b: (0, 0)),
                  pl.BlockSpec((1, GROUP_W), lambda b: (0, 0))],
        out_specs=pl.BlockSpec((1, S, GROUP_W), lambda b: (b, 0, 0)),
        out_shape=jax.ShapeDtypeStruct((B, S, GROUP_W), BF16),
        scratch_shapes=[
            pltpu.VMEM((2, S, GROUP_W), F32),
            pltpu.VMEM((2, GROUP_W, GROUP_W), F32),
            pltpu.VMEM((2, TILE // HG_BLOCK * TILE, GROUP_W), F32),
            pltpu.VMEM((2, TILE, GROUP_W), F32),
        ],
        compiler_params=_params("parallel"),
        name="hgrn2",
    )(rest, rest, rest, rest, rest, lb, lb.T, ng)


def _conv_kernel(ctx_len, b_ref, c_ref, u_ref, w_ref, bias_ref, o_ref):
    S = b_ref.shape[1]
    v = c_ref[0] * u_ref[0]
    row = lax.broadcasted_iota(jnp.int32, (S, 1), 0)
    prev = jnp.where((row == 0) | (row == ctx_len), 0.0, pltpu.roll(v, 1, 0))
    nxt = jnp.where((row == ctx_len - 1) | (row == S - 1), 0.0, pltpu.roll(v, S - 1, 0))
    y = w_ref[0:1, :] * prev + w_ref[1:2, :] * v + w_ref[2:3, :] * nxt
    o_ref[0] = (b_ref[0] * (y + bias_ref[...])).astype(o_ref.dtype)


def _short_conv(rest, w, bias, ctx_len):
    B, S, _ = rest.shape

    def part(j):
        return pl.BlockSpec((1, S, GROUP_W), lambda b: (b, 0, j))

    return pl.pallas_call(
        functools.partial(_conv_kernel, ctx_len),
        grid=(B,),
        in_specs=[part(5), part(6), part(7),
                  pl.BlockSpec((3, GROUP_W), lambda b: (0, 0)),
                  pl.BlockSpec((1, GROUP_W), lambda b: (0, 0))],
        out_specs=pl.BlockSpec((1, S, GROUP_W), lambda b: (b, 0, 0)),
        out_shape=jax.ShapeDtypeStruct((B, S, GROUP_W), BF16),
        compiler_params=_params("parallel"),
        name="short_conv",
    )(rest, rest, rest, w.astype(F32), bias.astype(F32).reshape(1, GROUP_W))


def _s5_back_tile(i, n_ctx_tiles, n_tiles):
    return jnp.where(i < n_ctx_tiles, n_ctx_tiles - 1 - i, n_tiles - 1 - (i - n_ctx_tiles))


def _s5_scan_kernel(uf_ref, ub_ref, bmat_ref, cmat_ref, a_ref, d_ref, yf_ref, yb_ref,
                    xsf_ref, xsb_ref, st_ref):
    tt = uf_ref.shape[1]
    n_rows = tt * SUBLANES
    n_strips = S5_COLS // LANES

    @pl.when(pl.program_id(1) == 0)
    def _():
        st_ref[...] = jnp.zeros_like(st_ref)

    def lanes(j):
        return slice(j * LANES, (j + 1) * LANES)

    us = []
    for d, (u_ref, xs_ref) in enumerate(((uf_ref, xsf_ref), (ub_ref, xsb_ref))):
        u = u_ref[0].reshape(n_rows, GROUP_W)
        xs_ref[...] = _dot(u.astype(BF16), bmat_ref[d])
        us.append(u)

    for d, (xs_ref, y_ref) in enumerate(((xsf_ref, yf_ref), (xsb_ref, yb_ref))):
        a_re = [jnp.broadcast_to(a_ref[d, 0:1, lanes(j)], (SUBLANES, LANES)) for j in range(n_strips)]
        a_im = [jnp.broadcast_to(a_ref[d, 1:2, lanes(j)], (SUBLANES, LANES)) for j in range(n_strips)]
        state = [st_ref[d, k] for k in range(2 * n_strips)]
        for s in range(tt):
            t = s if d == 0 else tt - 1 - s
            r = slice(t * SUBLANES, (t + 1) * SUBLANES)
            for j in range(n_strips):
                xr, xi = state[2 * j], state[2 * j + 1]
                nr = a_re[j] * xr - a_im[j] * xi + xs_ref[r, lanes(j)]
                ni = a_re[j] * xi + a_im[j] * xr + xs_ref[r, lanes(n_strips + j)]
                xs_ref[r, lanes(j)] = nr
                xs_ref[r, lanes(n_strips + j)] = ni
                state[2 * j], state[2 * j + 1] = nr, ni
        for k in range(2 * n_strips):
            st_ref[d, k] = state[k]
        y = _dot(xs_ref[...].astype(BF16), cmat_ref[d])
        if d == 0:
            y = y + d_ref[...] * us[0]
        y_ref[0] = y.reshape(tt, SUBLANES, GROUP_W)


def _s5_finish_kernel(yf_ref, yb_ref, wg_ref, bg_ref, o_ref):
    _, tt, nb, width = yf_ref.shape
    y = jax.nn.gelu(yf_ref[0].reshape(tt * nb, width) + yb_ref[0].reshape(tt * nb, width))
    gate = jax.nn.sigmoid(_dot(y.astype(BF16), wg_ref[...]) + bg_ref[...])
    o_ref[0] = (y * gate).reshape(tt, nb, width)


def _s5_discretise(a_re, a_im, log_dt, b_re, b_im):
    dt = jnp.exp(log_dt)[..., None]
    mag = jnp.exp(dt * a_re)
    ab_re, ab_im = mag * jnp.cos(dt * a_im), mag * jnp.sin(dt * a_im)
    den = a_re * a_re + a_im * a_im
    nr = ab_re - 1.0
    f_re = (nr * a_re + ab_im * a_im) / den
    f_im = (ab_im * a_re - nr * a_im) / den
    bb_re = f_re[..., None] * b_re - f_im[..., None] * b_im
    bb_im = f_re[..., None] * b_im + f_im[..., None] * b_re
    return ab_re, ab_im, bb_re, bb_im


def _block_diag(blocks):
    G, r, c = blocks.shape
    eye = jnp.eye(G, dtype=blocks.dtype)
    return (eye[:, None, :, None] * blocks[:, :, None, :]).reshape(G * r, G * c)


def _s5_matrices(a_re, a_im, log_dt, b_re, b_im, c_re, c_im):
    ab_re, ab_im, bb_re, bb_im = _s5_discretise(
        a_re.astype(F32), a_im.astype(F32), log_dt.astype(F32), b_re.astype(F32), b_im.astype(F32))
    bmats, cmats = [], []
    for d in range(2):
        br = _block_diag(jnp.swapaxes(bb_re[d], 1, 2))
        bi = _block_diag(jnp.swapaxes(bb_im[d], 1, 2))
        bmats.append(jnp.concatenate([br, bi], axis=1))
        cr = _block_diag(jnp.swapaxes(c_re[d].astype(F32), 1, 2))
        ci = _block_diag(jnp.swapaxes(c_im[d].astype(F32), 1, 2))
        cmats.append(jnp.concatenate([cr, -ci], axis=0))
    abar = jnp.stack([ab_re.reshape(2, S5_COLS), ab_im.reshape(2, S5_COLS)], axis=1)
    return jnp.stack(bmats).astype(BF16), jnp.stack(cmats).astype(BF16), abar


def _s5(u5, bmat, cmat, abar, d_skip, w_glu, b_glu, ctx_len):
    B, S, _ = u5.shape
    assert B % SUBLANES == 0 and ctx_len % S5_TT == 0 and S % S5_TT == 0
    ng = B // SUBLANES
    n_tiles = S // S5_TT
    n_ctx_tiles = ctx_len // S5_TT
    u = u5.reshape(ng, SUBLANES, S, GROUP_W).transpose(0, 2, 1, 3)

    def fwd_idx(g, i):
        return (g, i, 0, 0)

    def bwd_idx(g, i):
        return (g, _s5_back_tile(i, n_ctx_tiles, n_tiles), 0, 0)

    def whole(arr):
        return pl.BlockSpec(arr.shape, lambda g, i: (0,) * arr.ndim)

    tile_shape = (1, S5_TT, SUBLANES, GROUP_W)
    xs_shape = pltpu.VMEM((S5_TT * SUBLANES, 2 * S5_COLS), F32)
    y_sds = jax.ShapeDtypeStruct((ng, S, SUBLANES, GROUP_W), F32)
    d_row = d_skip.astype(F32).reshape(1, GROUP_W)
    yf, yb = pl.pallas_call(
        _s5_scan_kernel,
        grid=(ng, n_tiles),
        in_specs=[pl.BlockSpec(tile_shape, fwd_idx), pl.BlockSpec(tile_shape, bwd_idx),
                  whole(bmat), whole(cmat), whole(abar), whole(d_row)],
        out_specs=[pl.BlockSpec(tile_shape, fwd_idx), pl.BlockSpec(tile_shape, bwd_idx)],
        out_shape=[y_sds, y_sds],
        scratch_shapes=[
            xs_shape, xs_shape,
            pltpu.VMEM((2, 2 * S5_COLS // LANES, SUBLANES, LANES), F32),
        ],
        compiler_params=_params("parallel", "arbitrary"),
        name="s5_scan",
    )(u, u, bmat, cmat, abar, d_row)

    w_b = w_glu.astype(BF16)
    b_row = b_glu.astype(F32).reshape(1, GROUP_W)
    fin_shape = (1, TILE, SUBLANES, GROUP_W)
    out = pl.pallas_call(
        _s5_finish_kernel,
        grid=(ng, S // TILE),
        in_specs=[pl.BlockSpec(fin_shape, fwd_idx), pl.BlockSpec(fin_shape, fwd_idx),
                  whole(w_b), whole(b_row)],
        out_specs=pl.BlockSpec(fin_shape, fwd_idx),
        out_shape=y_sds,
        compiler_params=_params("parallel", "arbitrary"),
        name="s5_finish",
    )(yf, yb, w_b, b_row)
    return out.transpose(0, 2, 1, 3).reshape(B, S, GROUP_W).astype(BF16)


def _cpow_table(re1, im1, n):
    re, im = [jnp.ones_like(re1)], [jnp.zeros_like(im1)]
    for _ in range(n):
        r, i = re[-1], im[-1]
        re.append(r * re1 - i * im1)
        im.append(r * im1 + i * re1)
    return jnp.stack(re), jnp.stack(im)


def _s5_chunk_matrices(a_re, a_im, log_dt, b_re, b_im, c_re, c_im, d_skip):
    L, G, P, N = S5_L, S5_NGROUPS, S5_GROUP, S5_STATE
    hp = lax.Precision.HIGHEST
    ab_re, ab_im, bb_re, bb_im = _s5_discretise(
        a_re.astype(F32), a_im.astype(F32), log_dt.astype(F32), b_re.astype(F32), b_im.astype(F32))
    c_re, c_im = c_re.astype(F32), c_im.astype(F32)
    pr, pi = _cpow_table(ab_re, ab_im, L)
    ca_re = c_re[None] * pr[:, :, :, None, :] - c_im[None] * pi[:, :, :, None, :]
    ca_im = c_re[None] * pi[:, :, :, None, :] + c_im[None] * pr[:, :, :, None, :]
    k = (jnp.einsum("jdgqn,dgnp->jdgqp", ca_re[:L], bb_re, precision=hp)
         - jnp.einsum("jdgqn,dgnp->jdgqp", ca_im[:L], bb_im, precision=hp))
    steps = jnp.arange(L)
    lag = steps[None, :] - steps[:, None]

    def toeplitz(kd, lag):
        kk = jnp.where((lag >= 0)[:, :, None, None, None], kd[jnp.clip(lag, 0, L - 1)], 0.0)
        return kk.transpose(2, 4, 0, 3, 1)

    tmat = (toeplitz(k[:, 0], lag) + toeplitz(k[:, 1], -lag)).reshape(G, L * P, L * P)
    skip = jnp.repeat(d_skip.astype(F32).reshape(G, P), L, axis=1).reshape(G, 1, L * P)
    tmat = tmat + jnp.eye(L * P, dtype=F32)[None] * skip

    def to_state(pw_re, pw_im, d):
        br = jnp.swapaxes(bb_re[d], 1, 2)[None]
        bi = jnp.swapaxes(bb_im[d], 1, 2)[None]
        re = pw_re[:, :, None, :] * br - pw_im[:, :, None, :] * bi
        im = pw_re[:, :, None, :] * bi + pw_im[:, :, None, :] * br
        return (re.transpose(1, 2, 0, 3).reshape(G, L * P, N),
                im.transpose(1, 2, 0, 3).reshape(G, L * P, N))

    f_re, f_im = to_state(pr[:L, 0][::-1], pi[:L, 0][::-1], 0)
    g_re, g_im = to_state(pr[:L, 1], pi[:L, 1], 1)
    bs = jnp.concatenate([f_re, g_re, f_im, g_im], axis=2)

    def from_state(x, d, flip):
        xd = x[::-1, d] if flip else x[:, d]
        return xd.transpose(1, 3, 2, 0).reshape(G, N, L * P)

    cs = jnp.concatenate([from_state(ca_re[1:], 0, False), from_state(ca_re[1:], 1, True),
                          -from_state(ca_im[1:], 0, False), -from_state(ca_im[1:], 1, True)], axis=1)
    a_l = jnp.stack([jnp.concatenate([pr[L, 0], pr[L, 1]], axis=-1),
                     jnp.concatenate([pi[L, 0], pi[L, 1]], axis=-1)], axis=1)
    return tmat.astype(BF16), bs.astype(BF16), cs.astype(BF16), a_l


def _s5_chunk_kernel(n_ctx_chunks, n_chunks, u_ref, t_ref, bs_ref, cs_ref, al_ref, y_ref,
                     v_ref, xs_ref):
    n_ch, n_steps, n_rows = u_ref.shape
    nb = n_rows // n_chunks
    n = S5_STATE
    u = u_ref[...].reshape(n_ch * n_steps, n_rows).T.astype(BF16)
    v_ref[...] = _dot(u, bs_ref[0])
    fwd_lanes = lax.broadcasted_iota(jnp.int32, (1, 2 * n), 1) < n
    a_re = jnp.broadcast_to(al_ref[0, 0:1, :], (nb, 2 * n))
    a_im = jnp.broadcast_to(al_ref[0, 1:2, :], (nb, 2 * n))

    def step(i, carry):
        xr, xi = carry
        cb = jnp.where(i < n_ctx_chunks, n_ctx_chunks - 1 - i, n_chunks - 1 - (i - n_ctx_chunks))
        rf = pl.ds(pl.multiple_of(i * nb, nb), nb)
        rb = pl.ds(pl.multiple_of(cb * nb, nb), nb)
        xs_ref[rf, 0:n] = xr[:, 0:n]
        xs_ref[rb, n:2 * n] = xr[:, n:2 * n]
        xs_ref[rf, 2 * n:3 * n] = xi[:, 0:n]
        xs_ref[rb, 3 * n:4 * n] = xi[:, n:2 * n]
        vr = jnp.where(fwd_lanes, v_ref[rf, 0:2 * n], v_ref[rb, 0:2 * n])
        vi = jnp.where(fwd_lanes, v_ref[rf, 2 * n:4 * n], v_ref[rb, 2 * n:4 * n])
        return a_re * xr - a_im * xi + vr, a_re * xi + a_im * xr + vi

    zero = jnp.zeros((nb, 2 * n), F32)
    lax.fori_loop(0, n_chunks, step, (zero, zero))
    y = _dot(u, t_ref[0]) + _dot(xs_ref[...].astype(BF16), cs_ref[0])
    y_ref[...] = y.T.reshape(n_ch, n_steps, n_rows)


def _s5_glu_kernel(y_ref, wg_ref, bg_ref, o_ref):
    n_chunks, nb, n_steps, width = y_ref.shape
    for j in range(nb):
        y = jax.nn.gelu(y_ref[:, j, :, :].reshape(n_chunks * n_steps, width))
        gate = jax.nn.sigmoid(_dot(y.astype(BF16), wg_ref[...]) + bg_ref[...])
        o_ref[j] = (y * gate).astype(o_ref.dtype)


def _s5_chunked(u5c, tmat, bs, cs, a_l, w_glu, b_glu, ctx_len):
    n_chunks, B, L, W = u5c.shape
    P = S5_GROUP
    S = n_chunks * L
    assert L == S5_L and ctx_len % L == 0 and TILE % L == 0
    rows = n_chunks * B
    ut = u5c.reshape(rows, L, W).transpose(2, 1, 0)

    def per_group(arr):
        return pl.BlockSpec((1,) + arr.shape[1:], lambda g: (g,) + (0,) * (arr.ndim - 1))

    group_block = pl.BlockSpec((P, L, rows), lambda g: (g, 0, 0))
    yt = pl.pallas_call(
        functools.partial(_s5_chunk_kernel, ctx_len // L, n_chunks),
        grid=(W // P,),
        in_specs=[group_block, per_group(tmat), per_group(bs), per_group(cs), per_group(a_l)],
        out_specs=group_block,
        out_shape=jax.ShapeDtypeStruct((W, L, rows), F32),
        scratch_shapes=[pltpu.VMEM((rows, 4 * S5_STATE), F32),
                        pltpu.VMEM((rows, 4 * S5_STATE), F32)],
        compiler_params=_params("parallel"),
        name="s5_chunks",
    )(ut, tmat, bs, cs, a_l)
    y = yt.transpose(2, 1, 0).reshape(n_chunks, B, L, W)

    nb = POST_NB
    tile_chunks = TILE // L
    w_b = w_glu.astype(BF16)
    b_row = b_glu.astype(F32).reshape(1, W)
    return pl.pallas_call(
        _s5_glu_kernel,
        grid=(B // nb, n_chunks // tile_chunks),
        in_specs=[pl.BlockSpec((tile_chunks, nb, L, W), lambda b, t: (t, b, 0, 0)),
                  pl.BlockSpec(w_b.shape, lambda b, t: (0, 0)),
                  pl.BlockSpec(b_row.shape, lambda b, t: (0, 0))],
        out_specs=pl.BlockSpec((nb, TILE, W), lambda b, t: (b, t, 0)),
        out_shape=jax.ShapeDtypeStruct((B, S, W), BF16),
        compiler_params=_params("parallel", "arbitrary"),
        name="s5_glu",
    )(y, w_b, b_row)


def _post_kernel(n_hidden_chunks, ctx_tiles_here, hc_ref, hx_ref, a_ref, b_ref, c_ref, d_ref,
                 modx_ref, modc_ref, ng_ref, wo_ref, wi_ref, w2_ref, o_ref):
    nb, tile, d_model = hx_ref.shape
    n_rows = nb * tile
    is_ctx = pl.program_id(1) < ctx_tiles_here
    mod = _tile_mod(is_ctx, modx_ref, modc_ref)

    def flat(ref):
        return ref[...].reshape(n_rows, GROUP_W).astype(BF16)

    mix = (_dot(flat(a_ref), wo_ref[0:GROUP_W, :])
           + _dot(flat(b_ref), wo_ref[GROUP_W:2 * GROUP_W, :])
           + _dot(flat(c_ref), wo_ref[2 * GROUP_W:3 * GROUP_W, :])
           + _dot(flat(d_ref), wo_ref[3 * GROUP_W:4 * GROUP_W, :])).reshape(nb, tile, d_model)
    h = (jnp.where(is_ctx, hc_ref[...], hx_ref[...])
         + mod[:, 2:3, :] * (mix * _rms_scale(mix) * ng_ref[1:2, :]))
    y = (h * _rms_scale(h) * ng_ref[2:3, :]) * (1.0 + mod[:, 4:5, :]) + mod[:, 3:4, :]
    yb = y.reshape(n_rows, d_model).astype(BF16)
    hidden = w2_ref.shape[0]
    hc = hidden // n_hidden_chunks
    ffn = jnp.zeros((n_rows, d_model), F32)
    for j in range(n_hidden_chunks):
        gate = _dot(yb, wi_ref[:, j * hc:(j + 1) * hc])
        up = _dot(yb, wi_ref[:, hidden + j * hc:hidden + (j + 1) * hc])
        ffn = ffn + _dot((_silu(gate) * up).astype(BF16), w2_ref[j * hc:(j + 1) * hc, :])
    ffn = ffn.reshape(nb, tile, d_model)
    o_ref[...] = h + mod[:, 5:6, :] * (ffn * _rms_scale(ffn) * ng_ref[3:4, :])


def _post(h_parts, S, a, b, c, d, mod, norm_g, w_out, w_ffn_in, w_ffn_out, n_batch, n_ctx_tiles,
          want_ctx):
    B, _, D = h_parts[1].shape
    off = 0 if want_ctx else n_ctx_tiles
    nt = S // TILE - off
    a_off = off if a.shape[1] == S else 0
    nb = POST_NB
    assert B % nb == 0

    def rows(o):
        return lambda bi, t: (bi, t + o, 0)

    def whole(arr):
        return pl.BlockSpec(arr.shape, lambda bi, t: (0,) * arr.ndim, pipeline_mode=pl.Buffered(1))

    mix_spec = lambda o: pl.BlockSpec((nb, TILE, GROUP_W), rows(o))
    return pl.pallas_call(
        functools.partial(_post_kernel, 2, n_ctx_tiles - off),
        grid=(B // nb, nt),
        in_specs=_seq_specs(h_parts, (nb, TILE, D), n_ctx_tiles, first_tile=off) + [
            mix_spec(a_off), mix_spec(off), mix_spec(off), mix_spec(off),
            pl.BlockSpec((nb, 6, D), lambda bi, t: (bi, 0, 0)),
            pl.BlockSpec((1, 6, D), lambda bi, t: (n_batch, 0, 0)),
            whole(norm_g), whole(w_out), whole(w_ffn_in), whole(w_ffn_out),
        ],
        out_specs=pl.BlockSpec((nb, TILE, D), lambda bi, t: (bi, t, 0)),
        out_shape=jax.ShapeDtypeStruct((B, nt * TILE, D), F32),
        compiler_params=_params("parallel", "arbitrary"),
        name="post",
    )(h_parts[0], h_parts[1], a, b, c, d, mod, mod, norm_g, w_out, w_ffn_in, w_ffn_out)


def _rope_tables(n_rows, ctx_len):
    rows = jnp.broadcast_to(jnp.arange(n_rows, dtype=F32)[:, None], (n_rows, GRID_W)).reshape(-1)
    cols = jnp.broadcast_to(jnp.arange(GRID_W, dtype=F32)[None, :], (n_rows, GRID_W)).reshape(-1)
    n_freq = DA_HD // 4
    inv = ROPE_BASE ** (-jnp.arange(n_freq, dtype=F32) / n_freq)
    ang = jnp.concatenate([rows[:, None] * inv, cols[:, None] * inv], axis=-1)
    cos, sin = jnp.cos(ang), jnp.sin(ang)
    zero = jnp.zeros_like(sin)
    reps = GROUP_W // DA_HD

    def lanes(first, second, ctx_value):
        t = jnp.tile(jnp.concatenate([first, second], axis=-1), (1, reps))
        return jnp.concatenate([jnp.full((ctx_len, GROUP_W), ctx_value, F32), t], axis=0)

    return lanes(cos, cos, 1.0), lanes(-sin, zero, 0.0), lanes(zero, sin, 0.0)


def _deinterleave_qk(w_in):
    perm32 = jnp.concatenate([jnp.arange(0, DA_HD, 2), jnp.arange(1, DA_HD, 2)])
    perm = (jnp.arange(0, 2 * GROUP_W, DA_HD)[:, None] + perm32[None, :]).reshape(-1)
    cols = jnp.concatenate([perm, jnp.arange(2 * GROUP_W, w_in.shape[-1])])
    return w_in[..., cols]


def kernel(x, c, ctx, c_ctx, w_ada, b_ada, norm_g, w_in, w_out, da_lambda, da_subln, hg_lb, hg_norm, sc_w, sc_b, s5_a_re, s5_a_im, s5_log_dt, s5_b_re, s5_b_im, s5_c_re, s5_c_im, s5_d, s5_w_glu, s5_b_glu, w_ffn_in, w_ffn_out):
    B, T, D = x.shape
    ctx_len = ctx.shape[1]
    L = w_ada.shape[0]
    assert ctx_len % TILE == 0 and T % TILE == 0 and T % GRID_W == 0
    n_ctx_tiles = ctx_len // TILE

    bp = -(-(B + 1) // SUBLANES) * SUBLANES
    cvec = jnp.concatenate([c, c_ctx[None, :], jnp.zeros((bp - B - 1, D), c.dtype)], axis=0)
    mods = _modulation(cvec.astype(F32), w_ada, b_ada).reshape(L, bp, 6, D)

    cos, sin_a, sin_b = _rope_tables(T // GRID_W, ctx_len)
    lb = jnp.cumsum(jax.nn.softmax(hg_lb.astype(F32), axis=0), axis=0)
    lb = lb - lb[:1]
    w_in_b = _deinterleave_qk(w_in).astype(BF16)
    w_out_b = w_out.astype(BF16)
    w_ffn_in_b = w_ffn_in.astype(BF16)
    w_ffn_out_b = w_ffn_out.astype(BF16)

    S = ctx_len + T
    h_parts = (ctx, x, 0)
    for l in range(L):
        want_ctx = l < L - 1
        lam_init = 0.8 - 0.6 * math.exp(-0.3 * l)
        qkv, rest, u5 = _inproj(h_parts, S, mods[l], norm_g[l, 0], w_in_b[l], cos, sin_a, sin_b, B,
                                n_ctx_tiles)
        a = _attention(qkv, da_lambda[l], da_subln[l], lam_init, ctx_len, want_ctx)
        b = _hgrn(rest, lb[l], hg_norm[l])
        cc = _short_conv(rest, sc_w[l], sc_b[l], ctx_len)
        tmat, bs, cs, a_l = _s5_chunk_matrices(s5_a_re[l], s5_a_im[l], s5_log_dt[l], s5_b_re[l],
                                               s5_b_im[l], s5_c_re[l], s5_c_im[l], s5_d[l])
        dd = _s5_chunked(u5, tmat, bs, cs, a_l, s5_w_glu[l], s5_b_glu[l], ctx_len)
        h = _post(h_parts, S, a, b, cc, dd, mods[l], norm_g[l].astype(F32), w_out_b[l], w_ffn_in_b[l],
                  w_ffn_out_b[l], B, n_ctx_tiles, want_ctx)
        h_parts = (h, h, n_ctx_tiles)
    return h
```

```python
import functools
import math

import jax
import jax.numpy as jnp
from jax import lax
from jax.experimental import pallas as pl
from jax.experimental.pallas import tpu as pltpu

F32 = jnp.float32
BF16 = jnp.bfloat16

EPS = 1e-6
GRID_W = 64
ROPE_BASE = 10000.0
GROUP_W = 256
DA_HEADS = 4
DA_HD = 32
HG_HEADS = 4
HG_HD = 64
HG_CHUNK = 16
HG_BLOCK = 64
S5_NGROUPS = 16
S5_GROUP = 16
S5_STATE = 64
SUBLANES = 8
TILE = 256
INPROJ_NB = 4
POST_NB = 2
S5_L = 16
VMEM_LIMIT = 56 * 1024 * 1024

_NT = (((1,), (1,)), ((), ()))


def _dot(a, b):
    return jnp.dot(a, b, preferred_element_type=F32)


def _dot_nt(a, b):
    return lax.dot_general(a, b, _NT, preferred_element_type=F32)


def _split2_dot_rhs(m, x):
    x1 = x.astype(BF16)
    x2 = (x - x1.astype(F32)).astype(BF16)
    return _dot(m, x1) + _dot(m, x2)


def _split2_dot(x, m):
    x1 = x.astype(BF16)
    x2 = (x - x1.astype(F32)).astype(BF16)
    return _dot(x1, m) + _dot(x2, m)


def _silu(x):
    return x * jax.nn.sigmoid(x)


def _rms_scale(x):
    return lax.rsqrt(jnp.mean(x * x, axis=-1, keepdims=True) + EPS)


def _group_mean_matrix(n, group):
    sh = int(math.log2(group))
    r = lax.broadcasted_iota(jnp.int32, (n, n), 0) >> sh
    c = lax.broadcasted_iota(jnp.int32, (n, n), 1) >> sh
    return jnp.where(r == c, 1.0 / group, 0.0).astype(BF16)


def _params(*sem):
    return pltpu.CompilerParams(dimension_semantics=sem, vmem_limit_bytes=VMEM_LIMIT)


def _mod_kernel(c_ref, w_ref, b_ref, o_ref):
    sc = _silu(c_ref[...])
    o_ref[0] = _dot(sc.astype(BF16), w_ref[0].astype(BF16)) + b_ref[0]


def _modulation(cvec, w_ada, b_ada):
    L, D, N = w_ada.shape
    Bp = cvec.shape[0]
    tn = 1536
    return pl.pallas_call(
        _mod_kernel,
        grid=(L, N // tn),
        in_specs=[
            pl.BlockSpec((Bp, D), lambda l, j: (0, 0)),
            pl.BlockSpec((1, D, tn), lambda l, j: (l, 0, j)),
            pl.BlockSpec((1, 1, tn), lambda l, j: (l, 0, j)),
        ],
        out_specs=pl.BlockSpec((1, Bp, tn), lambda l, j: (l, 0, j)),
        out_shape=jax.ShapeDtypeStruct((L, Bp, N), F32),
        compiler_params=_params("arbitrary", "arbitrary"),
        name="modulation",
    )(cvec, w_ada, b_ada.reshape(L, 1, N))


def _rope(x, cos, sin_a, sin_b):
    return (x * cos + pltpu.roll(x, GROUP_W - DA_HD // 2, 1) * sin_a
            + pltpu.roll(x, DA_HD // 2, 1) * sin_b)


def _tile_mod(is_ctx, modx_ref, modc_ref):
    return jnp.where(is_ctx, modc_ref[...], modx_ref[...])


def _seq_specs(h_parts, block, n_ctx_tiles, first_tile=0):
    _, _, lat_off = h_parts

    def ctx_idx(b, t):
        return (b, jnp.minimum(t + first_tile, n_ctx_tiles - 1), 0)

    def lat_idx(b, t):
        return (b, jnp.maximum(t + first_tile, n_ctx_tiles) - n_ctx_tiles + lat_off, 0)

    return [pl.BlockSpec(block, ctx_idx), pl.BlockSpec(block, lat_idx)]


def _inproj_kernel(n_ctx_tiles, hc_ref, hx_ref, modx_ref, modc_ref, g_ref, w_ref, cos_ref, sa_ref,
                   sb_ref, qkv_ref, rest_ref, u5_ref):
    nb, tile, d_model = hx_ref.shape
    is_ctx = pl.program_id(1) < n_ctx_tiles
    mod = _tile_mod(is_ctx, modx_ref, modc_ref)
    h = jnp.where(is_ctx, hc_ref[...], hx_ref[...])
    hn = h * _rms_scale(h) * g_ref[...]
    y = hn * (1.0 + mod[:, 1:2, :]) + mod[:, 0:1, :]
    p = _dot(y.reshape(nb * tile, d_model).astype(BF16), w_ref[...])
    cos, sa, sb = cos_ref[...], sa_ref[...], sb_ref[...]
    for j in range(nb):
        pj = p[j * tile:(j + 1) * tile]
        q = _rope(pj[:, 0:GROUP_W], cos, sa, sb) * (DA_HD ** -0.5 * math.log2(math.e))
        k = _rope(pj[:, GROUP_W:2 * GROUP_W], cos, sa, sb)
        qkv_ref[j, :, 0:GROUP_W] = q.astype(BF16)
        qkv_ref[j, :, GROUP_W:2 * GROUP_W] = k.astype(BF16)
        qkv_ref[j, :, 2 * GROUP_W:3 * GROUP_W] = pj[:, 2 * GROUP_W:3 * GROUP_W].astype(BF16)
        rest_ref[j] = pj[:, 3 * GROUP_W:11 * GROUP_W]
        u5_ref[:, j, :, :] = pj[:, 11 * GROUP_W:].reshape(tile // S5_L, S5_L, GROUP_W)


def _inproj(h_parts, S, mod, g, w, cos, sa, sb, n_batch, n_ctx_tiles):
    B, _, D = h_parts[1].shape
    N = w.shape[1]
    nt = S // TILE
    nb = INPROJ_NB
    assert B % nb == 0
    tab = pl.BlockSpec((TILE, GROUP_W), lambda b, t: (t, 0))
    return pl.pallas_call(
        functools.partial(_inproj_kernel, n_ctx_tiles),
        grid=(B // nb, nt),
        in_specs=_seq_specs(h_parts, (nb, TILE, D), n_ctx_tiles) + [
            pl.BlockSpec((nb, 6, D), lambda b, t: (b, 0, 0)),
            pl.BlockSpec((1, 6, D), lambda b, t: (n_batch, 0, 0)),
            pl.BlockSpec((1, D), lambda b, t: (0, 0)),
            pl.BlockSpec((D, N), lambda b, t: (0, 0)),
            tab, tab, tab,
        ],
        out_specs=[
            pl.BlockSpec((nb, TILE, 3 * GROUP_W), lambda b, t: (b, t, 0)),
            pl.BlockSpec((nb, TILE, 8 * GROUP_W), lambda b, t: (b, t, 0)),
            pl.BlockSpec((TILE // S5_L, nb, S5_L, GROUP_W), lambda b, t: (t, b, 0, 0)),
        ],
        out_shape=[
            jax.ShapeDtypeStruct((B, S, 3 * GROUP_W), BF16),
            jax.ShapeDtypeStruct((B, S, 8 * GROUP_W), F32),
            jax.ShapeDtypeStruct((S // S5_L, B, S5_L, GROUP_W), F32),
        ],
        compiler_params=_params("parallel", "arbitrary"),
        name="inproj",
    )(h_parts[0], h_parts[1], mod, mod, g.reshape(1, D), w, cos, sa, sb)


def _attn_kernel(lam_init, ctx_len, q_off, q_ref, k_ref, v_ref, lamv_ref, g_ref, o_ref,
                 sc_ref, e_ref, va_ref, vb_ref):
    tq = q_ref.shape[1]
    n_keys = k_ref.shape[1]
    lv = lamv_ref[...]
    lam = (jnp.exp(jnp.sum(lv[0:1] * lv[1:2], axis=-1, keepdims=True))
           - jnp.exp(jnp.sum(lv[2:3] * lv[3:4], axis=-1, keepdims=True)) + lam_init)
    lane = lax.broadcasted_iota(jnp.int32, (1, GROUP_W), 1)
    lane_map = lane >> 5
    lane_head = lane >> 6

    @pl.when(pl.program_id(1) == 0)
    def _():
        v = v_ref[0]
        one = jnp.ones_like(v)
        va_ref[...] = jnp.where(lane == GROUP_W - 1, one, v)
        vb_ref[...] = jnp.where(lane == 0, one, v)

    q = q_ref[0]
    q8 = jnp.concatenate(
        [jnp.where(lane_map == j, q, jnp.zeros_like(q)) for j in range(2 * DA_HEADS)], axis=0)

    def attend(nk):
        sc_ref[:, 0:nk] = _dot_nt(q8, k_ref[0, 0:nk, :])
        acc = jnp.zeros((tq, GROUP_W), F32)
        for h in range(DA_HEADS):
            for m in range(2):
                rows = slice((2 * h + m) * tq, (2 * h + m + 1) * tq)
                s = sc_ref[rows, 0:nk]
                e_ref[rows, 0:nk] = jnp.exp2(s - jnp.max(s, axis=-1, keepdims=True)).astype(BF16)
            vv = va_ref if h < DA_HEADS - 1 else vb_ref
            sum_col = GROUP_W - 1 if h < DA_HEADS - 1 else 0
            o2 = _dot(e_ref[2 * h * tq:(2 * h + 2) * tq, 0:nk], vv[0:nk, :])
            o0, o1 = o2[0:tq], o2[tq:2 * tq]
            w0 = 1.0 / o0[:, sum_col:sum_col + 1]
            w1 = lam / o1[:, sum_col:sum_col + 1]
            acc = acc + jnp.where(lane_head == h, o0 * w0 - o1 * w1, 0.0)
        ms = _split2_dot(acc * acc, _group_mean_matrix(GROUP_W, 2 * DA_HD))
        o_ref[0] = (acc * lax.rsqrt(ms + EPS) * g_ref[...] * (1.0 - lam_init)).astype(o_ref.dtype)

    if q_off == 0 and ctx_len == tq:
        @pl.when(pl.program_id(1) == 0)
        def _():
            attend(ctx_len)

        @pl.when(pl.program_id(1) != 0)
        def _():
            attend(n_keys)
    else:
        attend(n_keys)


def _attention(qkv, lam_vecs, subln_g, lam_init, ctx_len, want_ctx):
    B, S, _ = qkv.shape
    tq = TILE
    q_off = 0 if want_ctx else ctx_len // tq
    nq = S // tq - q_off
    g = jnp.tile(subln_g.astype(F32), DA_HEADS).reshape(1, GROUP_W)
    return pl.pallas_call(
        functools.partial(_attn_kernel, lam_init, ctx_len, q_off),
        grid=(B, nq),
        in_specs=[
            pl.BlockSpec((1, tq, GROUP_W), lambda b, i: (b, i + q_off, 0)),
            pl.BlockSpec((1, S, GROUP_W), lambda b, i: (b, 0, 1)),
            pl.BlockSpec((1, S, GROUP_W), lambda b, i: (b, 0, 2)),
            pl.BlockSpec((4, DA_HD), lambda b, i: (0, 0)),
            pl.BlockSpec((1, GROUP_W), lambda b, i: (0, 0)),
        ],
        out_specs=pl.BlockSpec((1, tq, GROUP_W), lambda b, i: (b, i, 0)),
        out_shape=jax.ShapeDtypeStruct((B, nq * tq, GROUP_W), BF16),
        scratch_shapes=[
            pltpu.VMEM((2 * DA_HEADS * tq, S), F32),
            pltpu.VMEM((2 * DA_HEADS * tq, S), BF16),
            pltpu.VMEM((S, GROUP_W), BF16),
            pltpu.VMEM((S, GROUP_W), BF16),
        ],
        compiler_params=_params("parallel", "arbitrary"),
        name="diff_attention",
    )(qkv, qkv, qkv, lam_vecs.astype(F32), g)


def _hgrn_kernel(n_tiles, q_ref, ff_ref, fb_ref, i_ref, g_ref, lb_ref, lbt_ref, ng_ref, o_ref,
                 od_ref, st2_ref, u2_ref, oc2_ref):
    n_blocks = TILE // HG_BLOCK
    chunks_per_block = HG_BLOCK // HG_CHUNK
    row = lax.broadcasted_iota(jnp.int32, (TILE, TILE), 0)
    col = lax.broadcasted_iota(jnp.int32, (TILE, TILE), 1)
    same_chunk = (row >> 4) == (col >> 4)
    same_block = (row >> 6) == (col >> 6)
    lane = lax.broadcasted_iota(jnp.int32, (1, GROUP_W), 1)
    lane_head = lane >> 6
    lane_block = lane >> 6
    lane_pos = (lane >> 4) & (chunks_per_block - 1)
    gmean = _group_mean_matrix(GROUP_W, HG_HD)

    def ones(mask):
        return jnp.where(mask, 1.0, 0.0).astype(BF16)

    consts = []
    for d in range(2):
        dist = ((row >> 4) - (col >> 4)) if d == 0 else ((col >> 4) - (row >> 4))
        causal = same_chunk & ((col <= row) if d == 0 else (col >= row))
        allowed = same_block & ((dist >= 1) | causal)
        row_mats = [ones(causal), ones(same_block & (dist >= 1))]
        col_mats = [ones(same_chunk & ((row <= col) if d == 0 else (row >= col))), ones(same_chunk)]
        consts.append((allowed, jnp.concatenate(row_mats, axis=0), jnp.concatenate(col_mats, axis=1)))

    st2_ref[...] = jnp.zeros_like(st2_ref)

    def stack_heads(x):
        zero = jnp.zeros_like(x)
        return jnp.concatenate(
            [jnp.where(lane_head == h, x, zero).astype(BF16) for h in range(HG_HEADS)], axis=0)

    def tile_body(ti, carry):
        for d in range(2):
            allowed, row_mat, col_mat = consts[d]
            f_ref = ff_ref if d == 0 else fb_ref
            lb = lb_ref[d:d + 1, :]
            lb_t = lbt_ref[:, d:d + 1]
            st_ref, u_ref, oc_ref = st2_ref.at[d], u2_ref.at[d], oc2_ref.at[d]
            if d == 0:
                t = ti
            else:
                t = jnp.where(ti == 0, 0, n_tiles - ti)
            rows = pl.ds(pl.multiple_of(t * TILE, TILE), TILE)
            fr = f_ref[0, rows, :]
            lf = jnp.log(lb + (1.0 - lb) * jax.nn.sigmoid(fr))
            sums = _split2_dot_rhs(row_mat, lf)
            qd = _silu(q_ref[0, rows, :]) * jnp.exp(sums[0:TILE])
            q_state = qd * jnp.exp(sums[TILE:2 * TILE])
            f_t = lb_t + (1.0 - lb_t) * jax.nn.sigmoid(fr.T)
            kk_t = 1.0 - f_t
            sums_t = _split2_dot(jnp.log(f_t), col_mat)
            g_t, gt_t = sums_t[:, 0:TILE], sums_t[:, TILE:2 * TILE]
            kend_t = kk_t * jnp.exp(gt_t - g_t)
            k_var = [kk_t * jnp.exp(-g_t), kend_t]
            later = jnp.zeros_like(gt_t)
            for m in range(1, chunks_per_block):
                if d == 0:
                    shifted = pltpu.roll(gt_t, TILE - m * HG_CHUNK, 1)
                    inside = lane_pos <= chunks_per_block - 1 - m
                else:
                    shifted = pltpu.roll(gt_t, m * HG_CHUNK, 1)
                    inside = lane_pos >= m
                later = later + jnp.where(inside, shifted, 0.0)
                if m + 1 < chunks_per_block:
                    k_var.append(kend_t * jnp.exp(later))
            kblk_t = kend_t * jnp.exp(later)
            dec_t = jnp.exp(gt_t + later)
            vb = i_ref[0, rows, :].astype(BF16)

            qd4 = stack_heads(qd)
            x_pos = []
            for p in range(chunks_per_block):
                keys = jnp.zeros_like(kend_t)
                for n in range(chunks_per_block):
                    pc = p - n if d == 0 else p + n
                    if 0 <= pc < chunks_per_block:
                        keys = jnp.where(lane_pos == pc, k_var[n], keys)
                q_p = jnp.concatenate(
                    [qd4[h * TILE + b * HG_BLOCK + p * HG_CHUNK:h * TILE + b * HG_BLOCK + (p + 1) * HG_CHUNK]
                     for h in range(HG_HEADS) for b in range(n_blocks)], axis=0)
                x_pos.append(_dot(q_p, keys.astype(BF16)))
            a_heads = []
            for h in range(HG_HEADS):
                a = jnp.concatenate(
                    [x_pos[p][(h * n_blocks + b) * HG_CHUNK:(h * n_blocks + b + 1) * HG_CHUNK]
                     for b in range(n_blocks) for p in range(chunks_per_block)], axis=0)
                a_heads.append(jnp.where(allowed, a, 0.0).astype(BF16))
            zb = jnp.zeros_like(vb)
            v4 = jnp.concatenate(
                [jnp.where(lane_head == h, vb, zb) for h in range(HG_HEADS)], axis=0)
            o_tile = _dot(jnp.concatenate(a_heads, axis=1), v4)

            zero = jnp.zeros_like(kblk_t)
            k_stack = jnp.concatenate(
                [jnp.where(lane_block == b, kblk_t, zero).astype(BF16) for b in range(n_blocks)], axis=0)
            u_ref[...] = _dot(k_stack, vb)

            qs4 = stack_heads(q_state)
            order = range(n_blocks) if d == 0 else range(n_blocks - 1, -1, -1)
            for b in order:
                st = st_ref[...]
                q4b = jnp.concatenate(
                    [qs4[h * TILE + b * HG_BLOCK:h * TILE + (b + 1) * HG_BLOCK] for h in range(HG_HEADS)],
                    axis=0)
                ob4 = _dot(q4b, st.astype(BF16))
                ob = jnp.zeros((HG_BLOCK, GROUP_W), F32)
                for h in range(HG_HEADS):
                    ob = ob + jnp.where(lane_head == h, ob4[h * HG_BLOCK:(h + 1) * HG_BLOCK], 0.0)
                oc_ref[b * HG_BLOCK:(b + 1) * HG_BLOCK, :] = ob
                first = b * HG_BLOCK + (0 if d == 0 else HG_BLOCK - HG_CHUNK)
                st_ref[...] = st * dec_t[:, first:first + 1] + u_ref[b * TILE:(b + 1) * TILE, :]
            od_ref[d, rows, :] = o_tile + oc_ref[...]
        return carry

    lax.fori_loop(0, n_tiles, tile_body, 0)

    def finish_body(t, carry):
        rows = pl.ds(pl.multiple_of(t * TILE, TILE), TILE)
        tot = od_ref[0, rows, :] + od_ref[1, rows, :]
        ms = _split2_dot(tot * tot, gmean)
        o_ref[0, rows, :] = (tot * lax.rsqrt(ms + EPS) * ng_ref[...]
                             * _silu(g_ref[0, rows, :])).astype(o_ref.dtype)
        return carry

    lax.fori_loop(0, n_tiles, finish_body, 0)


def _hgrn(rest, lb, norm_g):
    B, S, _ = rest.shape
    n_tiles = S // TILE

    def part(j):
        return pl.BlockSpec((1, S, GROUP_W), lambda b: (b, 0, j))

    ng = jnp.tile(norm_g.astype(F32), HG_HEADS).reshape(1, GROUP_W)
    return pl.pallas_call(
        functools.partial(_hgrn_kernel, n_tiles),
        grid=(B,),
        in_specs=[part(0), part(1), part(2), part(3), part(4),
                  pl.BlockSpec((2, GROUP_W), lambda b: (0, 0)),
                  pl.BlockSpec((GROUP_W, 2), lambda b: (0, 0)),
                  pl.BlockSpec((1, GROUP_W), lambda b: (0, 0))],
        out_specs=pl.BlockSpec((1, S, GROUP_W), lambda b: (b, 0, 0)),
        out_shape=jax.ShapeDtypeStruct((B, S, GROUP_W), BF16),
        scratch_shapes=[
            pltpu.VMEM((2, S, GROUP_W), F32),
            pltpu.VMEM((2, GROUP_W, GROUP_W), F32),
            pltpu.VMEM((2, TILE // HG_BLOCK * TILE, GROUP_W), F32),
            pltpu.VMEM((2, TILE, GROUP_W), F32),
        ],
        compiler_params=_params("parallel"),
        name="hgrn2",
    )(rest, rest, rest, rest, rest, lb, lb.T, ng)


def _conv_kernel(ctx_len, b_ref, c_ref, u_ref, w_ref, bias_ref, o_ref):
    S = b_ref.shape[1]
    v = c_ref[0] * u_ref[0]
    row = lax.broadcasted_iota(jnp.int32, (S, 1), 0)
    prev = jnp.where((row == 0) | (row == ctx_len), 0.0, pltpu.roll(v, 1, 0))
    nxt = jnp.where((row == ctx_len - 1) | (row == S - 1), 0.0, pltpu.roll(v, S - 1, 0))
    y = w_ref[0:1, :] * prev + w_ref[1:2, :] * v + w_ref[2:3, :] * nxt
    o_ref[0] = (b_ref[0] * (y + bias_ref[...])).astype(o_ref.dtype)


def _short_conv(rest, w, bias, ctx_len):
    B, S, _ = rest.shape

    def part(j):
        return pl.BlockSpec((1, S, GROUP_W), lambda b: (b, 0, j))

    return pl.pallas_call(
        functools.partial(_conv_kernel, ctx_len),
        grid=(B,),
        in_specs=[part(5), part(6), part(7),
                  pl.BlockSpec((3, GROUP_W), lambda b: (0, 0)),
                  pl.BlockSpec((1, GROUP_W), lambda b: (0, 0))],
        out_specs=pl.BlockSpec((1, S, GROUP_W), lambda b: (b, 0, 0)),
        out_shape=jax.ShapeDtypeStruct((B, S, GROUP_W), BF16),
        compiler_params=_params("parallel"),
        name="short_conv",
    )(rest, rest, rest, w.astype(F32), bias.astype(F32).reshape(1, GROUP_W))


def _s5_discretise(a_re, a_im, log_dt, b_re, b_im):
    dt = jnp.exp(log_dt)[..., None]
    mag = jnp.exp(dt * a_re)
    ab_re, ab_im = mag * jnp.cos(dt * a_im), mag * jnp.sin(dt * a_im)
    den = a_re * a_re + a_im * a_im
    nr = ab_re - 1.0
    f_re = (nr * a_re + ab_im * a_im) / den
    f_im = (ab_im * a_re - nr * a_im) / den
    bb_re = f_re[..., None] * b_re - f_im[..., None] * b_im
    bb_im = f_re[..., None] * b_im + f_im[..., None] * b_re
    return ab_re, ab_im, bb_re, bb_im


def _s5_chunk_matrices(a_re, a_im, log_dt, b_re, b_im, c_re, c_im, d_skip):
    L, G, P, N = S5_L, S5_NGROUPS, S5_GROUP, S5_STATE
    hp = lax.Precision.HIGHEST
    ab_re, ab_im, bb_re, bb_im = _s5_discretise(
        a_re.astype(F32), a_im.astype(F32), log_dt.astype(F32), b_re.astype(F32), b_im.astype(F32))
    c_re, c_im = c_re.astype(F32), c_im.astype(F32)
    j_dt = (jnp.arange(L + 1, dtype=F32).reshape(L + 1, 1, 1, 1)
            * jnp.exp(log_dt.astype(F32))[None, :, :, None])
    mag = jnp.exp(j_dt * a_re.astype(F32)[None])
    pr, pi = mag * jnp.cos(j_dt * a_im.astype(F32)[None]), mag * jnp.sin(j_dt * a_im.astype(F32)[None])
    ca_re = c_re[None] * pr[:, :, :, None, :] - c_im[None] * pi[:, :, :, None, :]
    ca_im = c_re[None] * pi[:, :, :, None, :] + c_im[None] * pr[:, :, :, None, :]
    k = (jnp.einsum("jdgqn,dgnp->jdgqp", ca_re[:L], bb_re, precision=hp)
         - jnp.einsum("jdgqn,dgnp->jdgqp", ca_im[:L], bb_im, precision=hp))
    steps = jnp.arange(L)
    lag = steps[None, :] - steps[:, None]

    def toeplitz(kd, lag):
        kk = jnp.where((lag >= 0)[:, :, None, None, None], kd[jnp.clip(lag, 0, L - 1)], 0.0)
        return kk.transpose(2, 4, 0, 3, 1)

    tmat = (toeplitz(k[:, 0], lag) + toeplitz(k[:, 1], -lag)).reshape(G, L * P, L * P)
    skip = jnp.repeat(d_skip.astype(F32).reshape(G, P), L, axis=1).reshape(G, 1, L * P)
    tmat = tmat + jnp.eye(L * P, dtype=F32)[None] * skip

    def to_state(pw_re, pw_im, d):
        br = jnp.swapaxes(bb_re[d], 1, 2)[None]
        bi = jnp.swapaxes(bb_im[d], 1, 2)[None]
        re = pw_re[:, :, None, :] * br - pw_im[:, :, None, :] * bi
        im = pw_re[:, :, None, :] * bi + pw_im[:, :, None, :] * br
        return (re.transpose(1, 2, 0, 3).reshape(G, L * P, N),
                im.transpose(1, 2, 0, 3).reshape(G, L * P, N))

    f_re, f_im = to_state(pr[:L, 0][::-1], pi[:L, 0][::-1], 0)
    g_re, g_im = to_state(pr[:L, 1], pi[:L, 1], 1)
    bs = jnp.concatenate([f_re, g_re, f_im, g_im], axis=2)

    def from_state(x, d, flip):
        xd = x[::-1, d] if flip else x[:, d]
        return xd.transpose(1, 3, 2, 0).reshape(G, N, L * P)

    cs = jnp.concatenate([from_state(ca_re[1:], 0, False), from_state(ca_re[1:], 1, True),
                          -from_state(ca_im[1:], 0, False), -from_state(ca_im[1:], 1, True)], axis=1)
    a_l = jnp.stack([jnp.concatenate([pr[L, 0], pr[L, 1]], axis=-1),
                     jnp.concatenate([pi[L, 0], pi[L, 1]], axis=-1)], axis=1)
    return tmat.astype(BF16), bs.astype(BF16), cs.astype(BF16), a_l


def _s5_chunk_kernel(n_ctx_chunks, n_chunks, u_ref, t_ref, bs_ref, cs_ref, al_ref, y_ref,
                     v_ref, xs_ref):
    n_ch, n_steps, n_rows = u_ref.shape
    nb = n_rows // n_chunks
    n = S5_STATE
    u = u_ref[...].reshape(n_ch * n_steps, n_rows).T.astype(BF16)
    v_ref[...] = _dot(u, bs_ref[0])
    fwd_lanes = lax.broadcasted_iota(jnp.int32, (1, 2 * n), 1) < n
    a_re = jnp.broadcast_to(al_ref[0, 0:1, :], (nb, 2 * n))
    a_im = jnp.broadcast_to(al_ref[0, 1:2, :], (nb, 2 * n))

    def step(i, carry):
        xr, xi = carry
        cb = jnp.where(i < n_ctx_chunks, n_ctx_chunks - 1 - i, n_chunks - 1 - (i - n_ctx_chunks))
        rf = pl.ds(pl.multiple_of(i * nb, nb), nb)
        rb = pl.ds(pl.multiple_of(cb * nb, nb), nb)
        xs_ref[rf, 0:n] = xr[:, 0:n]
        xs_ref[rb, n:2 * n] = xr[:, n:2 * n]
        xs_ref[rf, 2 * n:3 * n] = xi[:, 0:n]
        xs_ref[rb, 3 * n:4 * n] = xi[:, n:2 * n]
        vr = jnp.where(fwd_lanes, v_ref[rf, 0:2 * n], v_ref[rb, 0:2 * n])
        vi = jnp.where(fwd_lanes, v_ref[rf, 2 * n:4 * n], v_ref[rb, 2 * n:4 * n])
        return a_re * xr - a_im * xi + vr, a_re * xi + a_im * xr + vi

    zero = jnp.zeros((nb, 2 * n), F32)
    lax.fori_loop(0, n_chunks, step, (zero, zero))
    y = _dot(u, t_ref[0]) + _dot(xs_ref[...].astype(BF16), cs_ref[0])
    y_ref[...] = y.T.reshape(n_ch, n_steps, n_rows)


def _s5_glu_kernel(y_ref, wg_ref, bg_ref, o_ref):
    n_chunks, nb, n_steps, width = y_ref.shape
    for j in range(nb):
        y = jax.nn.gelu(y_ref[:, j, :, :].reshape(n_chunks * n_steps, width))
        gate = jax.nn.sigmoid(_dot(y.astype(BF16), wg_ref[...]) + bg_ref[...])
        o_ref[j] = (y * gate).astype(o_ref.dtype)


def _s5_chunked(u5c, tmat, bs, cs, a_l, w_glu, b_glu, ctx_len):
    n_chunks, B, L, W = u5c.shape
    P = S5_GROUP
    S = n_chunks * L
    assert L == S5_L and ctx_len % L == 0 and TILE % L == 0
    rows = n_chunks * B
    ut = u5c.reshape(rows, L, W).transpose(2, 1, 0)

    def per_group(arr):
        return pl.BlockSpec((1,) + arr.shape[1:], lambda g: (g,) + (0,) * (arr.ndim - 1))

    group_block = pl.BlockSpec((P, L, rows), lambda g: (g, 0, 0))
    yt = pl.pallas_call(
        functools.partial(_s5_chunk_kernel, ctx_len // L, n_chunks),
        grid=(W // P,),
        in_specs=[group_block, per_group(tmat), per_group(bs), per_group(cs), per_group(a_l)],
        out_specs=group_block,
        out_shape=jax.ShapeDtypeStruct((W, L, rows), F32),
        scratch_shapes=[pltpu.VMEM((rows, 4 * S5_STATE), F32),
                        pltpu.VMEM((rows, 4 * S5_STATE), F32)],
        compiler_params=_params("parallel"),
        name="s5_chunks",
    )(ut, tmat, bs, cs, a_l)
    y = yt.transpose(2, 1, 0).reshape(n_chunks, B, L, W)

    nb = POST_NB
    tile_chunks = TILE // L
    w_b = w_glu.astype(BF16)
    b_row = b_glu.astype(F32).reshape(1, W)
    return pl.pallas_call(
        _s5_glu_kernel,
        grid=(B // nb, n_chunks // tile_chunks),
        in_specs=[pl.BlockSpec((tile_chunks, nb, L, W), lambda b, t: (t, b, 0, 0)),
                  pl.BlockSpec(w_b.shape, lambda b, t: (0, 0)),
                  pl.BlockSpec(b_row.shape, lambda b, t: (0, 0))],
        out_specs=pl.BlockSpec((nb, TILE, W), lambda b, t: (b, t, 0)),
        out_shape=jax.ShapeDtypeStruct((B, S, W), BF16),
        compiler_params=_params("parallel", "arbitrary"),
        name="s5_glu",
    )(y, w_b, b_row)


def _post_kernel(n_hidden_chunks, ctx_tiles_here, hc_ref, hx_ref, a_ref, b_ref, c_ref, d_ref,
                 modx_ref, modc_ref, ng_ref, wo_ref, wi_ref, w2_ref, o_ref):
    nb, tile, d_model = hx_ref.shape
    n_rows = nb * tile
    is_ctx = pl.program_id(1) < ctx_tiles_here
    mod = _tile_mod(is_ctx, modx_ref, modc_ref)

    def flat(ref):
        return ref[...].reshape(n_rows, GROUP_W).astype(BF16)

    mix = (_dot(flat(a_ref), wo_ref[0:GROUP_W, :])
           + _dot(flat(b_ref), wo_ref[GROUP_W:2 * GROUP_W, :])
           + _dot(flat(c_ref), wo_ref[2 * GROUP_W:3 * GROUP_W, :])
           + _dot(flat(d_ref), wo_ref[3 * GROUP_W:4 * GROUP_W, :])).reshape(nb, tile, d_model)
    h = (jnp.where(is_ctx, hc_ref[...], hx_ref[...])
         + mod[:, 2:3, :] * (mix * _rms_scale(mix) * ng_ref[1:2, :]))
    y = (h * _rms_scale(h) * ng_ref[2:3, :]) * (1.0 + mod[:, 4:5, :]) + mod[:, 3:4, :]
    yb = y.reshape(n_rows, d_model).astype(BF16)
    hidden = w2_ref.shape[0]
    hc = hidden // n_hidden_chunks
    ffn = jnp.zeros((n_rows, d_model), F32)
    for j in range(n_hidden_chunks):
        gate = _dot(yb, wi_ref[:, j * hc:(j + 1) * hc])
        up = _dot(yb, wi_ref[:, hidden + j * hc:hidden + (j + 1) * hc])
        ffn = ffn + _dot((_silu(gate) * up).astype(BF16), w2_ref[j * hc:(j + 1) * hc, :])
    ffn = ffn.reshape(nb, tile, d_model)
    o_ref[...] = h + mod[:, 5:6, :] * (ffn * _rms_scale(ffn) * ng_ref[3:4, :])


def _post(h_parts, S, a, b, c, d, mod, norm_g, w_out, w_ffn_in, w_ffn_out, n_batch, n_ctx_tiles,
          want_ctx):
    B, _, D = h_parts[1].shape
    off = 0 if want_ctx else n_ctx_tiles
    nt = S // TILE - off
    a_off = off if a.shape[1] == S else 0
    nb = POST_NB
    assert B % nb == 0

    def rows(o):
        return lambda bi, t: (bi, t + o, 0)

    def whole(arr):
        return pl.BlockSpec(arr.shape, lambda bi, t: (0,) * arr.ndim, pipeline_mode=pl.Buffered(1))

    mix_spec = lambda o: pl.BlockSpec((nb, TILE, GROUP_W), rows(o))
    return pl.pallas_call(
        functools.partial(_post_kernel, 2, n_ctx_tiles - off),
        grid=(B // nb, nt),
        in_specs=_seq_specs(h_parts, (nb, TILE, D), n_ctx_tiles, first_tile=off) + [
            mix_spec(a_off), mix_spec(off), mix_spec(off), mix_spec(off),
            pl.BlockSpec((nb, 6, D), lambda bi, t: (bi, 0, 0)),
            pl.BlockSpec((1, 6, D), lambda bi, t: (n_batch, 0, 0)),
            whole(norm_g), whole(w_out), whole(w_ffn_in), whole(w_ffn_out),
        ],
        out_specs=pl.BlockSpec((nb, TILE, D), lambda bi, t: (bi, t, 0)),
        out_shape=jax.ShapeDtypeStruct((B, nt * TILE, D), F32),
        compiler_params=_params("parallel", "arbitrary"),
        name="post",
    )(h_parts[0], h_parts[1], a, b, c, d, mod, mod, norm_g, w_out, w_ffn_in, w_ffn_out)


def _rope_tables(n_rows, ctx_len):
    rows = jnp.broadcast_to(jnp.arange(n_rows, dtype=F32)[:, None], (n_rows, GRID_W)).reshape(-1)
    cols = jnp.broadcast_to(jnp.arange(GRID_W, dtype=F32)[None, :], (n_rows, GRID_W)).reshape(-1)
    n_freq = DA_HD // 4
    inv = ROPE_BASE ** (-jnp.arange(n_freq, dtype=F32) / n_freq)
    ang = jnp.concatenate([rows[:, None] * inv, cols[:, None] * inv], axis=-1)
    cos, sin = jnp.cos(ang), jnp.sin(ang)
    zero = jnp.zeros_like(sin)
    reps = GROUP_W // DA_HD

    def lanes(first, second, ctx_value):
        t = jnp.tile(jnp.concatenate([first, second], axis=-1), (1, reps))
        return jnp.concatenate([jnp.full((ctx_len, GROUP_W), ctx_value, F32), t], axis=0)

    return lanes(cos, cos, 1.0), lanes(-sin, zero, 0.0), lanes(zero, sin, 0.0)


def _deinterleave_qk(w_in):
    perm32 = jnp.concatenate([jnp.arange(0, DA_HD, 2), jnp.arange(1, DA_HD, 2)])
    perm = (jnp.arange(0, 2 * GROUP_W, DA_HD)[:, None] + perm32[None, :]).reshape(-1)
    cols = jnp.concatenate([perm, jnp.arange(2 * GROUP_W, w_in.shape[-1])])
    return w_in[..., cols]


def kernel(x, c, ctx, c_ctx, w_ada, b_ada, norm_g, w_in, w_out, da_lambda, da_subln, hg_lb, hg_norm, sc_w, sc_b, s5_a_re, s5_a_im, s5_log_dt, s5_b_re, s5_b_im, s5_c_re, s5_c_im, s5_d, s5_w_glu, s5_b_glu, w_ffn_in, w_ffn_out):
    B, T, D = x.shape
    ctx_len = ctx.shape[1]
    L = w_ada.shape[0]
    assert ctx_len % TILE == 0 and T % TILE == 0 and T % GRID_W == 0
    n_ctx_tiles = ctx_len // TILE

    bp = -(-(B + 1) // SUBLANES) * SUBLANES
    cvec = jnp.concatenate([c, c_ctx[None, :], jnp.zeros((bp - B - 1, D), c.dtype)], axis=0)
    mods = _modulation(cvec.astype(F32), w_ada, b_ada).reshape(L, bp, 6, D)

    cos, sin_a, sin_b = _rope_tables(T // GRID_W, ctx_len)
    lb = jnp.cumsum(jax.nn.softmax(hg_lb.astype(F32), axis=0), axis=0)
    lb = lb - lb[:1]
    w_in_b = _deinterleave_qk(w_in).astype(BF16)
    w_out_b = w_out.astype(BF16)
    w_ffn_in_b = w_ffn_in.astype(BF16)
    w_ffn_out_b = w_ffn_out.astype(BF16)
    s5_mats = jax.vmap(_s5_chunk_matrices)(s5_a_re, s5_a_im, s5_log_dt, s5_b_re, s5_b_im, s5_c_re,
                                           s5_c_im, s5_d)

    S = ctx_len + T
    h_parts = (ctx, x, 0)
    for l in range(L):
        want_ctx = l < L - 1
        lam_init = 0.8 - 0.6 * math.exp(-0.3 * l)
        qkv, rest, u5 = _inproj(h_parts, S, mods[l], norm_g[l, 0], w_in_b[l], cos, sin_a, sin_b, B,
                                n_ctx_tiles)
        a = _attention(qkv, da_lambda[l], da_subln[l], lam_init, ctx_len, want_ctx)
        b = _hgrn(rest, lb[l], hg_norm[l])
        cc = _short_conv(rest, sc_w[l], sc_b[l], ctx_len)
        tmat, bs, cs, a_l = (m[l] for m in s5_mats)
        dd = _s5_chunked(u5, tmat, bs, cs, a_l, s5_w_glu[l], s5_b_glu[l], ctx_len)
        h = _post(h_parts, S, a, b, cc, dd, mods[l], norm_g[l].astype(F32), w_out_b[l], w_ffn_in_b[l],
                  w_ffn_out_b[l], B, n_ctx_tiles, want_ctx)
        h_parts = (h, h, n_ctx_tiles)
    return h
```

```python
import functools
import math

import jax
import jax.numpy as jnp
from jax import lax
from jax.experimental import pallas as pl
from jax.experimental.pallas import tpu as pltpu

F32 = jnp.float32
BF16 = jnp.bfloat16

EPS = 1e-6
LOG2E = math.log2(math.e)
GRID_W = 64
ROPE_BASE = 10000.0
GROUP_W = 256
DA_HEADS = 4
DA_HD = 32
HG_HEADS = 4
HG_HD = 64
HG_CHUNK = 16
HG_BLOCK = 64
S5_NGROUPS = 16
S5_GROUP = 16
S5_STATE = 64
SUBLANES = 8
TILE = 256
INPROJ_NB = 4
POST_NB = 2
S5_L = 16
VMEM_LIMIT = 56 * 1024 * 1024

_NT = (((1,), (1,)), ((), ()))


def _dot(a, b):
    return jnp.dot(a, b, preferred_element_type=F32)


def _dot_nt(a, b):
    return lax.dot_general(a, b, _NT, preferred_element_type=F32)


def _split2_dot_rhs(m, x):
    x1 = x.astype(BF16)
    x2 = (x - x1.astype(F32)).astype(BF16)
    return _dot(m, x1) + _dot(m, x2)


def _split2_dot(x, m):
    x1 = x.astype(BF16)
    x2 = (x - x1.astype(F32)).astype(BF16)
    return _dot(x1, m) + _dot(x2, m)


def _silu(x):
    return x * jax.nn.sigmoid(x)


def _rms_scale(x):
    return lax.rsqrt(jnp.mean(x * x, axis=-1, keepdims=True) + EPS)


def _group_mean_matrix(n, group):
    sh = int(math.log2(group))
    r = lax.broadcasted_iota(jnp.int32, (n, n), 0) >> sh
    c = lax.broadcasted_iota(jnp.int32, (n, n), 1) >> sh
    return jnp.where(r == c, 1.0 / group, 0.0).astype(BF16)


def _params(*sem):
    return pltpu.CompilerParams(dimension_semantics=sem, vmem_limit_bytes=VMEM_LIMIT)


def _mod_kernel(c_ref, w_ref, b_ref, o_ref):
    sc = _silu(c_ref[...])
    o_ref[0] = _dot(sc.astype(BF16), w_ref[0].astype(BF16)) + b_ref[0]


def _modulation(cvec, w_ada, b_ada):
    L, D, N = w_ada.shape
    Bp = cvec.shape[0]
    tn = 1536
    return pl.pallas_call(
        _mod_kernel,
        grid=(L, N // tn),
        in_specs=[
            pl.BlockSpec((Bp, D), lambda l, j: (0, 0)),
            pl.BlockSpec((1, D, tn), lambda l, j: (l, 0, j)),
            pl.BlockSpec((1, 1, tn), lambda l, j: (l, 0, j)),
        ],
        out_specs=pl.BlockSpec((1, Bp, tn), lambda l, j: (l, 0, j)),
        out_shape=jax.ShapeDtypeStruct((L, Bp, N), F32),
        compiler_params=_params("arbitrary", "arbitrary"),
        name="modulation",
    )(cvec, w_ada, b_ada.reshape(L, 1, N))


def _rope(x, cos, sin_a, sin_b):
    return x * cos + pltpu.roll(x, GROUP_W - 1, 1) * sin_a + pltpu.roll(x, 1, 1) * sin_b


def _tile_mod(is_ctx, modx_ref, modc_ref):
    return jnp.where(is_ctx, modc_ref[...], modx_ref[...])


def _seq_specs(h_parts, block, n_ctx_tiles, first_tile=0):
    _, _, lat_off = h_parts

    def ctx_idx(b, t):
        return (b, jnp.minimum(t + first_tile, n_ctx_tiles - 1), 0)

    def lat_idx(b, t):
        return (b, jnp.maximum(t + first_tile, n_ctx_tiles) - n_ctx_tiles + lat_off, 0)

    return [pl.BlockSpec(block, ctx_idx), pl.BlockSpec(block, lat_idx)]


def _inproj_kernel(n_ctx_tiles, hc_ref, hx_ref, modx_ref, modc_ref, g_ref, w_ref, cos_ref, sa_ref,
                   sb_ref, qkv_ref, rest_ref, u5_ref):
    nb, tile, d_model = hx_ref.shape
    is_ctx = pl.program_id(1) < n_ctx_tiles
    mod = _tile_mod(is_ctx, modx_ref, modc_ref)
    h = jnp.where(is_ctx, hc_ref[...], hx_ref[...])
    hn = h * _rms_scale(h) * g_ref[...]
    y = hn * (1.0 + mod[:, 1:2, :]) + mod[:, 0:1, :]
    p = _dot(y.reshape(nb * tile, d_model).astype(BF16), w_ref[...])
    cos, sa, sb = cos_ref[...], sa_ref[...], sb_ref[...]
    for j in range(nb):
        pj = p[j * tile:(j + 1) * tile]
        q = _rope(pj[:, 0:GROUP_W], cos, sa, sb) * (DA_HD ** -0.5 * math.log2(math.e))
        k = _rope(pj[:, GROUP_W:2 * GROUP_W], cos, sa, sb)
        qkv_ref[j, :, 0:GROUP_W] = q.astype(BF16)
        qkv_ref[j, :, GROUP_W:2 * GROUP_W] = k.astype(BF16)
        qkv_ref[j, :, 2 * GROUP_W:3 * GROUP_W] = pj[:, 2 * GROUP_W:3 * GROUP_W].astype(BF16)
        rest_ref[j] = pj[:, 3 * GROUP_W:11 * GROUP_W]
        u5_ref[:, j, :, :] = pj[:, 11 * GROUP_W:].reshape(tile // S5_L, S5_L, GROUP_W)


def _inproj(h_parts, S, mod, g, w, cos, sa, sb, n_batch, n_ctx_tiles):
    B, _, D = h_parts[1].shape
    N = w.shape[1]
    nt = S // TILE
    nb = INPROJ_NB
    assert B % nb == 0
    tab = pl.BlockSpec((TILE, GROUP_W), lambda b, t: (t, 0))
    return pl.pallas_call(
        functools.partial(_inproj_kernel, n_ctx_tiles),
        grid=(B // nb, nt),
        in_specs=_seq_specs(h_parts, (nb, TILE, D), n_ctx_tiles) + [
            pl.BlockSpec((nb, 6, D), lambda b, t: (b, 0, 0)),
            pl.BlockSpec((1, 6, D), lambda b, t: (n_batch, 0, 0)),
            pl.BlockSpec((1, D), lambda b, t: (0, 0)),
            pl.BlockSpec((D, N), lambda b, t: (0, 0)),
            tab, tab, tab,
        ],
        out_specs=[
            pl.BlockSpec((nb, TILE, 3 * GROUP_W), lambda b, t: (b, t, 0)),
            pl.BlockSpec((nb, TILE, 8 * GROUP_W), lambda b, t: (b, t, 0)),
            pl.BlockSpec((TILE // S5_L, nb, S5_L, GROUP_W), lambda b, t: (t, b, 0, 0)),
        ],
        out_shape=[
            jax.ShapeDtypeStruct((B, S, 3 * GROUP_W), BF16),
            jax.ShapeDtypeStruct((B, S, 8 * GROUP_W), F32),
            jax.ShapeDtypeStruct((S // S5_L, B, S5_L, GROUP_W), F32),
        ],
        compiler_params=_params("parallel", "arbitrary"),
        name="inproj",
    )(h_parts[0], h_parts[1], mod, mod, g.reshape(1, D), w, cos, sa, sb)


def _attn_kernel(lam_init, ctx_len, q_off, q_ref, k_ref, v_ref, lamv_ref, g_ref, o_ref,
                 sc_ref, e_ref, va_ref, vb_ref):
    tq = q_ref.shape[1]
    n_keys = k_ref.shape[1]
    lv = lamv_ref[...]
    lam = (jnp.exp(jnp.sum(lv[0:1] * lv[1:2], axis=-1, keepdims=True))
           - jnp.exp(jnp.sum(lv[2:3] * lv[3:4], axis=-1, keepdims=True)) + lam_init)
    lane = lax.broadcasted_iota(jnp.int32, (1, GROUP_W), 1)
    lane_map = lane >> 5
    lane_head = lane >> 6

    @pl.when(pl.program_id(1) == 0)
    def _():
        v = v_ref[0]
        one = jnp.ones_like(v)
        va_ref[...] = jnp.where(lane == GROUP_W - 1, one, v)
        vb_ref[...] = jnp.where(lane == 0, one, v)

    q = q_ref[0]
    q8 = jnp.concatenate(
        [jnp.where(lane_map == j, q, jnp.zeros_like(q)) for j in range(2 * DA_HEADS)], axis=0)

    def attend(nk):
        sc_ref[:, 0:nk] = _dot_nt(q8, k_ref[0, 0:nk, :])
        acc = jnp.zeros((tq, GROUP_W), F32)
        for h in range(DA_HEADS):
            for m in range(2):
                rows = slice((2 * h + m) * tq, (2 * h + m + 1) * tq)
                s = sc_ref[rows, 0:nk]
                e_ref[rows, 0:nk] = jnp.exp2(s - jnp.max(s, axis=-1, keepdims=True)).astype(BF16)
            vv = va_ref if h < DA_HEADS - 1 else vb_ref
            sum_col = GROUP_W - 1 if h < DA_HEADS - 1 else 0
            o2 = _dot(e_ref[2 * h * tq:(2 * h + 2) * tq, 0:nk], vv[0:nk, :])
            o0, o1 = o2[0:tq], o2[tq:2 * tq]
            w0 = 1.0 / o0[:, sum_col:sum_col + 1]
            w1 = lam / o1[:, sum_col:sum_col + 1]
            acc = acc + jnp.where(lane_head == h, o0 * w0 - o1 * w1, 0.0)
        ms = _split2_dot(acc * acc, _group_mean_matrix(GROUP_W, 2 * DA_HD))
        o_ref[0] = (acc * lax.rsqrt(ms + EPS) * g_ref[...] * (1.0 - lam_init)).astype(o_ref.dtype)

    if q_off == 0 and ctx_len == tq:
        @pl.when(pl.program_id(1) == 0)
        def _():
            attend(ctx_len)

        @pl.when(pl.program_id(1) != 0)
        def _():
            attend(n_keys)
    else:
        attend(n_keys)


def _attention(qkv, lam_vecs, subln_g, lam_init, ctx_len, want_ctx):
    B, S, _ = qkv.shape
    tq = TILE
    q_off = 0 if want_ctx else ctx_len // tq
    nq = S // tq - q_off
    g = jnp.tile(subln_g.astype(F32), DA_HEADS).reshape(1, GROUP_W)
    return pl.pallas_call(
        functools.partial(_attn_kernel, lam_init, ctx_len, q_off),
        grid=(B, nq),
        in_specs=[
            pl.BlockSpec((1, tq, GROUP_W), lambda b, i: (b, i + q_off, 0)),
            pl.BlockSpec((1, S, GROUP_W), lambda b, i: (b, 0, 1)),
            pl.BlockSpec((1, S, GROUP_W), lambda b, i: (b, 0, 2)),
            pl.BlockSpec((4, DA_HD), lambda b, i: (0, 0)),
            pl.BlockSpec((1, GROUP_W), lambda b, i: (0, 0)),
        ],
        out_specs=pl.BlockSpec((1, tq, GROUP_W), lambda b, i: (b, i, 0)),
        out_shape=jax.ShapeDtypeStruct((B, nq * tq, GROUP_W), BF16),
        scratch_shapes=[
            pltpu.VMEM((2 * DA_HEADS * tq, S), F32),
            pltpu.VMEM((2 * DA_HEADS * tq, S), BF16),
            pltpu.VMEM((S, GROUP_W), BF16),
            pltpu.VMEM((S, GROUP_W), BF16),
        ],
        compiler_params=_params("parallel", "arbitrary"),
        name="diff_attention",
    )(qkv, qkv, qkv, lam_vecs.astype(F32), g)


def _hgrn_kernel(n_tiles, q_ref, ff_ref, fb_ref, i_ref, g_ref, lb_ref, lbt_ref, ng_ref, o_ref,
                 od_ref, st2_ref, u2_ref, oc2_ref):
    n_blocks = TILE // HG_BLOCK
    chunks_per_block = HG_BLOCK // HG_CHUNK
    row = lax.broadcasted_iota(jnp.int32, (TILE, TILE), 0)
    col = lax.broadcasted_iota(jnp.int32, (TILE, TILE), 1)
    same_chunk = (row >> 4) == (col >> 4)
    same_block = (row >> 6) == (col >> 6)
    lane = lax.broadcasted_iota(jnp.int32, (1, GROUP_W), 1)
    lane_head = lane >> 6
    lane_block = lane >> 6
    lane_pos = (lane >> 4) & (chunks_per_block - 1)
    gmean = _group_mean_matrix(GROUP_W, HG_HD)

    def ones(mask):
        return jnp.where(mask, 1.0, 0.0).astype(BF16)

    consts = []
    for d in range(2):
        dist = ((row >> 4) - (col >> 4)) if d == 0 else ((col >> 4) - (row >> 4))
        causal = same_chunk & ((col <= row) if d == 0 else (col >= row))
        allowed = same_block & ((dist >= 1) | causal)
        row_mats = [ones(causal), ones(same_block & (dist >= 1))]
        col_mats = [ones(same_chunk & ((row <= col) if d == 0 else (row >= col))), ones(same_chunk)]
        consts.append((allowed, jnp.concatenate(row_mats, axis=0), jnp.concatenate(col_mats, axis=1)))

    st2_ref[...] = jnp.zeros_like(st2_ref)

    def stack_heads(x):
        zero = jnp.zeros_like(x)
        return jnp.concatenate(
            [jnp.where(lane_head == h, x, zero).astype(BF16) for h in range(HG_HEADS)], axis=0)

    def tile_body(ti, carry):
        for d in range(2):
            allowed, row_mat, col_mat = consts[d]
            f_ref = ff_ref if d == 0 else fb_ref
            lb = lb_ref[d:d + 1, :]
            lb_t = lbt_ref[:, d:d + 1]
            st_ref, u_ref, oc_ref = st2_ref.at[d], u2_ref.at[d], oc2_ref.at[d]
            if d == 0:
                t = ti
            else:
                t = jnp.where(ti == 0, 0, n_tiles - ti)
            rows = pl.ds(pl.multiple_of(t * TILE, TILE), TILE)
            fr = f_ref[0, rows, :]
            lf = jnp.log(lb + (1.0 - lb) * jax.nn.sigmoid(fr)) * LOG2E
            sums = _split2_dot_rhs(row_mat, lf)
            qd = _silu(q_ref[0, rows, :]) * jnp.exp2(sums[0:TILE])
            q_state = qd * jnp.exp2(sums[TILE:2 * TILE])
            f_t = lb_t + (1.0 - lb_t) * jax.nn.sigmoid(fr.T)
            kk_t = 1.0 - f_t
            sums_t = _split2_dot(jnp.log(f_t) * LOG2E, col_mat)
            g_t, gt_t = sums_t[:, 0:TILE], sums_t[:, TILE:2 * TILE]
            kend_t = kk_t * jnp.exp2(gt_t - g_t)
            k_var = [kk_t * jnp.exp2(-g_t), kend_t]
            later = jnp.zeros_like(gt_t)
            for m in range(1, chunks_per_block):
                if d == 0:
                    shifted = pltpu.roll(gt_t, TILE - m * HG_CHUNK, 1)
                    inside = lane_pos <= chunks_per_block - 1 - m
                else:
                    shifted = pltpu.roll(gt_t, m * HG_CHUNK, 1)
                    inside = lane_pos >= m
                later = later + jnp.where(inside, shifted, 0.0)
                if m + 1 < chunks_per_block:
                    k_var.append(kend_t * jnp.exp2(later))
            kblk_t = kend_t * jnp.exp2(later)
            dec_t = jnp.exp2(gt_t + later)
            vb = i_ref[0, rows, :].astype(BF16)

            qd4 = stack_heads(qd)
            x_pos = []
            for p in range(chunks_per_block):
                keys = jnp.zeros_like(kend_t)
                for n in range(chunks_per_block):
                    pc = p - n if d == 0 else p + n
                    if 0 <= pc < chunks_per_block:
                        keys = jnp.where(lane_pos == pc, k_var[n], keys)
                q_p = jnp.concatenate(
                    [qd4[h * TILE + b * HG_BLOCK + p * HG_CHUNK:h * TILE + b * HG_BLOCK + (p + 1) * HG_CHUNK]
                     for h in range(HG_HEADS) for b in range(n_blocks)], axis=0)
                x_pos.append(_dot(q_p, keys.astype(BF16)))
            a_heads = []
            for h in range(HG_HEADS):
                a = jnp.concatenate(
                    [x_pos[p][(h * n_blocks + b) * HG_CHUNK:(h * n_blocks + b + 1) * HG_CHUNK]
                     for b in range(n_blocks) for p in range(chunks_per_block)], axis=0)
                a_heads.append(jnp.where(allowed, a, 0.0).astype(BF16))
            zb = jnp.zeros_like(vb)
            v4 = jnp.concatenate(
                [jnp.where(lane_head == h, vb, zb) for h in range(HG_HEADS)], axis=0)
            o_tile = _dot(jnp.concatenate(a_heads, axis=1), v4)

            zero = jnp.zeros_like(kblk_t)
            k_stack = jnp.concatenate(
                [jnp.where(lane_block == b, kblk_t, zero).astype(BF16) for b in range(n_blocks)], axis=0)
            u_ref[...] = _dot(k_stack, vb)

            qs4 = stack_heads(q_state)
            order = range(n_blocks) if d == 0 else range(n_blocks - 1, -1, -1)
            for b in order:
                st = st_ref[...]
                q4b = jnp.concatenate(
                    [qs4[h * TILE + b * HG_BLOCK:h * TILE + (b + 1) * HG_BLOCK] for h in range(HG_HEADS)],
                    axis=0)
                ob4 = _dot(q4b, st.astype(BF16))
                ob = jnp.zeros((HG_BLOCK, GROUP_W), F32)
                for h in range(HG_HEADS):
                    ob = ob + jnp.where(lane_head == h, ob4[h * HG_BLOCK:(h + 1) * HG_BLOCK], 0.0)
                oc_ref[b * HG_BLOCK:(b + 1) * HG_BLOCK, :] = ob
                first = b * HG_BLOCK + (0 if d == 0 else HG_BLOCK - HG_CHUNK)
                st_ref[...] = st * dec_t[:, first:first + 1] + u_ref[b * TILE:(b + 1) * TILE, :]
            od_ref[d, rows, :] = o_tile + oc_ref[...]
        return carry

    lax.fori_loop(0, n_tiles, tile_body, 0)

    def finish_body(t, carry):
        rows = pl.ds(pl.multiple_of(t * TILE, TILE), TILE)
        tot = od_ref[0, rows, :] + od_ref[1, rows, :]
        ms = _split2_dot(tot * tot, gmean)
        o_ref[0, rows, :] = (tot * lax.rsqrt(ms + EPS) * ng_ref[...]
                             * _silu(g_ref[0, rows, :])).astype(o_ref.dtype)
        return carry

    lax.fori_loop(0, n_tiles, finish_body, 0)


def _hgrn(rest, lb, norm_g):
    B, S, _ = rest.shape
    n_tiles = S // TILE

    def part(j):
        return pl.BlockSpec((1, S, GROUP_W), lambda b: (b, 0, j))

    ng = jnp.tile(norm_g.astype(F32), HG_HEADS).reshape(1, GROUP_W)
    return pl.pallas_call(
        functools.partial(_hgrn_kernel, n_tiles),
        grid=(B,),
        in_specs=[part(0), part(1), part(2), part(3), part(4),
                  pl.BlockSpec((2, GROUP_W), lambda b: (0, 0)),
                  pl.BlockSpec((GROUP_W, 2), lambda b: (0, 0)),
                  pl.BlockSpec((1, GROUP_W), lambda b: (0, 0))],
        out_specs=pl.BlockSpec((1, S, GROUP_W), lambda b: (b, 0, 0)),
        out_shape=jax.ShapeDtypeStruct((B, S, GROUP_W), BF16),
        scratch_shapes=[
            pltpu.VMEM((2, S, GROUP_W), F32),
            pltpu.VMEM((2, GROUP_W, GROUP_W), F32),
            pltpu.VMEM((2, TILE // HG_BLOCK * TILE, GROUP_W), F32),
            pltpu.VMEM((2, TILE, GROUP_W), F32),
        ],
        compiler_params=_params("parallel"),
        name="hgrn2",
    )(rest, rest, rest, rest, rest, lb, lb.T, ng)


def _conv_kernel(ctx_len, b_ref, c_ref, u_ref, w_ref, bias_ref, o_ref):
    S = b_ref.shape[1]
    v = c_ref[0] * u_ref[0]
    row = lax.broadcasted_iota(jnp.int32, (S, 1), 0)
    prev = jnp.where((row == 0) | (row == ctx_len), 0.0, pltpu.roll(v, 1, 0))
    nxt = jnp.where((row == ctx_len - 1) | (row == S - 1), 0.0, pltpu.roll(v, S - 1, 0))
    y = w_ref[0:1, :] * prev + w_ref[1:2, :] * v + w_ref[2:3, :] * nxt
    o_ref[0] = (b_ref[0] * (y + bias_ref[...])).astype(o_ref.dtype)


def _short_conv(rest, w, bias, ctx_len):
    B, S, _ = rest.shape

    def part(j):
        return pl.BlockSpec((1, S, GROUP_W), lambda b: (b, 0, j))

    return pl.pallas_call(
        functools.partial(_conv_kernel, ctx_len),
        grid=(B,),
        in_specs=[part(5), part(6), part(7),
                  pl.BlockSpec((3, GROUP_W), lambda b: (0, 0)),
                  pl.BlockSpec((1, GROUP_W), lambda b: (0, 0))],
        out_specs=pl.BlockSpec((1, S, GROUP_W), lambda b: (b, 0, 0)),
        out_shape=jax.ShapeDtypeStruct((B, S, GROUP_W), BF16),
        compiler_params=_params("parallel"),
        name="short_conv",
    )(rest, rest, rest, w.astype(F32), bias.astype(F32).reshape(1, GROUP_W))


def _s5_discretise(a_re, a_im, log_dt, b_re, b_im):
    dt = jnp.exp(log_dt)[..., None]
    mag = jnp.exp(dt * a_re)
    ab_re, ab_im = mag * jnp.cos(dt * a_im), mag * jnp.sin(dt * a_im)
    den = a_re * a_re + a_im * a_im
    nr = ab_re - 1.0
    f_re = (nr * a_re + ab_im * a_im) / den
    f_im = (ab_im * a_re - nr * a_im) / den
    bb_re = f_re[..., None] * b_re - f_im[..., None] * b_im
    bb_im = f_re[..., None] * b_im + f_im[..., None] * b_re
    return ab_re, ab_im, bb_re, bb_im


def _s5_chunk_matrices(a_re, a_im, log_dt, b_re, b_im, c_re, c_im, d_skip):
    L, G, P, N = S5_L, S5_NGROUPS, S5_GROUP, S5_STATE
    hp = lax.Precision.HIGHEST
    ab_re, ab_im, bb_re, bb_im = _s5_discretise(
        a_re.astype(F32), a_im.astype(F32), log_dt.astype(F32), b_re.astype(F32), b_im.astype(F32))
    c_re, c_im = c_re.astype(F32), c_im.astype(F32)
    j_dt = (jnp.arange(L + 1, dtype=F32).reshape(L + 1, 1, 1, 1)
            * jnp.exp(log_dt.astype(F32))[None, :, :, None])
    mag = jnp.exp(j_dt * a_re.astype(F32)[None])
    pr, pi = mag * jnp.cos(j_dt * a_im.astype(F32)[None]), mag * jnp.sin(j_dt * a_im.astype(F32)[None])
    ca_re = c_re[None] * pr[:, :, :, None, :] - c_im[None] * pi[:, :, :, None, :]
    ca_im = c_re[None] * pi[:, :, :, None, :] + c_im[None] * pr[:, :, :, None, :]
    k = (jnp.einsum("jdgqn,dgnp->jdgqp", ca_re[:L], bb_re, precision=hp)
         - jnp.einsum("jdgqn,dgnp->jdgqp", ca_im[:L], bb_im, precision=hp))
    steps = jnp.arange(L)
    lag = steps[None, :] - steps[:, None]

    def toeplitz(kd, lag):
        kk = jnp.where((lag >= 0)[:, :, None, None, None], kd[jnp.clip(lag, 0, L - 1)], 0.0)
        return kk.transpose(2, 4, 0, 3, 1)

    tmat = (toeplitz(k[:, 0], lag) + toeplitz(k[:, 1], -lag)).reshape(G, L * P, L * P)
    skip = jnp.repeat(d_skip.astype(F32).reshape(G, P), L, axis=1).reshape(G, 1, L * P)
    tmat = tmat + jnp.eye(L * P, dtype=F32)[None] * skip

    def to_state(pw_re, pw_im, d):
        br = jnp.swapaxes(bb_re[d], 1, 2)[None]
        bi = jnp.swapaxes(bb_im[d], 1, 2)[None]
        re = pw_re[:, :, None, :] * br - pw_im[:, :, None, :] * bi
        im = pw_re[:, :, None, :] * bi + pw_im[:, :, None, :] * br
        return (re.transpose(1, 2, 0, 3).reshape(G, L * P, N),
                im.transpose(1, 2, 0, 3).reshape(G, L * P, N))

    f_re, f_im = to_state(pr[:L, 0][::-1], pi[:L, 0][::-1], 0)
    g_re, g_im = to_state(pr[:L, 1], pi[:L, 1], 1)
    bs = jnp.concatenate([f_re, g_re, f_im, g_im], axis=2)

    def from_state(x, d, flip):
        xd = x[::-1, d] if flip else x[:, d]
        return xd.transpose(1, 3, 2, 0).reshape(G, N, L * P)

    cs = jnp.concatenate([from_state(ca_re[1:], 0, False), from_state(ca_re[1:], 1, True),
                          -from_state(ca_im[1:], 0, False), -from_state(ca_im[1:], 1, True)], axis=1)
    a_l = jnp.stack([jnp.concatenate([pr[L, 0], pr[L, 1]], axis=-1),
                     jnp.concatenate([pi[L, 0], pi[L, 1]], axis=-1)], axis=1)
    return tmat.astype(BF16), bs.astype(BF16), cs.astype(BF16), a_l


def _s5_chunk_kernel(n_ctx_chunks, n_chunks, u_ref, t_ref, bs_ref, cs_ref, al_ref, y_ref,
                     v_ref, xs_ref):
    n_ch, n_steps, n_rows = u_ref.shape
    nb = n_rows // n_chunks
    n = S5_STATE
    u = u_ref[...].reshape(n_ch * n_steps, n_rows).T.astype(BF16)
    v_ref[...] = _dot(u, bs_ref[0])
    fwd_lanes = lax.broadcasted_iota(jnp.int32, (1, 2 * n), 1) < n
    a_re = jnp.broadcast_to(al_ref[0, 0:1, :], (nb, 2 * n))
    a_im = jnp.broadcast_to(al_ref[0, 1:2, :], (nb, 2 * n))

    def step(i, carry):
        xr, xi = carry
        cb = jnp.where(i < n_ctx_chunks, n_ctx_chunks - 1 - i, n_chunks - 1 - (i - n_ctx_chunks))
        rf = pl.ds(pl.multiple_of(i * nb, nb), nb)
        rb = pl.ds(pl.multiple_of(cb * nb, nb), nb)
        xs_ref[rf, 0:n] = xr[:, 0:n]
        xs_ref[rb, n:2 * n] = xr[:, n:2 * n]
        xs_ref[rf, 2 * n:3 * n] = xi[:, 0:n]
        xs_ref[rb, 3 * n:4 * n] = xi[:, n:2 * n]
        vr = jnp.where(fwd_lanes, v_ref[rf, 0:2 * n], v_ref[rb, 0:2 * n])
        vi = jnp.where(fwd_lanes, v_ref[rf, 2 * n:4 * n], v_ref[rb, 2 * n:4 * n])
        return a_re * xr - a_im * xi + vr, a_re * xi + a_im * xr + vi

    zero = jnp.zeros((nb, 2 * n), F32)
    lax.fori_loop(0, n_chunks, step, (zero, zero))
    y = _dot(u, t_ref[0]) + _dot(xs_ref[...].astype(BF16), cs_ref[0])
    y_ref[...] = y.T.reshape(n_ch, n_steps, n_rows)


def _s5_glu_kernel(y_ref, wg_ref, bg_ref, o_ref):
    n_chunks, nb, n_steps, width = y_ref.shape
    for j in range(nb):
        y = jax.nn.gelu(y_ref[:, j, :, :].reshape(n_chunks * n_steps, width))
        gate = jax.nn.sigmoid(_dot(y.astype(BF16), wg_ref[...]) + bg_ref[...])
        o_ref[j] = (y * gate).astype(o_ref.dtype)


def _s5_chunked(u5c, tmat, bs, cs, a_l, w_glu, b_glu, ctx_len):
    n_chunks, B, L, W = u5c.shape
    P = S5_GROUP
    S = n_chunks * L
    assert L == S5_L and ctx_len % L == 0 and TILE % L == 0
    rows = n_chunks * B
    ut = u5c.reshape(rows, L, W).transpose(2, 1, 0)

    def per_group(arr):
        return pl.BlockSpec((1,) + arr.shape[1:], lambda g: (g,) + (0,) * (arr.ndim - 1))

    group_block = pl.BlockSpec((P, L, rows), lambda g: (g, 0, 0))
    yt = pl.pallas_call(
        functools.partial(_s5_chunk_kernel, ctx_len // L, n_chunks),
        grid=(W // P,),
        in_specs=[group_block, per_group(tmat), per_group(bs), per_group(cs), per_group(a_l)],
        out_specs=group_block,
        out_shape=jax.ShapeDtypeStruct((W, L, rows), F32),
        scratch_shapes=[pltpu.VMEM((rows, 4 * S5_STATE), F32),
                        pltpu.VMEM((rows, 4 * S5_STATE), F32)],
        compiler_params=_params("parallel"),
        name="s5_chunks",
    )(ut, tmat, bs, cs, a_l)
    y = yt.transpose(2, 1, 0).reshape(n_chunks, B, L, W)

    nb = POST_NB
    tile_chunks = TILE // L
    w_b = w_glu.astype(BF16)
    b_row = b_glu.astype(F32).reshape(1, W)
    return pl.pallas_call(
        _s5_glu_kernel,
        grid=(B // nb, n_chunks // tile_chunks),
        in_specs=[pl.BlockSpec((tile_chunks, nb, L, W), lambda b, t: (t, b, 0, 0)),
                  pl.BlockSpec(w_b.shape, lambda b, t: (0, 0)),
                  pl.BlockSpec(b_row.shape, lambda b, t: (0, 0))],
        out_specs=pl.BlockSpec((nb, TILE, W), lambda b, t: (b, t, 0)),
        out_shape=jax.ShapeDtypeStruct((B, S, W), BF16),
        compiler_params=_params("parallel", "arbitrary"),
        name="s5_glu",
    )(y, w_b, b_row)


def _post_kernel(n_hidden_chunks, ctx_tiles_here, hc_ref, hx_ref, a_ref, b_ref, c_ref, d_ref,
                 modx_ref, modc_ref, ng_ref, wo_ref, wi_ref, w2_ref, o_ref):
    nb, tile, d_model = hx_ref.shape
    n_rows = nb * tile
    is_ctx = pl.program_id(1) < ctx_tiles_here
    mod = _tile_mod(is_ctx, modx_ref, modc_ref)

    def flat(ref):
        return ref[...].reshape(n_rows, GROUP_W).astype(BF16)

    mix = (_dot(flat(a_ref), wo_ref[0:GROUP_W, :])
           + _dot(flat(b_ref), wo_ref[GROUP_W:2 * GROUP_W, :])
           + _dot(flat(c_ref), wo_ref[2 * GROUP_W:3 * GROUP_W, :])
           + _dot(flat(d_ref), wo_ref[3 * GROUP_W:4 * GROUP_W, :])).reshape(nb, tile, d_model)
    h = (jnp.where(is_ctx, hc_ref[...], hx_ref[...])
         + mod[:, 2:3, :] * (mix * _rms_scale(mix) * ng_ref[1:2, :]))
    y = (h * _rms_scale(h) * ng_ref[2:3, :]) * (1.0 + mod[:, 4:5, :]) + mod[:, 3:4, :]
    yb = y.reshape(n_rows, d_model).astype(BF16)
    hidden = w2_ref.shape[0]
    hc = hidden // n_hidden_chunks
    ffn = jnp.zeros((n_rows, d_model), F32)
    for j in range(n_hidden_chunks):
        gate = _dot(yb, wi_ref[:, j * hc:(j + 1) * hc])
        up = _dot(yb, wi_ref[:, hidden + j * hc:hidden + (j + 1) * hc])
        ffn = ffn + _dot((_silu(gate) * up).astype(BF16), w2_ref[j * hc:(j + 1) * hc, :])
    ffn = ffn.reshape(nb, tile, d_model)
    o_ref[...] = h + mod[:, 5:6, :] * (ffn * _rms_scale(ffn) * ng_ref[3:4, :])


def _post(h_parts, S, a, b, c, d, mod, norm_g, w_out, w_ffn_in, w_ffn_out, n_batch, n_ctx_tiles,
          want_ctx):
    B, _, D = h_parts[1].shape
    off = 0 if want_ctx else n_ctx_tiles
    nt = S // TILE - off
    a_off = off if a.shape[1] == S else 0
    nb = POST_NB
    assert B % nb == 0

    def rows(o):
        return lambda bi, t: (bi, t + o, 0)

    def whole(arr):
        return pl.BlockSpec(arr.shape, lambda bi, t: (0,) * arr.ndim, pipeline_mode=pl.Buffered(1))

    mix_spec = lambda o: pl.BlockSpec((nb, TILE, GROUP_W), rows(o))
    return pl.pallas_call(
        functools.partial(_post_kernel, 2, n_ctx_tiles - off),
        grid=(B // nb, nt),
        in_specs=_seq_specs(h_parts, (nb, TILE, D), n_ctx_tiles, first_tile=off) + [
            mix_spec(a_off), mix_spec(off), mix_spec(off), mix_spec(off),
            pl.BlockSpec((nb, 6, D), lambda bi, t: (bi, 0, 0)),
            pl.BlockSpec((1, 6, D), lambda bi, t: (n_batch, 0, 0)),
            whole(norm_g), whole(w_out), whole(w_ffn_in), whole(w_ffn_out),
        ],
        out_specs=pl.BlockSpec((nb, TILE, D), lambda bi, t: (bi, t, 0)),
        out_shape=jax.ShapeDtypeStruct((B, nt * TILE, D), F32),
        compiler_params=_params("parallel", "arbitrary"),
        name="post",
    )(h_parts[0], h_parts[1], a, b, c, d, mod, mod, norm_g, w_out, w_ffn_in, w_ffn_out)


def _rope_tables(n_rows, ctx_len):
    rows = jnp.broadcast_to(jnp.arange(n_rows, dtype=F32)[:, None], (n_rows, GRID_W)).reshape(-1)
    cols = jnp.broadcast_to(jnp.arange(GRID_W, dtype=F32)[None, :], (n_rows, GRID_W)).reshape(-1)
    n_freq = DA_HD // 4
    inv = ROPE_BASE ** (-jnp.arange(n_freq, dtype=F32) / n_freq)
    ang = jnp.concatenate([rows[:, None] * inv, cols[:, None] * inv], axis=-1)
    cos, sin = jnp.cos(ang), jnp.sin(ang)
    zero = jnp.zeros_like(sin)
    reps = GROUP_W // DA_HD

    def lanes(even, odd, ctx_value):
        t = jnp.tile(jnp.stack([even, odd], axis=-1).reshape(even.shape[0], DA_HD), (1, reps))
        return jnp.concatenate([jnp.full((ctx_len, GROUP_W), ctx_value, F32), t], axis=0)

    return lanes(cos, cos, 1.0), lanes(-sin, zero, 0.0), lanes(zero, sin, 0.0)


def kernel(x, c, ctx, c_ctx, w_ada, b_ada, norm_g, w_in, w_out, da_lambda, da_subln, hg_lb, hg_norm, sc_w, sc_b, s5_a_re, s5_a_im, s5_log_dt, s5_b_re, s5_b_im, s5_c_re, s5_c_im, s5_d, s5_w_glu, s5_b_glu, w_ffn_in, w_ffn_out):
    B, T, D = x.shape
    ctx_len = ctx.shape[1]
    L = w_ada.shape[0]
    assert ctx_len % TILE == 0 and T % TILE == 0 and T % GRID_W == 0
    n_ctx_tiles = ctx_len // TILE

    bp = -(-(B + 1) // SUBLANES) * SUBLANES
    cvec = jnp.concatenate([c, c_ctx[None, :], jnp.zeros((bp - B - 1, D), c.dtype)], axis=0)
    mods = _modulation(cvec.astype(F32), w_ada, b_ada).reshape(L, bp, 6, D)

    cos, sin_a, sin_b = _rope_tables(T // GRID_W, ctx_len)
    lb = jnp.cumsum(jax.nn.softmax(hg_lb.astype(F32), axis=0), axis=0)
    lb = lb - lb[:1]
    w_in_b = w_in.astype(BF16)
    w_out_b = w_out.astype(BF16)
    w_ffn_in_b = w_ffn_in.astype(BF16)
    w_ffn_out_b = w_ffn_out.astype(BF16)
    s5_mats = jax.vmap(_s5_chunk_matrices)(s5_a_re, s5_a_im, s5_log_dt, s5_b_re, s5_b_im, s5_c_re,
                                           s5_c_im, s5_d)

    S = ctx_len + T
    h_parts = (ctx, x, 0)
    for l in range(L):
        want_ctx = l < L - 1
        lam_init = 0.8 - 0.6 * math.exp(-0.3 * l)
        qkv, rest, u5 = _inproj(h_parts, S, mods[l], norm_g[l, 0], w_in_b[l], cos, sin_a, sin_b, B,
                                n_ctx_tiles)
        a = _attention(qkv, da_lambda[l], da_subln[l], lam_init, ctx_len, want_ctx)
        b = _hgrn(rest, lb[l], hg_norm[l])
        cc = _short_conv(rest, sc_w[l], sc_b[l], ctx_len)
        tmat, bs, cs, a_l = (m[l] for m in s5_mats)
        dd = _s5_chunked(u5, tmat, bs, cs, a_l, s5_w_glu[l], s5_b_glu[l], ctx_len)
        h = _post(h_parts, S, a, b, cc, dd, mods[l], norm_g[l].astype(F32), w_out_b[l], w_ffn_in_b[l],
                  w_ffn_out_b[l], B, n_ctx_tiles, want_ctx)
        h_parts = (h, h, n_ctx_tiles)
    return h
```

```python
import functools
import math

import jax
import jax.numpy as jnp
from jax import lax
from jax.experimental import pallas as pl
from jax.experimental.pallas import tpu as pltpu

F32 = jnp.float32
BF16 = jnp.bfloat16

EPS = 1e-6
LOG2E = math.log2(math.e)
GRID_W = 64
ROPE_BASE = 10000.0
GROUP_W = 256
DA_HEADS = 4
DA_HD = 32
HG_HEADS = 4
HG_HD = 64
HG_CHUNK = 16
HG_BLOCK = 64
S5_NGROUPS = 16
S5_GROUP = 16
S5_STATE = 64
SUBLANES = 8
TILE = 256
INPROJ_NB = 4
POST_NB = 2
S5_L = 16
VMEM_LIMIT = 56 * 1024 * 1024

_NT = (((1,), (1,)), ((), ()))


def _dot(a, b):
    return jnp.dot(a, b, preferred_element_type=F32)


def _dot_nt(a, b):
    return lax.dot_general(a, b, _NT, preferred_element_type=F32)


def _split2_dot_rhs(m, x):
    x1 = x.astype(BF16)
    x2 = (x - x1.astype(F32)).astype(BF16)
    return _dot(m, x1) + _dot(m, x2)


def _split2_dot(x, m):
    x1 = x.astype(BF16)
    x2 = (x - x1.astype(F32)).astype(BF16)
    return _dot(x1, m) + _dot(x2, m)


def _silu(x):
    return x * jax.nn.sigmoid(x)


def _rms_scale(x):
    return lax.rsqrt(jnp.mean(x * x, axis=-1, keepdims=True) + EPS)


def _group_mean_matrix(n, group):
    sh = int(math.log2(group))
    r = lax.broadcasted_iota(jnp.int32, (n, n), 0) >> sh
    c = lax.broadcasted_iota(jnp.int32, (n, n), 1) >> sh
    return jnp.where(r == c, 1.0 / group, 0.0).astype(BF16)


def _params(*sem):
    return pltpu.CompilerParams(dimension_semantics=sem, vmem_limit_bytes=VMEM_LIMIT)


def _mod_kernel(c_ref, w_ref, b_ref, o_ref):
    sc = _silu(c_ref[...])
    o_ref[0] = _dot(sc.astype(BF16), w_ref[0].astype(BF16)) + b_ref[0]


def _modulation(cvec, w_ada, b_ada):
    L, D, N = w_ada.shape
    Bp = cvec.shape[0]
    tn = 1536
    return pl.pallas_call(
        _mod_kernel,
        grid=(L, N // tn),
        in_specs=[
            pl.BlockSpec((Bp, D), lambda l, j: (0, 0)),
            pl.BlockSpec((1, D, tn), lambda l, j: (l, 0, j)),
            pl.BlockSpec((1, 1, tn), lambda l, j: (l, 0, j)),
        ],
        out_specs=pl.BlockSpec((1, Bp, tn), lambda l, j: (l, 0, j)),
        out_shape=jax.ShapeDtypeStruct((L, Bp, N), F32),
        compiler_params=_params("arbitrary", "arbitrary"),
        name="modulation",
    )(cvec, w_ada, b_ada.reshape(L, 1, N))


def _rope(x, cos, sin_a, sin_b):
    return x * cos + pltpu.roll(x, GROUP_W - 1, 1) * sin_a + pltpu.roll(x, 1, 1) * sin_b


def _tile_mod(is_ctx, modx_ref, modc_ref):
    return jnp.where(is_ctx, modc_ref[...], modx_ref[...])


def _seq_specs(h_parts, block, n_ctx_tiles, first_tile=0):
    _, _, lat_off = h_parts

    def ctx_idx(b, t):
        return (b, jnp.minimum(t + first_tile, n_ctx_tiles - 1), 0)

    def lat_idx(b, t):
        return (b, jnp.maximum(t + first_tile, n_ctx_tiles) - n_ctx_tiles + lat_off, 0)

    return [pl.BlockSpec(block, ctx_idx), pl.BlockSpec(block, lat_idx)]


def _inproj_kernel(n_ctx_tiles, hc_ref, hx_ref, modx_ref, modc_ref, g_ref, w_ref, cos_ref, sa_ref,
                   sb_ref, qkv_ref, rest_ref, u5_ref):
    nb, tile, d_model = hx_ref.shape
    is_ctx = pl.program_id(1) < n_ctx_tiles
    mod = _tile_mod(is_ctx, modx_ref, modc_ref)
    h = jnp.where(is_ctx, hc_ref[...], hx_ref[...])
    hn = h * _rms_scale(h) * g_ref[...]
    y = hn * (1.0 + mod[:, 1:2, :]) + mod[:, 0:1, :]
    p = _dot(y.reshape(nb * tile, d_model).astype(BF16), w_ref[...])
    cos, sa, sb = cos_ref[...], sa_ref[...], sb_ref[...]
    for j in range(nb):
        pj = p[j * tile:(j + 1) * tile]
        q = _rope(pj[:, 0:GROUP_W], cos, sa, sb) * (DA_HD ** -0.5 * math.log2(math.e))
        k = _rope(pj[:, GROUP_W:2 * GROUP_W], cos, sa, sb)
        qkv_ref[j, :, 0:GROUP_W] = q.astype(BF16)
        qkv_ref[j, :, GROUP_W:2 * GROUP_W] = k.astype(BF16)
        qkv_ref[j, :, 2 * GROUP_W:3 * GROUP_W] = pj[:, 2 * GROUP_W:3 * GROUP_W].astype(BF16)
        rest_ref[j] = pj[:, 3 * GROUP_W:11 * GROUP_W]
        u5_ref[:, j, :, :] = pj[:, 11 * GROUP_W:].reshape(tile // S5_L, S5_L, GROUP_W)


def _inproj(h_parts, S, mod, g, w, cos, sa, sb, n_batch, n_ctx_tiles):
    B, _, D = h_parts[1].shape
    N = w.shape[1]
    nt = S // TILE
    nb = INPROJ_NB
    assert B % nb == 0
    tab = pl.BlockSpec((TILE, GROUP_W), lambda b, t: (t, 0))
    return pl.pallas_call(
        functools.partial(_inproj_kernel, n_ctx_tiles),
        grid=(B // nb, nt),
        in_specs=_seq_specs(h_parts, (nb, TILE, D), n_ctx_tiles) + [
            pl.BlockSpec((nb, 6, D), lambda b, t: (b, 0, 0)),
            pl.BlockSpec((1, 6, D), lambda b, t: (n_batch, 0, 0)),
            pl.BlockSpec((1, D), lambda b, t: (0, 0)),
            pl.BlockSpec((D, N), lambda b, t: (0, 0)),
            tab, tab, tab,
        ],
        out_specs=[
            pl.BlockSpec((nb, TILE, 3 * GROUP_W), lambda b, t: (b, t, 0)),
            pl.BlockSpec((nb, TILE, 8 * GROUP_W), lambda b, t: (b, t, 0)),
            pl.BlockSpec((TILE // S5_L, nb, S5_L, GROUP_W), lambda b, t: (t, b, 0, 0)),
        ],
        out_shape=[
            jax.ShapeDtypeStruct((B, S, 3 * GROUP_W), BF16),
            jax.ShapeDtypeStruct((B, S, 8 * GROUP_W), F32),
            jax.ShapeDtypeStruct((S // S5_L, B, S5_L, GROUP_W), F32),
        ],
        compiler_params=_params("parallel", "arbitrary"),
        name="inproj",
    )(h_parts[0], h_parts[1], mod, mod, g.reshape(1, D), w, cos, sa, sb)


def _attn_kernel(lam_init, ctx_len, q_off, q_ref, k_ref, v_ref, lamv_ref, g_ref, o_ref,
                 sc_ref, e_ref, va_ref, vb_ref):
    tq = q_ref.shape[1]
    n_keys = k_ref.shape[1]
    lv = lamv_ref[...]
    lam = (jnp.exp(jnp.sum(lv[0:1] * lv[1:2], axis=-1, keepdims=True))
           - jnp.exp(jnp.sum(lv[2:3] * lv[3:4], axis=-1, keepdims=True)) + lam_init)
    lane = lax.broadcasted_iota(jnp.int32, (1, GROUP_W), 1)
    lane_map = lane >> 5
    lane_head = lane >> 6

    @pl.when(pl.program_id(1) == 0)
    def _():
        v = v_ref[0]
        one = jnp.ones_like(v)
        va_ref[...] = jnp.where(lane == GROUP_W - 1, one, v)
        vb_ref[...] = jnp.where(lane == 0, one, v)

    q = q_ref[0]
    q8 = jnp.concatenate(
        [jnp.where(lane_map == j, q, jnp.zeros_like(q)) for j in range(2 * DA_HEADS)], axis=0)

    def attend(nk):
        sc_ref[:, 0:nk] = _dot_nt(q8, k_ref[0, 0:nk, :])
        acc = jnp.zeros((tq, GROUP_W), F32)
        for h in range(DA_HEADS):
            for m in range(2):
                rows = slice((2 * h + m) * tq, (2 * h + m + 1) * tq)
                s = sc_ref[rows, 0:nk]
                e_ref[rows, 0:nk] = jnp.exp2(s - jnp.max(s, axis=-1, keepdims=True)).astype(BF16)
            vv = va_ref if h < DA_HEADS - 1 else vb_ref
            sum_col = GROUP_W - 1 if h < DA_HEADS - 1 else 0
            o2 = _dot(e_ref[2 * h * tq:(2 * h + 2) * tq, 0:nk], vv[0:nk, :])
            o0, o1 = o2[0:tq], o2[tq:2 * tq]
            w0 = 1.0 / o0[:, sum_col:sum_col + 1]
            w1 = lam / o1[:, sum_col:sum_col + 1]
            acc = acc + jnp.where(lane_head == h, o0 * w0 - o1 * w1, 0.0)
        ms = _split2_dot(acc * acc, _group_mean_matrix(GROUP_W, 2 * DA_HD))
        o_ref[0] = (acc * lax.rsqrt(ms + EPS) * g_ref[...] * (1.0 - lam_init)).astype(o_ref.dtype)

    if q_off == 0 and ctx_len == tq:
        @pl.when(pl.program_id(1) == 0)
        def _():
            attend(ctx_len)

        @pl.when(pl.program_id(1) != 0)
        def _():
            attend(n_keys)
    else:
        attend(n_keys)


def _attention(qkv, lam_vecs, subln_g, lam_init, ctx_len, want_ctx):
    B, S, _ = qkv.shape
    tq = TILE
    q_off = 0 if want_ctx else ctx_len // tq
    nq = S // tq - q_off
    g = jnp.tile(subln_g.astype(F32), DA_HEADS).reshape(1, GROUP_W)
    return pl.pallas_call(
        functools.partial(_attn_kernel, lam_init, ctx_len, q_off),
        grid=(B, nq),
        in_specs=[
            pl.BlockSpec((1, tq, GROUP_W), lambda b, i: (b, i + q_off, 0)),
            pl.BlockSpec((1, S, GROUP_W), lambda b, i: (b, 0, 1)),
            pl.BlockSpec((1, S, GROUP_W), lambda b, i: (b, 0, 2)),
            pl.BlockSpec((4, DA_HD), lambda b, i: (0, 0)),
            pl.BlockSpec((1, GROUP_W), lambda b, i: (0, 0)),
        ],
        out_specs=pl.BlockSpec((1, tq, GROUP_W), lambda b, i: (b, i, 0)),
        out_shape=jax.ShapeDtypeStruct((B, nq * tq, GROUP_W), BF16),
        scratch_shapes=[
            pltpu.VMEM((2 * DA_HEADS * tq, S), F32),
            pltpu.VMEM((2 * DA_HEADS * tq, S), BF16),
            pltpu.VMEM((S, GROUP_W), BF16),
            pltpu.VMEM((S, GROUP_W), BF16),
        ],
        compiler_params=_params("parallel", "arbitrary"),
        name="diff_attention",
    )(qkv, qkv, qkv, lam_vecs.astype(F32), g)


def _hgrn_kernel(n_tiles, q_ref, ff_ref, fb_ref, i_ref, g_ref, lb_ref, lbt_ref, ng_ref, o_ref,
                 od_ref, st2_ref, u2_ref, oc2_ref):
    n_blocks = TILE // HG_BLOCK
    chunks_per_block = HG_BLOCK // HG_CHUNK
    row = lax.broadcasted_iota(jnp.int32, (TILE, TILE), 0)
    col = lax.broadcasted_iota(jnp.int32, (TILE, TILE), 1)
    same_chunk = (row >> 4) == (col >> 4)
    same_block = (row >> 6) == (col >> 6)
    lane = lax.broadcasted_iota(jnp.int32, (1, GROUP_W), 1)
    lane_head = lane >> 6
    lane_block = lane >> 6
    lane_pos = (lane >> 4) & (chunks_per_block - 1)
    gmean = _group_mean_matrix(GROUP_W, HG_HD)

    def ones(mask):
        return jnp.where(mask, 1.0, 0.0).astype(BF16)

    consts = []
    for d in range(2):
        dist = ((row >> 4) - (col >> 4)) if d == 0 else ((col >> 4) - (row >> 4))
        causal = same_chunk & ((col <= row) if d == 0 else (col >= row))
        allowed = same_block & ((dist >= 1) | causal)
        row_mats = [ones(causal), ones(same_block & (dist >= 1))]
        col_mats = [ones(same_chunk & ((row <= col) if d == 0 else (row >= col))), ones(same_chunk)]
        consts.append((allowed, jnp.concatenate(row_mats, axis=0), jnp.concatenate(col_mats, axis=1)))

    st2_ref[...] = jnp.zeros_like(st2_ref)

    def stack_heads(x):
        zero = jnp.zeros_like(x)
        return jnp.concatenate(
            [jnp.where(lane_head == h, x, zero).astype(BF16) for h in range(HG_HEADS)], axis=0)

    def tile_body(ti, carry):
        for d in range(2):
            allowed, row_mat, col_mat = consts[d]
            f_ref = ff_ref if d == 0 else fb_ref
            lb = lb_ref[d:d + 1, :]
            lb_t = lbt_ref[:, d:d + 1]
            st_ref, u_ref, oc_ref = st2_ref.at[d], u2_ref.at[d], oc2_ref.at[d]
            if d == 0:
                t = ti
            else:
                t = jnp.where(ti == 0, 0, n_tiles - ti)
            rows = pl.ds(pl.multiple_of(t * TILE, TILE), TILE)
            fr = f_ref[0, rows, :]
            lf = jnp.log(lb + (1.0 - lb) * jax.nn.sigmoid(fr)) * LOG2E
            sums = _split2_dot_rhs(row_mat, lf)
            qd = _silu(q_ref[0, rows, :]) * jnp.exp2(sums[0:TILE])
            q_state = qd * jnp.exp2(sums[TILE:2 * TILE])
            f_t = lb_t + (1.0 - lb_t) * jax.nn.sigmoid(fr.T)
            kk_t = 1.0 - f_t
            sums_t = _split2_dot(jnp.log(f_t) * LOG2E, col_mat)
            g_t, gt_t = sums_t[:, 0:TILE], sums_t[:, TILE:2 * TILE]
            kend_t = kk_t * jnp.exp2(gt_t - g_t)
            k_var = [kk_t * jnp.exp2(-g_t), kend_t]
            later = jnp.zeros_like(gt_t)
            for m in range(1, chunks_per_block):
                if d == 0:
                    shifted = pltpu.roll(gt_t, TILE - m * HG_CHUNK, 1)
                    inside = lane_pos <= chunks_per_block - 1 - m
                else:
                    shifted = pltpu.roll(gt_t, m * HG_CHUNK, 1)
                    inside = lane_pos >= m
                later = later + jnp.where(inside, shifted, 0.0)
                if m + 1 < chunks_per_block:
                    k_var.append(kend_t * jnp.exp2(later))
            kblk_t = kend_t * jnp.exp2(later)
            dec_t = jnp.exp2(gt_t + later)
            vb = i_ref[0, rows, :].astype(BF16)

            qd4 = stack_heads(qd)
            x_pos = []
            for p in range(chunks_per_block):
                keys = jnp.zeros_like(kend_t)
                for n in range(chunks_per_block):
                    pc = p - n if d == 0 else p + n
                    if 0 <= pc < chunks_per_block:
                        keys = jnp.where(lane_pos == pc, k_var[n], keys)
                q_p = jnp.concatenate(
                    [qd4[h * TILE + b * HG_BLOCK + p * HG_CHUNK:h * TILE + b * HG_BLOCK + (p + 1) * HG_CHUNK]
                     for h in range(HG_HEADS) for b in range(n_blocks)], axis=0)
                x_pos.append(_dot(q_p, keys.astype(BF16)))
            a_heads = []
            for h in range(HG_HEADS):
                a = jnp.concatenate(
                    [x_pos[p][(h * n_blocks + b) * HG_CHUNK:(h * n_blocks + b + 1) * HG_CHUNK]
                     for b in range(n_blocks) for p in range(chunks_per_block)], axis=0)
                a_heads.append(jnp.where(allowed, a, 0.0).astype(BF16))
            zb = jnp.zeros_like(vb)
            v4 = jnp.concatenate(
                [jnp.where(lane_head == h, vb, zb) for h in range(HG_HEADS)], axis=0)
            o_tile = _dot(jnp.concatenate(a_heads, axis=1), v4)

            zero = jnp.zeros_like(kblk_t)
            k_stack = jnp.concatenate(
                [jnp.where(lane_block == b, kblk_t, zero).astype(BF16) for b in range(n_blocks)], axis=0)
            u_ref[...] = _dot(k_stack, vb)

            qs4 = stack_heads(q_state)
            order = range(n_blocks) if d == 0 else range(n_blocks - 1, -1, -1)
            for b in order:
                st = st_ref[...]
                q4b = jnp.concatenate(
                    [qs4[h * TILE + b * HG_BLOCK:h * TILE + (b + 1) * HG_BLOCK] for h in range(HG_HEADS)],
                    axis=0)
                ob4 = _dot(q4b, st.astype(BF16))
                ob = jnp.zeros((HG_BLOCK, GROUP_W), F32)
                for h in range(HG_HEADS):
                    ob = ob + jnp.where(lane_head == h, ob4[h * HG_BLOCK:(h + 1) * HG_BLOCK], 0.0)
                oc_ref[b * HG_BLOCK:(b + 1) * HG_BLOCK, :] = ob
                first = b * HG_BLOCK + (0 if d == 0 else HG_BLOCK - HG_CHUNK)
                st_ref[...] = st * dec_t[:, first:first + 1] + u_ref[b * TILE:(b + 1) * TILE, :]
            od_ref[d, rows, :] = o_tile + oc_ref[...]
        return carry

    lax.fori_loop(0, n_tiles, tile_body, 0)

    def finish_body(t, carry):
        rows = pl.ds(pl.multiple_of(t * TILE, TILE), TILE)
        tot = od_ref[0, rows, :] + od_ref[1, rows, :]
        ms = _split2_dot(tot * tot, gmean)
        o_ref[0, rows, :] = (tot * lax.rsqrt(ms + EPS) * ng_ref[...]
                             * _silu(g_ref[0, rows, :])).astype(o_ref.dtype)
        return carry

    lax.fori_loop(0, n_tiles, finish_body, 0)


def _hgrn(rest, lb, norm_g):
    B, S, _ = rest.shape
    n_tiles = S // TILE

    def part(j):
        return pl.BlockSpec((1, S, GROUP_W), lambda b: (b, 0, j))

    ng = jnp.tile(norm_g.astype(F32), HG_HEADS).reshape(1, GROUP_W)
    return pl.pallas_call(
        functools.partial(_hgrn_kernel, n_tiles),
        grid=(B,),
        in_specs=[part(0), part(1), part(2), part(3), part(4),
                  pl.BlockSpec((2, GROUP_W), lambda b: (0, 0)),
                  pl.BlockSpec((GROUP_W, 2), lambda b: (0, 0)),
                  pl.BlockSpec((1, GROUP_W), lambda b: (0, 0))],
        out_specs=pl.BlockSpec((1, S, GROUP_W), lambda b: (b, 0, 0)),
        out_shape=jax.ShapeDtypeStruct((B, S, GROUP_W), BF16),
        scratch_shapes=[
            pltpu.VMEM((2, S, GROUP_W), F32),
            pltpu.VMEM((2, GROUP_W, GROUP_W), F32),
            pltpu.VMEM((2, TILE // HG_BLOCK * TILE, GROUP_W), F32),
            pltpu.VMEM((2, TILE, GROUP_W), F32),
        ],
        compiler_params=_params("parallel"),
        name="hgrn2",
    )(rest, rest, rest, rest, rest, lb, lb.T, ng)


def _conv_kernel(ctx_len, b_ref, c_ref, u_ref, w_ref, bias_ref, o_ref):
    S = b_ref.shape[1]
    v = c_ref[0] * u_ref[0]
    row = lax.broadcasted_iota(jnp.int32, (S, 1), 0)
    prev = jnp.where((row == 0) | (row == ctx_len), 0.0, pltpu.roll(v, 1, 0))
    nxt = jnp.where((row == ctx_len - 1) | (row == S - 1), 0.0, pltpu.roll(v, S - 1, 0))
    y = w_ref[0:1, :] * prev + w_ref[1:2, :] * v + w_ref[2:3, :] * nxt
    o_ref[0] = (b_ref[0] * (y + bias_ref[...])).astype(o_ref.dtype)


def _short_conv(rest, w, bias, ctx_len):
    B, S, _ = rest.shape

    def part(j):
        return pl.BlockSpec((1, S, GROUP_W), lambda b: (b, 0, j))

    return pl.pallas_call(
        functools.partial(_conv_kernel, ctx_len),
        grid=(B,),
        in_specs=[part(5), part(6), part(7),
                  pl.BlockSpec((3, GROUP_W), lambda b: (0, 0)),
                  pl.BlockSpec((1, GROUP_W), lambda b: (0, 0))],
        out_specs=pl.BlockSpec((1, S, GROUP_W), lambda b: (b, 0, 0)),
        out_shape=jax.ShapeDtypeStruct((B, S, GROUP_W), BF16),
        compiler_params=_params("parallel"),
        name="short_conv",
    )(rest, rest, rest, w.astype(F32), bias.astype(F32).reshape(1, GROUP_W))


def _s5_discretise(a_re, a_im, log_dt, b_re, b_im):
    dt = jnp.exp(log_dt)[..., None]
    mag = jnp.exp(dt * a_re)
    ab_re, ab_im = mag * jnp.cos(dt * a_im), mag * jnp.sin(dt * a_im)
    den = a_re * a_re + a_im * a_im
    nr = ab_re - 1.0
    f_re = (nr * a_re + ab_im * a_im) / den
    f_im = (ab_im * a_re - nr * a_im) / den
    bb_re = f_re[..., None] * b_re - f_im[..., None] * b_im
    bb_im = f_re[..., None] * b_im + f_im[..., None] * b_re
    return ab_re, ab_im, bb_re, bb_im


def _s5_chunk_matrices(a_re, a_im, log_dt, b_re, b_im, c_re, c_im, d_skip):
    L, G, P, N = S5_L, S5_NGROUPS, S5_GROUP, S5_STATE
    hp = lax.Precision.HIGHEST
    ab_re, ab_im, bb_re, bb_im = _s5_discretise(
        a_re.astype(F32), a_im.astype(F32), log_dt.astype(F32), b_re.astype(F32), b_im.astype(F32))
    c_re, c_im = c_re.astype(F32), c_im.astype(F32)
    j_dt = (jnp.arange(L + 1, dtype=F32).reshape(L + 1, 1, 1, 1)
            * jnp.exp(log_dt.astype(F32))[None, :, :, None])
    mag = jnp.exp(j_dt * a_re.astype(F32)[None])
    pr, pi = mag * jnp.cos(j_dt * a_im.astype(F32)[None]), mag * jnp.sin(j_dt * a_im.astype(F32)[None])
    ca_re = c_re[None] * pr[:, :, :, None, :] - c_im[None] * pi[:, :, :, None, :]
    ca_im = c_re[None] * pi[:, :, :, None, :] + c_im[None] * pr[:, :, :, None, :]
    k = (jnp.einsum("jdgqn,dgnp->dgpqj", ca_re[:L], bb_re, precision=hp)
         - jnp.einsum("jdgqn,dgnp->dgpqj", ca_im[:L], bb_im, precision=hp))
    k_f = k[0].reshape(G * P, P * L)
    k_b = k[1][..., ::-1].reshape(G * P, P * L)
    col_t = jnp.arange(P * L) % L
    t_rows = [jnp.where(col_t >= s, jnp.roll(k_f, s, axis=-1), 0.0)
              + jnp.where(col_t <= s, jnp.roll(k_b, s - (L - 1), axis=-1), 0.0) for s in range(L)]
    tmat = jnp.stack(t_rows, axis=1).reshape(G, P * L, P * L)
    skip = jnp.repeat(d_skip.astype(F32).reshape(G, P), L, axis=1).reshape(G, 1, L * P)
    tmat = tmat + jnp.eye(L * P, dtype=F32)[None] * skip

    def to_state(pw_re, pw_im, d):
        br = jnp.swapaxes(bb_re[d], 1, 2)[None]
        bi = jnp.swapaxes(bb_im[d], 1, 2)[None]
        re = pw_re[:, :, None, :] * br - pw_im[:, :, None, :] * bi
        im = pw_re[:, :, None, :] * bi + pw_im[:, :, None, :] * br
        return (re.transpose(1, 2, 0, 3).reshape(G, L * P, N),
                im.transpose(1, 2, 0, 3).reshape(G, L * P, N))

    f_re, f_im = to_state(pr[:L, 0][::-1], pi[:L, 0][::-1], 0)
    g_re, g_im = to_state(pr[:L, 1], pi[:L, 1], 1)
    bs = jnp.concatenate([f_re, g_re, f_im, g_im], axis=2)

    def from_state(d, flip):
        pw_re, pw_im = pr[1:, d], pi[1:, d]
        if flip:
            pw_re, pw_im = pw_re[::-1], pw_im[::-1]
        pw_re = jnp.tile(pw_re.transpose(1, 2, 0), (1, 1, P))
        pw_im = jnp.tile(pw_im.transpose(1, 2, 0), (1, 1, P))
        cr = jnp.repeat(jnp.swapaxes(c_re[d], 1, 2), L, axis=-1)
        ci = jnp.repeat(jnp.swapaxes(c_im[d], 1, 2), L, axis=-1)
        return cr * pw_re - ci * pw_im, cr * pw_im + ci * pw_re

    (cf_re, cf_im), (cb_re, cb_im) = from_state(0, False), from_state(1, True)
    cs = jnp.concatenate([cf_re, cb_re, -cf_im, -cb_im], axis=1)
    a_l = jnp.stack([jnp.concatenate([pr[L, 0], pr[L, 1]], axis=-1),
                     jnp.concatenate([pi[L, 0], pi[L, 1]], axis=-1)], axis=1)
    return tmat.astype(BF16), bs.astype(BF16), cs.astype(BF16), a_l


def _s5_chunk_kernel(n_ctx_chunks, n_chunks, u_ref, t_ref, bs_ref, cs_ref, al_ref, y_ref,
                     v_ref, xs_ref):
    n_ch, n_steps, n_rows = u_ref.shape
    nb = n_rows // n_chunks
    n = S5_STATE
    u = u_ref[...].reshape(n_ch * n_steps, n_rows).T.astype(BF16)
    v_ref[...] = _dot(u, bs_ref[0])
    fwd_lanes = lax.broadcasted_iota(jnp.int32, (1, 2 * n), 1) < n
    a_re = jnp.broadcast_to(al_ref[0, 0:1, :], (nb, 2 * n))
    a_im = jnp.broadcast_to(al_ref[0, 1:2, :], (nb, 2 * n))

    def step(i, carry):
        xr, xi = carry
        cb = jnp.where(i < n_ctx_chunks, n_ctx_chunks - 1 - i, n_chunks - 1 - (i - n_ctx_chunks))
        rf = pl.ds(pl.multiple_of(i * nb, nb), nb)
        rb = pl.ds(pl.multiple_of(cb * nb, nb), nb)
        xs_ref[rf, 0:n] = xr[:, 0:n]
        xs_ref[rb, n:2 * n] = xr[:, n:2 * n]
        xs_ref[rf, 2 * n:3 * n] = xi[:, 0:n]
        xs_ref[rb, 3 * n:4 * n] = xi[:, n:2 * n]
        vr = jnp.where(fwd_lanes, v_ref[rf, 0:2 * n], v_ref[rb, 0:2 * n])
        vi = jnp.where(fwd_lanes, v_ref[rf, 2 * n:4 * n], v_ref[rb, 2 * n:4 * n])
        return a_re * xr - a_im * xi + vr, a_re * xi + a_im * xr + vi

    zero = jnp.zeros((nb, 2 * n), F32)
    lax.fori_loop(0, n_chunks, step, (zero, zero))
    y = _dot(u, t_ref[0]) + _dot(xs_ref[...].astype(BF16), cs_ref[0])
    y_ref[...] = y.T.reshape(n_ch, n_steps, n_rows)


def _s5_glu_kernel(y_ref, wg_ref, bg_ref, o_ref):
    n_chunks, nb, n_steps, width = y_ref.shape
    for j in range(nb):
        y = jax.nn.gelu(y_ref[:, j, :, :].reshape(n_chunks * n_steps, width))
        gate = jax.nn.sigmoid(_dot(y.astype(BF16), wg_ref[...]) + bg_ref[...])
        o_ref[j] = (y * gate).astype(o_ref.dtype)


def _s5_chunked(u5c, tmat, bs, cs, a_l, w_glu, b_glu, ctx_len):
    n_chunks, B, L, W = u5c.shape
    P = S5_GROUP
    S = n_chunks * L
    assert L == S5_L and ctx_len % L == 0 and TILE % L == 0
    rows = n_chunks * B
    ut = u5c.reshape(rows, L, W).transpose(2, 1, 0)

    def per_group(arr):
        return pl.BlockSpec((1,) + arr.shape[1:], lambda g: (g,) + (0,) * (arr.ndim - 1))

    group_block = pl.BlockSpec((P, L, rows), lambda g: (g, 0, 0))
    yt = pl.pallas_call(
        functools.partial(_s5_chunk_kernel, ctx_len // L, n_chunks),
        grid=(W // P,),
        in_specs=[group_block, per_group(tmat), per_group(bs), per_group(cs), per_group(a_l)],
        out_specs=group_block,
        out_shape=jax.ShapeDtypeStruct((W, L, rows), F32),
        scratch_shapes=[pltpu.VMEM((rows, 4 * S5_STATE), F32),
                        pltpu.VMEM((rows, 4 * S5_STATE), F32)],
        compiler_params=_params("parallel"),
        name="s5_chunks",
    )(ut, tmat, bs, cs, a_l)
    y = yt.transpose(2, 1, 0).reshape(n_chunks, B, L, W)

    nb = POST_NB
    tile_chunks = TILE // L
    w_b = w_glu.astype(BF16)
    b_row = b_glu.astype(F32).reshape(1, W)
    return pl.pallas_call(
        _s5_glu_kernel,
        grid=(B // nb, n_chunks // tile_chunks),
        in_specs=[pl.BlockSpec((tile_chunks, nb, L, W), lambda b, t: (t, b, 0, 0)),
                  pl.BlockSpec(w_b.shape, lambda b, t: (0, 0)),
                  pl.BlockSpec(b_row.shape, lambda b, t: (0, 0))],
        out_specs=pl.BlockSpec((nb, TILE, W), lambda b, t: (b, t, 0)),
        out_shape=jax.ShapeDtypeStruct((B, S, W), BF16),
        compiler_params=_params("parallel", "arbitrary"),
        name="s5_glu",
    )(y, w_b, b_row)


def _post_kernel(n_hidden_chunks, ctx_tiles_here, hc_ref, hx_ref, a_ref, b_ref, c_ref, d_ref,
                 modx_ref, modc_ref, ng_ref, wo_ref, wi_ref, w2_ref, o_ref):
    nb, tile, d_model = hx_ref.shape
    n_rows = nb * tile
    is_ctx = pl.program_id(1) < ctx_tiles_here
    mod = _tile_mod(is_ctx, modx_ref, modc_ref)

    def flat(ref):
        return ref[...].reshape(n_rows, GROUP_W).astype(BF16)

    mix = (_dot(flat(a_ref), wo_ref[0:GROUP_W, :])
           + _dot(flat(b_ref), wo_ref[GROUP_W:2 * GROUP_W, :])
           + _dot(flat(c_ref), wo_ref[2 * GROUP_W:3 * GROUP_W, :])
           + _dot(flat(d_ref), wo_ref[3 * GROUP_W:4 * GROUP_W, :])).reshape(nb, tile, d_model)
    h = (jnp.where(is_ctx, hc_ref[...], hx_ref[...])
         + mod[:, 2:3, :] * (mix * _rms_scale(mix) * ng_ref[1:2, :]))
    y = (h * _rms_scale(h) * ng_ref[2:3, :]) * (1.0 + mod[:, 4:5, :]) + mod[:, 3:4, :]
    yb = y.reshape(n_rows, d_model).astype(BF16)
    hidden = w2_ref.shape[0]
    hc = hidden // n_hidden_chunks
    ffn = jnp.zeros((n_rows, d_model), F32)
    for j in range(n_hidden_chunks):
        gate = _dot(yb, wi_ref[:, j * hc:(j + 1) * hc])
        up = _dot(yb, wi_ref[:, hidden + j * hc:hidden + (j + 1) * hc])
        ffn = ffn + _dot((_silu(gate) * up).astype(BF16), w2_ref[j * hc:(j + 1) * hc, :])
    ffn = ffn.reshape(nb, tile, d_model)
    o_ref[...] = h + mod[:, 5:6, :] * (ffn * _rms_scale(ffn) * ng_ref[3:4, :])


def _post(h_parts, S, a, b, c, d, mod, norm_g, w_out, w_ffn_in, w_ffn_out, n_batch, n_ctx_tiles,
          want_ctx):
    B, _, D = h_parts[1].shape
    off = 0 if want_ctx else n_ctx_tiles
    nt = S // TILE - off
    a_off = off if a.shape[1] == S else 0
    nb = POST_NB
    assert B % nb == 0

    def rows(o):
        return lambda bi, t: (bi, t + o, 0)

    def whole(arr):
        return pl.BlockSpec(arr.shape, lambda bi, t: (0,) * arr.ndim, pipeline_mode=pl.Buffered(1))

    mix_spec = lambda o: pl.BlockSpec((nb, TILE, GROUP_W), rows(o))
    return pl.pallas_call(
        functools.partial(_post_kernel, 2, n_ctx_tiles - off),
        grid=(B // nb, nt),
        in_specs=_seq_specs(h_parts, (nb, TILE, D), n_ctx_tiles, first_tile=off) + [
            mix_spec(a_off), mix_spec(off), mix_spec(off), mix_spec(off),
            pl.BlockSpec((nb, 6, D), lambda bi, t: (bi, 0, 0)),
            pl.BlockSpec((1, 6, D), lambda bi, t: (n_batch, 0, 0)),
            whole(norm_g), whole(w_out), whole(w_ffn_in), whole(w_ffn_out),
        ],
        out_specs=pl.BlockSpec((nb, TILE, D), lambda bi, t: (bi, t, 0)),
        out_shape=jax.ShapeDtypeStruct((B, nt * TILE, D), F32),
        compiler_params=_params("parallel", "arbitrary"),
        name="post",
    )(h_parts[0], h_parts[1], a, b, c, d, mod, mod, norm_g, w_out, w_ffn_in, w_ffn_out)


def _rope_tables(n_rows, ctx_len):
    rows = jnp.broadcast_to(jnp.arange(n_rows, dtype=F32)[:, None], (n_rows, GRID_W)).reshape(-1)
    cols = jnp.broadcast_to(jnp.arange(GRID_W, dtype=F32)[None, :], (n_rows, GRID_W)).reshape(-1)
    n_freq = DA_HD // 4
    inv = ROPE_BASE ** (-jnp.arange(n_freq, dtype=F32) / n_freq)
    ang = jnp.concatenate([rows[:, None] * inv, cols[:, None] * inv], axis=-1)
    cos, sin = jnp.cos(ang), jnp.sin(ang)
    zero = jnp.zeros_like(sin)
    reps = GROUP_W // DA_HD

    def lanes(even, odd, ctx_value):
        t = jnp.tile(jnp.stack([even, odd], axis=-1).reshape(even.shape[0], DA_HD), (1, reps))
        return jnp.concatenate([jnp.full((ctx_len, GROUP_W), ctx_value, F32), t], axis=0)

    return lanes(cos, cos, 1.0), lanes(-sin, zero, 0.0), lanes(zero, sin, 0.0)


def kernel(x, c, ctx, c_ctx, w_ada, b_ada, norm_g, w_in, w_out, da_lambda, da_subln, hg_lb, hg_norm, sc_w, sc_b, s5_a_re, s5_a_im, s5_log_dt, s5_b_re, s5_b_im, s5_c_re, s5_c_im, s5_d, s5_w_glu, s5_b_glu, w_ffn_in, w_ffn_out):
    B, T, D = x.shape
    ctx_len = ctx.shape[1]
    L = w_ada.shape[0]
    assert ctx_len % TILE == 0 and T % TILE == 0 and T % GRID_W == 0
    n_ctx_tiles = ctx_len // TILE

    bp = -(-(B + 1) // SUBLANES) * SUBLANES
    cvec = jnp.concatenate([c, c_ctx[None, :], jnp.zeros((bp - B - 1, D), c.dtype)], axis=0)
    mods = _modulation(cvec.astype(F32), w_ada, b_ada).reshape(L, bp, 6, D)

    cos, sin_a, sin_b = _rope_tables(T // GRID_W, ctx_len)
    lb = jnp.cumsum(jax.nn.softmax(hg_lb.astype(F32), axis=0), axis=0)
    lb = lb - lb[:1]
    w_in_b = w_in.astype(BF16)
    w_out_b = w_out.astype(BF16)
    w_ffn_in_b = w_ffn_in.astype(BF16)
    w_ffn_out_b = w_ffn_out.astype(BF16)
    s5_mats = jax.vmap(_s5_chunk_matrices)(s5_a_re, s5_a_im, s5_log_dt, s5_b_re, s5_b_im, s5_c_re,
                                           s5_c_im, s5_d)

    S = ctx_len + T
    h_parts = (ctx, x, 0)
    for l in range(L):
        want_ctx = l < L - 1
        lam_init = 0.8 - 0.6 * math.exp(-0.3 * l)
        qkv, rest, u5 = _inproj(h_parts, S, mods[l], norm_g[l, 0], w_in_b[l], cos, sin_a, sin_b, B,
                                n_ctx_tiles)
        a = _attention(qkv, da_lambda[l], da_subln[l], lam_init, ctx_len, want_ctx)
        b = _hgrn(rest, lb[l], hg_norm[l])
        cc = _short_conv(rest, sc_w[l], sc_b[l], ctx_len)
        tmat, bs, cs, a_l = (m[l] for m in s5_mats)
        dd = _s5_chunked(u5, tmat, bs, cs, a_l, s5_w_glu[l], s5_b_glu[l], ctx_len)
        h = _post(h_parts, S, a, b, cc, dd, mods[l], norm_g[l].astype(F32), w_out_b[l], w_ffn_in_b[l],
                  w_ffn_out_b[l], B, n_ctx_tiles, want_ctx)
        h_parts = (h, h, n_ctx_tiles)
    return h
```

```python
import functools
import math

import jax
import jax.numpy as jnp
from jax import lax
from jax.experimental import pallas as pl
from jax.experimental.pallas import tpu as pltpu

F32 = jnp.float32
BF16 = jnp.bfloat16

EPS = 1e-6
LOG2E = math.log2(math.e)
GRID_W = 64
ROPE_BASE = 10000.0
GROUP_W = 256
DA_HEADS = 4
DA_HD = 32
HG_HEADS = 4
HG_HD = 64
HG_CHUNK = 16
HG_BLOCK = 64
S5_NGROUPS = 16
S5_GROUP = 16
S5_STATE = 64
SUBLANES = 8
TILE = 256
HGRN_NB = 2
INPROJ_NB = 4
POST_NB = 2
S5_L = 16
VMEM_LIMIT = 56 * 1024 * 1024

_NT = (((1,), (1,)), ((), ()))


def _dot(a, b):
    return jnp.dot(a, b, preferred_element_type=F32)


def _dot_nt(a, b):
    return lax.dot_general(a, b, _NT, preferred_element_type=F32)


def _split2_dot_rhs(m, x):
    x1 = x.astype(BF16)
    x2 = (x - x1.astype(F32)).astype(BF16)
    return _dot(m, x1) + _dot(m, x2)


def _split2_dot(x, m):
    x1 = x.astype(BF16)
    x2 = (x - x1.astype(F32)).astype(BF16)
    return _dot(x1, m) + _dot(x2, m)


def _silu(x):
    return x * jax.nn.sigmoid(x)


def _rms_scale(x):
    return lax.rsqrt(jnp.mean(x * x, axis=-1, keepdims=True) + EPS)


def _group_mean_matrix(n, group):
    sh = int(math.log2(group))
    r = lax.broadcasted_iota(jnp.int32, (n, n), 0) >> sh
    c = lax.broadcasted_iota(jnp.int32, (n, n), 1) >> sh
    return jnp.where(r == c, 1.0 / group, 0.0).astype(BF16)


def _params(*sem):
    return pltpu.CompilerParams(dimension_semantics=sem, vmem_limit_bytes=VMEM_LIMIT)


def _mod_kernel(c_ref, w_ref, b_ref, o_ref):
    sc = _silu(c_ref[...])
    o_ref[0] = _dot(sc.astype(BF16), w_ref[0].astype(BF16)) + b_ref[0]


def _modulation(cvec, w_ada, b_ada):
    L, D, N = w_ada.shape
    Bp = cvec.shape[0]
    tn = 1536
    return pl.pallas_call(
        _mod_kernel,
        grid=(L, N // tn),
        in_specs=[
            pl.BlockSpec((Bp, D), lambda l, j: (0, 0)),
            pl.BlockSpec((1, D, tn), lambda l, j: (l, 0, j)),
            pl.BlockSpec((1, 1, tn), lambda l, j: (l, 0, j)),
        ],
        out_specs=pl.BlockSpec((1, Bp, tn), lambda l, j: (l, 0, j)),
        out_shape=jax.ShapeDtypeStruct((L, Bp, N), F32),
        compiler_params=_params("arbitrary", "arbitrary"),
        name="modulation",
    )(cvec, w_ada, b_ada.reshape(L, 1, N))


def _rope(x, cos, sin_a, sin_b):
    return x * cos + pltpu.roll(x, GROUP_W - 1, 1) * sin_a + pltpu.roll(x, 1, 1) * sin_b


def _tile_mod(is_ctx, modx_ref, modc_ref):
    return jnp.where(is_ctx, modc_ref[...], modx_ref[...])


def _seq_specs(h_parts, block, n_ctx_tiles, first_tile=0):
    _, _, lat_off = h_parts

    def ctx_idx(b, t):
        return (b, jnp.minimum(t + first_tile, n_ctx_tiles - 1), 0)

    def lat_idx(b, t):
        return (b, jnp.maximum(t + first_tile, n_ctx_tiles) - n_ctx_tiles + lat_off, 0)

    return [pl.BlockSpec(block, ctx_idx), pl.BlockSpec(block, lat_idx)]


def _inproj_kernel(n_ctx_tiles, hc_ref, hx_ref, modx_ref, modc_ref, g_ref, w_ref, cos_ref, sa_ref,
                   sb_ref, qkv_ref, rest_ref, u5_ref):
    nb, tile, d_model = hx_ref.shape
    is_ctx = pl.program_id(1) < n_ctx_tiles
    mod = _tile_mod(is_ctx, modx_ref, modc_ref)
    h = jnp.where(is_ctx, hc_ref[...], hx_ref[...])
    hn = h * _rms_scale(h) * g_ref[...]
    y = hn * (1.0 + mod[:, 1:2, :]) + mod[:, 0:1, :]
    p = _dot(y.reshape(nb * tile, d_model).astype(BF16), w_ref[...])
    cos, sa, sb = cos_ref[...], sa_ref[...], sb_ref[...]
    for j in range(nb):
        pj = p[j * tile:(j + 1) * tile]
        q = _rope(pj[:, 0:GROUP_W], cos, sa, sb) * (DA_HD ** -0.5 * math.log2(math.e))
        k = _rope(pj[:, GROUP_W:2 * GROUP_W], cos, sa, sb)
        qkv_ref[j, :, 0:GROUP_W] = q.astype(BF16)
        qkv_ref[j, :, GROUP_W:2 * GROUP_W] = k.astype(BF16)
        qkv_ref[j, :, 2 * GROUP_W:3 * GROUP_W] = pj[:, 2 * GROUP_W:3 * GROUP_W].astype(BF16)
        rest_ref[j] = pj[:, 3 * GROUP_W:11 * GROUP_W]
        u5_ref[:, j, :, :] = pj[:, 11 * GROUP_W:].reshape(tile // S5_L, S5_L, GROUP_W)


def _inproj(h_parts, S, mod, g, w, cos, sa, sb, n_batch, n_ctx_tiles):
    B, _, D = h_parts[1].shape
    N = w.shape[1]
    nt = S // TILE
    nb = INPROJ_NB
    assert B % nb == 0
    tab = pl.BlockSpec((TILE, GROUP_W), lambda b, t: (t, 0))
    return pl.pallas_call(
        functools.partial(_inproj_kernel, n_ctx_tiles),
        grid=(B // nb, nt),
        in_specs=_seq_specs(h_parts, (nb, TILE, D), n_ctx_tiles) + [
            pl.BlockSpec((nb, 6, D), lambda b, t: (b, 0, 0)),
            pl.BlockSpec((1, 6, D), lambda b, t: (n_batch, 0, 0)),
            pl.BlockSpec((1, D), lambda b, t: (0, 0)),
            pl.BlockSpec((D, N), lambda b, t: (0, 0)),
            tab, tab, tab,
        ],
        out_specs=[
            pl.BlockSpec((nb, TILE, 3 * GROUP_W), lambda b, t: (b, t, 0)),
            pl.BlockSpec((nb, TILE, 8 * GROUP_W), lambda b, t: (b, t, 0)),
            pl.BlockSpec((TILE // S5_L, nb, S5_L, GROUP_W), lambda b, t: (t, b, 0, 0)),
        ],
        out_shape=[
            jax.ShapeDtypeStruct((B, S, 3 * GROUP_W), BF16),
            jax.ShapeDtypeStruct((B, S, 8 * GROUP_W), F32),
            jax.ShapeDtypeStruct((S // S5_L, B, S5_L, GROUP_W), F32),
        ],
        compiler_params=_params("parallel", "arbitrary"),
        name="inproj",
    )(h_parts[0], h_parts[1], mod, mod, g.reshape(1, D), w, cos, sa, sb)


def _attn_kernel(lam_init, ctx_len, q_off, q_ref, k_ref, v_ref, lamv_ref, g_ref, o_ref,
                 sc_ref, e_ref, va_ref, vb_ref):
    tq = q_ref.shape[1]
    n_keys = k_ref.shape[1]
    lv = lamv_ref[...]
    lam = (jnp.exp(jnp.sum(lv[0:1] * lv[1:2], axis=-1, keepdims=True))
           - jnp.exp(jnp.sum(lv[2:3] * lv[3:4], axis=-1, keepdims=True)) + lam_init)
    lane = lax.broadcasted_iota(jnp.int32, (1, GROUP_W), 1)
    lane_map = lane >> 5
    lane_head = lane >> 6

    @pl.when(pl.program_id(1) == 0)
    def _():
        v = v_ref[0]
        one = jnp.ones_like(v)
        va_ref[...] = jnp.where(lane == GROUP_W - 1, one, v)
        vb_ref[...] = jnp.where(lane == 0, one, v)

    q = q_ref[0]
    q8 = jnp.concatenate(
        [jnp.where(lane_map == j, q, jnp.zeros_like(q)) for j in range(2 * DA_HEADS)], axis=0)

    def attend(nk):
        sc_ref[:, 0:nk] = _dot_nt(q8, k_ref[0, 0:nk, :])
        acc = jnp.zeros((tq, GROUP_W), F32)
        for h in range(DA_HEADS):
            for m in range(2):
                rows = slice((2 * h + m) * tq, (2 * h + m + 1) * tq)
                s = sc_ref[rows, 0:nk]
                e_ref[rows, 0:nk] = jnp.exp2(s - jnp.max(s, axis=-1, keepdims=True)).astype(BF16)
            vv = va_ref if h < DA_HEADS - 1 else vb_ref
            sum_col = GROUP_W - 1 if h < DA_HEADS - 1 else 0
            o2 = _dot(e_ref[2 * h * tq:(2 * h + 2) * tq, 0:nk], vv[0:nk, :])
            o0, o1 = o2[0:tq], o2[tq:2 * tq]
            w0 = 1.0 / o0[:, sum_col:sum_col + 1]
            w1 = lam / o1[:, sum_col:sum_col + 1]
            acc = acc + jnp.where(lane_head == h, o0 * w0 - o1 * w1, 0.0)
        ms = _split2_dot(acc * acc, _group_mean_matrix(GROUP_W, 2 * DA_HD))
        o_ref[0] = (acc * lax.rsqrt(ms + EPS) * g_ref[...] * (1.0 - lam_init)).astype(o_ref.dtype)

    if q_off == 0 and ctx_len == tq:
        @pl.when(pl.program_id(1) == 0)
        def _():
            attend(ctx_len)

        @pl.when(pl.program_id(1) != 0)
        def _():
            attend(n_keys)
    else:
        attend(n_keys)


def _attention(qkv, lam_vecs, subln_g, lam_init, ctx_len, want_ctx):
    B, S, _ = qkv.shape
    tq = TILE
    q_off = 0 if want_ctx else ctx_len // tq
    nq = S // tq - q_off
    g = jnp.tile(subln_g.astype(F32), DA_HEADS).reshape(1, GROUP_W)
    return pl.pallas_call(
        functools.partial(_attn_kernel, lam_init, ctx_len, q_off),
        grid=(B, nq),
        in_specs=[
            pl.BlockSpec((1, tq, GROUP_W), lambda b, i: (b, i + q_off, 0)),
            pl.BlockSpec((1, S, GROUP_W), lambda b, i: (b, 0, 1)),
            pl.BlockSpec((1, S, GROUP_W), lambda b, i: (b, 0, 2)),
            pl.BlockSpec((4, DA_HD), lambda b, i: (0, 0)),
            pl.BlockSpec((1, GROUP_W), lambda b, i: (0, 0)),
        ],
        out_specs=pl.BlockSpec((1, tq, GROUP_W), lambda b, i: (b, i, 0)),
        out_shape=jax.ShapeDtypeStruct((B, nq * tq, GROUP_W), BF16),
        scratch_shapes=[
            pltpu.VMEM((2 * DA_HEADS * tq, S), F32),
            pltpu.VMEM((2 * DA_HEADS * tq, S), BF16),
            pltpu.VMEM((S, GROUP_W), BF16),
            pltpu.VMEM((S, GROUP_W), BF16),
        ],
        compiler_params=_params("parallel", "arbitrary"),
        name="diff_attention",
    )(qkv, qkv, qkv, lam_vecs.astype(F32), g)


def _hgrn_constants():
    row = lax.broadcasted_iota(jnp.int32, (TILE, TILE), 0)
    col = lax.broadcasted_iota(jnp.int32, (TILE, TILE), 1)
    same_chunk = (row // HG_CHUNK) == (col // HG_CHUNK)
    same_block = (row // HG_BLOCK) == (col // HG_BLOCK)

    def ones(mask):
        return jnp.where(mask, 1.0, 0.0).astype(BF16)

    row_mats, col_mats, allowed = [], [], []
    for d in range(2):
        dist = (row // HG_CHUNK - col // HG_CHUNK) * (1 if d == 0 else -1)
        causal = same_chunk & ((col <= row) if d == 0 else (col >= row))
        row_mats.append(jnp.concatenate([ones(causal), ones(same_block & (dist >= 1))], axis=0))
        col_mats.append(jnp.concatenate(
            [ones(same_chunk & ((row <= col) if d == 0 else (row >= col))), ones(same_chunk)], axis=1))
        allowed.append(ones(same_block & ((dist >= 1) | causal)))
    return jnp.stack(row_mats), jnp.stack(col_mats), jnp.stack(allowed)


def _hgrn_kernel(qf_ref, ff_ref, if_ref, qb_ref, fb_ref, ib_ref, lb_ref, rowmat_ref, colmat_ref,
                 allow_ref, of_ref, ob_ref, st2_ref, u2_ref, oc2_ref):
    nb = qf_ref.shape[0]
    n_blocks = TILE // HG_BLOCK
    chunks_per_block = HG_BLOCK // HG_CHUNK
    lane = lax.broadcasted_iota(jnp.int32, (1, GROUP_W), 1)
    lane_head = lane >> int(math.log2(HG_HD))
    lane_block = lane >> int(math.log2(HG_BLOCK))
    lane_pos = (lane >> int(math.log2(HG_CHUNK))) & (chunks_per_block - 1)

    @pl.when(pl.program_id(1) == 0)
    def _():
        st2_ref[...] = jnp.zeros_like(st2_ref)

    def stack_heads(x):
        zero = jnp.zeros_like(x)
        return jnp.concatenate(
            [jnp.where(lane_head == h, x, zero).astype(BF16) for h in range(HG_HEADS)], axis=0)

    def chain(j, d):
        allowed = allow_ref[d] != 0
        row_mat, col_mat = rowmat_ref[d], colmat_ref[d]
        q_ref, f_ref, i_ref, o_ref = ((qf_ref, ff_ref, if_ref, of_ref) if d == 0
                                      else (qb_ref, fb_ref, ib_ref, ob_ref))
        lb = lb_ref[d:d + 1, :]
        st_ref, u_ref, oc_ref = st2_ref.at[j, d], u2_ref.at[j, d], oc2_ref.at[j, d]
        f = lb + (1.0 - lb) * jax.nn.sigmoid(f_ref[j])
        lf = jnp.log(f) * LOG2E
        yield
        sums = _split2_dot_rhs(row_mat, lf)
        kk_t = 1.0 - f.T
        sums_t = _split2_dot(lf.T, col_mat)
        yield
        qd = _silu(q_ref[j]) * jnp.exp2(sums[0:TILE])
        q_state = qd * jnp.exp2(sums[TILE:2 * TILE])
        g_t, gt_t = sums_t[:, 0:TILE], sums_t[:, TILE:2 * TILE]
        kend_t = kk_t * jnp.exp2(gt_t - g_t)
        k_var = [kk_t * jnp.exp2(-g_t), kend_t]
        later = jnp.zeros_like(gt_t)
        for m in range(1, chunks_per_block):
            if d == 0:
                shifted = pltpu.roll(gt_t, TILE - m * HG_CHUNK, 1)
                inside = lane_pos <= chunks_per_block - 1 - m
            else:
                shifted = pltpu.roll(gt_t, m * HG_CHUNK, 1)
                inside = lane_pos >= m
            later = later + jnp.where(inside, shifted, 0.0)
            if m + 1 < chunks_per_block:
                k_var.append(kend_t * jnp.exp2(later))
        kblk_t = kend_t * jnp.exp2(later)
        dec_t = jnp.exp2(gt_t + later)
        vb = i_ref[j].astype(BF16)
        yield

        zero = jnp.zeros_like(kblk_t)
        k_stack = jnp.concatenate(
            [jnp.where(lane_block == b, kblk_t, zero).astype(BF16) for b in range(n_blocks)], axis=0)
        u_ref[...] = _dot(k_stack, vb)
        yield

        qd4 = stack_heads(qd)
        x_pos = []
        for p in range(chunks_per_block):
            keys = jnp.zeros_like(kend_t)
            for n in range(chunks_per_block):
                pc = p - n if d == 0 else p + n
                if 0 <= pc < chunks_per_block:
                    keys = jnp.where(lane_pos == pc, k_var[n], keys)
            q_p = jnp.concatenate(
                [qd4[h * TILE + b * HG_BLOCK + p * HG_CHUNK:h * TILE + b * HG_BLOCK + (p + 1) * HG_CHUNK]
                 for h in range(HG_HEADS) for b in range(n_blocks)], axis=0)
            x_pos.append(_dot(q_p, keys.astype(BF16)))
            yield
        a_heads = []
        for h in range(HG_HEADS):
            a = jnp.concatenate(
                [x_pos[p][(h * n_blocks + b) * HG_CHUNK:(h * n_blocks + b + 1) * HG_CHUNK]
                 for b in range(n_blocks) for p in range(chunks_per_block)], axis=0)
            a_heads.append(jnp.where(allowed, a, 0.0).astype(BF16))
        zb = jnp.zeros_like(vb)
        v4 = jnp.concatenate(
            [jnp.where(lane_head == h, vb, zb) for h in range(HG_HEADS)], axis=0)
        o_tile = _dot(jnp.concatenate(a_heads, axis=1), v4)
        yield

        qs4 = stack_heads(q_state)
        order = range(n_blocks) if d == 0 else range(n_blocks - 1, -1, -1)
        for b in order:
            st = st_ref[...]
            q4b = jnp.concatenate(
                [qs4[h * TILE + b * HG_BLOCK:h * TILE + (b + 1) * HG_BLOCK] for h in range(HG_HEADS)],
                axis=0)
            ob4 = _dot(q4b, st.astype(BF16))
            ob = jnp.zeros((HG_BLOCK, GROUP_W), F32)
            for h in range(HG_HEADS):
                ob = ob + jnp.where(lane_head == h, ob4[h * HG_BLOCK:(h + 1) * HG_BLOCK], 0.0)
            oc_ref[b * HG_BLOCK:(b + 1) * HG_BLOCK, :] = ob
            first = b * HG_BLOCK + (0 if d == 0 else HG_BLOCK - HG_CHUNK)
            st_ref[...] = st * dec_t[:, first:first + 1] + u_ref[b * TILE:(b + 1) * TILE, :]
            yield
        o_ref[j] = o_tile + oc_ref[...]

    done = object()
    pending = [chain(j, d) for j in range(nb) for d in range(2)]
    while pending:
        pending = [c for c in pending if next(c, done) is not done]


def _hgrn_finish_kernel(of_ref, ob_ref, g_ref, ng_ref, o_ref):
    nb, tile, width = of_ref.shape
    tot = (of_ref[...] + ob_ref[...]).reshape(nb * tile, width)
    ms = _split2_dot(tot * tot, _group_mean_matrix(width, HG_HD))
    out = tot * lax.rsqrt(ms + EPS) * ng_ref[...] * _silu(g_ref[...].reshape(nb * tile, width))
    o_ref[...] = out.reshape(nb, tile, width).astype(o_ref.dtype)


def _hgrn(rest, lb, norm_g):
    B, S, _ = rest.shape
    n_tiles = S // TILE
    nb = HGRN_NB
    assert B % nb == 0

    def back(i):
        return jnp.where(i == 0, 0, n_tiles - i)

    def fwd(part):
        return pl.BlockSpec((nb, TILE, GROUP_W), lambda b, i: (b, i, part))

    def bwd(part):
        return pl.BlockSpec((nb, TILE, GROUP_W), lambda b, i: (b, back(i), part))

    def whole(arr):
        return pl.BlockSpec(arr.shape, lambda b, i: (0,) * arr.ndim)

    consts = _hgrn_constants()
    dir_sds = jax.ShapeDtypeStruct((B, S, GROUP_W), F32)
    of, ob = pl.pallas_call(
        _hgrn_kernel,
        grid=(B // nb, n_tiles),
        in_specs=[fwd(0), fwd(1), fwd(3), bwd(0), bwd(2), bwd(3), whole(lb)]
        + [whole(c) for c in consts],
        out_specs=[pl.BlockSpec((nb, TILE, GROUP_W), lambda b, i: (b, i, 0)),
                   pl.BlockSpec((nb, TILE, GROUP_W), lambda b, i: (b, back(i), 0))],
        out_shape=[dir_sds, dir_sds],
        scratch_shapes=[
            pltpu.VMEM((nb, 2, GROUP_W, GROUP_W), F32),
            pltpu.VMEM((nb, 2, TILE // HG_BLOCK * TILE, GROUP_W), F32),
            pltpu.VMEM((nb, 2, TILE, GROUP_W), F32),
        ],
        compiler_params=_params("parallel", "arbitrary"),
        name="hgrn2",
    )(rest, rest, rest, rest, rest, rest, lb, *consts)

    ng = jnp.tile(norm_g.astype(F32), HG_HEADS).reshape(1, GROUP_W)
    tile_spec = pl.BlockSpec((nb, TILE, GROUP_W), lambda b, i: (b, i, 0))
    return pl.pallas_call(
        _hgrn_finish_kernel,
        grid=(B // nb, n_tiles),
        in_specs=[tile_spec, tile_spec, pl.BlockSpec((nb, TILE, GROUP_W), lambda b, i: (b, i, 4)),
                  whole(ng)],
        out_specs=tile_spec,
        out_shape=jax.ShapeDtypeStruct((B, S, GROUP_W), BF16),
        compiler_params=_params("parallel", "arbitrary"),
        name="hgrn2_finish",
    )(of, ob, rest, ng)


def _conv_kernel(ctx_len, b_ref, c_ref, u_ref, w_ref, bias_ref, o_ref):
    S = b_ref.shape[1]
    v = c_ref[0] * u_ref[0]
    row = lax.broadcasted_iota(jnp.int32, (S, 1), 0)
    prev = jnp.where((row == 0) | (row == ctx_len), 0.0, pltpu.roll(v, 1, 0))
    nxt = jnp.where((row == ctx_len - 1) | (row == S - 1), 0.0, pltpu.roll(v, S - 1, 0))
    y = w_ref[0:1, :] * prev + w_ref[1:2, :] * v + w_ref[2:3, :] * nxt
    o_ref[0] = (b_ref[0] * (y + bias_ref[...])).astype(o_ref.dtype)


def _short_conv(rest, w, bias, ctx_len):
    B, S, _ = rest.shape

    def part(j):
        return pl.BlockSpec((1, S, GROUP_W), lambda b: (b, 0, j))

    return pl.pallas_call(
        functools.partial(_conv_kernel, ctx_len),
        grid=(B,),
        in_specs=[part(5), part(6), part(7),
                  pl.BlockSpec((3, GROUP_W), lambda b: (0, 0)),
                  pl.BlockSpec((1, GROUP_W), lambda b: (0, 0))],
        out_specs=pl.BlockSpec((1, S, GROUP_W), lambda b: (b, 0, 0)),
        out_shape=jax.ShapeDtypeStruct((B, S, GROUP_W), BF16),
        compiler_params=_params("parallel"),
        name="short_conv",
    )(rest, rest, rest, w.astype(F32), bias.astype(F32).reshape(1, GROUP_W))


def _s5_discretise(a_re, a_im, log_dt, b_re, b_im):
    dt = jnp.exp(log_dt)[..., None]
    mag = jnp.exp(dt * a_re)
    ab_re, ab_im = mag * jnp.cos(dt * a_im), mag * jnp.sin(dt * a_im)
    den = a_re * a_re + a_im * a_im
    nr = ab_re - 1.0
    f_re = (nr * a_re + ab_im * a_im) / den
    f_im = (ab_im * a_re - nr * a_im) / den
    bb_re = f_re[..., None] * b_re - f_im[..., None] * b_im
    bb_im = f_re[..., None] * b_im + f_im[..., None] * b_re
    return ab_re, ab_im, bb_re, bb_im


def _s5_chunk_matrices(a_re, a_im, log_dt, b_re, b_im, c_re, c_im, d_skip):
    L, G, P, N = S5_L, S5_NGROUPS, S5_GROUP, S5_STATE
    hp = lax.Precision.HIGHEST
    ab_re, ab_im, bb_re, bb_im = _s5_discretise(
        a_re.astype(F32), a_im.astype(F32), log_dt.astype(F32), b_re.astype(F32), b_im.astype(F32))
    c_re, c_im = c_re.astype(F32), c_im.astype(F32)
    j_dt = (jnp.arange(L + 1, dtype=F32).reshape(L + 1, 1, 1, 1)
            * jnp.exp(log_dt.astype(F32))[None, :, :, None])
    mag = jnp.exp(j_dt * a_re.astype(F32)[None])
    pr, pi = mag * jnp.cos(j_dt * a_im.astype(F32)[None]), mag * jnp.sin(j_dt * a_im.astype(F32)[None])
    ca_re = c_re[None] * pr[:, :, :, None, :] - c_im[None] * pi[:, :, :, None, :]
    ca_im = c_re[None] * pi[:, :, :, None, :] + c_im[None] * pr[:, :, :, None, :]
    k = (jnp.einsum("jdgqn,dgnp->dgpqj", ca_re[:L], bb_re, precision=hp)
         - jnp.einsum("jdgqn,dgnp->dgpqj", ca_im[:L], bb_im, precision=hp))
    k_f = k[0].reshape(G * P, P * L)
    k_b = k[1][..., ::-1].reshape(G * P, P * L)
    col_t = jnp.arange(P * L) % L
    t_rows = [jnp.where(col_t >= s, jnp.roll(k_f, s, axis=-1), 0.0)
              + jnp.where(col_t <= s, jnp.roll(k_b, s - (L - 1), axis=-1), 0.0) for s in range(L)]
    tmat = jnp.stack(t_rows, axis=1).reshape(G, P * L, P * L)
    skip = jnp.repeat(d_skip.astype(F32).reshape(G, P), L, axis=1).reshape(G, 1, L * P)
    tmat = tmat + jnp.eye(L * P, dtype=F32)[None] * skip

    def to_state(pw_re, pw_im, d):
        br = jnp.swapaxes(bb_re[d], 1, 2)[None]
        bi = jnp.swapaxes(bb_im[d], 1, 2)[None]
        re = pw_re[:, :, None, :] * br - pw_im[:, :, None, :] * bi
        im = pw_re[:, :, None, :] * bi + pw_im[:, :, None, :] * br
        return (re.transpose(1, 2, 0, 3).reshape(G, L * P, N),
                im.transpose(1, 2, 0, 3).reshape(G, L * P, N))

    f_re, f_im = to_state(pr[:L, 0][::-1], pi[:L, 0][::-1], 0)
    g_re, g_im = to_state(pr[:L, 1], pi[:L, 1], 1)
    bs = jnp.concatenate([f_re, g_re, f_im, g_im], axis=2)

    def from_state(d, flip):
        pw_re, pw_im = pr[1:, d], pi[1:, d]
        if flip:
            pw_re, pw_im = pw_re[::-1], pw_im[::-1]
        pw_re = jnp.tile(pw_re.transpose(1, 2, 0), (1, 1, P))
        pw_im = jnp.tile(pw_im.transpose(1, 2, 0), (1, 1, P))
        cr = jnp.repeat(jnp.swapaxes(c_re[d], 1, 2), L, axis=-1)
        ci = jnp.repeat(jnp.swapaxes(c_im[d], 1, 2), L, axis=-1)
        return cr * pw_re - ci * pw_im, cr * pw_im + ci * pw_re

    (cf_re, cf_im), (cb_re, cb_im) = from_state(0, False), from_state(1, True)
    cs = jnp.concatenate([cf_re, cb_re, -cf_im, -cb_im], axis=1)
    a_l = jnp.stack([jnp.concatenate([pr[L, 0], pr[L, 1]], axis=-1),
                     jnp.concatenate([pi[L, 0], pi[L, 1]], axis=-1)], axis=1)
    return tmat.astype(BF16), bs.astype(BF16), cs.astype(BF16), a_l


def _s5_chunk_kernel(n_ctx_chunks, n_chunks, u_ref, t_ref, bs_ref, cs_ref, al_ref, y_ref,
                     v_ref, xs_ref):
    n_ch, n_steps, n_rows = u_ref.shape
    nb = n_rows // n_chunks
    n = S5_STATE
    u = u_ref[...].reshape(n_ch * n_steps, n_rows).T.astype(BF16)
    v_ref[...] = _dot(u, bs_ref[0])
    fwd_lanes = lax.broadcasted_iota(jnp.int32, (1, 2 * n), 1) < n
    a_re = jnp.broadcast_to(al_ref[0, 0:1, :], (nb, 2 * n))
    a_im = jnp.broadcast_to(al_ref[0, 1:2, :], (nb, 2 * n))

    def step(i, carry):
        xr, xi = carry
        cb = jnp.where(i < n_ctx_chunks, n_ctx_chunks - 1 - i, n_chunks - 1 - (i - n_ctx_chunks))
        rf = pl.ds(pl.multiple_of(i * nb, nb), nb)
        rb = pl.ds(pl.multiple_of(cb * nb, nb), nb)
        xs_ref[rf, 0:n] = xr[:, 0:n]
        xs_ref[rb, n:2 * n] = xr[:, n:2 * n]
        xs_ref[rf, 2 * n:3 * n] = xi[:, 0:n]
        xs_ref[rb, 3 * n:4 * n] = xi[:, n:2 * n]
        vr = jnp.where(fwd_lanes, v_ref[rf, 0:2 * n], v_ref[rb, 0:2 * n])
        vi = jnp.where(fwd_lanes, v_ref[rf, 2 * n:4 * n], v_ref[rb, 2 * n:4 * n])
        return a_re * xr - a_im * xi + vr, a_re * xi + a_im * xr + vi

    zero = jnp.zeros((nb, 2 * n), F32)
    lax.fori_loop(0, n_chunks, step, (zero, zero))
    y = _dot(u, t_ref[0]) + _dot(xs_ref[...].astype(BF16), cs_ref[0])
    y_ref[...] = y.T.reshape(n_ch, n_steps, n_rows)


def _s5_glu_kernel(y_ref, wg_ref, bg_ref, o_ref):
    n_chunks, nb, n_steps, width = y_ref.shape
    for j in range(nb):
        y = jax.nn.gelu(y_ref[:, j, :, :].reshape(n_chunks * n_steps, width))
        gate = jax.nn.sigmoid(_dot(y.astype(BF16), wg_ref[...]) + bg_ref[...])
        o_ref[j] = (y * gate).astype(o_ref.dtype)


def _s5_chunked(u5c, tmat, bs, cs, a_l, w_glu, b_glu, ctx_len):
    n_chunks, B, L, W = u5c.shape
    P = S5_GROUP
    S = n_chunks * L
    assert L == S5_L and ctx_len % L == 0 and TILE % L == 0
    rows = n_chunks * B
    ut = u5c.reshape(rows, L, W).transpose(2, 1, 0)

    def per_group(arr):
        return pl.BlockSpec((1,) + arr.shape[1:], lambda g: (g,) + (0,) * (arr.ndim - 1))

    group_block = pl.BlockSpec((P, L, rows), lambda g: (g, 0, 0))
    yt = pl.pallas_call(
        functools.partial(_s5_chunk_kernel, ctx_len // L, n_chunks),
        grid=(W // P,),
        in_specs=[group_block, per_group(tmat), per_group(bs), per_group(cs), per_group(a_l)],
        out_specs=group_block,
        out_shape=jax.ShapeDtypeStruct((W, L, rows), F32),
        scratch_shapes=[pltpu.VMEM((rows, 4 * S5_STATE), F32),
                        pltpu.VMEM((rows, 4 * S5_STATE), F32)],
        compiler_params=_params("parallel"),
        name="s5_chunks",
    )(ut, tmat, bs, cs, a_l)
    y = yt.transpose(2, 1, 0).reshape(n_chunks, B, L, W)

    nb = POST_NB
    tile_chunks = TILE // L
    w_b = w_glu.astype(BF16)
    b_row = b_glu.astype(F32).reshape(1, W)
    return pl.pallas_call(
        _s5_glu_kernel,
        grid=(B // nb, n_chunks // tile_chunks),
        in_specs=[pl.BlockSpec((tile_chunks, nb, L, W), lambda b, t: (t, b, 0, 0)),
                  pl.BlockSpec(w_b.shape, lambda b, t: (0, 0)),
                  pl.BlockSpec(b_row.shape, lambda b, t: (0, 0))],
        out_specs=pl.BlockSpec((nb, TILE, W), lambda b, t: (b, t, 0)),
        out_shape=jax.ShapeDtypeStruct((B, S, W), BF16),
        compiler_params=_params("parallel", "arbitrary"),
        name="s5_glu",
    )(y, w_b, b_row)


def _post_kernel(n_hidden_chunks, ctx_tiles_here, hc_ref, hx_ref, a_ref, b_ref, c_ref, d_ref,
                 modx_ref, modc_ref, ng_ref, wo_ref, wi_ref, w2_ref, o_ref):
    nb, tile, d_model = hx_ref.shape
    n_rows = nb * tile
    is_ctx = pl.program_id(1) < ctx_tiles_here
    mod = _tile_mod(is_ctx, modx_ref, modc_ref)

    def flat(ref):
        return ref[...].reshape(n_rows, GROUP_W).astype(BF16)

    mix = (_dot(flat(a_ref), wo_ref[0:GROUP_W, :])
           + _dot(flat(b_ref), wo_ref[GROUP_W:2 * GROUP_W, :])
           + _dot(flat(c_ref), wo_ref[2 * GROUP_W:3 * GROUP_W, :])
           + _dot(flat(d_ref), wo_ref[3 * GROUP_W:4 * GROUP_W, :])).reshape(nb, tile, d_model)
    h = (jnp.where(is_ctx, hc_ref[...], hx_ref[...])
         + mod[:, 2:3, :] * (mix * _rms_scale(mix) * ng_ref[1:2, :]))
    y = (h * _rms_scale(h) * ng_ref[2:3, :]) * (1.0 + mod[:, 4:5, :]) + mod[:, 3:4, :]
    yb = y.reshape(n_rows, d_model).astype(BF16)
    hidden = w2_ref.shape[0]
    hc = hidden // n_hidden_chunks
    ffn = jnp.zeros((n_rows, d_model), F32)
    for j in range(n_hidden_chunks):
        gate = _dot(yb, wi_ref[:, j * hc:(j + 1) * hc])
        up = _dot(yb, wi_ref[:, hidden + j * hc:hidden + (j + 1) * hc])
        ffn = ffn + _dot((_silu(gate) * up).astype(BF16), w2_ref[j * hc:(j + 1) * hc, :])
    ffn = ffn.reshape(nb, tile, d_model)
    o_ref[...] = h + mod[:, 5:6, :] * (ffn * _rms_scale(ffn) * ng_ref[3:4, :])


def _post(h_parts, S, a, b, c, d, mod, norm_g, w_out, w_ffn_in, w_ffn_out, n_batch, n_ctx_tiles,
          want_ctx):
    B, _, D = h_parts[1].shape
    off = 0 if want_ctx else n_ctx_tiles
    nt = S // TILE - off
    a_off = off if a.shape[1] == S else 0
    nb = POST_NB
    assert B % nb == 0

    def rows(o):
        return lambda bi, t: (bi, t + o, 0)

    def whole(arr):
        return pl.BlockSpec(arr.shape, lambda bi, t: (0,) * arr.ndim, pipeline_mode=pl.Buffered(1))

    mix_spec = lambda o: pl.BlockSpec((nb, TILE, GROUP_W), rows(o))
    return pl.pallas_call(
        functools.partial(_post_kernel, 2, n_ctx_tiles - off),
        grid=(B // nb, nt),
        in_specs=_seq_specs(h_parts, (nb, TILE, D), n_ctx_tiles, first_tile=off) + [
            mix_spec(a_off), mix_spec(off), mix_spec(off), mix_spec(off),
            pl.BlockSpec((nb, 6, D), lambda bi, t: (bi, 0, 0)),
            pl.BlockSpec((1, 6, D), lambda bi, t: (n_batch, 0, 0)),
            whole(norm_g), whole(w_out), whole(w_ffn_in), whole(w_ffn_out),
        ],
        out_specs=pl.BlockSpec((nb, TILE, D), lambda bi, t: (bi, t, 0)),
        out_shape=jax.ShapeDtypeStruct((B, nt * TILE, D), F32),
        compiler_params=_params("parallel", "arbitrary"),
        name="post",
    )(h_parts[0], h_parts[1], a, b, c, d, mod, mod, norm_g, w_out, w_ffn_in, w_ffn_out)


def _rope_tables(n_rows, ctx_len):
    rows = jnp.broadcast_to(jnp.arange(n_rows, dtype=F32)[:, None], (n_rows, GRID_W)).reshape(-1)
    cols = jnp.broadcast_to(jnp.arange(GRID_W, dtype=F32)[None, :], (n_rows, GRID_W)).reshape(-1)
    n_freq = DA_HD // 4
    inv = ROPE_BASE ** (-jnp.arange(n_freq, dtype=F32) / n_freq)
    ang = jnp.concatenate([rows[:, None] * inv, cols[:, None] * inv], axis=-1)
    cos, sin = jnp.cos(ang), jnp.sin(ang)
    zero = jnp.zeros_like(sin)
    reps = GROUP_W // DA_HD

    def lanes(even, odd, ctx_value):
        t = jnp.tile(jnp.stack([even, odd], axis=-1).reshape(even.shape[0], DA_HD), (1, reps))
        return jnp.concatenate([jnp.full((ctx_len, GROUP_W), ctx_value, F32), t], axis=0)

    return lanes(cos, cos, 1.0), lanes(-sin, zero, 0.0), lanes(zero, sin, 0.0)


def kernel(x, c, ctx, c_ctx, w_ada, b_ada, norm_g, w_in, w_out, da_lambda, da_subln, hg_lb, hg_norm, sc_w, sc_b, s5_a_re, s5_a_im, s5_log_dt, s5_b_re, s5_b_im, s5_c_re, s5_c_im, s5_d, s5_w_glu, s5_b_glu, w_ffn_in, w_ffn_out):
    B, T, D = x.shape
    ctx_len = ctx.shape[1]
    L = w_ada.shape[0]
    assert ctx_len % TILE == 0 and T % TILE == 0 and T % GRID_W == 0
    n_ctx_tiles = ctx_len // TILE

    bp = -(-(B + 1) // SUBLANES) * SUBLANES
    cvec = jnp.concatenate([c, c_ctx[None, :], jnp.zeros((bp - B - 1, D), c.dtype)], axis=0)
    mods = _modulation(cvec.astype(F32), w_ada, b_ada).reshape(L, bp, 6, D)

    cos, sin_a, sin_b = _rope_tables(T // GRID_W, ctx_len)
    lb = jnp.cumsum(jax.nn.softmax(hg_lb.astype(F32), axis=0), axis=0)
    lb = lb - lb[:1]
    w_in_b = w_in.astype(BF16)
    w_out_b = w_out.astype(BF16)
    w_ffn_in_b = w_ffn_in.astype(BF16)
    w_ffn_out_b = w_ffn_out.astype(BF16)
    s5_mats = jax.vmap(_s5_chunk_matrices)(s5_a_re, s5_a_im, s5_log_dt, s5_b_re, s5_b_im, s5_c_re,
                                           s5_c_im, s5_d)

    S = ctx_len + T
    h_parts = (ctx, x, 0)
    for l in range(L):
        want_ctx = l < L - 1
        lam_init = 0.8 - 0.6 * math.exp(-0.3 * l)
        qkv, rest, u5 = _inproj(h_parts, S, mods[l], norm_g[l, 0], w_in_b[l], cos, sin_a, sin_b, B,
                                n_ctx_tiles)
        a = _attention(qkv, da_lambda[l], da_subln[l], lam_init, ctx_len, want_ctx)
        b = _hgrn(rest, lb[l], hg_norm[l])
        cc = _short_conv(rest, sc_w[l], sc_b[l], ctx_len)
        tmat, bs, cs, a_l = (m[l] for m in s5_mats)
        dd = _s5_chunked(u5, tmat, bs, cs, a_l, s5_w_glu[l], s5_b_glu[l], ctx_len)
        h = _post(h_parts, S, a, b, cc, dd, mods[l], norm_g[l].astype(F32), w_out_b[l], w_ffn_in_b[l],
                  w_ffn_out_b[l], B, n_ctx_tiles, want_ctx)
        h_parts = (h, h, n_ctx_tiles)
    return h
```

```python
import functools
import math

import jax
import jax.numpy as jnp
from jax import lax
from jax.experimental import pallas as pl
from jax.experimental.pallas import tpu as pltpu

F32 = jnp.float32
BF16 = jnp.bfloat16

EPS = 1e-6
LOG2E = math.log2(math.e)
GRID_W = 64
ROPE_BASE = 10000.0
GROUP_W = 256
DA_HEADS = 4
DA_HD = 32
HG_HEADS = 4
HG_HD = 64
HG_CHUNK = 16
HG_BLOCK = 64
S5_NGROUPS = 16
S5_GROUP = 16
S5_STATE = 64
SUBLANES = 8
TILE = 256
HGRN_NB = 2
INPROJ_NB = 4
POST_NB = 2
S5_L = 16
VMEM_LIMIT = 56 * 1024 * 1024

_NT = (((1,), (1,)), ((), ()))


def _dot(a, b):
    return jnp.dot(a, b, preferred_element_type=F32)


def _dot_nt(a, b):
    return lax.dot_general(a, b, _NT, preferred_element_type=F32)


def _split2_dot_rhs(m, x):
    x1 = x.astype(BF16)
    x2 = (x - x1.astype(F32)).astype(BF16)
    return _dot(m, x1) + _dot(m, x2)


def _split2_dot(x, m):
    x1 = x.astype(BF16)
    x2 = (x - x1.astype(F32)).astype(BF16)
    return _dot(x1, m) + _dot(x2, m)


def _silu(x):
    return x * jax.nn.sigmoid(x)


def _rms_scale(x):
    return lax.rsqrt(jnp.mean(x * x, axis=-1, keepdims=True) + EPS)


def _group_mean_matrix(n, group):
    sh = int(math.log2(group))
    r = lax.broadcasted_iota(jnp.int32, (n, n), 0) >> sh
    c = lax.broadcasted_iota(jnp.int32, (n, n), 1) >> sh
    return jnp.where(r == c, 1.0 / group, 0.0).astype(BF16)


def _params(*sem):
    return pltpu.CompilerParams(dimension_semantics=sem, vmem_limit_bytes=VMEM_LIMIT)


def _mod_kernel(c_ref, w_ref, b_ref, o_ref):
    sc = _silu(c_ref[...])
    o_ref[0] = _dot(sc.astype(BF16), w_ref[0].astype(BF16)) + b_ref[0]


def _modulation(cvec, w_ada, b_ada):
    L, D, N = w_ada.shape
    Bp = cvec.shape[0]
    tn = 1536
    return pl.pallas_call(
        _mod_kernel,
        grid=(L, N // tn),
        in_specs=[
            pl.BlockSpec((Bp, D), lambda l, j: (0, 0)),
            pl.BlockSpec((1, D, tn), lambda l, j: (l, 0, j)),
            pl.BlockSpec((1, 1, tn), lambda l, j: (l, 0, j)),
        ],
        out_specs=pl.BlockSpec((1, Bp, tn), lambda l, j: (l, 0, j)),
        out_shape=jax.ShapeDtypeStruct((L, Bp, N), F32),
        compiler_params=_params("arbitrary", "arbitrary"),
        name="modulation",
    )(cvec, w_ada, b_ada.reshape(L, 1, N))


def _rope(x, cos, sin_a, sin_b):
    return x * cos + pltpu.roll(x, GROUP_W - 1, 1) * sin_a + pltpu.roll(x, 1, 1) * sin_b


def _tile_mod(is_ctx, modx_ref, modc_ref):
    return jnp.where(is_ctx, modc_ref[...], modx_ref[...])


def _seq_specs(h_parts, block, n_ctx_tiles, first_tile=0):
    _, _, lat_off = h_parts

    def ctx_idx(b, t):
        return (b, jnp.minimum(t + first_tile, n_ctx_tiles - 1), 0)

    def lat_idx(b, t):
        return (b, jnp.maximum(t + first_tile, n_ctx_tiles) - n_ctx_tiles + lat_off, 0)

    return [pl.BlockSpec(block, ctx_idx), pl.BlockSpec(block, lat_idx)]


def _inproj_kernel(n_ctx_tiles, hc_ref, hx_ref, modx_ref, modc_ref, g_ref, w_ref, cos_ref, sa_ref,
                   sb_ref, qkv_ref, rest_ref, u5_ref):
    nb, tile, d_model = hx_ref.shape
    is_ctx = pl.program_id(1) < n_ctx_tiles
    mod = _tile_mod(is_ctx, modx_ref, modc_ref)
    h = jnp.where(is_ctx, hc_ref[...], hx_ref[...])
    hn = h * _rms_scale(h) * g_ref[...]
    y = hn * (1.0 + mod[:, 1:2, :]) + mod[:, 0:1, :]
    p = _dot(y.reshape(nb * tile, d_model).astype(BF16), w_ref[...])
    cos, sa, sb = cos_ref[...], sa_ref[...], sb_ref[...]
    for j in range(nb):
        pj = p[j * tile:(j + 1) * tile]
        q = _rope(pj[:, 0:GROUP_W], cos, sa, sb) * (DA_HD ** -0.5 * math.log2(math.e))
        k = _rope(pj[:, GROUP_W:2 * GROUP_W], cos, sa, sb)
        qkv_ref[j, :, 0:GROUP_W] = q.astype(BF16)
        qkv_ref[j, :, GROUP_W:2 * GROUP_W] = k.astype(BF16)
        qkv_ref[j, :, 2 * GROUP_W:3 * GROUP_W] = pj[:, 2 * GROUP_W:3 * GROUP_W].astype(BF16)
        rest_ref[j] = pj[:, 3 * GROUP_W:11 * GROUP_W]
        u5_ref[:, j, :, :] = pj[:, 11 * GROUP_W:].reshape(tile // S5_L, S5_L, GROUP_W)


def _inproj(h_parts, S, mod, g, w, cos, sa, sb, n_batch, n_ctx_tiles):
    B, _, D = h_parts[1].shape
    N = w.shape[1]
    nt = S // TILE
    nb = INPROJ_NB
    assert B % nb == 0
    tab = pl.BlockSpec((TILE, GROUP_W), lambda b, t: (t, 0))
    return pl.pallas_call(
        functools.partial(_inproj_kernel, n_ctx_tiles),
        grid=(B // nb, nt),
        in_specs=_seq_specs(h_parts, (nb, TILE, D), n_ctx_tiles) + [
            pl.BlockSpec((nb, 6, D), lambda b, t: (b, 0, 0)),
            pl.BlockSpec((1, 6, D), lambda b, t: (n_batch, 0, 0)),
            pl.BlockSpec((1, D), lambda b, t: (0, 0)),
            pl.BlockSpec((D, N), lambda b, t: (0, 0)),
            tab, tab, tab,
        ],
        out_specs=[
            pl.BlockSpec((nb, TILE, 3 * GROUP_W), lambda b, t: (b, t, 0)),
            pl.BlockSpec((nb, TILE, 8 * GROUP_W), lambda b, t: (b, t, 0)),
            pl.BlockSpec((TILE // S5_L, nb, S5_L, GROUP_W), lambda b, t: (t, b, 0, 0)),
        ],
        out_shape=[
            jax.ShapeDtypeStruct((B, S, 3 * GROUP_W), BF16),
            jax.ShapeDtypeStruct((B, S, 8 * GROUP_W), F32),
            jax.ShapeDtypeStruct((S // S5_L, B, S5_L, GROUP_W), F32),
        ],
        compiler_params=_params("parallel", "arbitrary"),
        name="inproj",
    )(h_parts[0], h_parts[1], mod, mod, g.reshape(1, D), w, cos, sa, sb)


def _attn_kernel(lam_init, ctx_len, q_off, q_ref, k_ref, v_ref, lamv_ref, g_ref, o_ref,
                 sc_ref, e_ref, va_ref, vb_ref):
    tq = q_ref.shape[1]
    n_keys = k_ref.shape[1]
    lv = lamv_ref[...]
    lam = (jnp.exp(jnp.sum(lv[0:1] * lv[1:2], axis=-1, keepdims=True))
           - jnp.exp(jnp.sum(lv[2:3] * lv[3:4], axis=-1, keepdims=True)) + lam_init)
    lane = lax.broadcasted_iota(jnp.int32, (1, GROUP_W), 1)
    lane_map = lane >> 5
    lane_head = lane >> 6

    @pl.when(pl.program_id(1) == 0)
    def _():
        v = v_ref[0]
        one = jnp.ones_like(v)
        va_ref[...] = jnp.where(lane == GROUP_W - 1, one, v)
        vb_ref[...] = jnp.where(lane == 0, one, v)

    q = q_ref[0]
    q8 = jnp.concatenate(
        [jnp.where(lane_map == j, q, jnp.zeros_like(q)) for j in range(2 * DA_HEADS)], axis=0)

    def attend(nk):
        sc_ref[:, 0:nk] = _dot_nt(q8, k_ref[0, 0:nk, :])
        acc = jnp.zeros((tq, GROUP_W), F32)
        for h in range(DA_HEADS):
            for m in range(2):
                rows = slice((2 * h + m) * tq, (2 * h + m + 1) * tq)
                s = sc_ref[rows, 0:nk]
                e_ref[rows, 0:nk] = jnp.exp2(s - jnp.max(s, axis=-1, keepdims=True)).astype(BF16)
            vv = va_ref if h < DA_HEADS - 1 else vb_ref
            sum_col = GROUP_W - 1 if h < DA_HEADS - 1 else 0
            o2 = _dot(e_ref[2 * h * tq:(2 * h + 2) * tq, 0:nk], vv[0:nk, :])
            o0, o1 = o2[0:tq], o2[tq:2 * tq]
            w0 = 1.0 / o0[:, sum_col:sum_col + 1]
            w1 = lam / o1[:, sum_col:sum_col + 1]
            acc = acc + jnp.where(lane_head == h, o0 * w0 - o1 * w1, 0.0)
        ms = _split2_dot(acc * acc, _group_mean_matrix(GROUP_W, 2 * DA_HD))
        o_ref[0] = (acc * lax.rsqrt(ms + EPS) * g_ref[...] * (1.0 - lam_init)).astype(o_ref.dtype)

    if q_off == 0 and ctx_len == tq:
        @pl.when(pl.program_id(1) == 0)
        def _():
            attend(ctx_len)

        @pl.when(pl.program_id(1) != 0)
        def _():
            attend(n_keys)
    else:
        attend(n_keys)


def _attention(qkv, lam_vecs, subln_g, lam_init, ctx_len, want_ctx):
    B, S, _ = qkv.shape
    tq = TILE
    q_off = 0 if want_ctx else ctx_len // tq
    nq = S // tq - q_off
    g = jnp.tile(subln_g.astype(F32), DA_HEADS).reshape(1, GROUP_W)
    return pl.pallas_call(
        functools.partial(_attn_kernel, lam_init, ctx_len, q_off),
        grid=(B, nq),
        in_specs=[
            pl.BlockSpec((1, tq, GROUP_W), lambda b, i: (b, i + q_off, 0)),
            pl.BlockSpec((1, S, GROUP_W), lambda b, i: (b, 0, 1)),
            pl.BlockSpec((1, S, GROUP_W), lambda b, i: (b, 0, 2)),
            pl.BlockSpec((4, DA_HD), lambda b, i: (0, 0)),
            pl.BlockSpec((1, GROUP_W), lambda b, i: (0, 0)),
        ],
        out_specs=pl.BlockSpec((1, tq, GROUP_W), lambda b, i: (b, i, 0)),
        out_shape=jax.ShapeDtypeStruct((B, nq * tq, GROUP_W), BF16),
        scratch_shapes=[
            pltpu.VMEM((2 * DA_HEADS * tq, S), F32),
            pltpu.VMEM((2 * DA_HEADS * tq, S), BF16),
            pltpu.VMEM((S, GROUP_W), BF16),
            pltpu.VMEM((S, GROUP_W), BF16),
        ],
        compiler_params=_params("parallel", "arbitrary"),
        name="diff_attention",
    )(qkv, qkv, qkv, lam_vecs.astype(F32), g)


def _hgrn_constants():
    row = lax.broadcasted_iota(jnp.int32, (TILE, TILE), 0)
    col = lax.broadcasted_iota(jnp.int32, (TILE, TILE), 1)
    same_chunk = (row // HG_CHUNK) == (col // HG_CHUNK)
    same_block = (row // HG_BLOCK) == (col // HG_BLOCK)

    def ones(mask):
        return jnp.where(mask, 1.0, 0.0).astype(BF16)

    row_mats, col_mats, allowed = [], [], []
    for d in range(2):
        dist = (row // HG_CHUNK - col // HG_CHUNK) * (1 if d == 0 else -1)
        causal = same_chunk & ((col <= row) if d == 0 else (col >= row))
        row_mats.append(jnp.concatenate([ones(causal), ones(same_block & (dist >= 1))], axis=0))
        col_mats.append(jnp.concatenate(
            [ones(same_chunk & ((row <= col) if d == 0 else (row >= col))), ones(same_chunk)], axis=1))
        allowed.append(ones(same_block & ((dist >= 1) | causal)))
    return jnp.stack(row_mats), jnp.stack(col_mats), jnp.stack(allowed)


def _hgrn_kernel(qf_ref, ff_ref, if_ref, qb_ref, fb_ref, ib_ref, lb_ref, rowmat_ref, colmat_ref,
                 allow_ref, of_ref, ob_ref, st2_ref, u2_ref, oc2_ref):
    nb = qf_ref.shape[0]
    n_blocks = TILE // HG_BLOCK
    chunks_per_block = HG_BLOCK // HG_CHUNK
    lane = lax.broadcasted_iota(jnp.int32, (1, GROUP_W), 1)
    lane_head = lane >> int(math.log2(HG_HD))
    lane_block = lane >> int(math.log2(HG_BLOCK))
    lane_pos = (lane >> int(math.log2(HG_CHUNK))) & (chunks_per_block - 1)

    @pl.when(pl.program_id(1) == 0)
    def _():
        st2_ref[...] = jnp.zeros_like(st2_ref)

    def stack_heads(x):
        zero = jnp.zeros_like(x)
        return jnp.concatenate(
            [jnp.where(lane_head == h, x, zero).astype(BF16) for h in range(HG_HEADS)], axis=0)

    def chain(j, d):
        allowed = allow_ref[d] != 0
        row_mat, col_mat = rowmat_ref[d], colmat_ref[d]
        q_ref, f_ref, i_ref, o_ref = ((qf_ref, ff_ref, if_ref, of_ref) if d == 0
                                      else (qb_ref, fb_ref, ib_ref, ob_ref))
        lb = lb_ref[d:d + 1, :]
        st_ref, u_ref, oc_ref = st2_ref.at[j, d], u2_ref.at[j, d], oc2_ref.at[j, d]
        f = lb + (1.0 - lb) * jax.nn.sigmoid(f_ref[j])
        lf = jnp.log(f) * LOG2E
        yield
        sums = _split2_dot_rhs(row_mat, lf)
        kk_t = 1.0 - f.T
        sums_t = _split2_dot(lf.T, col_mat)
        yield
        qd = _silu(q_ref[j]) * jnp.exp2(sums[0:TILE])
        q_state = qd * jnp.exp2(sums[TILE:2 * TILE])
        g_t, gt_t = sums_t[:, 0:TILE], sums_t[:, TILE:2 * TILE]
        kend_t = kk_t * jnp.exp2(gt_t - g_t)
        k_var = [kk_t * jnp.exp2(-g_t), kend_t]
        later = jnp.zeros_like(gt_t)
        for m in range(1, chunks_per_block):
            if d == 0:
                shifted = pltpu.roll(gt_t, TILE - m * HG_CHUNK, 1)
                inside = lane_pos <= chunks_per_block - 1 - m
            else:
                shifted = pltpu.roll(gt_t, m * HG_CHUNK, 1)
                inside = lane_pos >= m
            later = later + jnp.where(inside, shifted, 0.0)
            if m + 1 < chunks_per_block:
                k_var.append(kend_t * jnp.exp2(later))
        kblk_t = kend_t * jnp.exp2(later)
        dec_t = jnp.exp2(gt_t + later)
        vb = i_ref[j].astype(BF16)
        yield

        zero = jnp.zeros_like(kblk_t)
        k_stack = jnp.concatenate(
            [jnp.where(lane_block == b, kblk_t, zero).astype(BF16) for b in range(n_blocks)], axis=0)
        u_ref[...] = _dot(k_stack, vb)
        yield

        qd4 = stack_heads(qd)
        x_pos = []
        for p in range(chunks_per_block):
            keys = jnp.zeros_like(kend_t)
            for n in range(chunks_per_block):
                pc = p - n if d == 0 else p + n
                if 0 <= pc < chunks_per_block:
                    keys = jnp.where(lane_pos == pc, k_var[n], keys)
            q_p = jnp.concatenate(
                [qd4[h * TILE + b * HG_BLOCK + p * HG_CHUNK:h * TILE + b * HG_BLOCK + (p + 1) * HG_CHUNK]
                 for h in range(HG_HEADS) for b in range(n_blocks)], axis=0)
            x_pos.append(_dot(q_p, keys.astype(BF16)))
            yield
        a_heads = []
        for h in range(HG_HEADS):
            a = jnp.concatenate(
                [x_pos[p][(h * n_blocks + b) * HG_CHUNK:(h * n_blocks + b + 1) * HG_CHUNK]
                 for b in range(n_blocks) for p in range(chunks_per_block)], axis=0)
            a_heads.append(jnp.where(allowed, a, 0.0).astype(BF16))
        zb = jnp.zeros_like(vb)
        v4 = jnp.concatenate(
            [jnp.where(lane_head == h, vb, zb) for h in range(HG_HEADS)], axis=0)
        o_tile = _dot(jnp.concatenate(a_heads, axis=1), v4)
        yield

        qs4 = stack_heads(q_state)
        order = range(n_blocks) if d == 0 else range(n_blocks - 1, -1, -1)
        for b in order:
            st = st_ref[...]
            q4b = jnp.concatenate(
                [qs4[h * TILE + b * HG_BLOCK:h * TILE + (b + 1) * HG_BLOCK] for h in range(HG_HEADS)],
                axis=0)
            ob4 = _dot(q4b, st.astype(BF16))
            ob = jnp.zeros((HG_BLOCK, GROUP_W), F32)
            for h in range(HG_HEADS):
                ob = ob + jnp.where(lane_head == h, ob4[h * HG_BLOCK:(h + 1) * HG_BLOCK], 0.0)
            oc_ref[b * HG_BLOCK:(b + 1) * HG_BLOCK, :] = ob
            first = b * HG_BLOCK + (0 if d == 0 else HG_BLOCK - HG_CHUNK)
            st_ref[...] = st * dec_t[:, first:first + 1] + u_ref[b * TILE:(b + 1) * TILE, :]
            yield
        o_ref[j] = o_tile + oc_ref[...]

    done = object()
    pending = [chain(j, d) for j in range(nb) for d in range(2)]
    while pending:
        pending = [c for c in pending if next(c, done) is not done]


def _hgrn(rest, lb):
    B, S, _ = rest.shape
    n_tiles = S // TILE
    nb = HGRN_NB
    assert B % nb == 0

    def back(i):
        return jnp.where(i == 0, 0, n_tiles - i)

    def fwd(part):
        return pl.BlockSpec((nb, TILE, GROUP_W), lambda b, i: (b, i, part))

    def bwd(part):
        return pl.BlockSpec((nb, TILE, GROUP_W), lambda b, i: (b, back(i), part))

    def whole(arr):
        return pl.BlockSpec(arr.shape, lambda b, i: (0,) * arr.ndim)

    consts = _hgrn_constants()
    dir_sds = jax.ShapeDtypeStruct((B, S, GROUP_W), F32)
    return pl.pallas_call(
        _hgrn_kernel,
        grid=(B // nb, n_tiles),
        in_specs=[fwd(0), fwd(1), fwd(3), bwd(0), bwd(2), bwd(3), whole(lb)]
        + [whole(c) for c in consts],
        out_specs=[pl.BlockSpec((nb, TILE, GROUP_W), lambda b, i: (b, i, 0)),
                   pl.BlockSpec((nb, TILE, GROUP_W), lambda b, i: (b, back(i), 0))],
        out_shape=[dir_sds, dir_sds],
        scratch_shapes=[
            pltpu.VMEM((nb, 2, GROUP_W, GROUP_W), F32),
            pltpu.VMEM((nb, 2, TILE // HG_BLOCK * TILE, GROUP_W), F32),
            pltpu.VMEM((nb, 2, TILE, GROUP_W), F32),
        ],
        compiler_params=_params("parallel", "arbitrary"),
        name="hgrn2",
    )(rest, rest, rest, rest, rest, rest, lb, *consts)


def _conv_kernel(ctx_len, b_ref, c_ref, u_ref, w_ref, bias_ref, o_ref):
    S = b_ref.shape[1]
    v = c_ref[0] * u_ref[0]
    row = lax.broadcasted_iota(jnp.int32, (S, 1), 0)
    prev = jnp.where((row == 0) | (row == ctx_len), 0.0, pltpu.roll(v, 1, 0))
    nxt = jnp.where((row == ctx_len - 1) | (row == S - 1), 0.0, pltpu.roll(v, S - 1, 0))
    y = w_ref[0:1, :] * prev + w_ref[1:2, :] * v + w_ref[2:3, :] * nxt
    o_ref[0] = (b_ref[0] * (y + bias_ref[...])).astype(o_ref.dtype)


def _short_conv(rest, w, bias, ctx_len):
    B, S, _ = rest.shape

    def part(j):
        return pl.BlockSpec((1, S, GROUP_W), lambda b: (b, 0, j))

    return pl.pallas_call(
        functools.partial(_conv_kernel, ctx_len),
        grid=(B,),
        in_specs=[part(5), part(6), part(7),
                  pl.BlockSpec((3, GROUP_W), lambda b: (0, 0)),
                  pl.BlockSpec((1, GROUP_W), lambda b: (0, 0))],
        out_specs=pl.BlockSpec((1, S, GROUP_W), lambda b: (b, 0, 0)),
        out_shape=jax.ShapeDtypeStruct((B, S, GROUP_W), BF16),
        compiler_params=_params("parallel"),
        name="short_conv",
    )(rest, rest, rest, w.astype(F32), bias.astype(F32).reshape(1, GROUP_W))


def _s5_discretise(a_re, a_im, log_dt, b_re, b_im):
    dt = jnp.exp(log_dt)[..., None]
    mag = jnp.exp(dt * a_re)
    ab_re, ab_im = mag * jnp.cos(dt * a_im), mag * jnp.sin(dt * a_im)
    den = a_re * a_re + a_im * a_im
    nr = ab_re - 1.0
    f_re = (nr * a_re + ab_im * a_im) / den
    f_im = (ab_im * a_re - nr * a_im) / den
    bb_re = f_re[..., None] * b_re - f_im[..., None] * b_im
    bb_im = f_re[..., None] * b_im + f_im[..., None] * b_re
    return ab_re, ab_im, bb_re, bb_im


def _s5_chunk_matrices(a_re, a_im, log_dt, b_re, b_im, c_re, c_im, d_skip):
    L, G, P, N = S5_L, S5_NGROUPS, S5_GROUP, S5_STATE
    hp = lax.Precision.HIGHEST
    ab_re, ab_im, bb_re, bb_im = _s5_discretise(
        a_re.astype(F32), a_im.astype(F32), log_dt.astype(F32), b_re.astype(F32), b_im.astype(F32))
    c_re, c_im = c_re.astype(F32), c_im.astype(F32)
    j_dt = (jnp.arange(L + 1, dtype=F32).reshape(L + 1, 1, 1, 1)
            * jnp.exp(log_dt.astype(F32))[None, :, :, None])
    mag = jnp.exp(j_dt * a_re.astype(F32)[None])
    pr, pi = mag * jnp.cos(j_dt * a_im.astype(F32)[None]), mag * jnp.sin(j_dt * a_im.astype(F32)[None])
    ca_re = c_re[None] * pr[:, :, :, None, :] - c_im[None] * pi[:, :, :, None, :]
    ca_im = c_re[None] * pi[:, :, :, None, :] + c_im[None] * pr[:, :, :, None, :]
    k = (jnp.einsum("jdgqn,dgnp->dgpqj", ca_re[:L], bb_re, precision=hp)
         - jnp.einsum("jdgqn,dgnp->dgpqj", ca_im[:L], bb_im, precision=hp))
    k_f = k[0].reshape(G * P, P * L)
    k_b = k[1][..., ::-1].reshape(G * P, P * L)
    col_t = jnp.arange(P * L) % L
    t_rows = [jnp.where(col_t >= s, jnp.roll(k_f, s, axis=-1), 0.0)
              + jnp.where(col_t <= s, jnp.roll(k_b, s - (L - 1), axis=-1), 0.0) for s in range(L)]
    tmat = jnp.stack(t_rows, axis=1).reshape(G, P * L, P * L)
    skip = jnp.repeat(d_skip.astype(F32).reshape(G, P), L, axis=1).reshape(G, 1, L * P)
    tmat = tmat + jnp.eye(L * P, dtype=F32)[None] * skip

    def to_state(pw_re, pw_im, d):
        br = jnp.swapaxes(bb_re[d], 1, 2)[None]
        bi = jnp.swapaxes(bb_im[d], 1, 2)[None]
        re = pw_re[:, :, None, :] * br - pw_im[:, :, None, :] * bi
        im = pw_re[:, :, None, :] * bi + pw_im[:, :, None, :] * br
        return (re.transpose(1, 2, 0, 3).reshape(G, L * P, N),
                im.transpose(1, 2, 0, 3).reshape(G, L * P, N))

    f_re, f_im = to_state(pr[:L, 0][::-1], pi[:L, 0][::-1], 0)
    g_re, g_im = to_state(pr[:L, 1], pi[:L, 1], 1)
    bs = jnp.concatenate([f_re, g_re, f_im, g_im], axis=2)

    def from_state(d, flip):
        pw_re, pw_im = pr[1:, d], pi[1:, d]
        if flip:
            pw_re, pw_im = pw_re[::-1], pw_im[::-1]
        pw_re = jnp.tile(pw_re.transpose(1, 2, 0), (1, 1, P))
        pw_im = jnp.tile(pw_im.transpose(1, 2, 0), (1, 1, P))
        cr = jnp.repeat(jnp.swapaxes(c_re[d], 1, 2), L, axis=-1)
        ci = jnp.repeat(jnp.swapaxes(c_im[d], 1, 2), L, axis=-1)
        return cr * pw_re - ci * pw_im, cr * pw_im + ci * pw_re

    (cf_re, cf_im), (cb_re, cb_im) = from_state(0, False), from_state(1, True)
    cs = jnp.concatenate([cf_re, cb_re, -cf_im, -cb_im], axis=1)
    a_l = jnp.stack([jnp.concatenate([pr[L, 0], pr[L, 1]], axis=-1),
                     jnp.concatenate([pi[L, 0], pi[L, 1]], axis=-1)], axis=1)
    return tmat.astype(BF16), bs.astype(BF16), cs.astype(BF16), a_l


def _s5_chunk_kernel(n_ctx_chunks, n_chunks, u_ref, t_ref, bs_ref, cs_ref, al_ref, y_ref,
                     v_ref, xs_ref):
    n_ch, n_steps, n_rows = u_ref.shape
    nb = n_rows // n_chunks
    n = S5_STATE
    u = u_ref[...].reshape(n_ch * n_steps, n_rows).T.astype(BF16)
    v_ref[...] = _dot(u, bs_ref[0])
    fwd_lanes = lax.broadcasted_iota(jnp.int32, (1, 2 * n), 1) < n
    a_re = jnp.broadcast_to(al_ref[0, 0:1, :], (nb, 2 * n))
    a_im = jnp.broadcast_to(al_ref[0, 1:2, :], (nb, 2 * n))

    def step(i, carry):
        xr, xi = carry
        cb = jnp.where(i < n_ctx_chunks, n_ctx_chunks - 1 - i, n_chunks - 1 - (i - n_ctx_chunks))
        rf = pl.ds(pl.multiple_of(i * nb, nb), nb)
        rb = pl.ds(pl.multiple_of(cb * nb, nb), nb)
        xs_ref[rf, 0:n] = xr[:, 0:n]
        xs_ref[rb, n:2 * n] = xr[:, n:2 * n]
        xs_ref[rf, 2 * n:3 * n] = xi[:, 0:n]
        xs_ref[rb, 3 * n:4 * n] = xi[:, n:2 * n]
        vr = jnp.where(fwd_lanes, v_ref[rf, 0:2 * n], v_ref[rb, 0:2 * n])
        vi = jnp.where(fwd_lanes, v_ref[rf, 2 * n:4 * n], v_ref[rb, 2 * n:4 * n])
        return a_re * xr - a_im * xi + vr, a_re * xi + a_im * xr + vi

    zero = jnp.zeros((nb, 2 * n), F32)
    lax.fori_loop(0, n_chunks, step, (zero, zero))
    y = _dot(u, t_ref[0]) + _dot(xs_ref[...].astype(BF16), cs_ref[0])
    y_ref[...] = y.T.reshape(n_ch, n_steps, n_rows)


def _s5_chunked(u5c, tmat, bs, cs, a_l, ctx_len):
    n_chunks, B, L, W = u5c.shape
    P = S5_GROUP
    S = n_chunks * L
    assert L == S5_L and ctx_len % L == 0 and TILE % L == 0
    rows = n_chunks * B
    ut = u5c.reshape(rows, L, W).transpose(2, 1, 0)

    def per_group(arr):
        return pl.BlockSpec((1,) + arr.shape[1:], lambda g: (g,) + (0,) * (arr.ndim - 1))

    group_block = pl.BlockSpec((P, L, rows), lambda g: (g, 0, 0))
    yt = pl.pallas_call(
        functools.partial(_s5_chunk_kernel, ctx_len // L, n_chunks),
        grid=(W // P,),
        in_specs=[group_block, per_group(tmat), per_group(bs), per_group(cs), per_group(a_l)],
        out_specs=group_block,
        out_shape=jax.ShapeDtypeStruct((W, L, rows), F32),
        scratch_shapes=[pltpu.VMEM((rows, 4 * S5_STATE), F32),
                        pltpu.VMEM((rows, 4 * S5_STATE), F32)],
        compiler_params=_params("parallel"),
        name="s5_chunks",
    )(ut, tmat, bs, cs, a_l)
    return yt.transpose(2, 1, 0).reshape(n_chunks, B, L, W)


def _post_kernel(n_hidden_chunks, ctx_tiles_here, hc_ref, hx_ref, a_ref, bf_ref, bb_ref, bg_ref,
                 c_ref, d_ref, modx_ref, modc_ref, ng_ref, hgn_ref, wglu_ref, bglu_ref, wo_ref,
                 wi_ref, w2_ref, o_ref):
    nb, tile, d_model = hx_ref.shape
    n_rows = nb * tile
    is_ctx = pl.program_id(1) < ctx_tiles_here
    mod = _tile_mod(is_ctx, modx_ref, modc_ref)

    def flat(ref):
        return ref[...].reshape(n_rows, GROUP_W)

    tot = flat(bf_ref) + flat(bb_ref)
    ms = _split2_dot(tot * tot, _group_mean_matrix(GROUP_W, HG_HD))
    b_val = tot * lax.rsqrt(ms + EPS) * hgn_ref[...] * _silu(flat(bg_ref))
    y5 = jax.nn.gelu(jnp.concatenate(
        [d_ref[:, j, :, :].reshape(tile, GROUP_W) for j in range(nb)], axis=0))
    d_val = y5 * jax.nn.sigmoid(_dot(y5.astype(BF16), wglu_ref[...]) + bglu_ref[...])

    mix = (_dot(flat(a_ref), wo_ref[0:GROUP_W, :])
           + _dot(b_val.astype(BF16), wo_ref[GROUP_W:2 * GROUP_W, :])
           + _dot(flat(c_ref), wo_ref[2 * GROUP_W:3 * GROUP_W, :])
           + _dot(d_val.astype(BF16), wo_ref[3 * GROUP_W:4 * GROUP_W, :])).reshape(nb, tile, d_model)
    h = (jnp.where(is_ctx, hc_ref[...], hx_ref[...])
         + mod[:, 2:3, :] * (mix * _rms_scale(mix) * ng_ref[1:2, :]))
    y = (h * _rms_scale(h) * ng_ref[2:3, :]) * (1.0 + mod[:, 4:5, :]) + mod[:, 3:4, :]
    yb = y.reshape(n_rows, d_model).astype(BF16)
    hidden = w2_ref.shape[0]
    hc = hidden // n_hidden_chunks
    ffn = jnp.zeros((n_rows, d_model), F32)
    for j in range(n_hidden_chunks):
        gate = _dot(yb, wi_ref[:, j * hc:(j + 1) * hc])
        up = _dot(yb, wi_ref[:, hidden + j * hc:hidden + (j + 1) * hc])
        ffn = ffn + _dot((_silu(gate) * up).astype(BF16), w2_ref[j * hc:(j + 1) * hc, :])
    ffn = ffn.reshape(nb, tile, d_model)
    o_ref[...] = h + mod[:, 5:6, :] * (ffn * _rms_scale(ffn) * ng_ref[3:4, :])


def _post(h_parts, S, a, hgrn_dirs, rest, c, y5, mod, norm_g, hg_norm, w_glu, b_glu, w_out, w_ffn_in,
          w_ffn_out, n_batch, n_ctx_tiles, want_ctx):
    B, _, D = h_parts[1].shape
    off = 0 if want_ctx else n_ctx_tiles
    nt = S // TILE - off
    a_off = off if a.shape[1] == S else 0
    nb = POST_NB
    assert B % nb == 0

    def rows(o):
        return lambda bi, t: (bi, t + o, 0)

    def whole(arr):
        return pl.BlockSpec(arr.shape, lambda bi, t: (0,) * arr.ndim, pipeline_mode=pl.Buffered(1))

    mix_spec = lambda o: pl.BlockSpec((nb, TILE, GROUP_W), rows(o))
    tile_chunks = TILE // S5_L
    hgn = jnp.tile(hg_norm.astype(F32), HG_HEADS).reshape(1, GROUP_W)
    w_glu_b = w_glu.astype(BF16)
    b_glu_row = b_glu.astype(F32).reshape(1, GROUP_W)
    return pl.pallas_call(
        functools.partial(_post_kernel, 2, n_ctx_tiles - off),
        grid=(B // nb, nt),
        in_specs=_seq_specs(h_parts, (nb, TILE, D), n_ctx_tiles, first_tile=off) + [
            mix_spec(a_off), mix_spec(off), mix_spec(off),
            pl.BlockSpec((nb, TILE, GROUP_W), lambda bi, t: (bi, t + off, 4)),
            mix_spec(off),
            pl.BlockSpec((tile_chunks, nb, S5_L, GROUP_W), lambda bi, t: (t + off, bi, 0, 0)),
            pl.BlockSpec((nb, 6, D), lambda bi, t: (bi, 0, 0)),
            pl.BlockSpec((1, 6, D), lambda bi, t: (n_batch, 0, 0)),
            whole(norm_g), whole(hgn), whole(w_glu_b), whole(b_glu_row),
            whole(w_out), whole(w_ffn_in), whole(w_ffn_out),
        ],
        out_specs=pl.BlockSpec((nb, TILE, D), lambda bi, t: (bi, t, 0)),
        out_shape=jax.ShapeDtypeStruct((B, nt * TILE, D), F32),
        compiler_params=_params("parallel", "arbitrary"),
        name="post",
    )(h_parts[0], h_parts[1], a, hgrn_dirs[0], hgrn_dirs[1], rest, c, y5, mod, mod, norm_g, hgn,
      w_glu_b, b_glu_row, w_out, w_ffn_in, w_ffn_out)


def _rope_tables(n_rows, ctx_len):
    rows = jnp.broadcast_to(jnp.arange(n_rows, dtype=F32)[:, None], (n_rows, GRID_W)).reshape(-1)
    cols = jnp.broadcast_to(jnp.arange(GRID_W, dtype=F32)[None, :], (n_rows, GRID_W)).reshape(-1)
    n_freq = DA_HD // 4
    inv = ROPE_BASE ** (-jnp.arange(n_freq, dtype=F32) / n_freq)
    ang = jnp.concatenate([rows[:, None] * inv, cols[:, None] * inv], axis=-1)
    cos, sin = jnp.cos(ang), jnp.sin(ang)
    zero = jnp.zeros_like(sin)
    reps = GROUP_W // DA_HD

    def lanes(even, odd, ctx_value):
        t = jnp.tile(jnp.stack([even, odd], axis=-1).reshape(even.shape[0], DA_HD), (1, reps))
        return jnp.concatenate([jnp.full((ctx_len, GROUP_W), ctx_value, F32), t], axis=0)

    return lanes(cos, cos, 1.0), lanes(-sin, zero, 0.0), lanes(zero, sin, 0.0)


def kernel(x, c, ctx, c_ctx, w_ada, b_ada, norm_g, w_in, w_out, da_lambda, da_subln, hg_lb, hg_norm, sc_w, sc_b, s5_a_re, s5_a_im, s5_log_dt, s5_b_re, s5_b_im, s5_c_re, s5_c_im, s5_d, s5_w_glu, s5_b_glu, w_ffn_in, w_ffn_out):
    B, T, D = x.shape
    ctx_len = ctx.shape[1]
    L = w_ada.shape[0]
    assert ctx_len % TILE == 0 and T % TILE == 0 and T % GRID_W == 0
    n_ctx_tiles = ctx_len // TILE

    bp = -(-(B + 1) // SUBLANES) * SUBLANES
    cvec = jnp.concatenate([c, c_ctx[None, :], jnp.zeros((bp - B - 1, D), c.dtype)], axis=0)
    mods = _modulation(cvec.astype(F32), w_ada, b_ada).reshape(L, bp, 6, D)

    cos, sin_a, sin_b = _rope_tables(T // GRID_W, ctx_len)
    lb = jnp.cumsum(jax.nn.softmax(hg_lb.astype(F32), axis=0), axis=0)
    lb = lb - lb[:1]
    w_in_b = w_in.astype(BF16)
    w_out_b = w_out.astype(BF16)
    w_ffn_in_b = w_ffn_in.astype(BF16)
    w_ffn_out_b = w_ffn_out.astype(BF16)
    s5_mats = jax.vmap(_s5_chunk_matrices)(s5_a_re, s5_a_im, s5_log_dt, s5_b_re, s5_b_im, s5_c_re,
                                           s5_c_im, s5_d)

    S = ctx_len + T
    h_parts = (ctx, x, 0)
    for l in range(L):
        want_ctx = l < L - 1
        lam_init = 0.8 - 0.6 * math.exp(-0.3 * l)
        qkv, rest, u5 = _inproj(h_parts, S, mods[l], norm_g[l, 0], w_in_b[l], cos, sin_a, sin_b, B,
                                n_ctx_tiles)
        a = _attention(qkv, da_lambda[l], da_subln[l], lam_init, ctx_len, want_ctx)
        hgrn_dirs = _hgrn(rest, lb[l])
        cc = _short_conv(rest, sc_w[l], sc_b[l], ctx_len)
        tmat, bs, cs, a_l = (m[l] for m in s5_mats)
        y5 = _s5_chunked(u5, tmat, bs, cs, a_l, ctx_len)
        h = _post(h_parts, S, a, hgrn_dirs, rest, cc, y5, mods[l], norm_g[l].astype(F32), hg_norm[l],
                  s5_w_glu[l], s5_b_glu[l], w_out_b[l], w_ffn_in_b[l], w_ffn_out_b[l], B, n_ctx_tiles,
                  want_ctx)
        h_parts = (h, h, n_ctx_tiles)
    return h
```

```python
import functools
import math

import jax
import jax.numpy as jnp
from jax import lax
from jax.experimental import pallas as pl
from jax.experimental.pallas import tpu as pltpu

F32 = jnp.float32
BF16 = jnp.bfloat16

EPS = 1e-6
LOG2E = math.log2(math.e)
GRID_W = 64
ROPE_BASE = 10000.0
GROUP_W = 256
DA_HEADS = 4
DA_HD = 32
HG_HEADS = 4
HG_HD = 64
HG_CHUNK = 16
HG_BLOCK = 64
S5_NGROUPS = 16
S5_GROUP = 16
S5_STATE = 64
SUBLANES = 8
TILE = 256
HGRN_NB = 2
INPROJ_NB = 4
POST_NB = 2
S5_L = 16
VMEM_LIMIT = 56 * 1024 * 1024

_NT = (((1,), (1,)), ((), ()))


def _dot(a, b):
    return jnp.dot(a, b, preferred_element_type=F32)


def _dot_nt(a, b):
    return lax.dot_general(a, b, _NT, preferred_element_type=F32)


def _split2_dot_rhs(m, x):
    x1 = x.astype(BF16)
    x2 = (x - x1.astype(F32)).astype(BF16)
    return _dot(m, x1) + _dot(m, x2)


def _split2_dot(x, m):
    x1 = x.astype(BF16)
    x2 = (x - x1.astype(F32)).astype(BF16)
    return _dot(x1, m) + _dot(x2, m)


def _silu(x):
    return x * jax.nn.sigmoid(x)


def _rms_scale(x):
    return lax.rsqrt(jnp.mean(x * x, axis=-1, keepdims=True) + EPS)


def _group_mean_matrix(n, group):
    sh = int(math.log2(group))
    r = lax.broadcasted_iota(jnp.int32, (n, n), 0) >> sh
    c = lax.broadcasted_iota(jnp.int32, (n, n), 1) >> sh
    return jnp.where(r == c, 1.0 / group, 0.0).astype(BF16)


def _params(*sem):
    return pltpu.CompilerParams(dimension_semantics=sem, vmem_limit_bytes=VMEM_LIMIT)


def _mod_kernel(c_ref, w_ref, b_ref, o_ref):
    sc = _silu(c_ref[...])
    o_ref[0] = _dot(sc.astype(BF16), w_ref[0].astype(BF16)) + b_ref[0]


def _modulation(cvec, w_ada, b_ada):
    L, D, N = w_ada.shape
    Bp = cvec.shape[0]
    tn = 1536
    return pl.pallas_call(
        _mod_kernel,
        grid=(L, N // tn),
        in_specs=[
            pl.BlockSpec((Bp, D), lambda l, j: (0, 0)),
            pl.BlockSpec((1, D, tn), lambda l, j: (l, 0, j)),
            pl.BlockSpec((1, 1, tn), lambda l, j: (l, 0, j)),
        ],
        out_specs=pl.BlockSpec((1, Bp, tn), lambda l, j: (l, 0, j)),
        out_shape=jax.ShapeDtypeStruct((L, Bp, N), F32),
        compiler_params=_params("arbitrary", "arbitrary"),
        name="modulation",
    )(cvec, w_ada, b_ada.reshape(L, 1, N))


def _rope(x, cos, sin_a, sin_b):
    return x * cos + pltpu.roll(x, GROUP_W - 1, 1) * sin_a + pltpu.roll(x, 1, 1) * sin_b


def _tile_mod(is_ctx, modx_ref, modc_ref):
    return jnp.where(is_ctx, modc_ref[...], modx_ref[...])


def _seq_specs(h_parts, block, n_ctx_tiles, first_tile=0):
    _, _, lat_off = h_parts

    def ctx_idx(b, t):
        return (b, jnp.minimum(t + first_tile, n_ctx_tiles - 1), 0)

    def lat_idx(b, t):
        return (b, jnp.maximum(t + first_tile, n_ctx_tiles) - n_ctx_tiles + lat_off, 0)

    return [pl.BlockSpec(block, ctx_idx), pl.BlockSpec(block, lat_idx)]


def _inproj_kernel(n_ctx_tiles, hc_ref, hx_ref, modx_ref, modc_ref, g_ref, w_ref, cos_ref, sa_ref,
                   sb_ref, qkv_ref, rest_ref, u5_ref):
    nb, tile, d_model = hx_ref.shape
    is_ctx = pl.program_id(1) < n_ctx_tiles
    mod = _tile_mod(is_ctx, modx_ref, modc_ref)
    h = jnp.where(is_ctx, hc_ref[...], hx_ref[...])
    hn = h * _rms_scale(h) * g_ref[...]
    y = hn * (1.0 + mod[:, 1:2, :]) + mod[:, 0:1, :]
    p = _dot(y.reshape(nb * tile, d_model).astype(BF16), w_ref[...])
    cos, sa, sb = cos_ref[...], sa_ref[...], sb_ref[...]
    for j in range(nb):
        pj = p[j * tile:(j + 1) * tile]
        q = _rope(pj[:, 0:GROUP_W], cos, sa, sb) * (DA_HD ** -0.5 * math.log2(math.e))
        k = _rope(pj[:, GROUP_W:2 * GROUP_W], cos, sa, sb)
        qkv_ref[j, :, 0:GROUP_W] = q.astype(BF16)
        qkv_ref[j, :, GROUP_W:2 * GROUP_W] = k.astype(BF16)
        qkv_ref[j, :, 2 * GROUP_W:3 * GROUP_W] = pj[:, 2 * GROUP_W:3 * GROUP_W].astype(BF16)
        rest_ref[j] = pj[:, 3 * GROUP_W:11 * GROUP_W]
        u5_ref[:, j, :, :] = pj[:, 11 * GROUP_W:].reshape(tile // S5_L, S5_L, GROUP_W)


def _inproj(h_parts, S, mod, g, w, cos, sa, sb, n_batch, n_ctx_tiles):
    B, _, D = h_parts[1].shape
    N = w.shape[1]
    nt = S // TILE
    nb = INPROJ_NB
    assert B % nb == 0
    tab = pl.BlockSpec((TILE, GROUP_W), lambda b, t: (t, 0))
    return pl.pallas_call(
        functools.partial(_inproj_kernel, n_ctx_tiles),
        grid=(B // nb, nt),
        in_specs=_seq_specs(h_parts, (nb, TILE, D), n_ctx_tiles) + [
            pl.BlockSpec((nb, 6, D), lambda b, t: (b, 0, 0)),
            pl.BlockSpec((1, 6, D), lambda b, t: (n_batch, 0, 0)),
            pl.BlockSpec((1, D), lambda b, t: (0, 0)),
            pl.BlockSpec((D, N), lambda b, t: (0, 0)),
            tab, tab, tab,
        ],
        out_specs=[
            pl.BlockSpec((nb, TILE, 3 * GROUP_W), lambda b, t: (b, t, 0)),
            pl.BlockSpec((nb, TILE, 8 * GROUP_W), lambda b, t: (b, t, 0)),
            pl.BlockSpec((TILE // S5_L, nb, S5_L, GROUP_W), lambda b, t: (t, b, 0, 0)),
        ],
        out_shape=[
            jax.ShapeDtypeStruct((B, S, 3 * GROUP_W), BF16),
            jax.ShapeDtypeStruct((B, S, 8 * GROUP_W), F32),
            jax.ShapeDtypeStruct((S // S5_L, B, S5_L, GROUP_W), F32),
        ],
        compiler_params=_params("parallel", "arbitrary"),
        name="inproj",
    )(h_parts[0], h_parts[1], mod, mod, g.reshape(1, D), w, cos, sa, sb)


def _attn_kernel(lam_init, ctx_len, q_off, q_ref, k_ref, v_ref, lamv_ref, g_ref, o_ref,
                 sc_ref, e_ref, va_ref, vb_ref):
    tq = q_ref.shape[1]
    n_keys = k_ref.shape[1]
    lv = lamv_ref[...]
    lam = (jnp.exp(jnp.sum(lv[0:1] * lv[1:2], axis=-1, keepdims=True))
           - jnp.exp(jnp.sum(lv[2:3] * lv[3:4], axis=-1, keepdims=True)) + lam_init)
    lane = lax.broadcasted_iota(jnp.int32, (1, GROUP_W), 1)
    lane_map = lane >> 5
    lane_head = lane >> 6

    @pl.when(pl.program_id(1) == 0)
    def _():
        v = v_ref[0]
        one = jnp.ones_like(v)
        va_ref[...] = jnp.where(lane == GROUP_W - 1, one, v)
        vb_ref[...] = jnp.where(lane == 0, one, v)

    q = q_ref[0]
    q8 = jnp.concatenate(
        [jnp.where(lane_map == j, q, jnp.zeros_like(q)) for j in range(2 * DA_HEADS)], axis=0)

    def attend(nk):
        sc_ref[:, 0:nk] = _dot_nt(q8, k_ref[0, 0:nk, :])
        acc = jnp.zeros((tq, GROUP_W), F32)
        for h in range(DA_HEADS):
            for m in range(2):
                rows = slice((2 * h + m) * tq, (2 * h + m + 1) * tq)
                s = sc_ref[rows, 0:nk]
                e_ref[rows, 0:nk] = jnp.exp2(s - jnp.max(s, axis=-1, keepdims=True)).astype(BF16)
            vv = va_ref if h < DA_HEADS - 1 else vb_ref
            sum_col = GROUP_W - 1 if h < DA_HEADS - 1 else 0
            o2 = _dot(e_ref[2 * h * tq:(2 * h + 2) * tq, 0:nk], vv[0:nk, :])
            o0, o1 = o2[0:tq], o2[tq:2 * tq]
            w0 = 1.0 / o0[:, sum_col:sum_col + 1]
            w1 = lam / o1[:, sum_col:sum_col + 1]
            acc = acc + jnp.where(lane_head == h, o0 * w0 - o1 * w1, 0.0)
        ms = _split2_dot(acc * acc, _group_mean_matrix(GROUP_W, 2 * DA_HD))
        o_ref[0] = (acc * lax.rsqrt(ms + EPS) * g_ref[...] * (1.0 - lam_init)).astype(o_ref.dtype)

    if q_off == 0 and ctx_len == tq:
        @pl.when(pl.program_id(1) == 0)
        def _():
            attend(ctx_len)

        @pl.when(pl.program_id(1) != 0)
        def _():
            attend(n_keys)
    else:
        attend(n_keys)


def _attention(qkv, lam_vecs, subln_g, lam_init, ctx_len, want_ctx):
    B, S, _ = qkv.shape
    tq = TILE
    q_off = 0 if want_ctx else ctx_len // tq
    nq = S // tq - q_off
    g = jnp.tile(subln_g.astype(F32), DA_HEADS).reshape(1, GROUP_W)
    return pl.pallas_call(
        functools.partial(_attn_kernel, lam_init, ctx_len, q_off),
        grid=(B, nq),
        in_specs=[
            pl.BlockSpec((1, tq, GROUP_W), lambda b, i: (b, i + q_off, 0)),
            pl.BlockSpec((1, S, GROUP_W), lambda b, i: (b, 0, 1)),
            pl.BlockSpec((1, S, GROUP_W), lambda b, i: (b, 0, 2)),
            pl.BlockSpec((4, DA_HD), lambda b, i: (0, 0)),
            pl.BlockSpec((1, GROUP_W), lambda b, i: (0, 0)),
        ],
        out_specs=pl.BlockSpec((1, tq, GROUP_W), lambda b, i: (b, i, 0)),
        out_shape=jax.ShapeDtypeStruct((B, nq * tq, GROUP_W), BF16),
        scratch_shapes=[
            pltpu.VMEM((2 * DA_HEADS * tq, S), F32),
            pltpu.VMEM((2 * DA_HEADS * tq, S), BF16),
            pltpu.VMEM((S, GROUP_W), BF16),
            pltpu.VMEM((S, GROUP_W), BF16),
        ],
        compiler_params=_params("parallel", "arbitrary"),
        name="diff_attention",
    )(qkv, qkv, qkv, lam_vecs.astype(F32), g)


def _hgrn_constants():
    row = lax.broadcasted_iota(jnp.int32, (TILE, TILE), 0)
    col = lax.broadcasted_iota(jnp.int32, (TILE, TILE), 1)
    same_chunk = (row // HG_CHUNK) == (col // HG_CHUNK)
    same_block = (row // HG_BLOCK) == (col // HG_BLOCK)

    def ones(mask):
        return jnp.where(mask, 1.0, 0.0).astype(BF16)

    row_mats, col_mats, allowed = [], [], []
    for d in range(2):
        dist = (row // HG_CHUNK - col // HG_CHUNK) * (1 if d == 0 else -1)
        causal = same_chunk & ((col <= row) if d == 0 else (col >= row))
        row_mats.append(jnp.concatenate([ones(causal), ones(same_block & (dist >= 1))], axis=0))
        col_mats.append(jnp.concatenate(
            [ones(same_chunk & ((row <= col) if d == 0 else (row >= col))), ones(same_chunk)], axis=1))
        allowed.append(ones(same_block & ((dist >= 1) | causal)))
    return jnp.stack(row_mats), jnp.stack(col_mats), jnp.stack(allowed)


def _hgrn_kernel(qf_ref, ff_ref, if_ref, qb_ref, fb_ref, ib_ref, lb_ref, rowmat_ref, colmat_ref,
                 allow_ref, of_ref, ob_ref, st2_ref, u2_ref, oc2_ref):
    nb = qf_ref.shape[0]
    n_blocks = TILE // HG_BLOCK
    chunks_per_block = HG_BLOCK // HG_CHUNK
    lane = lax.broadcasted_iota(jnp.int32, (1, GROUP_W), 1)
    lane_head = lane >> int(math.log2(HG_HD))
    lane_block = lane >> int(math.log2(HG_BLOCK))
    lane_pos = (lane >> int(math.log2(HG_CHUNK))) & (chunks_per_block - 1)

    @pl.when(pl.program_id(1) == 0)
    def _():
        st2_ref[...] = jnp.zeros_like(st2_ref)

    def stack_heads(x):
        zero = jnp.zeros_like(x)
        return jnp.concatenate(
            [jnp.where(lane_head == h, x, zero).astype(BF16) for h in range(HG_HEADS)], axis=0)

    def chain(j, d):
        allowed = allow_ref[d] != 0
        row_mat, col_mat = rowmat_ref[d], colmat_ref[d]
        q_ref, f_ref, i_ref, o_ref = ((qf_ref, ff_ref, if_ref, of_ref) if d == 0
                                      else (qb_ref, fb_ref, ib_ref, ob_ref))
        lb = lb_ref[d:d + 1, :]
        st_ref, u_ref, oc_ref = st2_ref.at[j, d], u2_ref.at[j, d], oc2_ref.at[j, d]
        f = lb + (1.0 - lb) * jax.nn.sigmoid(f_ref[j])
        lf = jnp.log(f) * LOG2E
        yield
        sums = _split2_dot_rhs(row_mat, lf)
        kk_t = 1.0 - f.T
        sums_t = _split2_dot(lf.T, col_mat)
        yield
        qd = _silu(q_ref[j]) * jnp.exp2(sums[0:TILE])
        q_state = qd * jnp.exp2(sums[TILE:2 * TILE])
        g_t, gt_t = sums_t[:, 0:TILE], sums_t[:, TILE:2 * TILE]
        kend_t = kk_t * jnp.exp2(gt_t - g_t)
        k_var = [kk_t * jnp.exp2(-g_t), kend_t]
        later = jnp.zeros_like(gt_t)
        for m in range(1, chunks_per_block):
            if d == 0:
                shifted = pltpu.roll(gt_t, TILE - m * HG_CHUNK, 1)
                inside = lane_pos <= chunks_per_block - 1 - m
            else:
                shifted = pltpu.roll(gt_t, m * HG_CHUNK, 1)
                inside = lane_pos >= m
            later = later + jnp.where(inside, shifted, 0.0)
            if m + 1 < chunks_per_block:
                k_var.append(kend_t * jnp.exp2(later))
        kblk_t = kend_t * jnp.exp2(later)
        dec_t = jnp.exp2(gt_t + later)
        vb = i_ref[j].astype(BF16)
        yield

        zero = jnp.zeros_like(kblk_t)
        k_stack = jnp.concatenate(
            [jnp.where(lane_block == b, kblk_t, zero).astype(BF16) for b in range(n_blocks)], axis=0)
        u_ref[...] = _dot(k_stack, vb)
        yield

        qd4 = stack_heads(qd)
        x_pos = []
        for p in range(chunks_per_block):
            keys = jnp.zeros_like(kend_t)
            for n in range(chunks_per_block):
                pc = p - n if d == 0 else p + n
                if 0 <= pc < chunks_per_block:
                    keys = jnp.where(lane_pos == pc, k_var[n], keys)
            q_p = jnp.concatenate(
                [qd4[h * TILE + b * HG_BLOCK + p * HG_CHUNK:h * TILE + b * HG_BLOCK + (p + 1) * HG_CHUNK]
                 for h in range(HG_HEADS) for b in range(n_blocks)], axis=0)
            x_pos.append(_dot(q_p, keys.astype(BF16)))
            yield
        a_heads = []
        for h in range(HG_HEADS):
            a = jnp.concatenate(
                [x_pos[p][(h * n_blocks + b) * HG_CHUNK:(h * n_blocks + b + 1) * HG_CHUNK]
                 for b in range(n_blocks) for p in range(chunks_per_block)], axis=0)
            a_heads.append(jnp.where(allowed, a, 0.0).astype(BF16))
        zb = jnp.zeros_like(vb)
        v4 = jnp.concatenate(
            [jnp.where(lane_head == h, vb, zb) for h in range(HG_HEADS)], axis=0)
        o_tile = _dot(jnp.concatenate(a_heads, axis=1), v4)
        yield

        qs4 = stack_heads(q_state)
        order = range(n_blocks) if d == 0 else range(n_blocks - 1, -1, -1)
        for b in order:
            st = st_ref[...]
            q4b = jnp.concatenate(
                [qs4[h * TILE + b * HG_BLOCK:h * TILE + (b + 1) * HG_BLOCK] for h in range(HG_HEADS)],
                axis=0)
            ob4 = _dot(q4b, st.astype(BF16))
            ob = jnp.zeros((HG_BLOCK, GROUP_W), F32)
            for h in range(HG_HEADS):
                ob = ob + jnp.where(lane_head == h, ob4[h * HG_BLOCK:(h + 1) * HG_BLOCK], 0.0)
            oc_ref[b * HG_BLOCK:(b + 1) * HG_BLOCK, :] = ob
            first = b * HG_BLOCK + (0 if d == 0 else HG_BLOCK - HG_CHUNK)
            st_ref[...] = st * dec_t[:, first:first + 1] + u_ref[b * TILE:(b + 1) * TILE, :]
            yield
        o_ref[j] = o_tile + oc_ref[...]

    done = object()
    pending = [chain(j, d) for j in range(nb) for d in range(2)]
    while pending:
        pending = [c for c in pending if next(c, done) is not done]


def _hgrn(rest, lb):
    B, S, _ = rest.shape
    n_tiles = S // TILE
    nb = HGRN_NB
    assert B % nb == 0

    def back(i):
        return jnp.where(i == 0, 0, n_tiles - i)

    def fwd(part):
        return pl.BlockSpec((nb, TILE, GROUP_W), lambda b, i: (b, i, part))

    def bwd(part):
        return pl.BlockSpec((nb, TILE, GROUP_W), lambda b, i: (b, back(i), part))

    def whole(arr):
        return pl.BlockSpec(arr.shape, lambda b, i: (0,) * arr.ndim)

    consts = _hgrn_constants()
    dir_sds = jax.ShapeDtypeStruct((B, S, GROUP_W), F32)
    return pl.pallas_call(
        _hgrn_kernel,
        grid=(B // nb, n_tiles),
        in_specs=[fwd(0), fwd(1), fwd(3), bwd(0), bwd(2), bwd(3), whole(lb)]
        + [whole(c) for c in consts],
        out_specs=[pl.BlockSpec((nb, TILE, GROUP_W), lambda b, i: (b, i, 0)),
                   pl.BlockSpec((nb, TILE, GROUP_W), lambda b, i: (b, back(i), 0))],
        out_shape=[dir_sds, dir_sds],
        scratch_shapes=[
            pltpu.VMEM((nb, 2, GROUP_W, GROUP_W), F32),
            pltpu.VMEM((nb, 2, TILE // HG_BLOCK * TILE, GROUP_W), F32),
            pltpu.VMEM((nb, 2, TILE, GROUP_W), F32),
        ],
        compiler_params=_params("parallel", "arbitrary"),
        name="hgrn2",
    )(rest, rest, rest, rest, rest, rest, lb, *consts)


def _s5_discretise(a_re, a_im, log_dt, b_re, b_im):
    dt = jnp.exp(log_dt)[..., None]
    mag = jnp.exp(dt * a_re)
    ab_re, ab_im = mag * jnp.cos(dt * a_im), mag * jnp.sin(dt * a_im)
    den = a_re * a_re + a_im * a_im
    nr = ab_re - 1.0
    f_re = (nr * a_re + ab_im * a_im) / den
    f_im = (ab_im * a_re - nr * a_im) / den
    bb_re = f_re[..., None] * b_re - f_im[..., None] * b_im
    bb_im = f_re[..., None] * b_im + f_im[..., None] * b_re
    return ab_re, ab_im, bb_re, bb_im


def _s5_chunk_matrices(a_re, a_im, log_dt, b_re, b_im, c_re, c_im, d_skip):
    L, G, P, N = S5_L, S5_NGROUPS, S5_GROUP, S5_STATE
    hp = lax.Precision.HIGHEST
    ab_re, ab_im, bb_re, bb_im = _s5_discretise(
        a_re.astype(F32), a_im.astype(F32), log_dt.astype(F32), b_re.astype(F32), b_im.astype(F32))
    c_re, c_im = c_re.astype(F32), c_im.astype(F32)
    j_dt = (jnp.arange(L + 1, dtype=F32).reshape(L + 1, 1, 1, 1)
            * jnp.exp(log_dt.astype(F32))[None, :, :, None])
    mag = jnp.exp(j_dt * a_re.astype(F32)[None])
    pr, pi = mag * jnp.cos(j_dt * a_im.astype(F32)[None]), mag * jnp.sin(j_dt * a_im.astype(F32)[None])
    ca_re = c_re[None] * pr[:, :, :, None, :] - c_im[None] * pi[:, :, :, None, :]
    ca_im = c_re[None] * pi[:, :, :, None, :] + c_im[None] * pr[:, :, :, None, :]
    k = (jnp.einsum("jdgqn,dgnp->dgpqj", ca_re[:L], bb_re, precision=hp)
         - jnp.einsum("jdgqn,dgnp->dgpqj", ca_im[:L], bb_im, precision=hp))
    k_f = k[0].reshape(G * P, P * L)
    k_b = k[1][..., ::-1].reshape(G * P, P * L)
    col_t = jnp.arange(P * L) % L
    t_rows = [jnp.where(col_t >= s, jnp.roll(k_f, s, axis=-1), 0.0)
              + jnp.where(col_t <= s, jnp.roll(k_b, s - (L - 1), axis=-1), 0.0) for s in range(L)]
    tmat = jnp.stack(t_rows, axis=1).reshape(G, P * L, P * L)
    skip = jnp.repeat(d_skip.astype(F32).reshape(G, P), L, axis=1).reshape(G, 1, L * P)
    tmat = tmat + jnp.eye(L * P, dtype=F32)[None] * skip

    def to_state(pw_re, pw_im, d):
        br = jnp.swapaxes(bb_re[d], 1, 2)[None]
        bi = jnp.swapaxes(bb_im[d], 1, 2)[None]
        re = pw_re[:, :, None, :] * br - pw_im[:, :, None, :] * bi
        im = pw_re[:, :, None, :] * bi + pw_im[:, :, None, :] * br
        return (re.transpose(1, 2, 0, 3).reshape(G, L * P, N),
                im.transpose(1, 2, 0, 3).reshape(G, L * P, N))

    f_re, f_im = to_state(pr[:L, 0][::-1], pi[:L, 0][::-1], 0)
    g_re, g_im = to_state(pr[:L, 1], pi[:L, 1], 1)
    bs = jnp.concatenate([f_re, g_re, f_im, g_im], axis=2)

    def from_state(d, flip):
        pw_re, pw_im = pr[1:, d], pi[1:, d]
        if flip:
            pw_re, pw_im = pw_re[::-1], pw_im[::-1]
        pw_re = jnp.tile(pw_re.transpose(1, 2, 0), (1, 1, P))
        pw_im = jnp.tile(pw_im.transpose(1, 2, 0), (1, 1, P))
        cr = jnp.repeat(jnp.swapaxes(c_re[d], 1, 2), L, axis=-1)
        ci = jnp.repeat(jnp.swapaxes(c_im[d], 1, 2), L, axis=-1)
        return cr * pw_re - ci * pw_im, cr * pw_im + ci * pw_re

    (cf_re, cf_im), (cb_re, cb_im) = from_state(0, False), from_state(1, True)
    cs = jnp.concatenate([cf_re, cb_re, -cf_im, -cb_im], axis=1)
    a_l = jnp.stack([jnp.concatenate([pr[L, 0], pr[L, 1]], axis=-1),
                     jnp.concatenate([pi[L, 0], pi[L, 1]], axis=-1)], axis=1)
    return tmat.astype(BF16), bs.astype(BF16), cs.astype(BF16), a_l


def _s5_chunk_kernel(n_ctx_chunks, n_chunks, u_ref, t_ref, bs_ref, cs_ref, al_ref, y_ref,
                     v_ref, xs_ref):
    n_ch, n_steps, n_rows = u_ref.shape
    nb = n_rows // n_chunks
    n = S5_STATE
    u = u_ref[...].reshape(n_ch * n_steps, n_rows).T.astype(BF16)
    v_ref[...] = _dot(u, bs_ref[0])
    fwd_lanes = lax.broadcasted_iota(jnp.int32, (1, 2 * n), 1) < n
    a_re = jnp.broadcast_to(al_ref[0, 0:1, :], (nb, 2 * n))
    a_im = jnp.broadcast_to(al_ref[0, 1:2, :], (nb, 2 * n))

    def step(i, carry):
        xr, xi = carry
        cb = jnp.where(i < n_ctx_chunks, n_ctx_chunks - 1 - i, n_chunks - 1 - (i - n_ctx_chunks))
        rf = pl.ds(pl.multiple_of(i * nb, nb), nb)
        rb = pl.ds(pl.multiple_of(cb * nb, nb), nb)
        xs_ref[rf, 0:n] = xr[:, 0:n]
        xs_ref[rb, n:2 * n] = xr[:, n:2 * n]
        xs_ref[rf, 2 * n:3 * n] = xi[:, 0:n]
        xs_ref[rb, 3 * n:4 * n] = xi[:, n:2 * n]
        vr = jnp.where(fwd_lanes, v_ref[rf, 0:2 * n], v_ref[rb, 0:2 * n])
        vi = jnp.where(fwd_lanes, v_ref[rf, 2 * n:4 * n], v_ref[rb, 2 * n:4 * n])
        return a_re * xr - a_im * xi + vr, a_re * xi + a_im * xr + vi

    zero = jnp.zeros((nb, 2 * n), F32)
    lax.fori_loop(0, n_chunks, step, (zero, zero))
    y = _dot(u, t_ref[0]) + _dot(xs_ref[...].astype(BF16), cs_ref[0])
    y_ref[...] = y.T.reshape(n_ch, n_steps, n_rows)


def _s5_chunked(u5c, tmat, bs, cs, a_l, ctx_len):
    n_chunks, B, L, W = u5c.shape
    P = S5_GROUP
    S = n_chunks * L
    assert L == S5_L and ctx_len % L == 0 and TILE % L == 0
    rows = n_chunks * B
    ut = u5c.reshape(rows, L, W).transpose(2, 1, 0)

    def per_group(arr):
        return pl.BlockSpec((1,) + arr.shape[1:], lambda g: (g,) + (0,) * (arr.ndim - 1))

    group_block = pl.BlockSpec((P, L, rows), lambda g: (g, 0, 0))
    yt = pl.pallas_call(
        functools.partial(_s5_chunk_kernel, ctx_len // L, n_chunks),
        grid=(W // P,),
        in_specs=[group_block, per_group(tmat), per_group(bs), per_group(cs), per_group(a_l)],
        out_specs=group_block,
        out_shape=jax.ShapeDtypeStruct((W, L, rows), F32),
        scratch_shapes=[pltpu.VMEM((rows, 4 * S5_STATE), F32),
                        pltpu.VMEM((rows, 4 * S5_STATE), F32)],
        compiler_params=_params("parallel"),
        name="s5_chunks",
    )(ut, tmat, bs, cs, a_l)
    return yt.transpose(2, 1, 0).reshape(n_chunks, B, L, W)


def _post_kernel(n_hidden_chunks, ctx_tiles_here, tiles_here, hc_ref, hx_ref, a_ref, bf_ref, bb_ref,
                 bg_ref, cb_ref, cc_ref, cu_ref, ccp_ref, cup_ref, ccn_ref, cun_ref, d_ref, modx_ref,
                 modc_ref, ng_ref, hgn_ref, scw_ref, scb_ref, wglu_ref, bglu_ref, wo_ref, wi_ref,
                 w2_ref, o_ref):
    nb, tile, d_model = hx_ref.shape
    n_rows = nb * tile
    t = pl.program_id(1)
    is_ctx = t < ctx_tiles_here
    mod = _tile_mod(is_ctx, modx_ref, modc_ref)

    def flat(ref):
        return ref[...].reshape(n_rows, GROUP_W)

    tot = flat(bf_ref) + flat(bb_ref)
    ms = _split2_dot(tot * tot, _group_mean_matrix(GROUP_W, HG_HD))
    b_val = tot * lax.rsqrt(ms + EPS) * hgn_ref[...] * _silu(flat(bg_ref))
    seg_start = (t == 0) | (t == ctx_tiles_here)
    seg_end = (t == ctx_tiles_here - 1) | (t == tiles_here - 1)
    row = lax.broadcasted_iota(jnp.int32, (tile, 1), 0)
    c_parts = []
    for j in range(nb):
        v = cc_ref[j] * cu_ref[j]
        v_before = jnp.where(seg_start, 0.0, ccp_ref[j, SUBLANES - 1:SUBLANES, :] * cup_ref[j, SUBLANES - 1:SUBLANES, :])
        v_after = jnp.where(seg_end, 0.0, ccn_ref[j, 0:1, :] * cun_ref[j, 0:1, :])
        prev = jnp.where(row == 0, v_before, pltpu.roll(v, 1, 0))
        nxt = jnp.where(row == tile - 1, v_after, pltpu.roll(v, tile - 1, 0))
        y = scw_ref[0:1, :] * prev + scw_ref[1:2, :] * v + scw_ref[2:3, :] * nxt
        c_parts.append(cb_ref[j] * (y + scb_ref[...]))
    c_val = jnp.concatenate(c_parts, axis=0)
    y5 = jax.nn.gelu(jnp.concatenate(
        [d_ref[:, j, :, :].reshape(tile, GROUP_W) for j in range(nb)], axis=0))
    d_val = y5 * jax.nn.sigmoid(_dot(y5.astype(BF16), wglu_ref[...]) + bglu_ref[...])

    mix = (_dot(flat(a_ref), wo_ref[0:GROUP_W, :])
           + _dot(b_val.astype(BF16), wo_ref[GROUP_W:2 * GROUP_W, :])
           + _dot(c_val.astype(BF16), wo_ref[2 * GROUP_W:3 * GROUP_W, :])
           + _dot(d_val.astype(BF16), wo_ref[3 * GROUP_W:4 * GROUP_W, :])).reshape(nb, tile, d_model)
    h = (jnp.where(is_ctx, hc_ref[...], hx_ref[...])
         + mod[:, 2:3, :] * (mix * _rms_scale(mix) * ng_ref[1:2, :]))
    y = (h * _rms_scale(h) * ng_ref[2:3, :]) * (1.0 + mod[:, 4:5, :]) + mod[:, 3:4, :]
    yb = y.reshape(n_rows, d_model).astype(BF16)
    hidden = w2_ref.shape[0]
    hc = hidden // n_hidden_chunks
    ffn = jnp.zeros((n_rows, d_model), F32)
    for j in range(n_hidden_chunks):
        gate = _dot(yb, wi_ref[:, j * hc:(j + 1) * hc])
        up = _dot(yb, wi_ref[:, hidden + j * hc:hidden + (j + 1) * hc])
        ffn = ffn + _dot((_silu(gate) * up).astype(BF16), w2_ref[j * hc:(j + 1) * hc, :])
    ffn = ffn.reshape(nb, tile, d_model)
    o_ref[...] = h + mod[:, 5:6, :] * (ffn * _rms_scale(ffn) * ng_ref[3:4, :])


def _post(h_parts, S, a, hgrn_dirs, rest, y5, mod, norm_g, hg_norm, sc_w, sc_b, w_glu, b_glu, w_out,
          w_ffn_in, w_ffn_out, n_batch, n_ctx_tiles, want_ctx):
    B, _, D = h_parts[1].shape
    off = 0 if want_ctx else n_ctx_tiles
    nt = S // TILE - off
    a_off = off if a.shape[1] == S else 0
    nb = POST_NB
    assert B % nb == 0

    def rows(o):
        return lambda bi, t: (bi, t + o, 0)

    def whole(arr):
        return pl.BlockSpec(arr.shape, lambda bi, t: (0,) * arr.ndim, pipeline_mode=pl.Buffered(1))

    mix_spec = lambda o: pl.BlockSpec((nb, TILE, GROUP_W), rows(o))

    def part(j):
        return pl.BlockSpec((nb, TILE, GROUP_W), lambda bi, t: (bi, t + off, j))

    halo_per_tile = TILE // SUBLANES

    def halo_before(j):
        return pl.BlockSpec((nb, SUBLANES, GROUP_W),
                            lambda bi, t: (bi, jnp.maximum((t + off) * halo_per_tile - 1, 0), j))

    def halo_after(j):
        return pl.BlockSpec((nb, SUBLANES, GROUP_W),
                            lambda bi, t: (bi, jnp.minimum((t + off + 1) * halo_per_tile,
                                                           S // SUBLANES - 1), j))

    tile_chunks = TILE // S5_L
    sc_w_f = sc_w.astype(F32)
    sc_b_row = sc_b.astype(F32).reshape(1, GROUP_W)
    hgn = jnp.tile(hg_norm.astype(F32), HG_HEADS).reshape(1, GROUP_W)
    w_glu_b = w_glu.astype(BF16)
    b_glu_row = b_glu.astype(F32).reshape(1, GROUP_W)
    return pl.pallas_call(
        functools.partial(_post_kernel, 2, n_ctx_tiles - off, nt),
        grid=(B // nb, nt),
        in_specs=_seq_specs(h_parts, (nb, TILE, D), n_ctx_tiles, first_tile=off) + [
            mix_spec(a_off), mix_spec(off), mix_spec(off),
            part(4),
            part(5), part(6), part(7),
            halo_before(6), halo_before(7), halo_after(6), halo_after(7),
            pl.BlockSpec((tile_chunks, nb, S5_L, GROUP_W), lambda bi, t: (t + off, bi, 0, 0)),
            pl.BlockSpec((nb, 6, D), lambda bi, t: (bi, 0, 0)),
            pl.BlockSpec((1, 6, D), lambda bi, t: (n_batch, 0, 0)),
            whole(norm_g), whole(hgn), whole(sc_w_f), whole(sc_b_row), whole(w_glu_b), whole(b_glu_row),
            whole(w_out), whole(w_ffn_in), whole(w_ffn_out),
        ],
        out_specs=pl.BlockSpec((nb, TILE, D), lambda bi, t: (bi, t, 0)),
        out_shape=jax.ShapeDtypeStruct((B, nt * TILE, D), F32),
        compiler_params=_params("parallel", "arbitrary"),
        name="post",
    )(h_parts[0], h_parts[1], a, hgrn_dirs[0], hgrn_dirs[1], rest, rest, rest, rest, rest, rest, rest,
      rest, y5, mod, mod, norm_g, hgn, sc_w_f, sc_b_row, w_glu_b, b_glu_row, w_out, w_ffn_in, w_ffn_out)


def _rope_tables(n_rows, ctx_len):
    rows = jnp.broadcast_to(jnp.arange(n_rows, dtype=F32)[:, None], (n_rows, GRID_W)).reshape(-1)
    cols = jnp.broadcast_to(jnp.arange(GRID_W, dtype=F32)[None, :], (n_rows, GRID_W)).reshape(-1)
    n_freq = DA_HD // 4
    inv = ROPE_BASE ** (-jnp.arange(n_freq, dtype=F32) / n_freq)
    ang = jnp.concatenate([rows[:, None] * inv, cols[:, None] * inv], axis=-1)
    cos, sin = jnp.cos(ang), jnp.sin(ang)
    zero = jnp.zeros_like(sin)
    reps = GROUP_W // DA_HD

    def lanes(even, odd, ctx_value):
        t = jnp.tile(jnp.stack([even, odd], axis=-1).reshape(even.shape[0], DA_HD), (1, reps))
        return jnp.concatenate([jnp.full((ctx_len, GROUP_W), ctx_value, F32), t], axis=0)

    return lanes(cos, cos, 1.0), lanes(-sin, zero, 0.0), lanes(zero, sin, 0.0)


def kernel(x, c, ctx, c_ctx, w_ada, b_ada, norm_g, w_in, w_out, da_lambda, da_subln, hg_lb, hg_norm, sc_w, sc_b, s5_a_re, s5_a_im, s5_log_dt, s5_b_re, s5_b_im, s5_c_re, s5_c_im, s5_d, s5_w_glu, s5_b_glu, w_ffn_in, w_ffn_out):
    B, T, D = x.shape
    ctx_len = ctx.shape[1]
    L = w_ada.shape[0]
    assert ctx_len % TILE == 0 and T % TILE == 0 and T % GRID_W == 0
    n_ctx_tiles = ctx_len // TILE

    bp = -(-(B + 1) // SUBLANES) * SUBLANES
    cvec = jnp.concatenate([c, c_ctx[None, :], jnp.zeros((bp - B - 1, D), c.dtype)], axis=0)
    mods = _modulation(cvec.astype(F32), w_ada, b_ada).reshape(L, bp, 6, D)

    cos, sin_a, sin_b = _rope_tables(T // GRID_W, ctx_len)
    lb = jnp.cumsum(jax.nn.softmax(hg_lb.astype(F32), axis=0), axis=0)
    lb = lb - lb[:1]
    w_in_b = w_in.astype(BF16)
    w_out_b = w_out.astype(BF16)
    w_ffn_in_b = w_ffn_in.astype(BF16)
    w_ffn_out_b = w_ffn_out.astype(BF16)
    s5_mats = jax.vmap(_s5_chunk_matrices)(s5_a_re, s5_a_im, s5_log_dt, s5_b_re, s5_b_im, s5_c_re,
                                           s5_c_im, s5_d)

    S = ctx_len + T
    h_parts = (ctx, x, 0)
    for l in range(L):
        want_ctx = l < L - 1
        lam_init = 0.8 - 0.6 * math.exp(-0.3 * l)
        qkv, rest, u5 = _inproj(h_parts, S, mods[l], norm_g[l, 0], w_in_b[l], cos, sin_a, sin_b, B,
                                n_ctx_tiles)
        a = _attention(qkv, da_lambda[l], da_subln[l], lam_init, ctx_len, want_ctx)
        hgrn_dirs = _hgrn(rest, lb[l])
        tmat, bs, cs, a_l = (m[l] for m in s5_mats)
        y5 = _s5_chunked(u5, tmat, bs, cs, a_l, ctx_len)
        h = _post(h_parts, S, a, hgrn_dirs, rest, y5, mods[l], norm_g[l].astype(F32), hg_norm[l],
                  sc_w[l], sc_b[l], s5_w_glu[l], s5_b_glu[l], w_out_b[l], w_ffn_in_b[l], w_ffn_out_b[l],
                  B, n_ctx_tiles, want_ctx)
        h_parts = (h, h, n_ctx_tiles)
    return h
```

```python
import functools
import math

import jax
import jax.numpy as jnp
from jax import lax
from jax.experimental import pallas as pl
from jax.experimental.pallas import tpu as pltpu

F32 = jnp.float32
BF16 = jnp.bfloat16

EPS = 1e-6
LOG2E = math.log2(math.e)
GRID_W = 64
ROPE_BASE = 10000.0
GROUP_W = 256
DA_HEADS = 4
DA_HD = 32
HG_HEADS = 4
HG_HD = 64
HG_CHUNK = 16
HG_BLOCK = 64
S5_NGROUPS = 16
S5_GROUP = 16
S5_STATE = 64
SUBLANES = 8
TILE = 256
HGRN_NB = 4
INPROJ_NB = 4
POST_NB = 2
S5_L = 16
VMEM_LIMIT = 56 * 1024 * 1024

_NT = (((1,), (1,)), ((), ()))


def _dot(a, b):
    return jnp.dot(a, b, preferred_element_type=F32)


def _dot_nt(a, b):
    return lax.dot_general(a, b, _NT, preferred_element_type=F32)


def _split2_dot_rhs(m, x):
    x1 = x.astype(BF16)
    x2 = (x - x1.astype(F32)).astype(BF16)
    return _dot(m, x1) + _dot(m, x2)


def _split2_dot(x, m):
    x1 = x.astype(BF16)
    x2 = (x - x1.astype(F32)).astype(BF16)
    return _dot(x1, m) + _dot(x2, m)


def _silu(x):
    return x * jax.nn.sigmoid(x)


def _rms_scale(x):
    return lax.rsqrt(jnp.mean(x * x, axis=-1, keepdims=True) + EPS)


def _group_mean_matrix(n, group):
    sh = int(math.log2(group))
    r = lax.broadcasted_iota(jnp.int32, (n, n), 0) >> sh
    c = lax.broadcasted_iota(jnp.int32, (n, n), 1) >> sh
    return jnp.where(r == c, 1.0 / group, 0.0).astype(BF16)


def _params(*sem):
    return pltpu.CompilerParams(dimension_semantics=sem, vmem_limit_bytes=VMEM_LIMIT)


def _mod_kernel(c_ref, w_ref, b_ref, o_ref):
    sc = _silu(c_ref[...])
    o_ref[0] = _dot(sc.astype(BF16), w_ref[0].astype(BF16)) + b_ref[0]


def _modulation(cvec, w_ada, b_ada):
    L, D, N = w_ada.shape
    Bp = cvec.shape[0]
    tn = 1536
    return pl.pallas_call(
        _mod_kernel,
        grid=(L, N // tn),
        in_specs=[
            pl.BlockSpec((Bp, D), lambda l, j: (0, 0)),
            pl.BlockSpec((1, D, tn), lambda l, j: (l, 0, j)),
            pl.BlockSpec((1, 1, tn), lambda l, j: (l, 0, j)),
        ],
        out_specs=pl.BlockSpec((1, Bp, tn), lambda l, j: (l, 0, j)),
        out_shape=jax.ShapeDtypeStruct((L, Bp, N), F32),
        compiler_params=_params("arbitrary", "arbitrary"),
        name="modulation",
    )(cvec, w_ada, b_ada.reshape(L, 1, N))


def _rope(x, cos, sin_a, sin_b):
    return x * cos + pltpu.roll(x, GROUP_W - 1, 1) * sin_a + pltpu.roll(x, 1, 1) * sin_b


def _tile_mod(is_ctx, modx_ref, modc_ref):
    return jnp.where(is_ctx, modc_ref[...], modx_ref[...])


def _seq_specs(h_parts, block, n_ctx_tiles, first_tile=0):
    _, _, lat_off = h_parts

    def ctx_idx(b, t):
        return (b, jnp.minimum(t + first_tile, n_ctx_tiles - 1), 0)

    def lat_idx(b, t):
        return (b, jnp.maximum(t + first_tile, n_ctx_tiles) - n_ctx_tiles + lat_off, 0)

    return [pl.BlockSpec(block, ctx_idx), pl.BlockSpec(block, lat_idx)]


def _inproj_kernel(n_ctx_tiles, hc_ref, hx_ref, modx_ref, modc_ref, g_ref, w_ref, cos_ref, sa_ref,
                   sb_ref, qkv_ref, rest_ref, u5_ref):
    nb, tile, d_model = hx_ref.shape
    is_ctx = pl.program_id(1) < n_ctx_tiles
    mod = _tile_mod(is_ctx, modx_ref, modc_ref)
    h = jnp.where(is_ctx, hc_ref[...], hx_ref[...])
    hn = h * _rms_scale(h) * g_ref[...]
    y = hn * (1.0 + mod[:, 1:2, :]) + mod[:, 0:1, :]
    p = _dot(y.reshape(nb * tile, d_model).astype(BF16), w_ref[...])
    cos, sa, sb = cos_ref[...], sa_ref[...], sb_ref[...]
    for j in range(nb):
        pj = p[j * tile:(j + 1) * tile]
        q = _rope(pj[:, 0:GROUP_W], cos, sa, sb) * (DA_HD ** -0.5 * math.log2(math.e))
        k = _rope(pj[:, GROUP_W:2 * GROUP_W], cos, sa, sb)
        qkv_ref[j, :, 0:GROUP_W] = q.astype(BF16)
        qkv_ref[j, :, GROUP_W:2 * GROUP_W] = k.astype(BF16)
        qkv_ref[j, :, 2 * GROUP_W:3 * GROUP_W] = pj[:, 2 * GROUP_W:3 * GROUP_W].astype(BF16)
        rest_ref[j] = pj[:, 3 * GROUP_W:11 * GROUP_W]
        u5_ref[:, j, :, :] = pj[:, 11 * GROUP_W:].reshape(tile // S5_L, S5_L, GROUP_W)


def _inproj(h_parts, S, mod, g, w, cos, sa, sb, n_batch, n_ctx_tiles):
    B, _, D = h_parts[1].shape
    N = w.shape[1]
    nt = S // TILE
    nb = INPROJ_NB
    assert B % nb == 0
    tab = pl.BlockSpec((TILE, GROUP_W), lambda b, t: (t, 0))
    return pl.pallas_call(
        functools.partial(_inproj_kernel, n_ctx_tiles),
        grid=(B // nb, nt),
        in_specs=_seq_specs(h_parts, (nb, TILE, D), n_ctx_tiles) + [
            pl.BlockSpec((nb, 6, D), lambda b, t: (b, 0, 0)),
            pl.BlockSpec((1, 6, D), lambda b, t: (n_batch, 0, 0)),
            pl.BlockSpec((1, D), lambda b, t: (0, 0)),
            pl.BlockSpec((D, N), lambda b, t: (0, 0)),
            tab, tab, tab,
        ],
        out_specs=[
            pl.BlockSpec((nb, TILE, 3 * GROUP_W), lambda b, t: (b, t, 0)),
            pl.BlockSpec((nb, TILE, 8 * GROUP_W), lambda b, t: (b, t, 0)),
            pl.BlockSpec((TILE // S5_L, nb, S5_L, GROUP_W), lambda b, t: (t, b, 0, 0)),
        ],
        out_shape=[
            jax.ShapeDtypeStruct((B, S, 3 * GROUP_W), BF16),
            jax.ShapeDtypeStruct((B, S, 8 * GROUP_W), F32),
            jax.ShapeDtypeStruct((S // S5_L, B, S5_L, GROUP_W), F32),
        ],
        compiler_params=_params("parallel", "arbitrary"),
        name="inproj",
    )(h_parts[0], h_parts[1], mod, mod, g.reshape(1, D), w, cos, sa, sb)


def _attn_kernel(lam_init, ctx_len, q_off, q_ref, k_ref, v_ref, lamv_ref, g_ref, o_ref,
                 sc_ref, e_ref, va_ref, vb_ref):
    tq = q_ref.shape[1]
    n_keys = k_ref.shape[1]
    lv = lamv_ref[...]
    lam = (jnp.exp(jnp.sum(lv[0:1] * lv[1:2], axis=-1, keepdims=True))
           - jnp.exp(jnp.sum(lv[2:3] * lv[3:4], axis=-1, keepdims=True)) + lam_init)
    lane = lax.broadcasted_iota(jnp.int32, (1, GROUP_W), 1)
    lane_map = lane >> 5
    lane_head = lane >> 6

    @pl.when(pl.program_id(1) == 0)
    def _():
        v = v_ref[0]
        one = jnp.ones_like(v)
        va_ref[...] = jnp.where(lane == GROUP_W - 1, one, v)
        vb_ref[...] = jnp.where(lane == 0, one, v)

    q = q_ref[0]
    q8 = jnp.concatenate(
        [jnp.where(lane_map == j, q, jnp.zeros_like(q)) for j in range(2 * DA_HEADS)], axis=0)

    def attend(nk):
        sc_ref[:, 0:nk] = _dot_nt(q8, k_ref[0, 0:nk, :])
        acc = jnp.zeros((tq, GROUP_W), F32)
        for h in range(DA_HEADS):
            for m in range(2):
                rows = slice((2 * h + m) * tq, (2 * h + m + 1) * tq)
                s = sc_ref[rows, 0:nk]
                e_ref[rows, 0:nk] = jnp.exp2(s - jnp.max(s, axis=-1, keepdims=True)).astype(BF16)
            vv = va_ref if h < DA_HEADS - 1 else vb_ref
            sum_col = GROUP_W - 1 if h < DA_HEADS - 1 else 0
            o2 = _dot(e_ref[2 * h * tq:(2 * h + 2) * tq, 0:nk], vv[0:nk, :])
            o0, o1 = o2[0:tq], o2[tq:2 * tq]
            w0 = 1.0 / o0[:, sum_col:sum_col + 1]
            w1 = lam / o1[:, sum_col:sum_col + 1]
            acc = acc + jnp.where(lane_head == h, o0 * w0 - o1 * w1, 0.0)
        ms = _split2_dot(acc * acc, _group_mean_matrix(GROUP_W, 2 * DA_HD))
        o_ref[0] = (acc * lax.rsqrt(ms + EPS) * g_ref[...] * (1.0 - lam_init)).astype(o_ref.dtype)

    if q_off == 0 and ctx_len == tq:
        @pl.when(pl.program_id(1) == 0)
        def _():
            attend(ctx_len)

        @pl.when(pl.program_id(1) != 0)
        def _():
            attend(n_keys)
    else:
        attend(n_keys)


def _attention(qkv, lam_vecs, subln_g, lam_init, ctx_len, want_ctx):
    B, S, _ = qkv.shape
    tq = TILE
    q_off = 0 if want_ctx else ctx_len // tq
    nq = S // tq - q_off
    g = jnp.tile(subln_g.astype(F32), DA_HEADS).reshape(1, GROUP_W)
    return pl.pallas_call(
        functools.partial(_attn_kernel, lam_init, ctx_len, q_off),
        grid=(B, nq),
        in_specs=[
            pl.BlockSpec((1, tq, GROUP_W), lambda b, i: (b, i + q_off, 0)),
            pl.BlockSpec((1, S, GROUP_W), lambda b, i: (b, 0, 1)),
            pl.BlockSpec((1, S, GROUP_W), lambda b, i: (b, 0, 2)),
            pl.BlockSpec((4, DA_HD), lambda b, i: (0, 0)),
            pl.BlockSpec((1, GROUP_W), lambda b, i: (0, 0)),
        ],
        out_specs=pl.BlockSpec((1, tq, GROUP_W), lambda b, i: (b, i, 0)),
        out_shape=jax.ShapeDtypeStruct((B, nq * tq, GROUP_W), BF16),
        scratch_shapes=[
            pltpu.VMEM((2 * DA_HEADS * tq, S), F32),
            pltpu.VMEM((2 * DA_HEADS * tq, S), BF16),
            pltpu.VMEM((S, GROUP_W), BF16),
            pltpu.VMEM((S, GROUP_W), BF16),
        ],
        compiler_params=_params("parallel", "arbitrary"),
        name="diff_attention",
    )(qkv, qkv, qkv, lam_vecs.astype(F32), g)


def _hgrn_constants():
    row = lax.broadcasted_iota(jnp.int32, (TILE, TILE), 0)
    col = lax.broadcasted_iota(jnp.int32, (TILE, TILE), 1)
    same_chunk = (row // HG_CHUNK) == (col // HG_CHUNK)
    same_block = (row // HG_BLOCK) == (col // HG_BLOCK)

    def ones(mask):
        return jnp.where(mask, 1.0, 0.0).astype(BF16)

    row_mats, col_mats, allowed = [], [], []
    for d in range(2):
        dist = (row // HG_CHUNK - col // HG_CHUNK) * (1 if d == 0 else -1)
        causal = same_chunk & ((col <= row) if d == 0 else (col >= row))
        row_mats.append(jnp.concatenate([ones(causal), ones(same_block & (dist >= 1))], axis=0))
        col_mats.append(jnp.concatenate(
            [ones(same_chunk & ((row <= col) if d == 0 else (row >= col))), ones(same_chunk)], axis=1))
        allowed.append(ones(same_block & ((dist >= 1) | causal)))
    return jnp.stack(row_mats), jnp.stack(col_mats), jnp.stack(allowed)


def _hgrn_kernel(qf_ref, ff_ref, if_ref, qb_ref, fb_ref, ib_ref, lb_ref, rowmat_ref, colmat_ref,
                 allow_ref, of_ref, ob_ref, st2_ref, u2_ref, oc2_ref):
    nb = qf_ref.shape[0]
    n_blocks = TILE // HG_BLOCK
    chunks_per_block = HG_BLOCK // HG_CHUNK
    lane = lax.broadcasted_iota(jnp.int32, (1, GROUP_W), 1)
    lane_head = lane >> int(math.log2(HG_HD))
    lane_block = lane >> int(math.log2(HG_BLOCK))
    lane_pos = (lane >> int(math.log2(HG_CHUNK))) & (chunks_per_block - 1)

    @pl.when(pl.program_id(1) == 0)
    def _():
        st2_ref[...] = jnp.zeros_like(st2_ref)

    def stack_heads(x):
        zero = jnp.zeros_like(x)
        return jnp.concatenate(
            [jnp.where(lane_head == h, x, zero).astype(BF16) for h in range(HG_HEADS)], axis=0)

    def chain(j, d):
        allowed = allow_ref[d] != 0
        row_mat, col_mat = rowmat_ref[d], colmat_ref[d]
        q_ref, f_ref, i_ref, o_ref = ((qf_ref, ff_ref, if_ref, of_ref) if d == 0
                                      else (qb_ref, fb_ref, ib_ref, ob_ref))
        lb = lb_ref[d:d + 1, :]
        st_ref, u_ref, oc_ref = st2_ref.at[j, d], u2_ref.at[j, d], oc2_ref.at[j, d]
        f = lb + (1.0 - lb) * jax.nn.sigmoid(f_ref[j])
        lf = jnp.log(f) * LOG2E
        yield
        sums = _split2_dot_rhs(row_mat, lf)
        kk_t = 1.0 - f.T
        sums_t = _split2_dot(lf.T, col_mat)
        yield
        qd = _silu(q_ref[j]) * jnp.exp2(sums[0:TILE])
        q_state = qd * jnp.exp2(sums[TILE:2 * TILE])
        g_t, gt_t = sums_t[:, 0:TILE], sums_t[:, TILE:2 * TILE]
        kend_t = kk_t * jnp.exp2(gt_t - g_t)
        k_var = [kk_t * jnp.exp2(-g_t), kend_t]
        later = jnp.zeros_like(gt_t)
        for m in range(1, chunks_per_block):
            if d == 0:
                shifted = pltpu.roll(gt_t, TILE - m * HG_CHUNK, 1)
                inside = lane_pos <= chunks_per_block - 1 - m
            else:
                shifted = pltpu.roll(gt_t, m * HG_CHUNK, 1)
                inside = lane_pos >= m
            later = later + jnp.where(inside, shifted, 0.0)
            if m + 1 < chunks_per_block:
                k_var.append(kend_t * jnp.exp2(later))
        kblk_t = kend_t * jnp.exp2(later)
        dec_t = jnp.exp2(gt_t + later)
        vb = i_ref[j].astype(BF16)
        yield

        zero = jnp.zeros_like(kblk_t)
        k_stack = jnp.concatenate(
            [jnp.where(lane_block == b, kblk_t, zero).astype(BF16) for b in range(n_blocks)], axis=0)
        u_ref[...] = _dot(k_stack, vb)
        yield

        qd4 = stack_heads(qd)
        x_pos = []
        for p in range(chunks_per_block):
            keys = jnp.zeros_like(kend_t)
            for n in range(chunks_per_block):
                pc = p - n if d == 0 else p + n
                if 0 <= pc < chunks_per_block:
                    keys = jnp.where(lane_pos == pc, k_var[n], keys)
            q_p = jnp.concatenate(
                [qd4[h * TILE + b * HG_BLOCK + p * HG_CHUNK:h * TILE + b * HG_BLOCK + (p + 1) * HG_CHUNK]
                 for h in range(HG_HEADS) for b in range(n_blocks)], axis=0)
            x_pos.append(_dot(q_p, keys.astype(BF16)))
            yield
        a_heads = []
        for h in range(HG_HEADS):
            a = jnp.concatenate(
                [x_pos[p][(h * n_blocks + b) * HG_CHUNK:(h * n_blocks + b + 1) * HG_CHUNK]
                 for b in range(n_blocks) for p in range(chunks_per_block)], axis=0)
            a_heads.append(jnp.where(allowed, a, 0.0).astype(BF16))
        zb = jnp.zeros_like(vb)
        v4 = jnp.concatenate(
            [jnp.where(lane_head == h, vb, zb) for h in range(HG_HEADS)], axis=0)
        o_tile = _dot(jnp.concatenate(a_heads, axis=1), v4)
        yield

        qs4 = stack_heads(q_state)
        order = range(n_blocks) if d == 0 else range(n_blocks - 1, -1, -1)
        for b in order:
            st = st_ref[...]
            q4b = jnp.concatenate(
                [qs4[h * TILE + b * HG_BLOCK:h * TILE + (b + 1) * HG_BLOCK] for h in range(HG_HEADS)],
                axis=0)
            ob4 = _dot(q4b, st.astype(BF16))
            ob = jnp.zeros((HG_BLOCK, GROUP_W), F32)
            for h in range(HG_HEADS):
                ob = ob + jnp.where(lane_head == h, ob4[h * HG_BLOCK:(h + 1) * HG_BLOCK], 0.0)
            oc_ref[b * HG_BLOCK:(b + 1) * HG_BLOCK, :] = ob
            first = b * HG_BLOCK + (0 if d == 0 else HG_BLOCK - HG_CHUNK)
            st_ref[...] = st * dec_t[:, first:first + 1] + u_ref[b * TILE:(b + 1) * TILE, :]
            yield
        o_ref[j] = o_tile + oc_ref[...]

    done = object()
    pending = [chain(j, d) for j in range(nb) for d in range(2)]
    while pending:
        pending = [c for c in pending if next(c, done) is not done]


def _hgrn(rest, lb):
    B, S, _ = rest.shape
    n_tiles = S // TILE
    nb = HGRN_NB
    assert B % nb == 0

    def back(i):
        return jnp.where(i == 0, 0, n_tiles - i)

    def fwd(part):
        return pl.BlockSpec((nb, TILE, GROUP_W), lambda b, i: (b, i, part))

    def bwd(part):
        return pl.BlockSpec((nb, TILE, GROUP_W), lambda b, i: (b, back(i), part))

    def whole(arr):
        return pl.BlockSpec(arr.shape, lambda b, i: (0,) * arr.ndim)

    consts = _hgrn_constants()
    dir_sds = jax.ShapeDtypeStruct((B, S, GROUP_W), F32)
    return pl.pallas_call(
        _hgrn_kernel,
        grid=(B // nb, n_tiles),
        in_specs=[fwd(0), fwd(1), fwd(3), bwd(0), bwd(2), bwd(3), whole(lb)]
        + [whole(c) for c in consts],
        out_specs=[pl.BlockSpec((nb, TILE, GROUP_W), lambda b, i: (b, i, 0)),
                   pl.BlockSpec((nb, TILE, GROUP_W), lambda b, i: (b, back(i), 0))],
        out_shape=[dir_sds, dir_sds],
        scratch_shapes=[
            pltpu.VMEM((nb, 2, GROUP_W, GROUP_W), F32),
            pltpu.VMEM((nb, 2, TILE // HG_BLOCK * TILE, GROUP_W), F32),
            pltpu.VMEM((nb, 2, TILE, GROUP_W), F32),
        ],
        compiler_params=_params("parallel", "arbitrary"),
        name="hgrn2",
    )(rest, rest, rest, rest, rest, rest, lb, *consts)


def _s5_discretise(a_re, a_im, log_dt, b_re, b_im):
    dt = jnp.exp(log_dt)[..., None]
    mag = jnp.exp(dt * a_re)
    ab_re, ab_im = mag * jnp.cos(dt * a_im), mag * jnp.sin(dt * a_im)
    den = a_re * a_re + a_im * a_im
    nr = ab_re - 1.0
    f_re = (nr * a_re + ab_im * a_im) / den
    f_im = (ab_im * a_re - nr * a_im) / den
    bb_re = f_re[..., None] * b_re - f_im[..., None] * b_im
    bb_im = f_re[..., None] * b_im + f_im[..., None] * b_re
    return ab_re, ab_im, bb_re, bb_im


def _s5_chunk_matrices(a_re, a_im, log_dt, b_re, b_im, c_re, c_im, d_skip):
    L, G, P, N = S5_L, S5_NGROUPS, S5_GROUP, S5_STATE
    hp = lax.Precision.HIGHEST
    ab_re, ab_im, bb_re, bb_im = _s5_discretise(
        a_re.astype(F32), a_im.astype(F32), log_dt.astype(F32), b_re.astype(F32), b_im.astype(F32))
    c_re, c_im = c_re.astype(F32), c_im.astype(F32)
    j_dt = (jnp.arange(L + 1, dtype=F32).reshape(L + 1, 1, 1, 1)
            * jnp.exp(log_dt.astype(F32))[None, :, :, None])
    mag = jnp.exp(j_dt * a_re.astype(F32)[None])
    pr, pi = mag * jnp.cos(j_dt * a_im.astype(F32)[None]), mag * jnp.sin(j_dt * a_im.astype(F32)[None])
    ca_re = c_re[None] * pr[:, :, :, None, :] - c_im[None] * pi[:, :, :, None, :]
    ca_im = c_re[None] * pi[:, :, :, None, :] + c_im[None] * pr[:, :, :, None, :]
    k = (jnp.einsum("jdgqn,dgnp->dgpqj", ca_re[:L], bb_re, precision=hp)
         - jnp.einsum("jdgqn,dgnp->dgpqj", ca_im[:L], bb_im, precision=hp))
    k_f = k[0].reshape(G * P, P * L)
    k_b = k[1][..., ::-1].reshape(G * P, P * L)
    col_t = jnp.arange(P * L) % L
    t_rows = [jnp.where(col_t >= s, jnp.roll(k_f, s, axis=-1), 0.0)
              + jnp.where(col_t <= s, jnp.roll(k_b, s - (L - 1), axis=-1), 0.0) for s in range(L)]
    tmat = jnp.stack(t_rows, axis=1).reshape(G, P * L, P * L)
    skip = jnp.repeat(d_skip.astype(F32).reshape(G, P), L, axis=1).reshape(G, 1, L * P)
    tmat = tmat + jnp.eye(L * P, dtype=F32)[None] * skip

    def to_state(pw_re, pw_im, d):
        br = jnp.swapaxes(bb_re[d], 1, 2)[None]
        bi = jnp.swapaxes(bb_im[d], 1, 2)[None]
        re = pw_re[:, :, None, :] * br - pw_im[:, :, None, :] * bi
        im = pw_re[:, :, None, :] * bi + pw_im[:, :, None, :] * br
        return (re.transpose(1, 2, 0, 3).reshape(G, L * P, N),
                im.transpose(1, 2, 0, 3).reshape(G, L * P, N))

    f_re, f_im = to_state(pr[:L, 0][::-1], pi[:L, 0][::-1], 0)
    g_re, g_im = to_state(pr[:L, 1], pi[:L, 1], 1)
    bs = jnp.concatenate([f_re, g_re, f_im, g_im], axis=2)

    def from_state(d, flip):
        pw_re, pw_im = pr[1:, d], pi[1:, d]
        if flip:
            pw_re, pw_im = pw_re[::-1], pw_im[::-1]
        pw_re = jnp.tile(pw_re.transpose(1, 2, 0), (1, 1, P))
        pw_im = jnp.tile(pw_im.transpose(1, 2, 0), (1, 1, P))
        cr = jnp.repeat(jnp.swapaxes(c_re[d], 1, 2), L, axis=-1)
        ci = jnp.repeat(jnp.swapaxes(c_im[d], 1, 2), L, axis=-1)
        return cr * pw_re - ci * pw_im, cr * pw_im + ci * pw_re

    (cf_re, cf_im), (cb_re, cb_im) = from_state(0, False), from_state(1, True)
    cs = jnp.concatenate([cf_re, cb_re, -cf_im, -cb_im], axis=1)
    a_l = jnp.stack([jnp.concatenate([pr[L, 0], pr[L, 1]], axis=-1),
                     jnp.concatenate([pi[L, 0], pi[L, 1]], axis=-1)], axis=1)
    return tmat.astype(BF16), bs.astype(BF16), cs.astype(BF16), a_l


def _s5_chunk_kernel(n_ctx_chunks, n_chunks, u_ref, t_ref, bs_ref, cs_ref, al_ref, y_ref,
                     v_ref, xs_ref):
    n_ch, n_steps, n_rows = u_ref.shape
    nb = n_rows // n_chunks
    n = S5_STATE
    u = u_ref[...].reshape(n_ch * n_steps, n_rows).T.astype(BF16)
    v_ref[...] = _dot(u, bs_ref[0])
    fwd_lanes = lax.broadcasted_iota(jnp.int32, (1, 2 * n), 1) < n
    a_re = jnp.broadcast_to(al_ref[0, 0:1, :], (nb, 2 * n))
    a_im = jnp.broadcast_to(al_ref[0, 1:2, :], (nb, 2 * n))

    def step(i, carry):
        xr, xi = carry
        cb = jnp.where(i < n_ctx_chunks, n_ctx_chunks - 1 - i, n_chunks - 1 - (i - n_ctx_chunks))
        rf = pl.ds(pl.multiple_of(i * nb, nb), nb)
        rb = pl.ds(pl.multiple_of(cb * nb, nb), nb)
        xs_ref[rf, 0:n] = xr[:, 0:n]
        xs_ref[rb, n:2 * n] = xr[:, n:2 * n]
        xs_ref[rf, 2 * n:3 * n] = xi[:, 0:n]
        xs_ref[rb, 3 * n:4 * n] = xi[:, n:2 * n]
        vr = jnp.where(fwd_lanes, v_ref[rf, 0:2 * n], v_ref[rb, 0:2 * n])
        vi = jnp.where(fwd_lanes, v_ref[rf, 2 * n:4 * n], v_ref[rb, 2 * n:4 * n])
        return a_re * xr - a_im * xi + vr, a_re * xi + a_im * xr + vi

    zero = jnp.zeros((nb, 2 * n), F32)
    lax.fori_loop(0, n_chunks, step, (zero, zero))
    y = _dot(u, t_ref[0]) + _dot(xs_ref[...].astype(BF16), cs_ref[0])
    y_ref[...] = y.T.reshape(n_ch, n_steps, n_rows)


def _s5_chunked(u5c, tmat, bs, cs, a_l, ctx_len):
    n_chunks, B, L, W = u5c.shape
    P = S5_GROUP
    S = n_chunks * L
    assert L == S5_L and ctx_len % L == 0 and TILE % L == 0
    rows = n_chunks * B
    ut = u5c.reshape(rows, L, W).transpose(2, 1, 0)

    def per_group(arr):
        return pl.BlockSpec((1,) + arr.shape[1:], lambda g: (g,) + (0,) * (arr.ndim - 1))

    group_block = pl.BlockSpec((P, L, rows), lambda g: (g, 0, 0))
    yt = pl.pallas_call(
        functools.partial(_s5_chunk_kernel, ctx_len // L, n_chunks),
        grid=(W // P,),
        in_specs=[group_block, per_group(tmat), per_group(bs), per_group(cs), per_group(a_l)],
        out_specs=group_block,
        out_shape=jax.ShapeDtypeStruct((W, L, rows), F32),
        scratch_shapes=[pltpu.VMEM((rows, 4 * S5_STATE), F32),
                        pltpu.VMEM((rows, 4 * S5_STATE), F32)],
        compiler_params=_params("parallel"),
        name="s5_chunks",
    )(ut, tmat, bs, cs, a_l)
    return yt.transpose(2, 1, 0).reshape(n_chunks, B, L, W)


def _post_kernel(n_hidden_chunks, ctx_tiles_here, tiles_here, hc_ref, hx_ref, a_ref, bf_ref, bb_ref,
                 bg_ref, cb_ref, cc_ref, cu_ref, ccp_ref, cup_ref, ccn_ref, cun_ref, d_ref, modx_ref,
                 modc_ref, ng_ref, hgn_ref, scw_ref, scb_ref, wglu_ref, bglu_ref, wo_ref, wi_ref,
                 w2_ref, o_ref):
    nb, tile, d_model = hx_ref.shape
    n_rows = nb * tile
    t = pl.program_id(1)
    is_ctx = t < ctx_tiles_here
    mod = _tile_mod(is_ctx, modx_ref, modc_ref)

    def flat(ref):
        return ref[...].reshape(n_rows, GROUP_W)

    tot = flat(bf_ref) + flat(bb_ref)
    ms = _split2_dot(tot * tot, _group_mean_matrix(GROUP_W, HG_HD))
    b_val = tot * lax.rsqrt(ms + EPS) * hgn_ref[...] * _silu(flat(bg_ref))
    seg_start = (t == 0) | (t == ctx_tiles_here)
    seg_end = (t == ctx_tiles_here - 1) | (t == tiles_here - 1)
    row = lax.broadcasted_iota(jnp.int32, (tile, 1), 0)
    c_parts = []
    for j in range(nb):
        v = cc_ref[j] * cu_ref[j]
        v_before = jnp.where(seg_start, 0.0, ccp_ref[j, SUBLANES - 1:SUBLANES, :] * cup_ref[j, SUBLANES - 1:SUBLANES, :])
        v_after = jnp.where(seg_end, 0.0, ccn_ref[j, 0:1, :] * cun_ref[j, 0:1, :])
        prev = jnp.where(row == 0, v_before, pltpu.roll(v, 1, 0))
        nxt = jnp.where(row == tile - 1, v_after, pltpu.roll(v, tile - 1, 0))
        y = scw_ref[0:1, :] * prev + scw_ref[1:2, :] * v + scw_ref[2:3, :] * nxt
        c_parts.append(cb_ref[j] * (y + scb_ref[...]))
    c_val = jnp.concatenate(c_parts, axis=0)
    y5 = jax.nn.gelu(jnp.concatenate(
        [d_ref[:, j, :, :].reshape(tile, GROUP_W) for j in range(nb)], axis=0))
    d_val = y5 * jax.nn.sigmoid(_dot(y5.astype(BF16), wglu_ref[...]) + bglu_ref[...])

    mix = (_dot(flat(a_ref), wo_ref[0:GROUP_W, :])
           + _dot(b_val.astype(BF16), wo_ref[GROUP_W:2 * GROUP_W, :])
           + _dot(c_val.astype(BF16), wo_ref[2 * GROUP_W:3 * GROUP_W, :])
           + _dot(d_val.astype(BF16), wo_ref[3 * GROUP_W:4 * GROUP_W, :])).reshape(nb, tile, d_model)
    h = (jnp.where(is_ctx, hc_ref[...], hx_ref[...])
         + mod[:, 2:3, :] * (mix * _rms_scale(mix) * ng_ref[1:2, :]))
    y = (h * _rms_scale(h) * ng_ref[2:3, :]) * (1.0 + mod[:, 4:5, :]) + mod[:, 3:4, :]
    yb = y.reshape(n_rows, d_model).astype(BF16)
    hidden = w2_ref.shape[0]
    hc = hidden // n_hidden_chunks
    ffn = jnp.zeros((n_rows, d_model), F32)
    for j in range(n_hidden_chunks):
        gate = _dot(yb, wi_ref[:, j * hc:(j + 1) * hc])
        up = _dot(yb, wi_ref[:, hidden + j * hc:hidden + (j + 1) * hc])
        ffn = ffn + _dot((_silu(gate) * up).astype(BF16), w2_ref[j * hc:(j + 1) * hc, :])
    ffn = ffn.reshape(nb, tile, d_model)
    o_ref[...] = h + mod[:, 5:6, :] * (ffn * _rms_scale(ffn) * ng_ref[3:4, :])


def _post(h_parts, S, a, hgrn_dirs, rest, y5, mod, norm_g, hg_norm, sc_w, sc_b, w_glu, b_glu, w_out,
          w_ffn_in, w_ffn_out, n_batch, n_ctx_tiles, want_ctx):
    B, _, D = h_parts[1].shape
    off = 0 if want_ctx else n_ctx_tiles
    nt = S // TILE - off
    a_off = off if a.shape[1] == S else 0
    nb = POST_NB
    assert B % nb == 0

    def rows(o):
        return lambda bi, t: (bi, t + o, 0)

    def whole(arr):
        return pl.BlockSpec(arr.shape, lambda bi, t: (0,) * arr.ndim, pipeline_mode=pl.Buffered(1))

    mix_spec = lambda o: pl.BlockSpec((nb, TILE, GROUP_W), rows(o))

    def part(j):
        return pl.BlockSpec((nb, TILE, GROUP_W), lambda bi, t: (bi, t + off, j))

    halo_per_tile = TILE // SUBLANES

    def halo_before(j):
        return pl.BlockSpec((nb, SUBLANES, GROUP_W),
                            lambda bi, t: (bi, jnp.maximum((t + off) * halo_per_tile - 1, 0), j))

    def halo_after(j):
        return pl.BlockSpec((nb, SUBLANES, GROUP_W),
                            lambda bi, t: (bi, jnp.minimum((t + off + 1) * halo_per_tile,
                                                           S // SUBLANES - 1), j))

    tile_chunks = TILE // S5_L
    sc_w_f = sc_w.astype(F32)
    sc_b_row = sc_b.astype(F32).reshape(1, GROUP_W)
    hgn = jnp.tile(hg_norm.astype(F32), HG_HEADS).reshape(1, GROUP_W)
    w_glu_b = w_glu.astype(BF16)
    b_glu_row = b_glu.astype(F32).reshape(1, GROUP_W)
    return pl.pallas_call(
        functools.partial(_post_kernel, 2, n_ctx_tiles - off, nt),
        grid=(B // nb, nt),
        in_specs=_seq_specs(h_parts, (nb, TILE, D), n_ctx_tiles, first_tile=off) + [
            mix_spec(a_off), mix_spec(off), mix_spec(off),
            part(4),
            part(5), part(6), part(7),
            halo_before(6), halo_before(7), halo_after(6), halo_after(7),
            pl.BlockSpec((tile_chunks, nb, S5_L, GROUP_W), lambda bi, t: (t + off, bi, 0, 0)),
            pl.BlockSpec((nb, 6, D), lambda bi, t: (bi, 0, 0)),
            pl.BlockSpec((1, 6, D), lambda bi, t: (n_batch, 0, 0)),
            whole(norm_g), whole(hgn), whole(sc_w_f), whole(sc_b_row), whole(w_glu_b), whole(b_glu_row),
            whole(w_out), whole(w_ffn_in), whole(w_ffn_out),
        ],
        out_specs=pl.BlockSpec((nb, TILE, D), lambda bi, t: (bi, t, 0)),
        out_shape=jax.ShapeDtypeStruct((B, nt * TILE, D), F32),
        compiler_params=_params("parallel", "arbitrary"),
        name="post",
    )(h_parts[0], h_parts[1], a, hgrn_dirs[0], hgrn_dirs[1], rest, rest, rest, rest, rest, rest, rest,
      rest, y5, mod, mod, norm_g, hgn, sc_w_f, sc_b_row, w_glu_b, b_glu_row, w_out, w_ffn_in, w_ffn_out)


def _rope_tables(n_rows, ctx_len):
    rows = jnp.broadcast_to(jnp.arange(n_rows, dtype=F32)[:, None], (n_rows, GRID_W)).reshape(-1)
    cols = jnp.broadcast_to(jnp.arange(GRID_W, dtype=F32)[None, :], (n_rows, GRID_W)).reshape(-1)
    n_freq = DA_HD // 4
    inv = ROPE_BASE ** (-jnp.arange(n_freq, dtype=F32) / n_freq)
    ang = jnp.concatenate([rows[:, None] * inv, cols[:, None] * inv], axis=-1)
    cos, sin = jnp.cos(ang), jnp.sin(ang)
    zero = jnp.zeros_like(sin)
    reps = GROUP_W // DA_HD

    def lanes(even, odd, ctx_value):
        t = jnp.tile(jnp.stack([even, odd], axis=-1).reshape(even.shape[0], DA_HD), (1, reps))
        return jnp.concatenate([jnp.full((ctx_len, GROUP_W), ctx_value, F32), t], axis=0)

    return lanes(cos, cos, 1.0), lanes(-sin, zero, 0.0), lanes(zero, sin, 0.0)


def kernel(x, c, ctx, c_ctx, w_ada, b_ada, norm_g, w_in, w_out, da_lambda, da_subln, hg_lb, hg_norm, sc_w, sc_b, s5_a_re, s5_a_im, s5_log_dt, s5_b_re, s5_b_im, s5_c_re, s5_c_im, s5_d, s5_w_glu, s5_b_glu, w_ffn_in, w_ffn_out):
    B, T, D = x.shape
    ctx_len = ctx.shape[1]
    L = w_ada.shape[0]
    assert ctx_len % TILE == 0 and T % TILE == 0 and T % GRID_W == 0
    n_ctx_tiles = ctx_len // TILE

    bp = -(-(B + 1) // SUBLANES) * SUBLANES
    cvec = jnp.concatenate([c, c_ctx[None, :], jnp.zeros((bp - B - 1, D), c.dtype)], axis=0)
    mods = _modulation(cvec.astype(F32), w_ada, b_ada).reshape(L, bp, 6, D)

    cos, sin_a, sin_b = _rope_tables(T // GRID_W, ctx_len)
    lb = jnp.cumsum(jax.nn.softmax(hg_lb.astype(F32), axis=0), axis=0)
    lb = lb - lb[:1]
    w_in_b = w_in.astype(BF16)
    w_out_b = w_out.astype(BF16)
    w_ffn_in_b = w_ffn_in.astype(BF16)
    w_ffn_out_b = w_ffn_out.astype(BF16)
    s5_mats = jax.vmap(_s5_chunk_matrices)(s5_a_re, s5_a_im, s5_log_dt, s5_b_re, s5_b_im, s5_c_re,
                                           s5_c_im, s5_d)

    S = ctx_len + T
    h_parts = (ctx, x, 0)
    for l in range(L):
        want_ctx = l < L - 1
        lam_init = 0.8 - 0.6 * math.exp(-0.3 * l)
        qkv, rest, u5 = _inproj(h_parts, S, mods[l], norm_g[l, 0], w_in_b[l], cos, sin_a, sin_b, B,
                                n_ctx_tiles)
        a = _attention(qkv, da_lambda[l], da_subln[l], lam_init, ctx_len, want_ctx)
        hgrn_dirs = _hgrn(rest, lb[l])
        tmat, bs, cs, a_l = (m[l] for m in s5_mats)
        y5 = _s5_chunked(u5, tmat, bs, cs, a_l, ctx_len)
        h = _post(h_parts, S, a, hgrn_dirs, rest, y5, mods[l], norm_g[l].astype(F32), hg_norm[l],
                  sc_w[l], sc_b[l], s5_w_glu[l], s5_b_glu[l], w_out_b[l], w_ffn_in_b[l], w_ffn_out_b[l],
                  B, n_ctx_tiles, want_ctx)
        h_parts = (h, h, n_ctx_tiles)
    return h
```

```python
import functools
import math

import jax
import jax.numpy as jnp
from jax import lax
from jax.experimental import pallas as pl
from jax.experimental.pallas import tpu as pltpu

F32 = jnp.float32
BF16 = jnp.bfloat16

EPS = 1e-6
LOG2E = math.log2(math.e)
GRID_W = 64
ROPE_BASE = 10000.0
GROUP_W = 256
DA_HEADS = 4
DA_HD = 32
HG_HEADS = 4
HG_HD = 64
HG_CHUNK = 16
HG_BLOCK = 128
S5_NGROUPS = 16
S5_GROUP = 16
S5_STATE = 64
SUBLANES = 8
TILE = 256
HGRN_NB = 4
INPROJ_NB = 4
POST_NB = 2
S5_L = 16
VMEM_LIMIT = 56 * 1024 * 1024

_NT = (((1,), (1,)), ((), ()))


def _dot(a, b):
    return jnp.dot(a, b, preferred_element_type=F32)


def _dot_nt(a, b):
    return lax.dot_general(a, b, _NT, preferred_element_type=F32)


def _split2_dot_rhs(m, x):
    x1 = x.astype(BF16)
    x2 = (x - x1.astype(F32)).astype(BF16)
    return _dot(m, x1) + _dot(m, x2)


def _split2_dot(x, m):
    x1 = x.astype(BF16)
    x2 = (x - x1.astype(F32)).astype(BF16)
    return _dot(x1, m) + _dot(x2, m)


def _silu(x):
    return x * jax.nn.sigmoid(x)


def _rms_scale(x):
    return lax.rsqrt(jnp.mean(x * x, axis=-1, keepdims=True) + EPS)


def _group_mean_matrix(n, group):
    sh = int(math.log2(group))
    r = lax.broadcasted_iota(jnp.int32, (n, n), 0) >> sh
    c = lax.broadcasted_iota(jnp.int32, (n, n), 1) >> sh
    return jnp.where(r == c, 1.0 / group, 0.0).astype(BF16)


def _params(*sem):
    return pltpu.CompilerParams(dimension_semantics=sem, vmem_limit_bytes=VMEM_LIMIT)


def _mod_kernel(c_ref, w_ref, b_ref, o_ref):
    sc = _silu(c_ref[...])
    o_ref[0] = _dot(sc.astype(BF16), w_ref[0].astype(BF16)) + b_ref[0]


def _modulation(cvec, w_ada, b_ada):
    L, D, N = w_ada.shape
    Bp = cvec.shape[0]
    tn = 1536
    return pl.pallas_call(
        _mod_kernel,
        grid=(L, N // tn),
        in_specs=[
            pl.BlockSpec((Bp, D), lambda l, j: (0, 0)),
            pl.BlockSpec((1, D, tn), lambda l, j: (l, 0, j)),
            pl.BlockSpec((1, 1, tn), lambda l, j: (l, 0, j)),
        ],
        out_specs=pl.BlockSpec((1, Bp, tn), lambda l, j: (l, 0, j)),
        out_shape=jax.ShapeDtypeStruct((L, Bp, N), F32),
        compiler_params=_params("arbitrary", "arbitrary"),
        name="modulation",
    )(cvec, w_ada, b_ada.reshape(L, 1, N))


def _rope(x, cos, sin_a, sin_b):
    return x * cos + pltpu.roll(x, GROUP_W - 1, 1) * sin_a + pltpu.roll(x, 1, 1) * sin_b


def _tile_mod(is_ctx, modx_ref, modc_ref):
    return jnp.where(is_ctx, modc_ref[...], modx_ref[...])


def _seq_specs(h_parts, block, n_ctx_tiles, first_tile=0):
    _, _, lat_off = h_parts

    def ctx_idx(b, t):
        return (b, jnp.minimum(t + first_tile, n_ctx_tiles - 1), 0)

    def lat_idx(b, t):
        return (b, jnp.maximum(t + first_tile, n_ctx_tiles) - n_ctx_tiles + lat_off, 0)

    return [pl.BlockSpec(block, ctx_idx), pl.BlockSpec(block, lat_idx)]


def _inproj_kernel(n_ctx_tiles, hc_ref, hx_ref, modx_ref, modc_ref, g_ref, w_ref, cos_ref, sa_ref,
                   sb_ref, qkv_ref, rest_ref, u5_ref):
    nb, tile, d_model = hx_ref.shape
    is_ctx = pl.program_id(1) < n_ctx_tiles
    mod = _tile_mod(is_ctx, modx_ref, modc_ref)
    h = jnp.where(is_ctx, hc_ref[...], hx_ref[...])
    hn = h * _rms_scale(h) * g_ref[...]
    y = hn * (1.0 + mod[:, 1:2, :]) + mod[:, 0:1, :]
    p = _dot(y.reshape(nb * tile, d_model).astype(BF16), w_ref[...])
    cos, sa, sb = cos_ref[...], sa_ref[...], sb_ref[...]
    for j in range(nb):
        pj = p[j * tile:(j + 1) * tile]
        q = _rope(pj[:, 0:GROUP_W], cos, sa, sb) * (DA_HD ** -0.5 * math.log2(math.e))
        k = _rope(pj[:, GROUP_W:2 * GROUP_W], cos, sa, sb)
        qkv_ref[j, :, 0:GROUP_W] = q.astype(BF16)
        qkv_ref[j, :, GROUP_W:2 * GROUP_W] = k.astype(BF16)
        qkv_ref[j, :, 2 * GROUP_W:3 * GROUP_W] = pj[:, 2 * GROUP_W:3 * GROUP_W].astype(BF16)
        rest_ref[j] = pj[:, 3 * GROUP_W:11 * GROUP_W]
        u5_ref[:, j, :, :] = pj[:, 11 * GROUP_W:].reshape(tile // S5_L, S5_L, GROUP_W)


def _inproj(h_parts, S, mod, g, w, cos, sa, sb, n_batch, n_ctx_tiles):
    B, _, D = h_parts[1].shape
    N = w.shape[1]
    nt = S // TILE
    nb = INPROJ_NB
    assert B % nb == 0
    tab = pl.BlockSpec((TILE, GROUP_W), lambda b, t: (t, 0))
    return pl.pallas_call(
        functools.partial(_inproj_kernel, n_ctx_tiles),
        grid=(B // nb, nt),
        in_specs=_seq_specs(h_parts, (nb, TILE, D), n_ctx_tiles) + [
            pl.BlockSpec((nb, 6, D), lambda b, t: (b, 0, 0)),
            pl.BlockSpec((1, 6, D), lambda b, t: (n_batch, 0, 0)),
            pl.BlockSpec((1, D), lambda b, t: (0, 0)),
            pl.BlockSpec((D, N), lambda b, t: (0, 0)),
            tab, tab, tab,
        ],
        out_specs=[
            pl.BlockSpec((nb, TILE, 3 * GROUP_W), lambda b, t: (b, t, 0)),
            pl.BlockSpec((nb, TILE, 8 * GROUP_W), lambda b, t: (b, t, 0)),
            pl.BlockSpec((TILE // S5_L, nb, S5_L, GROUP_W), lambda b, t: (t, b, 0, 0)),
        ],
        out_shape=[
            jax.ShapeDtypeStruct((B, S, 3 * GROUP_W), BF16),
            jax.ShapeDtypeStruct((B, S, 8 * GROUP_W), F32),
            jax.ShapeDtypeStruct((S // S5_L, B, S5_L, GROUP_W), F32),
        ],
        compiler_params=_params("parallel", "arbitrary"),
        name="inproj",
    )(h_parts[0], h_parts[1], mod, mod, g.reshape(1, D), w, cos, sa, sb)


def _attn_kernel(lam_init, ctx_len, q_off, q_ref, k_ref, v_ref, lamv_ref, g_ref, o_ref,
                 sc_ref, e_ref, va_ref, vb_ref):
    tq = q_ref.shape[1]
    n_keys = k_ref.shape[1]
    lv = lamv_ref[...]
    lam = (jnp.exp(jnp.sum(lv[0:1] * lv[1:2], axis=-1, keepdims=True))
           - jnp.exp(jnp.sum(lv[2:3] * lv[3:4], axis=-1, keepdims=True)) + lam_init)
    lane = lax.broadcasted_iota(jnp.int32, (1, GROUP_W), 1)
    lane_map = lane >> 5
    lane_head = lane >> 6

    @pl.when(pl.program_id(1) == 0)
    def _():
        v = v_ref[0]
        one = jnp.ones_like(v)
        va_ref[...] = jnp.where(lane == GROUP_W - 1, one, v)
        vb_ref[...] = jnp.where(lane == 0, one, v)

    q = q_ref[0]
    q8 = jnp.concatenate(
        [jnp.where(lane_map == j, q, jnp.zeros_like(q)) for j in range(2 * DA_HEADS)], axis=0)

    def attend(nk):
        sc_ref[:, 0:nk] = _dot_nt(q8, k_ref[0, 0:nk, :])
        acc = jnp.zeros((tq, GROUP_W), F32)
        for h in range(DA_HEADS):
            for m in range(2):
                rows = slice((2 * h + m) * tq, (2 * h + m + 1) * tq)
                s = sc_ref[rows, 0:nk]
                e_ref[rows, 0:nk] = jnp.exp2(s - jnp.max(s, axis=-1, keepdims=True)).astype(BF16)
            vv = va_ref if h < DA_HEADS - 1 else vb_ref
            sum_col = GROUP_W - 1 if h < DA_HEADS - 1 else 0
            o2 = _dot(e_ref[2 * h * tq:(2 * h + 2) * tq, 0:nk], vv[0:nk, :])
            o0, o1 = o2[0:tq], o2[tq:2 * tq]
            w0 = 1.0 / o0[:, sum_col:sum_col + 1]
            w1 = lam / o1[:, sum_col:sum_col + 1]
            acc = acc + jnp.where(lane_head == h, o0 * w0 - o1 * w1, 0.0)
        ms = _split2_dot(acc * acc, _group_mean_matrix(GROUP_W, 2 * DA_HD))
        o_ref[0] = (acc * lax.rsqrt(ms + EPS) * g_ref[...] * (1.0 - lam_init)).astype(o_ref.dtype)

    if q_off == 0 and ctx_len == tq:
        @pl.when(pl.program_id(1) == 0)
        def _():
            attend(ctx_len)

        @pl.when(pl.program_id(1) != 0)
        def _():
            attend(n_keys)
    else:
        attend(n_keys)


def _attention(qkv, lam_vecs, subln_g, lam_init, ctx_len, want_ctx):
    B, S, _ = qkv.shape
    tq = TILE
    q_off = 0 if want_ctx else ctx_len // tq
    nq = S // tq - q_off
    g = jnp.tile(subln_g.astype(F32), DA_HEADS).reshape(1, GROUP_W)
    return pl.pallas_call(
        functools.partial(_attn_kernel, lam_init, ctx_len, q_off),
        grid=(B, nq),
        in_specs=[
            pl.BlockSpec((1, tq, GROUP_W), lambda b, i: (b, i + q_off, 0)),
            pl.BlockSpec((1, S, GROUP_W), lambda b, i: (b, 0, 1)),
            pl.BlockSpec((1, S, GROUP_W), lambda b, i: (b, 0, 2)),
            pl.BlockSpec((4, DA_HD), lambda b, i: (0, 0)),
            pl.BlockSpec((1, GROUP_W), lambda b, i: (0, 0)),
        ],
        out_specs=pl.BlockSpec((1, tq, GROUP_W), lambda b, i: (b, i, 0)),
        out_shape=jax.ShapeDtypeStruct((B, nq * tq, GROUP_W), BF16),
        scratch_shapes=[
            pltpu.VMEM((2 * DA_HEADS * tq, S), F32),
            pltpu.VMEM((2 * DA_HEADS * tq, S), BF16),
            pltpu.VMEM((S, GROUP_W), BF16),
            pltpu.VMEM((S, GROUP_W), BF16),
        ],
        compiler_params=_params("parallel", "arbitrary"),
        name="diff_attention",
    )(qkv, qkv, qkv, lam_vecs.astype(F32), g)


def _hgrn_constants():
    row = lax.broadcasted_iota(jnp.int32, (TILE, TILE), 0)
    col = lax.broadcasted_iota(jnp.int32, (TILE, TILE), 1)
    same_chunk = (row // HG_CHUNK) == (col // HG_CHUNK)
    same_block = (row // HG_BLOCK) == (col // HG_BLOCK)

    def ones(mask):
        return jnp.where(mask, 1.0, 0.0).astype(BF16)

    row_mats, col_mats, allowed = [], [], []
    for d in range(2):
        dist = (row // HG_CHUNK - col // HG_CHUNK) * (1 if d == 0 else -1)
        causal = same_chunk & ((col <= row) if d == 0 else (col >= row))
        row_mats.append(jnp.concatenate([ones(causal), ones(same_block & (dist >= 1))], axis=0))
        col_mats.append(jnp.concatenate(
            [ones(same_chunk & ((row <= col) if d == 0 else (row >= col))), ones(same_chunk)], axis=1))
        allowed.append(ones(same_block & ((dist >= 1) | causal)))
    return jnp.stack(row_mats), jnp.stack(col_mats), jnp.stack(allowed)


def _hgrn_kernel(qf_ref, ff_ref, if_ref, qb_ref, fb_ref, ib_ref, lb_ref, rowmat_ref, colmat_ref,
                 allow_ref, of_ref, ob_ref, st2_ref, u2_ref, oc2_ref):
    nb = qf_ref.shape[0]
    n_blocks = TILE // HG_BLOCK
    chunks_per_block = HG_BLOCK // HG_CHUNK
    lane = lax.broadcasted_iota(jnp.int32, (1, GROUP_W), 1)
    lane_head = lane >> int(math.log2(HG_HD))
    lane_block = lane >> int(math.log2(HG_BLOCK))
    lane_pos = (lane >> int(math.log2(HG_CHUNK))) & (chunks_per_block - 1)

    @pl.when(pl.program_id(1) == 0)
    def _():
        st2_ref[...] = jnp.zeros_like(st2_ref)

    def stack_heads(x):
        zero = jnp.zeros_like(x)
        return jnp.concatenate(
            [jnp.where(lane_head == h, x, zero).astype(BF16) for h in range(HG_HEADS)], axis=0)

    def chain(j, d):
        allowed = allow_ref[d] != 0
        row_mat, col_mat = rowmat_ref[d], colmat_ref[d]
        q_ref, f_ref, i_ref, o_ref = ((qf_ref, ff_ref, if_ref, of_ref) if d == 0
                                      else (qb_ref, fb_ref, ib_ref, ob_ref))
        lb = lb_ref[d:d + 1, :]
        st_ref, u_ref, oc_ref = st2_ref.at[j, d], u2_ref.at[j, d], oc2_ref.at[j, d]
        f = lb + (1.0 - lb) * jax.nn.sigmoid(f_ref[j])
        lf = jnp.log(f) * LOG2E
        yield
        sums = _split2_dot_rhs(row_mat, lf)
        kk_t = 1.0 - f.T
        sums_t = _split2_dot(lf.T, col_mat)
        yield
        qd = _silu(q_ref[j]) * jnp.exp2(sums[0:TILE])
        q_state = qd * jnp.exp2(sums[TILE:2 * TILE])
        g_t, gt_t = sums_t[:, 0:TILE], sums_t[:, TILE:2 * TILE]
        kend_t = kk_t * jnp.exp2(gt_t - g_t)
        k_var = [kk_t * jnp.exp2(-g_t), kend_t]
        later = jnp.zeros_like(gt_t)
        for m in range(1, chunks_per_block):
            if d == 0:
                shifted = pltpu.roll(gt_t, TILE - m * HG_CHUNK, 1)
                inside = lane_pos <= chunks_per_block - 1 - m
            else:
                shifted = pltpu.roll(gt_t, m * HG_CHUNK, 1)
                inside = lane_pos >= m
            later = later + jnp.where(inside, shifted, 0.0)
            if m + 1 < chunks_per_block:
                k_var.append(kend_t * jnp.exp2(later))
        kblk_t = kend_t * jnp.exp2(later)
        dec_t = jnp.exp2(gt_t + later)
        vb = i_ref[j].astype(BF16)
        yield

        zero = jnp.zeros_like(kblk_t)
        k_stack = jnp.concatenate(
            [jnp.where(lane_block == b, kblk_t, zero).astype(BF16) for b in range(n_blocks)], axis=0)
        u_ref[...] = _dot(k_stack, vb)
        yield

        qd4 = stack_heads(qd)
        x_pos = []
        for p in range(chunks_per_block):
            keys = jnp.zeros_like(kend_t)
            for n in range(chunks_per_block):
                pc = p - n if d == 0 else p + n
                if 0 <= pc < chunks_per_block:
                    keys = jnp.where(lane_pos == pc, k_var[n], keys)
            q_p = jnp.concatenate(
                [qd4[h * TILE + b * HG_BLOCK + p * HG_CHUNK:h * TILE + b * HG_BLOCK + (p + 1) * HG_CHUNK]
                 for h in range(HG_HEADS) for b in range(n_blocks)], axis=0)
            x_pos.append(_dot(q_p, keys.astype(BF16)))
            yield
        a_heads = []
        for h in range(HG_HEADS):
            a = jnp.concatenate(
                [x_pos[p][(h * n_blocks + b) * HG_CHUNK:(h * n_blocks + b + 1) * HG_CHUNK]
                 for b in range(n_blocks) for p in range(chunks_per_block)], axis=0)
            a_heads.append(jnp.where(allowed, a, 0.0).astype(BF16))
        zb = jnp.zeros_like(vb)
        v4 = jnp.concatenate(
            [jnp.where(lane_head == h, vb, zb) for h in range(HG_HEADS)], axis=0)
        o_tile = _dot(jnp.concatenate(a_heads, axis=1), v4)
        yield

        qs4 = stack_heads(q_state)
        order = range(n_blocks) if d == 0 else range(n_blocks - 1, -1, -1)
        for b in order:
            st = st_ref[...]
            q4b = jnp.concatenate(
                [qs4[h * TILE + b * HG_BLOCK:h * TILE + (b + 1) * HG_BLOCK] for h in range(HG_HEADS)],
                axis=0)
            ob4 = _dot(q4b, st.astype(BF16))
            ob = jnp.zeros((HG_BLOCK, GROUP_W), F32)
            for h in range(HG_HEADS):
                ob = ob + jnp.where(lane_head == h, ob4[h * HG_BLOCK:(h + 1) * HG_BLOCK], 0.0)
            oc_ref[b * HG_BLOCK:(b + 1) * HG_BLOCK, :] = ob
            first = b * HG_BLOCK + (0 if d == 0 else HG_BLOCK - HG_CHUNK)
            st_ref[...] = st * dec_t[:, first:first + 1] + u_ref[b * TILE:(b + 1) * TILE, :]
            yield
        o_ref[j] = o_tile + oc_ref[...]

    done = object()
    pending = [chain(j, d) for j in range(nb) for d in range(2)]
    while pending:
        pending = [c for c in pending if next(c, done) is not done]


def _hgrn(rest, lb):
    B, S, _ = rest.shape
    n_tiles = S // TILE
    nb = HGRN_NB
    assert B % nb == 0

    def back(i):
        return jnp.where(i == 0, 0, n_tiles - i)

    def fwd(part):
        return pl.BlockSpec((nb, TILE, GROUP_W), lambda b, i: (b, i, part))

    def bwd(part):
        return pl.BlockSpec((nb, TILE, GROUP_W), lambda b, i: (b, back(i), part))

    def whole(arr):
        return pl.BlockSpec(arr.shape, lambda b, i: (0,) * arr.ndim)

    consts = _hgrn_constants()
    dir_sds = jax.ShapeDtypeStruct((B, S, GROUP_W), F32)
    return pl.pallas_call(
        _hgrn_kernel,
        grid=(B // nb, n_tiles),
        in_specs=[fwd(0), fwd(1), fwd(3), bwd(0), bwd(2), bwd(3), whole(lb)]
        + [whole(c) for c in consts],
        out_specs=[pl.BlockSpec((nb, TILE, GROUP_W), lambda b, i: (b, i, 0)),
                   pl.BlockSpec((nb, TILE, GROUP_W), lambda b, i: (b, back(i), 0))],
        out_shape=[dir_sds, dir_sds],
        scratch_shapes=[
            pltpu.VMEM((nb, 2, GROUP_W, GROUP_W), F32),
            pltpu.VMEM((nb, 2, TILE // HG_BLOCK * TILE, GROUP_W), F32),
            pltpu.VMEM((nb, 2, TILE, GROUP_W), F32),
        ],
        compiler_params=_params("parallel", "arbitrary"),
        name="hgrn2",
    )(rest, rest, rest, rest, rest, rest, lb, *consts)


def _s5_discretise(a_re, a_im, log_dt, b_re, b_im):
    dt = jnp.exp(log_dt)[..., None]
    mag = jnp.exp(dt * a_re)
    ab_re, ab_im = mag * jnp.cos(dt * a_im), mag * jnp.sin(dt * a_im)
    den = a_re * a_re + a_im * a_im
    nr = ab_re - 1.0
    f_re = (nr * a_re + ab_im * a_im) / den
    f_im = (ab_im * a_re - nr * a_im) / den
    bb_re = f_re[..., None] * b_re - f_im[..., None] * b_im
    bb_im = f_re[..., None] * b_im + f_im[..., None] * b_re
    return ab_re, ab_im, bb_re, bb_im


def _s5_chunk_matrices(a_re, a_im, log_dt, b_re, b_im, c_re, c_im, d_skip):
    L, G, P, N = S5_L, S5_NGROUPS, S5_GROUP, S5_STATE
    hp = lax.Precision.HIGHEST
    ab_re, ab_im, bb_re, bb_im = _s5_discretise(
        a_re.astype(F32), a_im.astype(F32), log_dt.astype(F32), b_re.astype(F32), b_im.astype(F32))
    c_re, c_im = c_re.astype(F32), c_im.astype(F32)
    j_dt = (jnp.arange(L + 1, dtype=F32).reshape(L + 1, 1, 1, 1)
            * jnp.exp(log_dt.astype(F32))[None, :, :, None])
    mag = jnp.exp(j_dt * a_re.astype(F32)[None])
    pr, pi = mag * jnp.cos(j_dt * a_im.astype(F32)[None]), mag * jnp.sin(j_dt * a_im.astype(F32)[None])
    ca_re = c_re[None] * pr[:, :, :, None, :] - c_im[None] * pi[:, :, :, None, :]
    ca_im = c_re[None] * pi[:, :, :, None, :] + c_im[None] * pr[:, :, :, None, :]
    k = (jnp.einsum("jdgqn,dgnp->dgpqj", ca_re[:L], bb_re, precision=hp)
         - jnp.einsum("jdgqn,dgnp->dgpqj", ca_im[:L], bb_im, precision=hp))
    k_f = k[0].reshape(G * P, P * L)
    k_b = k[1][..., ::-1].reshape(G * P, P * L)
    col_t = jnp.arange(P * L) % L
    t_rows = [jnp.where(col_t >= s, jnp.roll(k_f, s, axis=-1), 0.0)
              + jnp.where(col_t <= s, jnp.roll(k_b, s - (L - 1), axis=-1), 0.0) for s in range(L)]
    tmat = jnp.stack(t_rows, axis=1).reshape(G, P * L, P * L)
    skip = jnp.repeat(d_skip.astype(F32).reshape(G, P), L, axis=1).reshape(G, 1, L * P)
    tmat = tmat + jnp.eye(L * P, dtype=F32)[None] * skip

    def to_state(pw_re, pw_im, d):
        br = jnp.swapaxes(bb_re[d], 1, 2)[None]
        bi = jnp.swapaxes(bb_im[d], 1, 2)[None]
        re = pw_re[:, :, None, :] * br - pw_im[:, :, None, :] * bi
        im = pw_re[:, :, None, :] * bi + pw_im[:, :, None, :] * br
        return (re.transpose(1, 2, 0, 3).reshape(G, L * P, N),
                im.transpose(1, 2, 0, 3).reshape(G, L * P, N))

    f_re, f_im = to_state(pr[:L, 0][::-1], pi[:L, 0][::-1], 0)
    g_re, g_im = to_state(pr[:L, 1], pi[:L, 1], 1)
    bs = jnp.concatenate([f_re, g_re, f_im, g_im], axis=2)

    def from_state(d, flip):
        pw_re, pw_im = pr[1:, d], pi[1:, d]
        if flip:
            pw_re, pw_im = pw_re[::-1], pw_im[::-1]
        pw_re = jnp.tile(pw_re.transpose(1, 2, 0), (1, 1, P))
        pw_im = jnp.tile(pw_im.transpose(1, 2, 0), (1, 1, P))
        cr = jnp.repeat(jnp.swapaxes(c_re[d], 1, 2), L, axis=-1)
        ci = jnp.repeat(jnp.swapaxes(c_im[d], 1, 2), L, axis=-1)
        return cr * pw_re - ci * pw_im, cr * pw_im + ci * pw_re

    (cf_re, cf_im), (cb_re, cb_im) = from_state(0, False), from_state(1, True)
    cs = jnp.concatenate([cf_re, cb_re, -cf_im, -cb_im], axis=1)
    a_l = jnp.stack([jnp.concatenate([pr[L, 0], pr[L, 1]], axis=-1),
                     jnp.concatenate([pi[L, 0], pi[L, 1]], axis=-1)], axis=1)
    return tmat.astype(BF16), bs.astype(BF16), cs.astype(BF16), a_l


def _s5_chunk_kernel(n_ctx_chunks, n_chunks, u_ref, t_ref, bs_ref, cs_ref, al_ref, y_ref,
                     v_ref, xs_ref):
    n_ch, n_steps, n_rows = u_ref.shape
    nb = n_rows // n_chunks
    n = S5_STATE
    u = u_ref[...].reshape(n_ch * n_steps, n_rows).T.astype(BF16)
    v_ref[...] = _dot(u, bs_ref[0])
    fwd_lanes = lax.broadcasted_iota(jnp.int32, (1, 2 * n), 1) < n
    a_re = jnp.broadcast_to(al_ref[0, 0:1, :], (nb, 2 * n))
    a_im = jnp.broadcast_to(al_ref[0, 1:2, :], (nb, 2 * n))

    def step(i, carry):
        xr, xi = carry
        cb = jnp.where(i < n_ctx_chunks, n_ctx_chunks - 1 - i, n_chunks - 1 - (i - n_ctx_chunks))
        rf = pl.ds(pl.multiple_of(i * nb, nb), nb)
        rb = pl.ds(pl.multiple_of(cb * nb, nb), nb)
        xs_ref[rf, 0:n] = xr[:, 0:n]
        xs_ref[rb, n:2 * n] = xr[:, n:2 * n]
        xs_ref[rf, 2 * n:3 * n] = xi[:, 0:n]
        xs_ref[rb, 3 * n:4 * n] = xi[:, n:2 * n]
        vr = jnp.where(fwd_lanes, v_ref[rf, 0:2 * n], v_ref[rb, 0:2 * n])
        vi = jnp.where(fwd_lanes, v_ref[rf, 2 * n:4 * n], v_ref[rb, 2 * n:4 * n])
        return a_re * xr - a_im * xi + vr, a_re * xi + a_im * xr + vi

    zero = jnp.zeros((nb, 2 * n), F32)
    lax.fori_loop(0, n_chunks, step, (zero, zero))
    y = _dot(u, t_ref[0]) + _dot(xs_ref[...].astype(BF16), cs_ref[0])
    y_ref[...] = y.T.reshape(n_ch, n_steps, n_rows)


def _s5_chunked(u5c, tmat, bs, cs, a_l, ctx_len):
    n_chunks, B, L, W = u5c.shape
    P = S5_GROUP
    S = n_chunks * L
    assert L == S5_L and ctx_len % L == 0 and TILE % L == 0
    rows = n_chunks * B
    ut = u5c.reshape(rows, L, W).transpose(2, 1, 0)

    def per_group(arr):
        return pl.BlockSpec((1,) + arr.shape[1:], lambda g: (g,) + (0,) * (arr.ndim - 1))

    group_block = pl.BlockSpec((P, L, rows), lambda g: (g, 0, 0))
    yt = pl.pallas_call(
        functools.partial(_s5_chunk_kernel, ctx_len // L, n_chunks),
        grid=(W // P,),
        in_specs=[group_block, per_group(tmat), per_group(bs), per_group(cs), per_group(a_l)],
        out_specs=group_block,
        out_shape=jax.ShapeDtypeStruct((W, L, rows), F32),
        scratch_shapes=[pltpu.VMEM((rows, 4 * S5_STATE), F32),
                        pltpu.VMEM((rows, 4 * S5_STATE), F32)],
        compiler_params=_params("parallel"),
        name="s5_chunks",
    )(ut, tmat, bs, cs, a_l)
    return yt.transpose(2, 1, 0).reshape(n_chunks, B, L, W)


def _post_kernel(n_hidden_chunks, ctx_tiles_here, tiles_here, hc_ref, hx_ref, a_ref, bf_ref, bb_ref,
                 bg_ref, cb_ref, cc_ref, cu_ref, ccp_ref, cup_ref, ccn_ref, cun_ref, d_ref, modx_ref,
                 modc_ref, ng_ref, hgn_ref, scw_ref, scb_ref, wglu_ref, bglu_ref, wo_ref, wi_ref,
                 w2_ref, o_ref):
    nb, tile, d_model = hx_ref.shape
    n_rows = nb * tile
    t = pl.program_id(1)
    is_ctx = t < ctx_tiles_here
    mod = _tile_mod(is_ctx, modx_ref, modc_ref)

    def flat(ref):
        return ref[...].reshape(n_rows, GROUP_W)

    tot = flat(bf_ref) + flat(bb_ref)
    ms = _split2_dot(tot * tot, _group_mean_matrix(GROUP_W, HG_HD))
    b_val = tot * lax.rsqrt(ms + EPS) * hgn_ref[...] * _silu(flat(bg_ref))
    seg_start = (t == 0) | (t == ctx_tiles_here)
    seg_end = (t == ctx_tiles_here - 1) | (t == tiles_here - 1)
    row = lax.broadcasted_iota(jnp.int32, (tile, 1), 0)
    c_parts = []
    for j in range(nb):
        v = cc_ref[j] * cu_ref[j]
        v_before = jnp.where(seg_start, 0.0, ccp_ref[j, SUBLANES - 1:SUBLANES, :] * cup_ref[j, SUBLANES - 1:SUBLANES, :])
        v_after = jnp.where(seg_end, 0.0, ccn_ref[j, 0:1, :] * cun_ref[j, 0:1, :])
        prev = jnp.where(row == 0, v_before, pltpu.roll(v, 1, 0))
        nxt = jnp.where(row == tile - 1, v_after, pltpu.roll(v, tile - 1, 0))
        y = scw_ref[0:1, :] * prev + scw_ref[1:2, :] * v + scw_ref[2:3, :] * nxt
        c_parts.append(cb_ref[j] * (y + scb_ref[...]))
    c_val = jnp.concatenate(c_parts, axis=0)
    y5 = jax.nn.gelu(jnp.concatenate(
        [d_ref[:, j, :, :].reshape(tile, GROUP_W) for j in range(nb)], axis=0))
    d_val = y5 * jax.nn.sigmoid(_dot(y5.astype(BF16), wglu_ref[...]) + bglu_ref[...])

    mix = (_dot(flat(a_ref), wo_ref[0:GROUP_W, :])
           + _dot(b_val.astype(BF16), wo_ref[GROUP_W:2 * GROUP_W, :])
           + _dot(c_val.astype(BF16), wo_ref[2 * GROUP_W:3 * GROUP_W, :])
           + _dot(d_val.astype(BF16), wo_ref[3 * GROUP_W:4 * GROUP_W, :])).reshape(nb, tile, d_model)
    h = (jnp.where(is_ctx, hc_ref[...], hx_ref[...])
         + mod[:, 2:3, :] * (mix * _rms_scale(mix) * ng_ref[1:2, :]))
    y = (h * _rms_scale(h) * ng_ref[2:3, :]) * (1.0 + mod[:, 4:5, :]) + mod[:, 3:4, :]
    yb = y.reshape(n_rows, d_model).astype(BF16)
    hidden = w2_ref.shape[0]
    hc = hidden // n_hidden_chunks
    ffn = jnp.zeros((n_rows, d_model), F32)
    for j in range(n_hidden_chunks):
        gate = _dot(yb, wi_ref[:, j * hc:(j + 1) * hc])
        up = _dot(yb, wi_ref[:, hidden + j * hc:hidden + (j + 1) * hc])
        ffn = ffn + _dot((_silu(gate) * up).astype(BF16), w2_ref[j * hc:(j + 1) * hc, :])
    ffn = ffn.reshape(nb, tile, d_model)
    o_ref[...] = h + mod[:, 5:6, :] * (ffn * _rms_scale(ffn) * ng_ref[3:4, :])


def _post(h_parts, S, a, hgrn_dirs, rest, y5, mod, norm_g, hg_norm, sc_w, sc_b, w_glu, b_glu, w_out,
          w_ffn_in, w_ffn_out, n_batch, n_ctx_tiles, want_ctx):
    B, _, D = h_parts[1].shape
    off = 0 if want_ctx else n_ctx_tiles
    nt = S // TILE - off
    a_off = off if a.shape[1] == S else 0
    nb = POST_NB
    assert B % nb == 0

    def rows(o):
        return lambda bi, t: (bi, t + o, 0)

    def whole(arr):
        return pl.BlockSpec(arr.shape, lambda bi, t: (0,) * arr.ndim, pipeline_mode=pl.Buffered(1))

    mix_spec = lambda o: pl.BlockSpec((nb, TILE, GROUP_W), rows(o))

    def part(j):
        return pl.BlockSpec((nb, TILE, GROUP_W), lambda bi, t: (bi, t + off, j))

    halo_per_tile = TILE // SUBLANES

    def halo_before(j):
        return pl.BlockSpec((nb, SUBLANES, GROUP_W),
                            lambda bi, t: (bi, jnp.maximum((t + off) * halo_per_tile - 1, 0), j))

    def halo_after(j):
        return pl.BlockSpec((nb, SUBLANES, GROUP_W),
                            lambda bi, t: (bi, jnp.minimum((t + off + 1) * halo_per_tile,
                                                           S // SUBLANES - 1), j))

    tile_chunks = TILE // S5_L
    sc_w_f = sc_w.astype(F32)
    sc_b_row = sc_b.astype(F32).reshape(1, GROUP_W)
    hgn = jnp.tile(hg_norm.astype(F32), HG_HEADS).reshape(1, GROUP_W)
    w_glu_b = w_glu.astype(BF16)
    b_glu_row = b_glu.astype(F32).reshape(1, GROUP_W)
    return pl.pallas_call(
        functools.partial(_post_kernel, 2, n_ctx_tiles - off, nt),
        grid=(B // nb, nt),
        in_specs=_seq_specs(h_parts, (nb, TILE, D), n_ctx_tiles, first_tile=off) + [
            mix_spec(a_off), mix_spec(off), mix_spec(off),
            part(4),
            part(5), part(6), part(7),
            halo_before(6), halo_before(7), halo_after(6), halo_after(7),
            pl.BlockSpec((tile_chunks, nb, S5_L, GROUP_W), lambda bi, t: (t + off, bi, 0, 0)),
            pl.BlockSpec((nb, 6, D), lambda bi, t: (bi, 0, 0)),
            pl.BlockSpec((1, 6, D), lambda bi, t: (n_batch, 0, 0)),
            whole(norm_g), whole(hgn), whole(sc_w_f), whole(sc_b_row), whole(w_glu_b), whole(b_glu_row),
            whole(w_out), whole(w_ffn_in), whole(w_ffn_out),
        ],
        out_specs=pl.BlockSpec((nb, TILE, D), lambda bi, t: (bi, t, 0)),
        out_shape=jax.ShapeDtypeStruct((B, nt * TILE, D), F32),
        compiler_params=_params("parallel", "arbitrary"),
        name="post",
    )(h_parts[0], h_parts[1], a, hgrn_dirs[0], hgrn_dirs[1], rest, rest, rest, rest, rest, rest, rest,
      rest, y5, mod, mod, norm_g, hgn, sc_w_f, sc_b_row, w_glu_b, b_glu_row, w_out, w_ffn_in, w_ffn_out)


def _rope_tables(n_rows, ctx_len):
    rows = jnp.broadcast_to(jnp.arange(n_rows, dtype=F32)[:, None], (n_rows, GRID_W)).reshape(-1)
    cols = jnp.broadcast_to(jnp.arange(GRID_W, dtype=F32)[None, :], (n_rows, GRID_W)).reshape(-1)
    n_freq = DA_HD // 4
    inv = ROPE_BASE ** (-jnp.arange(n_freq, dtype=F32) / n_freq)
    ang = jnp.concatenate([rows[:, None] * inv, cols[:, None] * inv], axis=-1)
    cos, sin = jnp.cos(ang), jnp.sin(ang)
    zero = jnp.zeros_like(sin)
    reps = GROUP_W // DA_HD

    def lanes(even, odd, ctx_value):
        t = jnp.tile(jnp.stack([even, odd], axis=-1).reshape(even.shape[0], DA_HD), (1, reps))
        return jnp.concatenate([jnp.full((ctx_len, GROUP_W), ctx_value, F32), t], axis=0)

    return lanes(cos, cos, 1.0), lanes(-sin, zero, 0.0), lanes(zero, sin, 0.0)


def kernel(x, c, ctx, c_ctx, w_ada, b_ada, norm_g, w_in, w_out, da_lambda, da_subln, hg_lb, hg_norm, sc_w, sc_b, s5_a_re, s5_a_im, s5_log_dt, s5_b_re, s5_b_im, s5_c_re, s5_c_im, s5_d, s5_w_glu, s5_b_glu, w_ffn_in, w_ffn_out):
    B, T, D = x.shape
    ctx_len = ctx.shape[1]
    L = w_ada.shape[0]
    assert ctx_len % TILE == 0 and T % TILE == 0 and T % GRID_W == 0
    n_ctx_tiles = ctx_len // TILE

    bp = -(-(B + 1) // SUBLANES) * SUBLANES
    cvec = jnp.concatenate([c, c_ctx[None, :], jnp.zeros((bp - B - 1, D), c.dtype)], axis=0)
    mods = _modulation(cvec.astype(F32), w_ada, b_ada).reshape(L, bp, 6, D)

    cos, sin_a, sin_b = _rope_tables(T // GRID_W, ctx_len)
    lb = jnp.cumsum(jax.nn.softmax(hg_lb.astype(F32), axis=0), axis=0)
    lb = lb - lb[:1]
    w_in_b = w_in.astype(BF16)
    w_out_b = w_out.astype(BF16)
    w_ffn_in_b = w_ffn_in.astype(BF16)
    w_ffn_out_b = w_ffn_out.astype(BF16)
    s5_mats = jax.vmap(_s5_chunk_matrices)(s5_a_re, s5_a_im, s5_log_dt, s5_b_re, s5_b_im, s5_c_re,
                                           s5_c_im, s5_d)

    S = ctx_len + T
    h_parts = (ctx, x, 0)
    for l in range(L):
        want_ctx = l < L - 1
        lam_init = 0.8 - 0.6 * math.exp(-0.3 * l)
        qkv, rest, u5 = _inproj(h_parts, S, mods[l], norm_g[l, 0], w_in_b[l], cos, sin_a, sin_b, B,
                                n_ctx_tiles)
        a = _attention(qkv, da_lambda[l], da_subln[l], lam_init, ctx_len, want_ctx)
        hgrn_dirs = _hgrn(rest, lb[l])
        tmat, bs, cs, a_l = (m[l] for m in s5_mats)
        y5 = _s5_chunked(u5, tmat, bs, cs, a_l, ctx_len)
        h = _post(h_parts, S, a, hgrn_dirs, rest, y5, mods[l], norm_g[l].astype(F32), hg_norm[l],
                  sc_w[l], sc_b[l], s5_w_glu[l], s5_b_glu[l], w_out_b[l], w_ffn_in_b[l], w_ffn_out_b[l],
                  B, n_ctx_tiles, want_ctx)
        h_parts = (h, h, n_ctx_tiles)
    return h
```

```python
import functools
import math

import jax
import jax.numpy as jnp
from jax import lax
from jax.experimental import pallas as pl
from jax.experimental.pallas import tpu as pltpu

F32 = jnp.float32
BF16 = jnp.bfloat16

EPS = 1e-6
LOG2E = math.log2(math.e)
GRID_W = 64
ROPE_BASE = 10000.0
GROUP_W = 256
DA_HEADS = 4
DA_HD = 32
HG_HEADS = 4
HG_HD = 64
HG_CHUNK = 16
HG_BLOCK = 32
S5_NGROUPS = 16
S5_GROUP = 16
S5_STATE = 64
SUBLANES = 8
TILE = 256
HGRN_NB = 4
INPROJ_NB = 4
POST_NB = 2
S5_L = 16
VMEM_LIMIT = 56 * 1024 * 1024

_NT = (((1,), (1,)), ((), ()))


def _dot(a, b):
    return jnp.dot(a, b, preferred_element_type=F32)


def _dot_nt(a, b):
    return lax.dot_general(a, b, _NT, preferred_element_type=F32)


def _split2_dot_rhs(m, x):
    x1 = x.astype(BF16)
    x2 = (x - x1.astype(F32)).astype(BF16)
    return _dot(m, x1) + _dot(m, x2)


def _split2_dot(x, m):
    x1 = x.astype(BF16)
    x2 = (x - x1.astype(F32)).astype(BF16)
    return _dot(x1, m) + _dot(x2, m)


def _silu(x):
    return x * jax.nn.sigmoid(x)


def _rms_scale(x):
    return lax.rsqrt(jnp.mean(x * x, axis=-1, keepdims=True) + EPS)


def _group_mean_matrix(n, group):
    sh = int(math.log2(group))
    r = lax.broadcasted_iota(jnp.int32, (n, n), 0) >> sh
    c = lax.broadcasted_iota(jnp.int32, (n, n), 1) >> sh
    return jnp.where(r == c, 1.0 / group, 0.0).astype(BF16)


def _params(*sem):
    return pltpu.CompilerParams(dimension_semantics=sem, vmem_limit_bytes=VMEM_LIMIT)


def _mod_kernel(c_ref, w_ref, b_ref, o_ref):
    sc = _silu(c_ref[...])
    o_ref[0] = _dot(sc.astype(BF16), w_ref[0].astype(BF16)) + b_ref[0]


def _modulation(cvec, w_ada, b_ada):
    L, D, N = w_ada.shape
    Bp = cvec.shape[0]
    tn = 1536
    return pl.pallas_call(
        _mod_kernel,
        grid=(L, N // tn),
        in_specs=[
            pl.BlockSpec((Bp, D), lambda l, j: (0, 0)),
            pl.BlockSpec((1, D, tn), lambda l, j: (l, 0, j)),
            pl.BlockSpec((1, 1, tn), lambda l, j: (l, 0, j)),
        ],
        out_specs=pl.BlockSpec((1, Bp, tn), lambda l, j: (l, 0, j)),
        out_shape=jax.ShapeDtypeStruct((L, Bp, N), F32),
        compiler_params=_params("arbitrary", "arbitrary"),
        name="modulation",
    )(cvec, w_ada, b_ada.reshape(L, 1, N))


def _rope(x, cos, sin_a, sin_b):
    return x * cos + pltpu.roll(x, GROUP_W - 1, 1) * sin_a + pltpu.roll(x, 1, 1) * sin_b


def _tile_mod(is_ctx, modx_ref, modc_ref):
    return jnp.where(is_ctx, modc_ref[...], modx_ref[...])


def _seq_specs(h_parts, block, n_ctx_tiles, first_tile=0):
    _, _, lat_off = h_parts

    def ctx_idx(b, t):
        return (b, jnp.minimum(t + first_tile, n_ctx_tiles - 1), 0)

    def lat_idx(b, t):
        return (b, jnp.maximum(t + first_tile, n_ctx_tiles) - n_ctx_tiles + lat_off, 0)

    return [pl.BlockSpec(block, ctx_idx), pl.BlockSpec(block, lat_idx)]


def _inproj_kernel(n_ctx_tiles, hc_ref, hx_ref, modx_ref, modc_ref, g_ref, w_ref, cos_ref, sa_ref,
                   sb_ref, qkv_ref, rest_ref, u5_ref):
    nb, tile, d_model = hx_ref.shape
    is_ctx = pl.program_id(1) < n_ctx_tiles
    mod = _tile_mod(is_ctx, modx_ref, modc_ref)
    h = jnp.where(is_ctx, hc_ref[...], hx_ref[...])
    hn = h * _rms_scale(h) * g_ref[...]
    y = hn * (1.0 + mod[:, 1:2, :]) + mod[:, 0:1, :]
    p = _dot(y.reshape(nb * tile, d_model).astype(BF16), w_ref[...])
    cos, sa, sb = cos_ref[...], sa_ref[...], sb_ref[...]
    for j in range(nb):
        pj = p[j * tile:(j + 1) * tile]
        q = _rope(pj[:, 0:GROUP_W], cos, sa, sb) * (DA_HD ** -0.5 * math.log2(math.e))
        k = _rope(pj[:, GROUP_W:2 * GROUP_W], cos, sa, sb)
        qkv_ref[j, :, 0:GROUP_W] = q.astype(BF16)
        qkv_ref[j, :, GROUP_W:2 * GROUP_W] = k.astype(BF16)
        qkv_ref[j, :, 2 * GROUP_W:3 * GROUP_W] = pj[:, 2 * GROUP_W:3 * GROUP_W].astype(BF16)
        rest_ref[j] = pj[:, 3 * GROUP_W:11 * GROUP_W]
        u5_ref[:, j, :, :] = pj[:, 11 * GROUP_W:].reshape(tile // S5_L, S5_L, GROUP_W)


def _inproj(h_parts, S, mod, g, w, cos, sa, sb, n_batch, n_ctx_tiles):
    B, _, D = h_parts[1].shape
    N = w.shape[1]
    nt = S // TILE
    nb = INPROJ_NB
    assert B % nb == 0
    tab = pl.BlockSpec((TILE, GROUP_W), lambda b, t: (t, 0))
    return pl.pallas_call(
        functools.partial(_inproj_kernel, n_ctx_tiles),
        grid=(B // nb, nt),
        in_specs=_seq_specs(h_parts, (nb, TILE, D), n_ctx_tiles) + [
            pl.BlockSpec((nb, 6, D), lambda b, t: (b, 0, 0)),
            pl.BlockSpec((1, 6, D), lambda b, t: (n_batch, 0, 0)),
            pl.BlockSpec((1, D), lambda b, t: (0, 0)),
            pl.BlockSpec((D, N), lambda b, t: (0, 0)),
            tab, tab, tab,
        ],
        out_specs=[
            pl.BlockSpec((nb, TILE, 3 * GROUP_W), lambda b, t: (b, t, 0)),
            pl.BlockSpec((nb, TILE, 8 * GROUP_W), lambda b, t: (b, t, 0)),
            pl.BlockSpec((TILE // S5_L, nb, S5_L, GROUP_W), lambda b, t: (t, b, 0, 0)),
        ],
        out_shape=[
            jax.ShapeDtypeStruct((B, S, 3 * GROUP_W), BF16),
            jax.ShapeDtypeStruct((B, S, 8 * GROUP_W), F32),
            jax.ShapeDtypeStruct((S // S5_L, B, S5_L, GROUP_W), F32),
        ],
        compiler_params=_params("parallel", "arbitrary"),
        name="inproj",
    )(h_parts[0], h_parts[1], mod, mod, g.reshape(1, D), w, cos, sa, sb)


def _attn_kernel(lam_init, ctx_len, q_off, q_ref, k_ref, v_ref, lamv_ref, g_ref, o_ref,
                 sc_ref, e_ref, va_ref, vb_ref):
    tq = q_ref.shape[1]
    n_keys = k_ref.shape[1]
    lv = lamv_ref[...]
    lam = (jnp.exp(jnp.sum(lv[0:1] * lv[1:2], axis=-1, keepdims=True))
           - jnp.exp(jnp.sum(lv[2:3] * lv[3:4], axis=-1, keepdims=True)) + lam_init)
    lane = lax.broadcasted_iota(jnp.int32, (1, GROUP_W), 1)
    lane_map = lane >> 5
    lane_head = lane >> 6

    @pl.when(pl.program_id(1) == 0)
    def _():
        v = v_ref[0]
        one = jnp.ones_like(v)
        va_ref[...] = jnp.where(lane == GROUP_W - 1, one, v)
        vb_ref[...] = jnp.where(lane == 0, one, v)

    q = q_ref[0]
    q8 = jnp.concatenate(
        [jnp.where(lane_map == j, q, jnp.zeros_like(q)) for j in range(2 * DA_HEADS)], axis=0)

    def attend(nk):
        sc_ref[:, 0:nk] = _dot_nt(q8, k_ref[0, 0:nk, :])
        acc = jnp.zeros((tq, GROUP_W), F32)
        for h in range(DA_HEADS):
            for m in range(2):
                rows = slice((2 * h + m) * tq, (2 * h + m + 1) * tq)
                s = sc_ref[rows, 0:nk]
                e_ref[rows, 0:nk] = jnp.exp2(s - jnp.max(s, axis=-1, keepdims=True)).astype(BF16)
            vv = va_ref if h < DA_HEADS - 1 else vb_ref
            sum_col = GROUP_W - 1 if h < DA_HEADS - 1 else 0
            o2 = _dot(e_ref[2 * h * tq:(2 * h + 2) * tq, 0:nk], vv[0:nk, :])
            o0, o1 = o2[0:tq], o2[tq:2 * tq]
            w0 = 1.0 / o0[:, sum_col:sum_col + 1]
            w1 = lam / o1[:, sum_col:sum_col + 1]
            acc = acc + jnp.where(lane_head == h, o0 * w0 - o1 * w1, 0.0)
        ms = _split2_dot(acc * acc, _group_mean_matrix(GROUP_W, 2 * DA_HD))
        o_ref[0] = (acc * lax.rsqrt(ms + EPS) * g_ref[...] * (1.0 - lam_init)).astype(o_ref.dtype)

    if q_off == 0 and ctx_len == tq:
        @pl.when(pl.program_id(1) == 0)
        def _():
            attend(ctx_len)

        @pl.when(pl.program_id(1) != 0)
        def _():
            attend(n_keys)
    else:
        attend(n_keys)


def _attention(qkv, lam_vecs, subln_g, lam_init, ctx_len, want_ctx):
    B, S, _ = qkv.shape
    tq = TILE
    q_off = 0 if want_ctx else ctx_len // tq
    nq = S // tq - q_off
    g = jnp.tile(subln_g.astype(F32), DA_HEADS).reshape(1, GROUP_W)
    return pl.pallas_call(
        functools.partial(_attn_kernel, lam_init, ctx_len, q_off),
        grid=(B, nq),
        in_specs=[
            pl.BlockSpec((1, tq, GROUP_W), lambda b, i: (b, i + q_off, 0)),
            pl.BlockSpec((1, S, GROUP_W), lambda b, i: (b, 0, 1)),
            pl.BlockSpec((1, S, GROUP_W), lambda b, i: (b, 0, 2)),
            pl.BlockSpec((4, DA_HD), lambda b, i: (0, 0)),
            pl.BlockSpec((1, GROUP_W), lambda b, i: (0, 0)),
        ],
        out_specs=pl.BlockSpec((1, tq, GROUP_W), lambda b, i: (b, i, 0)),
        out_shape=jax.ShapeDtypeStruct((B, nq * tq, GROUP_W), BF16),
        scratch_shapes=[
            pltpu.VMEM((2 * DA_HEADS * tq, S), F32),
            pltpu.VMEM((2 * DA_HEADS * tq, S), BF16),
            pltpu.VMEM((S, GROUP_W), BF16),
            pltpu.VMEM((S, GROUP_W), BF16),
        ],
        compiler_params=_params("parallel", "arbitrary"),
        name="diff_attention",
    )(qkv, qkv, qkv, lam_vecs.astype(F32), g)


def _hgrn_constants():
    row = lax.broadcasted_iota(jnp.int32, (TILE, TILE), 0)
    col = lax.broadcasted_iota(jnp.int32, (TILE, TILE), 1)
    same_chunk = (row // HG_CHUNK) == (col // HG_CHUNK)
    same_block = (row // HG_BLOCK) == (col // HG_BLOCK)

    def ones(mask):
        return jnp.where(mask, 1.0, 0.0).astype(BF16)

    row_mats, col_mats, allowed = [], [], []
    for d in range(2):
        dist = (row // HG_CHUNK - col // HG_CHUNK) * (1 if d == 0 else -1)
        causal = same_chunk & ((col <= row) if d == 0 else (col >= row))
        row_mats.append(jnp.concatenate([ones(causal), ones(same_block & (dist >= 1))], axis=0))
        col_mats.append(jnp.concatenate(
            [ones(same_chunk & ((row <= col) if d == 0 else (row >= col))), ones(same_chunk)], axis=1))
        allowed.append(ones(same_block & ((dist >= 1) | causal)))
    return jnp.stack(row_mats), jnp.stack(col_mats), jnp.stack(allowed)


def _hgrn_kernel(qf_ref, ff_ref, if_ref, qb_ref, fb_ref, ib_ref, lb_ref, rowmat_ref, colmat_ref,
                 allow_ref, of_ref, ob_ref, st2_ref, u2_ref, oc2_ref):
    nb = qf_ref.shape[0]
    n_blocks = TILE // HG_BLOCK
    chunks_per_block = HG_BLOCK // HG_CHUNK
    lane = lax.broadcasted_iota(jnp.int32, (1, GROUP_W), 1)
    lane_head = lane >> int(math.log2(HG_HD))
    lane_block = lane >> int(math.log2(HG_BLOCK))
    lane_pos = (lane >> int(math.log2(HG_CHUNK))) & (chunks_per_block - 1)

    @pl.when(pl.program_id(1) == 0)
    def _():
        st2_ref[...] = jnp.zeros_like(st2_ref)

    def stack_heads(x):
        zero = jnp.zeros_like(x)
        return jnp.concatenate(
            [jnp.where(lane_head == h, x, zero).astype(BF16) for h in range(HG_HEADS)], axis=0)

    def chain(j, d):
        allowed = allow_ref[d] != 0
        row_mat, col_mat = rowmat_ref[d], colmat_ref[d]
        q_ref, f_ref, i_ref, o_ref = ((qf_ref, ff_ref, if_ref, of_ref) if d == 0
                                      else (qb_ref, fb_ref, ib_ref, ob_ref))
        lb = lb_ref[d:d + 1, :]
        st_ref, u_ref, oc_ref = st2_ref.at[j, d], u2_ref.at[j, d], oc2_ref.at[j, d]
        f = lb + (1.0 - lb) * jax.nn.sigmoid(f_ref[j])
        lf = jnp.log(f) * LOG2E
        yield
        sums = _split2_dot_rhs(row_mat, lf)
        kk_t = 1.0 - f.T
        sums_t = _split2_dot(lf.T, col_mat)
        yield
        qd = _silu(q_ref[j]) * jnp.exp2(sums[0:TILE])
        q_state = qd * jnp.exp2(sums[TILE:2 * TILE])
        g_t, gt_t = sums_t[:, 0:TILE], sums_t[:, TILE:2 * TILE]
        kend_t = kk_t * jnp.exp2(gt_t - g_t)
        k_var = [kk_t * jnp.exp2(-g_t), kend_t]
        later = jnp.zeros_like(gt_t)
        for m in range(1, chunks_per_block):
            if d == 0:
                shifted = pltpu.roll(gt_t, TILE - m * HG_CHUNK, 1)
                inside = lane_pos <= chunks_per_block - 1 - m
            else:
                shifted = pltpu.roll(gt_t, m * HG_CHUNK, 1)
                inside = lane_pos >= m
            later = later + jnp.where(inside, shifted, 0.0)
            if m + 1 < chunks_per_block:
                k_var.append(kend_t * jnp.exp2(later))
        kblk_t = kend_t * jnp.exp2(later)
        dec_t = jnp.exp2(gt_t + later)
        vb = i_ref[j].astype(BF16)
        yield

        zero = jnp.zeros_like(kblk_t)
        k_stack = jnp.concatenate(
            [jnp.where(lane_block == b, kblk_t, zero).astype(BF16) for b in range(n_blocks)], axis=0)
        u_ref[...] = _dot(k_stack, vb)
        yield

        qd4 = stack_heads(qd)
        x_pos = []
        for p in range(chunks_per_block):
            keys = jnp.zeros_like(kend_t)
            for n in range(chunks_per_block):
                pc = p - n if d == 0 else p + n
                if 0 <= pc < chunks_per_block:
                    keys = jnp.where(lane_pos == pc, k_var[n], keys)
            q_p = jnp.concatenate(
                [qd4[h * TILE + b * HG_BLOCK + p * HG_CHUNK:h * TILE + b * HG_BLOCK + (p + 1) * HG_CHUNK]
                 for h in range(HG_HEADS) for b in range(n_blocks)], axis=0)
            x_pos.append(_dot(q_p, keys.astype(BF16)))
            yield
        a_heads = []
        for h in range(HG_HEADS):
            a = jnp.concatenate(
                [x_pos[p][(h * n_blocks + b) * HG_CHUNK:(h * n_blocks + b + 1) * HG_CHUNK]
                 for b in range(n_blocks) for p in range(chunks_per_block)], axis=0)
            a_heads.append(jnp.where(allowed, a, 0.0).astype(BF16))
        zb = jnp.zeros_like(vb)
        v4 = jnp.concatenate(
            [jnp.where(lane_head == h, vb, zb) for h in range(HG_HEADS)], axis=0)
        o_tile = _dot(jnp.concatenate(a_heads, axis=1), v4)
        yield

        qs4 = stack_heads(q_state)
        order = range(n_blocks) if d == 0 else range(n_blocks - 1, -1, -1)
        for b in order:
            st = st_ref[...]
            q4b = jnp.concatenate(
                [qs4[h * TILE + b * HG_BLOCK:h * TILE + (b + 1) * HG_BLOCK] for h in range(HG_HEADS)],
                axis=0)
            ob4 = _dot(q4b, st.astype(BF16))
            ob = jnp.zeros((HG_BLOCK, GROUP_W), F32)
            for h in range(HG_HEADS):
                ob = ob + jnp.where(lane_head == h, ob4[h * HG_BLOCK:(h + 1) * HG_BLOCK], 0.0)
            oc_ref[b * HG_BLOCK:(b + 1) * HG_BLOCK, :] = ob
            first = b * HG_BLOCK + (0 if d == 0 else HG_BLOCK - HG_CHUNK)
            st_ref[...] = st * dec_t[:, first:first + 1] + u_ref[b * TILE:(b + 1) * TILE, :]
            yield
        o_ref[j] = o_tile + oc_ref[...]

    done = object()
    pending = [chain(j, d) for j in range(nb) for d in range(2)]
    while pending:
        pending = [c for c in pending if next(c, done) is not done]


def _hgrn(rest, lb):
    B, S, _ = rest.shape
    n_tiles = S // TILE
    nb = HGRN_NB
    assert B % nb == 0

    def back(i):
        return jnp.where(i == 0, 0, n_tiles - i)

    def fwd(part):
        return pl.BlockSpec((nb, TILE, GROUP_W), lambda b, i: (b, i, part))

    def bwd(part):
        return pl.BlockSpec((nb, TILE, GROUP_W), lambda b, i: (b, back(i), part))

    def whole(arr):
        return pl.BlockSpec(arr.shape, lambda b, i: (0,) * arr.ndim)

    consts = _hgrn_constants()
    dir_sds = jax.ShapeDtypeStruct((B, S, GROUP_W), F32)
    return pl.pallas_call(
        _hgrn_kernel,
        grid=(B // nb, n_tiles),
        in_specs=[fwd(0), fwd(1), fwd(3), bwd(0), bwd(2), bwd(3), whole(lb)]
        + [whole(c) for c in consts],
        out_specs=[pl.BlockSpec((nb, TILE, GROUP_W), lambda b, i: (b, i, 0)),
                   pl.BlockSpec((nb, TILE, GROUP_W), lambda b, i: (b, back(i), 0))],
        out_shape=[dir_sds, dir_sds],
        scratch_shapes=[
            pltpu.VMEM((nb, 2, GROUP_W, GROUP_W), F32),
            pltpu.VMEM((nb, 2, TILE // HG_BLOCK * TILE, GROUP_W), F32),
            pltpu.VMEM((nb, 2, TILE, GROUP_W), F32),
        ],
        compiler_params=_params("parallel", "arbitrary"),
        name="hgrn2",
    )(rest, rest, rest, rest, rest, rest, lb, *consts)


def _s5_discretise(a_re, a_im, log_dt, b_re, b_im):
    dt = jnp.exp(log_dt)[..., None]
    mag = jnp.exp(dt * a_re)
    ab_re, ab_im = mag * jnp.cos(dt * a_im), mag * jnp.sin(dt * a_im)
    den = a_re * a_re + a_im * a_im
    nr = ab_re - 1.0
    f_re = (nr * a_re + ab_im * a_im) / den
    f_im = (ab_im * a_re - nr * a_im) / den
    bb_re = f_re[..., None] * b_re - f_im[..., None] * b_im
    bb_im = f_re[..., None] * b_im + f_im[..., None] * b_re
    return ab_re, ab_im, bb_re, bb_im


def _s5_chunk_matrices(a_re, a_im, log_dt, b_re, b_im, c_re, c_im, d_skip):
    L, G, P, N = S5_L, S5_NGROUPS, S5_GROUP, S5_STATE
    hp = lax.Precision.HIGHEST
    ab_re, ab_im, bb_re, bb_im = _s5_discretise(
        a_re.astype(F32), a_im.astype(F32), log_dt.astype(F32), b_re.astype(F32), b_im.astype(F32))
    c_re, c_im = c_re.astype(F32), c_im.astype(F32)
    j_dt = (jnp.arange(L + 1, dtype=F32).reshape(L + 1, 1, 1, 1)
            * jnp.exp(log_dt.astype(F32))[None, :, :, None])
    mag = jnp.exp(j_dt * a_re.astype(F32)[None])
    pr, pi = mag * jnp.cos(j_dt * a_im.astype(F32)[None]), mag * jnp.sin(j_dt * a_im.astype(F32)[None])
    ca_re = c_re[None] * pr[:, :, :, None, :] - c_im[None] * pi[:, :, :, None, :]
    ca_im = c_re[None] * pi[:, :, :, None, :] + c_im[None] * pr[:, :, :, None, :]
    k = (jnp.einsum("jdgqn,dgnp->dgpqj", ca_re[:L], bb_re, precision=hp)
         - jnp.einsum("jdgqn,dgnp->dgpqj", ca_im[:L], bb_im, precision=hp))
    k_f = k[0].reshape(G * P, P * L)
    k_b = k[1][..., ::-1].reshape(G * P, P * L)
    col_t = jnp.arange(P * L) % L
    t_rows = [jnp.where(col_t >= s, jnp.roll(k_f, s, axis=-1), 0.0)
              + jnp.where(col_t <= s, jnp.roll(k_b, s - (L - 1), axis=-1), 0.0) for s in range(L)]
    tmat = jnp.stack(t_rows, axis=1).reshape(G, P * L, P * L)
    skip = jnp.repeat(d_skip.astype(F32).reshape(G, P), L, axis=1).reshape(G, 1, L * P)
    tmat = tmat + jnp.eye(L * P, dtype=F32)[None] * skip

    def to_state(pw_re, pw_im, d):
        br = jnp.swapaxes(bb_re[d], 1, 2)[None]
        bi = jnp.swapaxes(bb_im[d], 1, 2)[None]
        re = pw_re[:, :, None, :] * br - pw_im[:, :, None, :] * bi
        im = pw_re[:, :, None, :] * bi + pw_im[:, :, None, :] * br
        return (re.transpose(1, 2, 0, 3).reshape(G, L * P, N),
                im.transpose(1, 2, 0, 3).reshape(G, L * P, N))

    f_re, f_im = to_state(pr[:L, 0][::-1], pi[:L, 0][::-1], 0)
    g_re, g_im = to_state(pr[:L, 1], pi[:L, 1], 1)
    bs = jnp.concatenate([f_re, g_re, f_im, g_im], axis=2)

    def from_state(d, flip):
        pw_re, pw_im = pr[1:, d], pi[1:, d]
        if flip:
            pw_re, pw_im = pw_re[::-1], pw_im[::-1]
        pw_re = jnp.tile(pw_re.transpose(1, 2, 0), (1, 1, P))
        pw_im = jnp.tile(pw_im.transpose(1, 2, 0), (1, 1, P))
        cr = jnp.repeat(jnp.swapaxes(c_re[d], 1, 2), L, axis=-1)
        ci = jnp.repeat(jnp.swapaxes(c_im[d], 1, 2), L, axis=-1)
        return cr * pw_re - ci * pw_im, cr * pw_im + ci * pw_re

    (cf_re, cf_im), (cb_re, cb_im) = from_state(0, False), from_state(1, True)
    cs = jnp.concatenate([cf_re, cb_re, -cf_im, -cb_im], axis=1)
    a_l = jnp.stack([jnp.concatenate([pr[L, 0], pr[L, 1]], axis=-1),
                     jnp.concatenate([pi[L, 0], pi[L, 1]], axis=-1)], axis=1)
    return tmat.astype(BF16), bs.astype(BF16), cs.astype(BF16), a_l


def _s5_chunk_kernel(n_ctx_chunks, n_chunks, u_ref, t_ref, bs_ref, cs_ref, al_ref, y_ref,
                     v_ref, xs_ref):
    n_ch, n_steps, n_rows = u_ref.shape
    nb = n_rows // n_chunks
    n = S5_STATE
    u = u_ref[...].reshape(n_ch * n_steps, n_rows).T.astype(BF16)
    v_ref[...] = _dot(u, bs_ref[0])
    fwd_lanes = lax.broadcasted_iota(jnp.int32, (1, 2 * n), 1) < n
    a_re = jnp.broadcast_to(al_ref[0, 0:1, :], (nb, 2 * n))
    a_im = jnp.broadcast_to(al_ref[0, 1:2, :], (nb, 2 * n))

    def step(i, carry):
        xr, xi = carry
        cb = jnp.where(i < n_ctx_chunks, n_ctx_chunks - 1 - i, n_chunks - 1 - (i - n_ctx_chunks))
        rf = pl.ds(pl.multiple_of(i * nb, nb), nb)
        rb = pl.ds(pl.multiple_of(cb * nb, nb), nb)
        xs_ref[rf, 0:n] = xr[:, 0:n]
        xs_ref[rb, n:2 * n] = xr[:, n:2 * n]
        xs_ref[rf, 2 * n:3 * n] = xi[:, 0:n]
        xs_ref[rb, 3 * n:4 * n] = xi[:, n:2 * n]
        vr = jnp.where(fwd_lanes, v_ref[rf, 0:2 * n], v_ref[rb, 0:2 * n])
        vi = jnp.where(fwd_lanes, v_ref[rf, 2 * n:4 * n], v_ref[rb, 2 * n:4 * n])
        return a_re * xr - a_im * xi + vr, a_re * xi + a_im * xr + vi

    zero = jnp.zeros((nb, 2 * n), F32)
    lax.fori_loop(0, n_chunks, step, (zero, zero))
    y = _dot(u, t_ref[0]) + _dot(xs_ref[...].astype(BF16), cs_ref[0])
    y_ref[...] = y.T.reshape(n_ch, n_steps, n_rows)


def _s5_chunked(u5c, tmat, bs, cs, a_l, ctx_len):
    n_chunks, B, L, W = u5c.shape
    P = S5_GROUP
    S = n_chunks * L
    assert L == S5_L and ctx_len % L == 0 and TILE % L == 0
    rows = n_chunks * B
    ut = u5c.reshape(rows, L, W).transpose(2, 1, 0)

    def per_group(arr):
        return pl.BlockSpec((1,) + arr.shape[1:], lambda g: (g,) + (0,) * (arr.ndim - 1))

    group_block = pl.BlockSpec((P, L, rows), lambda g: (g, 0, 0))
    yt = pl.pallas_call(
        functools.partial(_s5_chunk_kernel, ctx_len // L, n_chunks),
        grid=(W // P,),
        in_specs=[group_block, per_group(tmat), per_group(bs), per_group(cs), per_group(a_l)],
        out_specs=group_block,
        out_shape=jax.ShapeDtypeStruct((W, L, rows), F32),
        scratch_shapes=[pltpu.VMEM((rows, 4 * S5_STATE), F32),
                        pltpu.VMEM((rows, 4 * S5_STATE), F32)],
        compiler_params=_params("parallel"),
        name="s5_chunks",
    )(ut, tmat, bs, cs, a_l)
    return yt.transpose(2, 1, 0).reshape(n_chunks, B, L, W)


def _post_kernel(n_hidden_chunks, ctx_tiles_here, tiles_here, hc_ref, hx_ref, a_ref, bf_ref, bb_ref,
                 bg_ref, cb_ref, cc_ref, cu_ref, ccp_ref, cup_ref, ccn_ref, cun_ref, d_ref, modx_ref,
                 modc_ref, ng_ref, hgn_ref, scw_ref, scb_ref, wglu_ref, bglu_ref, wo_ref, wi_ref,
                 w2_ref, o_ref):
    nb, tile, d_model = hx_ref.shape
    n_rows = nb * tile
    t = pl.program_id(1)
    is_ctx = t < ctx_tiles_here
    mod = _tile_mod(is_ctx, modx_ref, modc_ref)

    def flat(ref):
        return ref[...].reshape(n_rows, GROUP_W)

    tot = flat(bf_ref) + flat(bb_ref)
    ms = _split2_dot(tot * tot, _group_mean_matrix(GROUP_W, HG_HD))
    b_val = tot * lax.rsqrt(ms + EPS) * hgn_ref[...] * _silu(flat(bg_ref))
    seg_start = (t == 0) | (t == ctx_tiles_here)
    seg_end = (t == ctx_tiles_here - 1) | (t == tiles_here - 1)
    row = lax.broadcasted_iota(jnp.int32, (tile, 1), 0)
    c_parts = []
    for j in range(nb):
        v = cc_ref[j] * cu_ref[j]
        v_before = jnp.where(seg_start, 0.0, ccp_ref[j, SUBLANES - 1:SUBLANES, :] * cup_ref[j, SUBLANES - 1:SUBLANES, :])
        v_after = jnp.where(seg_end, 0.0, ccn_ref[j, 0:1, :] * cun_ref[j, 0:1, :])
        prev = jnp.where(row == 0, v_before, pltpu.roll(v, 1, 0))
        nxt = jnp.where(row == tile - 1, v_after, pltpu.roll(v, tile - 1, 0))
        y = scw_ref[0:1, :] * prev + scw_ref[1:2, :] * v + scw_ref[2:3, :] * nxt
        c_parts.append(cb_ref[j] * (y + scb_ref[...]))
    c_val = jnp.concatenate(c_parts, axis=0)
    y5 = jax.nn.gelu(jnp.concatenate(
        [d_ref[:, j, :, :].reshape(tile, GROUP_W) for j in range(nb)], axis=0))
    d_val = y5 * jax.nn.sigmoid(_dot(y5.astype(BF16), wglu_ref[...]) + bglu_ref[...])

    mix = (_dot(flat(a_ref), wo_ref[0:GROUP_W, :])
           + _dot(b_val.astype(BF16), wo_ref[GROUP_W:2 * GROUP_W, :])
           + _dot(c_val.astype(BF16), wo_ref[2 * GROUP_W:3 * GROUP_W, :])
           + _dot(d_val.astype(BF16), wo_ref[3 * GROUP_W:4 * GROUP_W, :])).reshape(nb, tile, d_model)
    h = (jnp.where(is_ctx, hc_ref[...], hx_ref[...])
         + mod[:, 2:3, :] * (mix * _rms_scale(mix) * ng_ref[1:2, :]))
    y = (h * _rms_scale(h) * ng_ref[2:3, :]) * (1.0 + mod[:, 4:5, :]) + mod[:, 3:4, :]
    yb = y.reshape(n_rows, d_model).astype(BF16)
    hidden = w2_ref.shape[0]
    hc = hidden // n_hidden_chunks
    ffn = jnp.zeros((n_rows, d_model), F32)
    for j in range(n_hidden_chunks):
        gate = _dot(yb, wi_ref[:, j * hc:(j + 1) * hc])
        up = _dot(yb, wi_ref[:, hidden + j * hc:hidden + (j + 1) * hc])
        ffn = ffn + _dot((_silu(gate) * up).astype(BF16), w2_ref[j * hc:(j + 1) * hc, :])
    ffn = ffn.reshape(nb, tile, d_model)
    o_ref[...] = h + mod[:, 5:6, :] * (ffn * _rms_scale(ffn) * ng_ref[3:4, :])


def _post(h_parts, S, a, hgrn_dirs, rest, y5, mod, norm_g, hg_norm, sc_w, sc_b, w_glu, b_glu, w_out,
          w_ffn_in, w_ffn_out, n_batch, n_ctx_tiles, want_ctx):
    B, _, D = h_parts[1].shape
    off = 0 if want_ctx else n_ctx_tiles
    nt = S // TILE - off
    a_off = off if a.shape[1] == S else 0
    nb = POST_NB
    assert B % nb == 0

    def rows(o):
        return lambda bi, t: (bi, t + o, 0)

    def whole(arr):
        return pl.BlockSpec(arr.shape, lambda bi, t: (0,) * arr.ndim, pipeline_mode=pl.Buffered(1))

    mix_spec = lambda o: pl.BlockSpec((nb, TILE, GROUP_W), rows(o))

    def part(j):
        return pl.BlockSpec((nb, TILE, GROUP_W), lambda bi, t: (bi, t + off, j))

    halo_per_tile = TILE // SUBLANES

    def halo_before(j):
        return pl.BlockSpec((nb, SUBLANES, GROUP_W),
                            lambda bi, t: (bi, jnp.maximum((t + off) * halo_per_tile - 1, 0), j))

    def halo_after(j):
        return pl.BlockSpec((nb, SUBLANES, GROUP_W),
                            lambda bi, t: (bi, jnp.minimum((t + off + 1) * halo_per_tile,
                                                           S // SUBLANES - 1), j))

    tile_chunks = TILE // S5_L
    sc_w_f = sc_w.astype(F32)
    sc_b_row = sc_b.astype(F32).reshape(1, GROUP_W)
    hgn = jnp.tile(hg_norm.astype(F32), HG_HEADS).reshape(1, GROUP_W)
    w_glu_b = w_glu.astype(BF16)
    b_glu_row = b_glu.astype(F32).reshape(1, GROUP_W)
    return pl.pallas_call(
        functools.partial(_post_kernel, 2, n_ctx_tiles - off, nt),
        grid=(B // nb, nt),
        in_specs=_seq_specs(h_parts, (nb, TILE, D), n_ctx_tiles, first_tile=off) + [
            mix_spec(a_off), mix_spec(off), mix_spec(off),
            part(4),
            part(5), part(6), part(7),
            halo_before(6), halo_before(7), halo_after(6), halo_after(7),
            pl.BlockSpec((tile_chunks, nb, S5_L, GROUP_W), lambda bi, t: (t + off, bi, 0, 0)),
            pl.BlockSpec((nb, 6, D), lambda bi, t: (bi, 0, 0)),
            pl.BlockSpec((1, 6, D), lambda bi, t: (n_batch, 0, 0)),
            whole(norm_g), whole(hgn), whole(sc_w_f), whole(sc_b_row), whole(w_glu_b), whole(b_glu_row),
            whole(w_out), whole(w_ffn_in), whole(w_ffn_out),
        ],
        out_specs=pl.BlockSpec((nb, TILE, D), lambda bi, t: (bi, t, 0)),
        out_shape=jax.ShapeDtypeStruct((B, nt * TILE, D), F32),
        compiler_params=_params("parallel", "arbitrary"),
        name="post",
    )(h_parts[0], h_parts[1], a, hgrn_dirs[0], hgrn_dirs[1], rest, rest, rest, rest, rest, rest, rest,
      rest, y5, mod, mod, norm_g, hgn, sc_w_f, sc_b_row, w_glu_b, b_glu_row, w_out, w_ffn_in, w_ffn_out)


def _rope_tables(n_rows, ctx_len):
    rows = jnp.broadcast_to(jnp.arange(n_rows, dtype=F32)[:, None], (n_rows, GRID_W)).reshape(-1)
    cols = jnp.broadcast_to(jnp.arange(GRID_W, dtype=F32)[None, :], (n_rows, GRID_W)).reshape(-1)
    n_freq = DA_HD // 4
    inv = ROPE_BASE ** (-jnp.arange(n_freq, dtype=F32) / n_freq)
    ang = jnp.concatenate([rows[:, None] * inv, cols[:, None] * inv], axis=-1)
    cos, sin = jnp.cos(ang), jnp.sin(ang)
    zero = jnp.zeros_like(sin)
    reps = GROUP_W // DA_HD

    def lanes(even, odd, ctx_value):
        t = jnp.tile(jnp.stack([even, odd], axis=-1).reshape(even.shape[0], DA_HD), (1, reps))
        return jnp.concatenate([jnp.full((ctx_len, GROUP_W), ctx_value, F32), t], axis=0)

    return lanes(cos, cos, 1.0), lanes(-sin, zero, 0.0), lanes(zero, sin, 0.0)


def kernel(x, c, ctx, c_ctx, w_ada, b_ada, norm_g, w_in, w_out, da_lambda, da_subln, hg_lb, hg_norm, sc_w, sc_b, s5_a_re, s5_a_im, s5_log_dt, s5_b_re, s5_b_im, s5_c_re, s5_c_im, s5_d, s5_w_glu, s5_b_glu, w_ffn_in, w_ffn_out):
    B, T, D = x.shape
    ctx_len = ctx.shape[1]
    L = w_ada.shape[0]
    assert ctx_len % TILE == 0 and T % TILE == 0 and T % GRID_W == 0
    n_ctx_tiles = ctx_len // TILE

    bp = -(-(B + 1) // SUBLANES) * SUBLANES
    cvec = jnp.concatenate([c, c_ctx[None, :], jnp.zeros((bp - B - 1, D), c.dtype)], axis=0)
    mods = _modulation(cvec.astype(F32), w_ada, b_ada).reshape(L, bp, 6, D)

    cos, sin_a, sin_b = _rope_tables(T // GRID_W, ctx_len)
    lb = jnp.cumsum(jax.nn.softmax(hg_lb.astype(F32), axis=0), axis=0)
    lb = lb - lb[:1]
    w_in_b = w_in.astype(BF16)
    w_out_b = w_out.astype(BF16)
    w_ffn_in_b = w_ffn_in.astype(BF16)
    w_ffn_out_b = w_ffn_out.astype(BF16)
    s5_mats = jax.vmap(_s5_chunk_matrices)(s5_a_re, s5_a_im, s5_log_dt, s5_b_re, s5_b_im, s5_c_re,
                                           s5_c_im, s5_d)

    S = ctx_len + T
    h_parts = (ctx, x, 0)
    for l in range(L):
        want_ctx = l < L - 1
        lam_init = 0.8 - 0.6 * math.exp(-0.3 * l)
        qkv, rest, u5 = _inproj(h_parts, S, mods[l], norm_g[l, 0], w_in_b[l], cos, sin_a, sin_b, B,
                                n_ctx_tiles)
        a = _attention(qkv, da_lambda[l], da_subln[l], lam_init, ctx_len, want_ctx)
        hgrn_dirs = _hgrn(rest, lb[l])
        tmat, bs, cs, a_l = (m[l] for m in s5_mats)
        y5 = _s5_chunked(u5, tmat, bs, cs, a_l, ctx_len)
        h = _post(h_parts, S, a, hgrn_dirs, rest, y5, mods[l], norm_g[l].astype(F32), hg_norm[l],
                  sc_w[l], sc_b[l], s5_w_glu[l], s5_b_glu[l], w_out_b[l], w_ffn_in_b[l], w_ffn_out_b[l],
                  B, n_ctx_tiles, want_ctx)
        h_parts = (h, h, n_ctx_tiles)
    return h
```
